```python
import jax, jax.numpy as jnp
from jax import lax
import numpy as np

D_MODEL = 2048
BATCH = 8
SEQ = 4096
DEPTH = 2

CHUNK = 64
GMLP_BLOCK = 128
MIX_WIDTH = D_MODEL
A_WIDTH = MIX_WIDTH // 2
B_WIDTH = MIX_WIDTH - A_WIDTH
A_HEADS = 8
A_HEAD_DIM = A_WIDTH // A_HEADS
B_GROUPS = 8
CONV_WIDTH = 3
IN_COLS = 2 * A_WIDTH + 3 * B_WIDTH
D_FF = -(-8 * D_MODEL // (3 * 256)) * 256
RMS_EPS = 1e-6
LN_EPS = 1e-5

kernel_name = "hybrid_gmlp_shortconv_swiglu_trunk"


def rmsnorm(x, g):
    xf = x.astype(jnp.float32)
    y = xf * lax.rsqrt(jnp.mean(xf * xf, axis=-1, keepdims=True) + RMS_EPS)
    return (y * g.astype(jnp.float32)).astype(x.dtype)


def layernorm(x, g, b):
    xf = x.astype(jnp.float32)
    mu = jnp.mean(xf, axis=-1, keepdims=True)
    xc = xf - mu
    y = xc * lax.rsqrt(jnp.mean(xc * xc, axis=-1, keepdims=True) + LN_EPS)
    return (y * g.astype(jnp.float32) + b.astype(jnp.float32)).astype(x.dtype)


def chunk_causal_block_mask():
    pos = jnp.arange(GMLP_BLOCK)
    return (pos[None, :] // CHUNK) <= (pos[:, None] // CHUNK)


def spatial_gating(u, v, w_s, b_s, ln_g, ln_b):
    bsz, seq, _ = v.shape
    v = layernorm(v, ln_g, ln_b)
    v = v.reshape(bsz, seq // GMLP_BLOCK, GMLP_BLOCK, A_HEADS, A_HEAD_DIM)
    w = jnp.where(chunk_causal_block_mask()[None], w_s, jnp.zeros((), w_s.dtype))
    mixed = jnp.einsum('hij,bcjhd->bcihd', w, v) + b_s.T[None, None, :, :, None]
    return u * mixed.reshape(bsz, seq, A_WIDTH)


def causal_depthwise_conv(h, w):
    c = h.shape[-1]
    return lax.conv_general_dilated(
        h, w[:, None, :].astype(h.dtype), window_strides=(1,),
        padding=[(CONV_WIDTH - 1, 0)],
        dimension_numbers=('NWC', 'WIO', 'NWC'),
        feature_group_count=c)


def hybrid_layer(x, norm1_g, w_in, ln_g, ln_b, w_s, b_s, conv_w, group_norm_g,
                 w_out, norm2_g, w_gate, w_up, w_down):
    h = rmsnorm(x, norm1_g)
    z = jnp.einsum('bsd,dn->bsn', h, w_in)
    z_a = jax.nn.gelu(z[..., :2 * A_WIDTH])
    u, v = z_a[..., :A_WIDTH], z_a[..., A_WIDTH:]
    off = 2 * A_WIDTH
    gate_b = z[..., off:off + B_WIDTH]
    gate_c = z[..., off + B_WIDTH:off + 2 * B_WIDTH]
    h_b = z[..., off + 2 * B_WIDTH:]
    y_a = spatial_gating(u, v, w_s, b_s, ln_g, ln_b)
    y_b = gate_b * causal_depthwise_conv(gate_c * h_b, conv_w)
    y = jnp.concatenate([rmsnorm(y_a, group_norm_g[:A_WIDTH]),
                         rmsnorm(y_b, group_norm_g[A_WIDTH:])], axis=-1)
    x = x + jnp.einsum('bsm,md->bsd', y, w_out)
    h2 = rmsnorm(x, norm2_g)
    act = jax.nn.silu(jnp.einsum('bsd,df->bsf', h2, w_gate)) * jnp.einsum('bsd,df->bsf', h2, w_up)
    return x + jnp.einsum('bsf,fd->bsd', act, w_down)


def _fwd_setup_inputs(seed: int = 0) -> dict:
    key = jax.random.key(seed)
    ks = jax.random.split(key, 16)
    f32 = jnp.float32
    nrm = lambda k, shape, scale: jax.random.normal(k, shape, f32) * scale
    return {
        "x": jax.random.normal(ks[0], (BATCH, SEQ, D_MODEL), f32),
        "norm1_g": 1.0 + nrm(ks[1], (DEPTH, D_MODEL), 0.02),
        "w_in": nrm(ks[2], (DEPTH, D_MODEL, IN_COLS), D_MODEL ** -0.5),
        "gmlp_ln_g": 1.0 + nrm(ks[3], (DEPTH, A_WIDTH), 0.02),
        "gmlp_ln_b": nrm(ks[4], (DEPTH, A_WIDTH), 0.02),
        "w_spatial": nrm(ks[5], (DEPTH, A_HEADS, GMLP_BLOCK, GMLP_BLOCK), GMLP_BLOCK ** -0.5),
        "b_spatial": 1.0 + nrm(ks[6], (DEPTH, A_HEADS, GMLP_BLOCK), 0.1),
        "conv_w": nrm(ks[7], (DEPTH, CONV_WIDTH, B_WIDTH), CONV_WIDTH ** -0.5),
        "group_norm_g": 1.0 + nrm(ks[8], (DEPTH, MIX_WIDTH), 0.02),
        "w_out": nrm(ks[9], (DEPTH, MIX_WIDTH, D_MODEL), MIX_WIDTH ** -0.5),
        "norm2_g": 1.0 + nrm(ks[10], (DEPTH, D_MODEL), 0.02),
        "w_gate": nrm(ks[11], (DEPTH, D_MODEL, D_FF), D_MODEL ** -0.5),
        "w_up": nrm(ks[12], (DEPTH, D_MODEL, D_FF), D_MODEL ** -0.5),
        "w_down": nrm(ks[13], (DEPTH, D_FF, D_MODEL), D_FF ** -0.5),
        "final_norm_g": 1.0 + nrm(ks[14], (D_MODEL,), 0.02),
    }


def _fwd_reference(x, norm1_g, w_in, gmlp_ln_g, gmlp_ln_b, w_spatial, b_spatial, conv_w,
              group_norm_g, w_out, norm2_g, w_gate, w_up, w_down, final_norm_g):
    for layer in range(DEPTH):
        x = hybrid_layer(x, norm1_g[layer], w_in[layer], gmlp_ln_g[layer], gmlp_ln_b[layer],
                         w_spatial[layer], b_spatial[layer], conv_w[layer], group_norm_g[layer],
                         w_out[layer], norm2_g[layer], w_gate[layer], w_up[layer], w_down[layer])
    return rmsnorm(x, final_norm_g)


import jax as _jax
import jax.numpy as _jnp

TWIN_FORMAT = 'train_step'
FWD_PARAMS = ['x', 'norm1_g', 'w_in', 'gmlp_ln_g', 'gmlp_ln_b', 'w_spatial', 'b_spatial', 'conv_w', 'group_norm_g', 'w_out', 'norm2_g', 'w_gate', 'w_up', 'w_down', 'final_norm_g']
TWIN_WEIGHTS = ['norm1_g', 'w_in', 'gmlp_ln_g', 'gmlp_ln_b', 'w_spatial', 'b_spatial', 'conv_w', 'group_norm_g', 'w_out', 'norm2_g', 'w_gate', 'w_up', 'w_down', 'final_norm_g']
TWIN_DIFF_INPUT = 'x'
TWIN_INPUTS = ['x', 'norm1_g', 'w_in', 'gmlp_ln_g', 'gmlp_ln_b', 'w_spatial', 'b_spatial', 'conv_w', 'group_norm_g', 'w_out', 'norm2_g', 'w_gate', 'w_up', 'w_down', 'final_norm_g', 'loss_target', 'm_norm1_g', 'm_w_in', 'm_gmlp_ln_g', 'm_gmlp_ln_b', 'm_w_spatial', 'm_b_spatial', 'm_conv_w', 'm_group_norm_g', 'm_w_out', 'm_norm2_g', 'm_w_gate', 'm_w_up', 'm_w_down', 'm_final_norm_g', 'v_norm1_g', 'v_w_in', 'v_gmlp_ln_g', 'v_gmlp_ln_b', 'v_w_spatial', 'v_b_spatial', 'v_conv_w', 'v_group_norm_g', 'v_w_out', 'v_norm2_g', 'v_w_gate', 'v_w_up', 'v_w_down', 'v_final_norm_g']
TWIN_OUTPUTS = ['loss', 'grad_x', 'grad_norm1_g', 'grad_w_in', 'grad_gmlp_ln_g', 'grad_gmlp_ln_b', 'grad_w_spatial', 'grad_b_spatial', 'grad_conv_w', 'grad_group_norm_g', 'grad_w_out', 'grad_norm2_g', 'grad_w_gate', 'grad_w_up', 'grad_w_down', 'grad_final_norm_g', 'delta_norm1_g', 'delta_w_in', 'delta_gmlp_ln_g', 'delta_gmlp_ln_b', 'delta_w_spatial', 'delta_b_spatial', 'delta_conv_w', 'delta_group_norm_g', 'delta_w_out', 'delta_norm2_g', 'delta_w_gate', 'delta_w_up', 'delta_w_down', 'delta_final_norm_g', 'new_m_norm1_g', 'new_m_w_in', 'new_m_gmlp_ln_g', 'new_m_gmlp_ln_b', 'new_m_w_spatial', 'new_m_b_spatial', 'new_m_conv_w', 'new_m_group_norm_g', 'new_m_w_out', 'new_m_norm2_g', 'new_m_w_gate', 'new_m_w_up', 'new_m_w_down', 'new_m_final_norm_g', 'new_v_norm1_g', 'new_v_w_in', 'new_v_gmlp_ln_g', 'new_v_gmlp_ln_b', 'new_v_w_spatial', 'new_v_b_spatial', 'new_v_conv_w', 'new_v_group_norm_g', 'new_v_w_out', 'new_v_norm2_g', 'new_v_w_gate', 'new_v_w_up', 'new_v_w_down', 'new_v_final_norm_g']
TWIN_LEAF_KINDS = {'loss': 'loss', 'grad_x': 'grad_x', 'grad_norm1_g': 'grad_w', 'grad_w_in': 'grad_w', 'grad_gmlp_ln_g': 'grad_w', 'grad_gmlp_ln_b': 'grad_w', 'grad_w_spatial': 'grad_w', 'grad_b_spatial': 'grad_w', 'grad_conv_w': 'grad_w', 'grad_group_norm_g': 'grad_w', 'grad_w_out': 'grad_w', 'grad_norm2_g': 'grad_w', 'grad_w_gate': 'grad_w', 'grad_w_up': 'grad_w', 'grad_w_down': 'grad_w', 'grad_final_norm_g': 'grad_w', 'delta_norm1_g': 'delta_w', 'delta_w_in': 'delta_w', 'delta_gmlp_ln_g': 'delta_w', 'delta_gmlp_ln_b': 'delta_w', 'delta_w_spatial': 'delta_w', 'delta_b_spatial': 'delta_w', 'delta_conv_w': 'delta_w', 'delta_group_norm_g': 'delta_w', 'delta_w_out': 'delta_w', 'delta_norm2_g': 'delta_w', 'delta_w_gate': 'delta_w', 'delta_w_up': 'delta_w', 'delta_w_down': 'delta_w', 'delta_final_norm_g': 'delta_w', 'new_m_norm1_g': 'new_m', 'new_m_w_in': 'new_m', 'new_m_gmlp_ln_g': 'new_m', 'new_m_gmlp_ln_b': 'new_m', 'new_m_w_spatial': 'new_m', 'new_m_b_spatial': 'new_m', 'new_m_conv_w': 'new_m', 'new_m_group_norm_g': 'new_m', 'new_m_w_out': 'new_m', 'new_m_norm2_g': 'new_m', 'new_m_w_gate': 'new_m', 'new_m_w_up': 'new_m', 'new_m_w_down': 'new_m', 'new_m_final_norm_g': 'new_m', 'new_v_norm1_g': 'new_v', 'new_v_w_in': 'new_v', 'new_v_gmlp_ln_g': 'new_v', 'new_v_gmlp_ln_b': 'new_v', 'new_v_w_spatial': 'new_v', 'new_v_b_spatial': 'new_v', 'new_v_conv_w': 'new_v', 'new_v_group_norm_g': 'new_v', 'new_v_w_out': 'new_v', 'new_v_norm2_g': 'new_v', 'new_v_w_gate': 'new_v', 'new_v_w_up': 'new_v', 'new_v_w_down': 'new_v', 'new_v_final_norm_g': 'new_v'}


def _forward(args):
    return _fwd_reference(*[args[k] for k in FWD_PARAMS])


def _output_shape():
    out = _jax.eval_shape(lambda: _forward(_fwd_setup_inputs(0)))
    return out.shape, out.dtype

N_MICROBATCH = 1
ADAM_LR = 0.001
ADAM_B1 = 0.9
ADAM_B2 = 0.999
ADAM_EPS = 1e-08
ADAM_WD = 0.01
ADAM_STEP = 10
PER_EXAMPLE_BATCH_AXIS = {'x': 0, 'loss_target': 0}
SHARED_INPUTS = []
_WEIGHT_DTYPES = {'norm1_g': _jnp.float32, 'w_in': _jnp.float32, 'gmlp_ln_g': _jnp.float32, 'gmlp_ln_b': _jnp.float32, 'w_spatial': _jnp.float32, 'b_spatial': _jnp.float32, 'conv_w': _jnp.float32, 'group_norm_g': _jnp.float32, 'w_out': _jnp.float32, 'norm2_g': _jnp.float32, 'w_gate': _jnp.float32, 'w_up': _jnp.float32, 'w_down': _jnp.float32, 'final_norm_g': _jnp.float32}
MOMENT_SCALE = {'norm1_g': 1.051981e-01, 'w_in': 6.663672e-02, 'gmlp_ln_g': 4.538012e-02, 'gmlp_ln_b': 4.280388e-02, 'w_spatial': 4.457261e-02, 'b_spatial': 5.095983e-02, 'conv_w': 7.035872e-02, 'group_norm_g': 6.898862e-02, 'w_out': 6.980697e-02, 'norm2_g': 5.238641e-02, 'w_gate': 2.235576e-02, 'w_up': 2.166671e-02, 'w_down': 3.597575e-02, 'final_norm_g': 1.599595e+01}


def _to_microbatches(a, axis):
    t = _jnp.moveaxis(a, axis, 0)
    t = t.reshape((N_MICROBATCH, t.shape[0] // N_MICROBATCH) + t.shape[1:])
    return _jnp.moveaxis(t, 1, axis + 1)


def setup_inputs(seed: int = 0) -> dict:
    inp = _fwd_setup_inputs(seed)
    key = _jax.random.fold_in(_jax.random.key(seed), 7919)
    shape, _ = _output_shape()
    out = dict(inp)
    out["loss_target"] = _jax.random.normal(_jax.random.fold_in(key, 0), shape, _jnp.float32)
    for i, name in enumerate(TWIN_WEIGHTS):
        w = inp[name].astype(_jnp.float32)
        if MOMENT_SCALE is None:
            s = _jnp.sqrt(_jnp.mean(_jnp.square(w)) + 1e-30)
        else:
            s = MOMENT_SCALE[name]
        km, kv = _jax.random.split(_jax.random.fold_in(key, i + 1))
        out[name] = w
        out["m_" + name] = s * _jax.random.normal(km, w.shape, _jnp.float32)
        out["v_" + name] = (s * s) * _jax.random.uniform(kv, w.shape, _jnp.float32, 0.5, 1.5)
    if N_MICROBATCH > 1:
        for name, axis in PER_EXAMPLE_BATCH_AXIS.items():
            out[name] = _to_microbatches(out[name], axis)
    return {'x': out['x'], 'norm1_g': out['norm1_g'], 'w_in': out['w_in'], 'gmlp_ln_g': out['gmlp_ln_g'], 'gmlp_ln_b': out['gmlp_ln_b'], 'w_spatial': out['w_spatial'], 'b_spatial': out['b_spatial'], 'conv_w': out['conv_w'], 'group_norm_g': out['group_norm_g'], 'w_out': out['w_out'], 'norm2_g': out['norm2_g'], 'w_gate': out['w_gate'], 'w_up': out['w_up'], 'w_down': out['w_down'], 'final_norm_g': out['final_norm_g'], 'loss_target': out['loss_target'], 'm_norm1_g': out['m_norm1_g'], 'm_w_in': out['m_w_in'], 'm_gmlp_ln_g': out['m_gmlp_ln_g'], 'm_gmlp_ln_b': out['m_gmlp_ln_b'], 'm_w_spatial': out['m_w_spatial'], 'm_b_spatial': out['m_b_spatial'], 'm_conv_w': out['m_conv_w'], 'm_group_norm_g': out['m_group_norm_g'], 'm_w_out': out['m_w_out'], 'm_norm2_g': out['m_norm2_g'], 'm_w_gate': out['m_w_gate'], 'm_w_up': out['m_w_up'], 'm_w_down': out['m_w_down'], 'm_final_norm_g': out['m_final_norm_g'], 'v_norm1_g': out['v_norm1_g'], 'v_w_in': out['v_w_in'], 'v_gmlp_ln_g': out['v_gmlp_ln_g'], 'v_gmlp_ln_b': out['v_gmlp_ln_b'], 'v_w_spatial': out['v_w_spatial'], 'v_b_spatial': out['v_b_spatial'], 'v_conv_w': out['v_conv_w'], 'v_group_norm_g': out['v_group_norm_g'], 'v_w_out': out['v_w_out'], 'v_norm2_g': out['v_norm2_g'], 'v_w_gate': out['v_w_gate'], 'v_w_up': out['v_w_up'], 'v_w_down': out['v_w_down'], 'v_final_norm_g': out['v_final_norm_g']}


def _loss(weights, diff, rest, loss_target):
    with _jax.named_scope("forward"):
        args = {**rest, TWIN_DIFF_INPUT: diff, **{k: w.astype(_WEIGHT_DTYPES[k]) for k, w in weights.items()}}
        y = _forward(args)
    with _jax.named_scope("loss_head"):
        err = _jnp.square(y.astype(_jnp.float32) - loss_target)
        return 0.5 * _jnp.sum(_jnp.mean(err, axis=-1)) if err.ndim else 0.5 * err


def _adamw(w, g, m, v):
    m = ADAM_B1 * m + (1.0 - ADAM_B1) * g
    v = ADAM_B2 * v + (1.0 - ADAM_B2) * _jnp.square(g)
    m_hat = m / (1.0 - ADAM_B1 ** ADAM_STEP)
    v_hat = v / (1.0 - ADAM_B2 ** ADAM_STEP)
    delta = -ADAM_LR * (m_hat / (_jnp.sqrt(v_hat) + ADAM_EPS) + ADAM_WD * w)
    return delta, m, v


def reference(x, norm1_g, w_in, gmlp_ln_g, gmlp_ln_b, w_spatial, b_spatial, conv_w, group_norm_g, w_out, norm2_g, w_gate, w_up, w_down, final_norm_g, loss_target, m_norm1_g, m_w_in, m_gmlp_ln_g, m_gmlp_ln_b, m_w_spatial, m_b_spatial, m_conv_w, m_group_norm_g, m_w_out, m_norm2_g, m_w_gate, m_w_up, m_w_down, m_final_norm_g, v_norm1_g, v_w_in, v_gmlp_ln_g, v_gmlp_ln_b, v_w_spatial, v_b_spatial, v_conv_w, v_group_norm_g, v_w_out, v_norm2_g, v_w_gate, v_w_up, v_w_down, v_final_norm_g):
    given = dict(x=x, norm1_g=norm1_g, w_in=w_in, gmlp_ln_g=gmlp_ln_g, gmlp_ln_b=gmlp_ln_b, w_spatial=w_spatial, b_spatial=b_spatial, conv_w=conv_w, group_norm_g=group_norm_g, w_out=w_out, norm2_g=norm2_g, w_gate=w_gate, w_up=w_up, w_down=w_down, final_norm_g=final_norm_g, loss_target=loss_target, m_norm1_g=m_norm1_g, m_w_in=m_w_in, m_gmlp_ln_g=m_gmlp_ln_g, m_gmlp_ln_b=m_gmlp_ln_b, m_w_spatial=m_w_spatial, m_b_spatial=m_b_spatial, m_conv_w=m_conv_w, m_group_norm_g=m_group_norm_g, m_w_out=m_w_out, m_norm2_g=m_norm2_g, m_w_gate=m_w_gate, m_w_up=m_w_up, m_w_down=m_w_down, m_final_norm_g=m_final_norm_g, v_norm1_g=v_norm1_g, v_w_in=v_w_in, v_gmlp_ln_g=v_gmlp_ln_g, v_gmlp_ln_b=v_gmlp_ln_b, v_w_spatial=v_w_spatial, v_b_spatial=v_b_spatial, v_conv_w=v_conv_w, v_group_norm_g=v_group_norm_g, v_w_out=v_w_out, v_norm2_g=v_norm2_g, v_w_gate=v_w_gate, v_w_up=v_w_up, v_w_down=v_w_down, v_final_norm_g=v_final_norm_g)
    weights = {n: given[n] for n in TWIN_WEIGHTS}
    shared = {n: given[n] for n in SHARED_INPUTS}
    per_example = {n: given[n] for n in ['x']}
    grad_fn = _jax.value_and_grad(_loss, argnums=(0, 1))

    def one_microbatch(ex, loss_target):
        ex = dict(ex)
        diff = ex.pop(TWIN_DIFF_INPUT)
        return grad_fn(weights, diff, {**shared, **ex}, loss_target)

    if N_MICROBATCH == 1:
        loss, (grad_w, grad_x) = one_microbatch(per_example, given["loss_target"])
    else:
        def body(carry, xs):
            loss_sum, grad_sum = carry
            l_k, (gw_k, gx_k) = one_microbatch(xs[0], xs[1])
            with _jax.named_scope("update"):
                return (loss_sum + l_k, _jax.tree.map(_jnp.add, grad_sum, gw_k)), gx_k

        init = (_jnp.zeros((), _jnp.float32), _jax.tree.map(_jnp.zeros_like, weights))
        (loss, grad_w), grad_x = _jax.lax.scan(body, init, (per_example, given["loss_target"]))
    with _jax.named_scope("update"):
        delta_w, new_m, new_v = {}, {}, {}
        for n in TWIN_WEIGHTS:
            delta_w[n], new_m[n], new_v[n] = _adamw(weights[n], grad_w[n], given["m_" + n], given["v_" + n])
    return (loss, grad_x, *[grad_w[n] for n in TWIN_WEIGHTS], *[delta_w[n] for n in TWIN_WEIGHTS],
            *[new_m[n] for n in TWIN_WEIGHTS], *[new_v[n] for n in TWIN_WEIGHTS])
```

```python
import functools
import math

import jax
import jax.numpy as jnp
from jax import lax
from jax.experimental import pallas as pl
from jax.experimental.pallas import tpu as pltpu

RMS_EPS = 1e-6
LN_EPS = 1e-5
HEAD_DIM = 128
CHUNK = 64
CONV_TAPS = 3
HALO = 16
ADAM_LR = 0.001
ADAM_B1 = 0.9
ADAM_B2 = 0.999
ADAM_EPS = 1e-08
ADAM_WD = 0.01
ADAM_STEP = 10
V7X_VMEM_LIMIT = 56 * 1024 * 1024
N_CHIPS = 4
MESH = pl.DeviceIdType.MESH
F32 = jnp.float32
BF16 = jnp.bfloat16


def _tile(n, pref):
    if n <= pref:
        return n
    best = None
    for t in range(128, pref + 1, 128):
        if n % t == 0:
            best = t
    assert best is not None, (n, pref)
    return best


def _params(*sem):
    return pltpu.CompilerParams(dimension_semantics=sem if sem else None,
                                vmem_limit_bytes=V7X_VMEM_LIMIT)


def _gelu(x):
    c = math.sqrt(2.0 / math.pi)
    return 0.5 * x * (1.0 + jnp.tanh(c * (x + 0.044715 * x * x * x)))


def _gelu_and_grad(x):
    c = math.sqrt(2.0 / math.pi)
    x2 = x * x
    th = jnp.tanh(c * (x + 0.044715 * x * x2))
    val = 0.5 * x * (1.0 + th)
    grad = 0.5 * (1.0 + th) + 0.5 * x * (1.0 - th * th) * (c * (1.0 + 3.0 * 0.044715 * x2))
    return val, grad


def _sigmoid(x):
    return 1.0 / (1.0 + jnp.exp(-x))


def _nt(a, b):
    return lax.dot_general(a, b, (((1,), (1,)), ((), ())), preferred_element_type=F32)


def _tn(a, b):
    return lax.dot_general(a, b, (((0,), (0,)), ((), ())), preferred_element_type=F32)


def _rms_fwd(x, g, name):
    T, D = x.shape
    tr = _tile(T, 512)

    def body(x_ref, g_ref, h_ref):
        xv = x_ref[...]
        r = lax.rsqrt(jnp.mean(xv * xv, axis=-1, keepdims=True) + RMS_EPS)
        h_ref[...] = ((xv * r) * g_ref[...]).astype(h_ref.dtype)

    return pl.pallas_call(
        body, out_shape=jax.ShapeDtypeStruct((T, D), BF16), grid=(T // tr,),
        in_specs=[pl.BlockSpec((tr, D), lambda i: (i, 0)), pl.BlockSpec((1, D), lambda i: (0, 0))],
        out_specs=pl.BlockSpec((tr, D), lambda i: (i, 0)),
        name=name, compiler_params=_params("parallel"))(x, g)


def _rms_bwd(dh, x, g, dres, name):
    T, D = x.shape
    tr = _tile(T, 256)

    def body(dh_ref, x_ref, g_ref, dres_ref, dx_ref, dxb_ref, dg_ref):
        i = pl.program_id(0)
        xv = x_ref[...]
        dhv = dh_ref[...]
        r = lax.rsqrt(jnp.mean(xv * xv, axis=-1, keepdims=True) + RMS_EPS)
        xh = xv * r
        q = dhv * g_ref[...]
        dx = dres_ref[...] + r * (q - xh * jnp.mean(q * xh, axis=-1, keepdims=True))
        dx_ref[...] = dx
        dxb_ref[...] = dx.astype(BF16)
        part = jnp.sum(dhv * xh, axis=0, keepdims=True)

        @pl.when(i == 0)
        def _():
            dg_ref[...] = part

        @pl.when(i > 0)
        def _():
            dg_ref[...] += part

    row = pl.BlockSpec((tr, D), lambda i: (i, 0))
    vec = pl.BlockSpec((1, D), lambda i: (0, 0))
    return pl.pallas_call(
        body,
        out_shape=(jax.ShapeDtypeStruct((T, D), F32), jax.ShapeDtypeStruct((T, D), BF16),
                   jax.ShapeDtypeStruct((1, D), F32)),
        grid=(T // tr,), in_specs=[row, row, vec, row], out_specs=(row, row, vec),
        name=name, compiler_params=_params("arbitrary"))(dh, x, g, dres)


def _loss_head(x, g, tgt, name):
    T, D = x.shape
    tr = _tile(T, 256)

    def body(x_ref, g_ref, t_ref, loss_ref, dx_ref, dxb_ref, dg_ref):
        i = pl.program_id(0)
        xv = x_ref[...]
        gv = g_ref[...]
        r = lax.rsqrt(jnp.mean(xv * xv, axis=-1, keepdims=True) + RMS_EPS)
        xh = xv * r
        err = xh * gv - t_ref[...]
        lpart = jnp.full((1, 128), 0.5 * jnp.sum(jnp.mean(err * err, axis=-1, keepdims=True)), F32)
        dy = err * (1.0 / D)
        q = dy * gv
        dx = r * (q - xh * jnp.mean(q * xh, axis=-1, keepdims=True))
        dx_ref[...] = dx
        dxb_ref[...] = dx.astype(BF16)
        gpart = jnp.sum(dy * xh, axis=0, keepdims=True)

        @pl.when(i == 0)
        def _():
            loss_ref[...] = lpart
            dg_ref[...] = gpart

        @pl.when(i > 0)
        def _():
            loss_ref[...] += lpart
            dg_ref[...] += gpart

    row = pl.BlockSpec((tr, D), lambda i: (i, 0))
    vec = pl.BlockSpec((1, D), lambda i: (0, 0))
    return pl.pallas_call(
        body,
        out_shape=(jax.ShapeDtypeStruct((1, 128), F32), jax.ShapeDtypeStruct((T, D), F32),
                   jax.ShapeDtypeStruct((T, D), BF16), jax.ShapeDtypeStruct((1, D), F32)),
        grid=(T // tr,), in_specs=[row, vec, row],
        out_specs=(pl.BlockSpec((1, 128), lambda i: (0, 0)), row, row, vec),
        name=name, compiler_params=_params("arbitrary"))(x, g, tgt)


def _mm_nn(a, w, l, *, res=None, out_dtype, tm=1024, tn=1024, tk=None, name):
    M, K = a.shape
    N = w.shape[2]
    tm, tn = _tile(M, tm), _tile(N, tn)
    tk = K if tk is None else _tile(K, tk)
    nk = K // tk
    has_res = res is not None

    def body(*refs):
        a_ref, w_ref = refs[0], refs[1]
        r_ref = refs[2] if has_res else None
        o_ref = refs[2 + has_res]
        part = jnp.dot(a_ref[...], w_ref[...], preferred_element_type=F32)

        def finish(acc):
            if has_res:
                acc = r_ref[...] + acc
            o_ref[...] = acc.astype(o_ref.dtype)

        if nk == 1:
            finish(part)
        else:
            acc_ref = refs[3 + has_res]
            k = pl.program_id(2)

            @pl.when(k == 0)
            def _():
                acc_ref[...] = part

            @pl.when(jnp.logical_and(k > 0, k < nk - 1))
            def _():
                acc_ref[...] += part

            @pl.when(k == nk - 1)
            def _():
                finish(acc_ref[...] + part)

    in_specs = [pl.BlockSpec((tm, tk), lambda i, j, k: (i, k)),
                pl.BlockSpec((None, tk, tn), lambda i, j, k: (l, k, j))]
    args = [a, w]
    if has_res:
        in_specs.append(pl.BlockSpec((tm, tn), lambda i, j, k: (i, j)))
        args.append(res)
    return pl.pallas_call(
        body, out_shape=jax.ShapeDtypeStruct((M, N), out_dtype), grid=(M // tm, N // tn, nk),
        in_specs=in_specs, out_specs=pl.BlockSpec((tm, tn), lambda i, j, k: (i, j)),
        scratch_shapes=[pltpu.VMEM((tm, tn), F32)] if nk > 1 else [],
        name=name, compiler_params=_params("parallel", "parallel", "arbitrary"))(*args)


def _mm_swiglu(h, wgu, l, *, tm=1024, tn=512, name):
    T, D = h.shape
    F = wgu.shape[2] // 2
    tm, tn = _tile(T, tm), _tile(F, tn)
    nf = F // tn

    def body(h_ref, wg_ref, wu_ref, gu_ref, act_ref):
        hv = h_ref[...]
        g = jnp.dot(hv, wg_ref[...], preferred_element_type=F32)
        u = jnp.dot(hv, wu_ref[...], preferred_element_type=F32)
        act_ref[...] = ((g * _sigmoid(g)) * u).astype(BF16)
        gu_ref[0] = g.astype(BF16)
        gu_ref[1] = u.astype(BF16)

    return pl.pallas_call(
        body,
        out_shape=(jax.ShapeDtypeStruct((2, T, F), BF16), jax.ShapeDtypeStruct((T, F), BF16)),
        grid=(T // tm, nf),
        in_specs=[pl.BlockSpec((tm, D), lambda i, j: (i, 0)),
                  pl.BlockSpec((None, D, tn), lambda i, j: (l, 0, j)),
                  pl.BlockSpec((None, D, tn), lambda i, j: (l, 0, j + nf))],
        out_specs=(pl.BlockSpec((2, tm, tn), lambda i, j: (0, i, j)),
                   pl.BlockSpec((tm, tn), lambda i, j: (i, j))),
        name=name, compiler_params=_params("parallel", "parallel"))(h, wgu, wgu)


def _mm_nt(a, w, l, *, out_dtype, tm=1024, tn=1024, tk=None, name):
    M, K = a.shape
    N = w.shape[1]
    tm, tn = _tile(M, tm), _tile(N, tn)
    tk = K if tk is None else _tile(K, tk)
    nk = K // tk

    def body(*refs):
        a_ref, w_ref, o_ref = refs[0], refs[1], refs[2]
        part = _nt(a_ref[...], w_ref[...])
        if nk == 1:
            o_ref[...] = part.astype(o_ref.dtype)
        else:
            acc_ref = refs[3]
            k = pl.program_id(2)

            @pl.when(k == 0)
            def _():
                acc_ref[...] = part

            @pl.when(jnp.logical_and(k > 0, k < nk - 1))
            def _():
                acc_ref[...] += part

            @pl.when(k == nk - 1)
            def _():
                o_ref[...] = (acc_ref[...] + part).astype(o_ref.dtype)

    return pl.pallas_call(
        body, out_shape=jax.ShapeDtypeStruct((M, N), out_dtype), grid=(M // tm, N // tn, nk),
        in_specs=[pl.BlockSpec((tm, tk), lambda i, j, k: (i, k)),
                  pl.BlockSpec((None, tn, tk), lambda i, j, k: (l, j, k))],
        out_specs=pl.BlockSpec((tm, tn), lambda i, j, k: (i, j)),
        scratch_shapes=[pltpu.VMEM((tm, tn), F32)] if nk > 1 else [],
        name=name, compiler_params=_params("parallel", "parallel", "arbitrary"))(a, w)


def _mm_nt_swiglu_bwd(dxb, wdown, l, gu, *, tm=1024, tn=512, name):
    T, D = dxb.shape
    F = wdown.shape[1]
    tm, tn = _tile(T, tm), _tile(F, tn)

    def body(dx_ref, w_ref, gu_ref, dgu_ref):
        da = _nt(dx_ref[...], w_ref[...])
        g = gu_ref[0].astype(F32)
        u = gu_ref[1].astype(F32)
        s = _sigmoid(g)
        dgu_ref[0] = ((da * u) * (s * (1.0 + g * (1.0 - s)))).astype(BF16)
        dgu_ref[1] = (da * (g * s)).astype(BF16)

    blk3 = pl.BlockSpec((2, tm, tn), lambda i, j: (0, i, j))
    return pl.pallas_call(
        body, out_shape=jax.ShapeDtypeStruct((2, T, F), BF16), grid=(T // tm, F // tn),
        in_specs=[pl.BlockSpec((tm, D), lambda i, j: (i, 0)),
                  pl.BlockSpec((None, tn, D), lambda i, j: (l, j, 0)), blk3],
        out_specs=blk3, name=name, compiler_params=_params("parallel", "parallel"))(dxb, wdown, gu)


def _mm_nt_dgu(dgu, wgu, l, *, tm=1024, tn=1024, tk=1408, name):
    _, T, F = dgu.shape
    D = wgu.shape[1]
    tm, tn, tk = _tile(T, tm), _tile(D, tn), _tile(F, tk)
    nkf = F // tk
    nk = 2 * nkf

    def body(a_ref, w_ref, o_ref, acc_ref):
        k = pl.program_id(2)
        part = _nt(a_ref[...], w_ref[...])

        @pl.when(k == 0)
        def _():
            acc_ref[...] = part

        @pl.when(jnp.logical_and(k > 0, k < nk - 1))
        def _():
            acc_ref[...] += part

        @pl.when(k == nk - 1)
        def _():
            o_ref[...] = acc_ref[...] + part

    return pl.pallas_call(
        body, out_shape=jax.ShapeDtypeStruct((T, D), F32), grid=(T // tm, D // tn, nk),
        in_specs=[pl.BlockSpec((None, tm, tk), lambda i, j, k: (k // nkf, i, k % nkf)),
                  pl.BlockSpec((None, tn, tk), lambda i, j, k: (l, j, k))],
        out_specs=pl.BlockSpec((tm, tn), lambda i, j, k: (i, j)),
        scratch_shapes=[pltpu.VMEM((tm, tn), F32)],
        name=name, compiler_params=_params("parallel", "parallel", "arbitrary"))(dgu, wgu)


def _mm_tn(a, g, l, n_layers, prev, *, tkw=512, tnw=1024, name):
    T, Kw = a.shape
    pair = g.ndim == 3
    Nw = 2 * g.shape[2] if pair else g.shape[1]
    tkw = _tile(Kw, tkw)
    tnw = _tile(g.shape[2] if pair else Nw, tnw)
    nf = (Nw // 2) // tnw if pair else 0

    def body(*refs):
        a_ref, g_ref, o_ref = refs[0], refs[1], refs[-1]
        o_ref[...] = _tn(a_ref[...], g_ref[...]).astype(o_ref.dtype)

    if pair:
        g_spec = pl.BlockSpec((None, T, tnw), lambda i, j: (j // nf, 0, j % nf))
    else:
        g_spec = pl.BlockSpec((T, tnw), lambda i, j: (0, j))
    in_specs = [pl.BlockSpec((T, tkw), lambda i, j: (0, i)), g_spec]
    args = [a, g]
    aliases = {}
    if prev is not None:
        in_specs.append(pl.BlockSpec(memory_space=pl.ANY))
        args.append(prev)
        aliases = {2: 0}
    return pl.pallas_call(
        body, out_shape=jax.ShapeDtypeStruct((n_layers, Kw, Nw), BF16), grid=(Kw // tkw, Nw // tnw),
        in_specs=in_specs, out_specs=pl.BlockSpec((None, tkw, tnw), lambda i, j: (l, i, j)),
        input_output_aliases=aliases,
        name=name, compiler_params=_params("parallel", "parallel"))(*args)


def _mixer_specs(T, A, tr):
    nin = 5 * A
    nb = tr // HALO
    last = T // HALO - 1
    prev = pl.BlockSpec((HALO, nin), lambda i: (jnp.maximum(i * nb - 1, 0), 0))
    cur = pl.BlockSpec((tr, nin), lambda i: (i, 0))
    nxt = pl.BlockSpec((HALO, nin), lambda i: (jnp.minimum((i + 1) * nb, last), 0))
    return prev, cur, nxt


def _conv(p_ext, cw_ref):
    return (cw_ref[2:3, :] * p_ext + cw_ref[1:2, :] * pltpu.roll(p_ext, 1, 0)
            + cw_ref[0:1, :] * pltpu.roll(p_ext, 2, 0))


def _mixer_fwd(z, lng, lnb, wm, bfull, cw, gn, name):
    T, nin = z.shape
    A = nin // 5
    H = A // HEAD_DIM
    tr = _tile(T, 256)
    nblk = tr // HEAD_DIM

    def body(zp_ref, z_ref, lng_ref, lnb_ref, wm_ref, bf_ref, cw_ref, gn_ref, y_ref, vn_ref, mix_ref):
        i = pl.program_id(0)
        u = _gelu(z_ref[:, 0:A].astype(F32))
        vg = _gelu(z_ref[:, A:2 * A].astype(F32))
        xc = vg - jnp.mean(vg, axis=-1, keepdims=True)
        rstd = lax.rsqrt(jnp.mean(xc * xc, axis=-1, keepdims=True) + LN_EPS)
        vn_ref[...] = ((xc * rstd) * lng_ref[...] + lnb_ref[...]).astype(BF16)
        for cb in range(nblk):
            rows = slice(cb * HEAD_DIM, (cb + 1) * HEAD_DIM)
            for h in range(H):
                cols = slice(h * HEAD_DIM, (h + 1) * HEAD_DIM)
                mix_ref[rows, cols] = jnp.dot(wm_ref[h], vn_ref[rows, cols],
                                              preferred_element_type=F32) + bf_ref[h]
        ya = u * mix_ref[...]
        ra = lax.rsqrt(jnp.mean(ya * ya, axis=-1, keepdims=True) + RMS_EPS)
        y_ref[:, 0:A] = ((ya * ra) * gn_ref[:, 0:A]).astype(BF16)

        p_prev = zp_ref[:, 3 * A:4 * A].astype(F32) * zp_ref[:, 4 * A:5 * A].astype(F32)
        p_prev = jnp.where(i > 0, p_prev, 0.0)
        p_cur = z_ref[:, 3 * A:4 * A].astype(F32) * z_ref[:, 4 * A:5 * A].astype(F32)
        cv = _conv(jnp.concatenate([p_prev, p_cur], axis=0), cw_ref)[HALO:]
        yb = z_ref[:, 2 * A:3 * A].astype(F32) * cv
        rb = lax.rsqrt(jnp.mean(yb * yb, axis=-1, keepdims=True) + RMS_EPS)
        y_ref[:, A:2 * A] = ((yb * rb) * gn_ref[:, A:2 * A]).astype(BF16)

    prev, cur, _ = _mixer_specs(T, A, tr)
    full = lambda shape: pl.BlockSpec(shape, lambda i: (0,) * len(shape))
    return pl.pallas_call(
        body, out_shape=jax.ShapeDtypeStruct((T, 2 * A), BF16), grid=(T // tr,),
        in_specs=[prev, cur, full((1, A)), full((1, A)), full((H, HEAD_DIM, HEAD_DIM)),
                  full((H, HEAD_DIM, HEAD_DIM)), full((8, A)), full((1, 2 * A))],
        out_specs=pl.BlockSpec((tr, 2 * A), lambda i: (i, 0)),
        scratch_shapes=[pltpu.VMEM((tr, A), BF16), pltpu.VMEM((tr, A), F32)],
        name=name, compiler_params=_params("parallel"))(z, z, lng, lnb, wm, bfull, cw, gn)


def _mixer_bwd(z, dy, lng, lnb, wm, wmt, bfull, cw, gn, name):
    T, nin = z.shape
    A = nin // 5
    H = A // HEAD_DIM
    tr = _tile(T, 256)
    nblk = tr // HEAD_DIM
    ngrid = T // tr
    next_ = tr + 2 * HALO

    def body(zp_ref, z_ref, zn_ref, dy_ref, dyn_ref, lng_ref, lnb_ref, wm_ref, wmt_ref, bf_ref, cw_ref, gn_ref,
             dz_ref, dgn_ref, dlng_ref, dlnb_ref, dws_ref, dbs_ref, dcw_ref,
             vn_ref, mix_ref, dmix_ref, dvn_ref):
        i = pl.program_id(0)

        @pl.when(i == 0)
        def _():
            dgn_ref[...] = jnp.zeros_like(dgn_ref)
            dlng_ref[...] = jnp.zeros_like(dlng_ref)
            dlnb_ref[...] = jnp.zeros_like(dlnb_ref)
            dws_ref[...] = jnp.zeros_like(dws_ref)
            dbs_ref[...] = jnp.zeros_like(dbs_ref)
            dcw_ref[...] = jnp.zeros_like(dcw_ref)

        u, du_dz = _gelu_and_grad(z_ref[:, 0:A].astype(F32))
        vg, dv_dz = _gelu_and_grad(z_ref[:, A:2 * A].astype(F32))
        xc = vg - jnp.mean(vg, axis=-1, keepdims=True)
        rstd = lax.rsqrt(jnp.mean(xc * xc, axis=-1, keepdims=True) + LN_EPS)
        vhat = xc * rstd
        vn_ref[...] = (vhat * lng_ref[...] + lnb_ref[...]).astype(BF16)
        for cb in range(nblk):
            rows = slice(cb * HEAD_DIM, (cb + 1) * HEAD_DIM)
            for h in range(H):
                cols = slice(h * HEAD_DIM, (h + 1) * HEAD_DIM)
                mix_ref[rows, cols] = jnp.dot(wm_ref[h], vn_ref[rows, cols],
                                              preferred_element_type=F32) + bf_ref[h]
        mixed = mix_ref[...]
        ya = u * mixed
        ra = lax.rsqrt(jnp.mean(ya * ya, axis=-1, keepdims=True) + RMS_EPS)
        yha = ya * ra
        dyan = dy_ref[:, 0:A]
        dgn_ref[:, 0:A] += jnp.sum(dyan * yha, axis=0, keepdims=True)
        qa = dyan * gn_ref[:, 0:A]
        dya = ra * (qa - yha * jnp.mean(qa * yha, axis=-1, keepdims=True))
        dz_ref[:, 0:A] = ((dya * mixed) * du_dz).astype(BF16)
        dmix_ref[...] = (dya * u).astype(BF16)

        ii = lax.broadcasted_iota(jnp.int32, (HEAD_DIM, HEAD_DIM), 0)
        jj = lax.broadcasted_iota(jnp.int32, (HEAD_DIM, HEAD_DIM), 1)
        mask = (jj // CHUNK <= ii // CHUNK).astype(F32)
        ones = jnp.ones((8, HEAD_DIM), BF16)
        for h in range(H):
            cols = slice(h * HEAD_DIM, (h + 1) * HEAD_DIM)
            dws = jnp.zeros((HEAD_DIM, HEAD_DIM), F32)
            dbs = jnp.zeros((8, HEAD_DIM), F32)
            for cb in range(nblk):
                rows = slice(cb * HEAD_DIM, (cb + 1) * HEAD_DIM)
                dm = dmix_ref[rows, cols]
                dws = dws + _nt(dm, vn_ref[rows, cols])
                dbs = dbs + _nt(ones, dm)
                dvn_ref[rows, cols] = jnp.dot(wmt_ref[h], dm, preferred_element_type=F32)
            dws_ref[h] += dws * mask
            dbs_ref[h] += dbs
        dvn = dvn_ref[...]
        dlnb_ref[...] += jnp.sum(dvn, axis=0, keepdims=True)
        dlng_ref[...] += jnp.sum(dvn * vhat, axis=0, keepdims=True)
        dvh = dvn * lng_ref[...]
        dvg = rstd * (dvh - jnp.mean(dvh, axis=-1, keepdims=True)
                      - vhat * jnp.mean(dvh * vhat, axis=-1, keepdims=True))
        dz_ref[:, A:2 * A] = (dvg * dv_dz).astype(BF16)

        def ext(lo):
            mid = z_ref[:, lo:lo + A].astype(F32)
            return jnp.concatenate([zp_ref[:, lo:lo + A].astype(F32), mid, zn_ref[:, lo:lo + A].astype(F32)], axis=0)

        zb, zc, zh = ext(2 * A), ext(3 * A), ext(4 * A)
        row = lax.broadcasted_iota(jnp.int32, (next_, 1), 0)
        p = zc * zh
        p = jnp.where(jnp.logical_and(row < HALO, i == 0), 0.0, p)
        cv = _conv(p, cw_ref)
        yb = zb * cv
        rb = lax.rsqrt(jnp.mean(yb * yb, axis=-1, keepdims=True) + RMS_EPS)
        yhb = yb * rb
        dyn_rows = jnp.where(i < ngrid - 1, dyn_ref[:, A:2 * A], 0.0)
        dybn = jnp.concatenate([jnp.zeros((HALO, A), F32), dy_ref[:, A:2 * A], dyn_rows], axis=0)
        ctr = slice(HALO, HALO + tr)
        dgn_ref[:, A:2 * A] += jnp.sum((dybn * yhb)[ctr], axis=0, keepdims=True)
        qb = dybn * gn_ref[:, A:2 * A]
        dyb = rb * (qb - yhb * jnp.mean(qb * yhb, axis=-1, keepdims=True))
        dcv = dyb * zb
        dp = (cw_ref[2:3, :] * dcv + cw_ref[1:2, :] * pltpu.roll(dcv, next_ - 1, 0)
              + cw_ref[0:1, :] * pltpu.roll(dcv, next_ - 2, 0))
        dz_ref[:, 2 * A:3 * A] = (dyb * cv)[ctr].astype(BF16)
        dz_ref[:, 3 * A:4 * A] = (dp * zh)[ctr].astype(BF16)
        dz_ref[:, 4 * A:5 * A] = (dp * zc)[ctr].astype(BF16)
        dcw_ref[2:3, :] += jnp.sum((dcv * p)[ctr], axis=0, keepdims=True)
        dcw_ref[1:2, :] += jnp.sum((dcv * pltpu.roll(p, 1, 0))[ctr], axis=0, keepdims=True)
        dcw_ref[0:1, :] += jnp.sum((dcv * pltpu.roll(p, 2, 0))[ctr], axis=0, keepdims=True)

    prev, cur, nxt = _mixer_specs(T, A, tr)
    nb = tr // HALO
    dy_cur = pl.BlockSpec((tr, 2 * A), lambda i: (i, 0))
    dy_nxt = pl.BlockSpec((HALO, 2 * A), lambda i: (jnp.minimum((i + 1) * nb, T // HALO - 1), 0))
    full = lambda shape: pl.BlockSpec(shape, lambda i: (0,) * len(shape))
    hh = (H, HEAD_DIM, HEAD_DIM)
    return pl.pallas_call(
        body,
        out_shape=(jax.ShapeDtypeStruct((T, nin), BF16), jax.ShapeDtypeStruct((1, 2 * A), F32),
                   jax.ShapeDtypeStruct((1, A), F32), jax.ShapeDtypeStruct((1, A), F32),
                   jax.ShapeDtypeStruct(hh, F32), jax.ShapeDtypeStruct((H, 8, HEAD_DIM), F32),
                   jax.ShapeDtypeStruct((8, A), F32)),
        grid=(ngrid,),
        in_specs=[prev, cur, nxt, dy_cur, dy_nxt, full((1, A)), full((1, A)), full(hh), full(hh), full(hh),
                  full((8, A)), full((1, 2 * A))],
        out_specs=(pl.BlockSpec((tr, nin), lambda i: (i, 0)), full((1, 2 * A)), full((1, A)), full((1, A)),
                   full(hh), full((H, 8, HEAD_DIM)), full((8, A))),
        scratch_shapes=[pltpu.VMEM((tr, A), BF16), pltpu.VMEM((tr, A), F32), pltpu.VMEM((tr, A), BF16),
                        pltpu.VMEM((tr, A), F32)],
        name=name, compiler_params=_params("arbitrary"))(z, z, z, dy, dy, lng, lnb, wm, wmt, bfull, cw, gn)


def _blocks3(shape, budget_elems=256 * 1024):
    _, R, C = shape
    tr = R
    while tr * C > budget_elems and tr % 2 == 0 and (tr // 2) % 16 == 0:
        tr //= 2
    return tr


def _add2_bf16(a, b, name):
    L, R, C = a.shape
    tr = _blocks3(a.shape, 512 * 1024)

    def body(a_ref, b_ref, o_ref):
        o_ref[...] = (a_ref[...].astype(F32) + b_ref[...].astype(F32)).astype(BF16)

    blk = pl.BlockSpec((None, tr, C), lambda l, i: (l, i, 0))
    return pl.pallas_call(
        body, out_shape=jax.ShapeDtypeStruct(a.shape, BF16), grid=(L, R // tr),
        in_specs=[blk, blk], out_specs=blk, name=name, compiler_params=_params("parallel", "parallel"))(a, b)


def _sum4_f32(own, recv, name):
    L, R, C = own.shape
    tr = _blocks3(own.shape, 512 * 1024)

    def body(o_ref, r_ref, out_ref):
        acc = o_ref[...].astype(F32)
        for k in range(N_CHIPS - 1):
            acc = acc + r_ref[k].astype(F32)
        out_ref[...] = acc

    blk = pl.BlockSpec((None, tr, C), lambda l, i: (l, i, 0))
    return pl.pallas_call(
        body, out_shape=jax.ShapeDtypeStruct(own.shape, F32), grid=(L, R // tr),
        in_specs=[blk, pl.BlockSpec((N_CHIPS - 1, None, tr, C), lambda l, i: (0, l, i, 0))],
        out_specs=blk, name=name, compiler_params=_params("parallel", "parallel"))(own, recv)


def _adamw_math(w, g, m, v):
    m = ADAM_B1 * m + (1.0 - ADAM_B1) * g
    v = ADAM_B2 * v + (1.0 - ADAM_B2) * (g * g)
    m_hat = m / (1.0 - ADAM_B1 ** ADAM_STEP)
    v_hat = v / (1.0 - ADAM_B2 ** ADAM_STEP)
    delta = -ADAM_LR * (m_hat / (jnp.sqrt(v_hat) + ADAM_EPS) + ADAM_WD * w)
    return delta, m, v


def _adamw(w, g, m, v, name):
    L, R, C = w.shape
    tr = _blocks3(w.shape)

    def body(w_ref, g_ref, m_ref, v_ref, go_ref, d_ref, mo_ref, vo_ref):
        gv = g_ref[...]
        d, mn, vn = _adamw_math(w_ref[...], gv, m_ref[...], v_ref[...])
        go_ref[...] = gv
        d_ref[...] = d
        mo_ref[...] = mn
        vo_ref[...] = vn

    blk = pl.BlockSpec((None, tr, C), lambda l, i: (l, i, 0))
    sds = jax.ShapeDtypeStruct(w.shape, F32)
    return pl.pallas_call(
        body, out_shape=(sds, sds, sds, sds), grid=(L, R // tr), in_specs=[blk] * 4, out_specs=(blk,) * 4,
        name=name, compiler_params=_params("parallel", "parallel"))(w, g, m, v)


def _adamw_flat(w, g, m, v, name):
    R, C = w.shape
    tr = _tile(R, 1024) if R % 128 == 0 else R

    def body(w_ref, g_ref, m_ref, v_ref, d_ref, mo_ref, vo_ref):
        d, mn, vn = _adamw_math(w_ref[...], g_ref[...], m_ref[...], v_ref[...])
        d_ref[...] = d
        mo_ref[...] = mn
        vo_ref[...] = vn

    blk = pl.BlockSpec((tr, C), lambda i: (i, 0))
    sds = jax.ShapeDtypeStruct(w.shape, F32)
    return pl.pallas_call(
        body, out_shape=(sds, sds, sds), grid=(R // tr,), in_specs=[blk] * 4, out_specs=(blk,) * 3,
        name=name, compiler_params=_params("parallel"))(w, g, m, v)


class _Geom:
    def __init__(self, rows, cols, axis, size, base=0):
        self.rows, self.cols, self.axis, self.size, self.base = rows, cols, axis, size, base

    def in_full(self, j, h):
        if self.axis == 1:
            r = (0, self.rows) if h is None else (h * (self.rows // 2), self.rows // 2)
            return r, (self.base + j * self.size, self.size)
        c = (0, self.cols) if h is None else (h * (self.cols // 2), self.cols // 2)
        return (self.base + j * self.size, self.size), c

    def in_shard(self, h):
        if self.axis == 1:
            return (h * (self.rows // 2), self.rows // 2), (0, self.size)
        return (0, self.size), (h * (self.cols // 2), self.cols // 2)

    def half_of_full(self, h):
        if self.axis == 1:
            return (h * (self.rows // 2), self.rows // 2), (0, self.cols)
        return (0, self.rows), (h * (self.cols // 2), self.cols // 2)

    def in_half(self, j):
        if self.axis == 1:
            return (0, self.rows // 2), (self.base + j * self.size, self.size)
        return (self.base + j * self.size, self.size), (0, self.cols // 2)

    def half_shape(self, L):
        return (L, self.rows // 2, self.cols) if self.axis == 1 else (L, self.rows, self.cols // 2)

    def shard_half_shape(self, L):
        return (L, self.rows // 2, self.size) if self.axis == 1 else (L, self.size, self.cols // 2)

    def shard_shape(self, L):
        return (L, self.rows, self.size) if self.axis == 1 else (L, self.size, self.cols)


def _at(ref, region):
    (r0, rn), (c0, cn) = region
    if not isinstance(r0, int):
        r0 = pl.multiple_of(r0, 128)
    if not isinstance(c0, int):
        c0 = pl.multiple_of(c0, 128)
    return ref.at[:, pl.ds(r0, rn), pl.ds(c0, cn)]


def _mesh_place():
    x, y, c = lax.axis_index("x"), lax.axis_index("y"), lax.axis_index("c")
    chips = [(1 - x, y), (x, 1 - y), (1 - x, 1 - y)]
    return x, y, c, 2 * x + y, chips


def _remote(src, dst, ssem, rsem, dev):
    return pltpu.make_async_remote_copy(src_ref=src, dst_ref=dst, send_sem=ssem, recv_sem=rsem,
                                        device_id=dev, device_id_type=MESH)


ANY = pl.BlockSpec(memory_space=pl.ANY)


def _gather_weights(shards, conv, geoms, L):
    names = ["in", "out", "gate", "up", "down"]
    full_of = {"in": 0, "out": 1, "gate": 2, "up": 2, "down": 3}
    g_in, g_out, g_gate, g_down = geoms["in"], geoms["out"], geoms["gate"], geoms["down"]
    full_shapes = [(L, g_in.rows, g_in.cols), (L, g_out.rows, g_out.cols),
                   (L, g_gate.rows, g_gate.cols), (L, g_down.rows, g_down.cols)]
    cb = conv.shape[2]
    nt = len(names)
    nk = N_CHIPS - 1

    def body(s_in, s_out, s_gate, s_up, s_down, s_conv, f_in, f_out, f_gu, f_down, f_conv,
             send_sems, recv_sems, fsend_sems, frecv_sems, local_sems):
        x, y, c, j, chips = _mesh_place()
        sibling = (x, y, 1 - c)
        srcs = [s_in, s_out, s_gate, s_up, s_down]
        fulls = [f_in, f_out, f_gu, f_down]
        sends, locals_ = [], []
        for t, nm in enumerate(names):
            geo, full = geoms[nm], fulls[full_of[nm]]
            lc = pltpu.make_async_copy(srcs[t], _at(full, geo.in_full(j, None)), local_sems.at[t])
            lc.start()
            locals_.append(lc)
            for k, chip in enumerate(chips):
                cp = _remote(_at(srcs[t], geo.in_shard(c)), _at(full, geo.in_full(j, c)),
                             send_sems.at[nk * t + k], recv_sems.at[nk * t + k], (*chip, c))
                cp.start()
                sends.append(cp)
        conv_dst = lambda jj: f_conv.at[:, :, pl.ds(pl.multiple_of(jj * cb, 128), cb)]
        lc = pltpu.make_async_copy(s_conv, conv_dst(j), local_sems.at[nt])
        lc.start()
        locals_.append(lc)
        for k, chip in enumerate(chips):
            cp = _remote(s_conv, conv_dst(j), send_sems.at[nk * nt + k], recv_sems.at[nk * nt + k], (*chip, c))
            cp.start()
            sends.append(cp)
        for k, chip in enumerate(chips):
            jk = 2 * chip[0] + chip[1]
            for t, nm in enumerate(names):
                geo, full = geoms[nm], fulls[full_of[nm]]
                reg = _at(full, geo.in_full(jk, c))
                _remote(reg, reg, send_sems.at[nk * t + k], recv_sems.at[nk * t + k], sibling).wait_recv()
                fw = _remote(reg, reg, fsend_sems.at[nk * t + k], frecv_sems.at[nk * t + k], sibling)
                fw.start()
                sends.append(fw)
        for k, chip in enumerate(chips):
            jk = 2 * chip[0] + chip[1]
            for t, nm in enumerate(names):
                geo, full = geoms[nm], fulls[full_of[nm]]
                reg = _at(full, geo.in_full(jk, 1 - c))
                _remote(reg, reg, fsend_sems.at[nk * t + k], frecv_sems.at[nk * t + k], sibling).wait_recv()
            _remote(conv_dst(jk), conv_dst(jk), send_sems.at[nk * nt + k], recv_sems.at[nk * nt + k],
                    sibling).wait_recv()
        for cp in sends:
            cp.wait_send()
        for lc in locals_:
            lc.wait()

    out_shape = [jax.ShapeDtypeStruct(s, BF16) for s in full_shapes]
    out_shape.append(jax.ShapeDtypeStruct((L, 8, N_CHIPS * cb), F32))
    return pl.pallas_call(
        body, out_shape=tuple(out_shape), in_specs=[ANY] * 6, out_specs=(ANY,) * 5,
        scratch_shapes=[pltpu.SemaphoreType.DMA((nk * (nt + 1),)), pltpu.SemaphoreType.DMA((nk * (nt + 1),)),
                        pltpu.SemaphoreType.DMA((nk * nt,)), pltpu.SemaphoreType.DMA((nk * nt,)),
                        pltpu.SemaphoreType.DMA((nt + 1,))],
        name="gather_weights")(*shards, conv)


def _swap_halves(dws, geos, L):
    n = len(dws)

    def body(*refs):
        ins, owns, recvs = refs[:n], refs[n:2 * n], refs[2 * n:3 * n]
        send_sems, recv_sems, local_sems = refs[3 * n:]
        x, y, c, _, _ = _mesh_place()
        sibling = (x, y, 1 - c)
        cps = []
        for t in range(n):
            lc = pltpu.make_async_copy(_at(ins[t], geos[t].half_of_full(c)), owns[t], local_sems.at[t])
            lc.start()
            cp = _remote(_at(ins[t], geos[t].half_of_full(1 - c)), recvs[t], send_sems.at[t], recv_sems.at[t],
                         sibling)
            cp.start()
            cps.append((lc, cp))
        for lc, cp in cps:
            cp.wait()
            lc.wait()

    halves = tuple(jax.ShapeDtypeStruct(g.half_shape(L), BF16) for g in geos)
    res = pl.pallas_call(
        body, out_shape=halves + halves, in_specs=[ANY] * n, out_specs=(ANY,) * (2 * n),
        scratch_shapes=[pltpu.SemaphoreType.DMA((n,)), pltpu.SemaphoreType.DMA((n,)),
                        pltpu.SemaphoreType.DMA((n,))],
        name="grad_swap_halves")(*dws)
    return res[:n], res[n:]


def _exchange_shards(ps, geos, L):
    n = len(geos)
    nk = N_CHIPS - 1

    def body(*refs):
        ins, owns, recvs = refs[:n], refs[n:2 * n], refs[2 * n:3 * n]
        send_sems, recv_sems, local_sems = refs[3 * n:]
        x, y, c, j, chips = _mesh_place()
        cps, lcs = [], []
        for t in range(n):
            lc = pltpu.make_async_copy(_at(ins[t], geos[t].in_half(j)), owns[t], local_sems.at[t])
            lc.start()
            lcs.append(lc)
            for k, chip in enumerate(chips):
                jk = 2 * chip[0] + chip[1]
                cp = _remote(_at(ins[t], geos[t].in_half(jk)), recvs[t].at[k], send_sems.at[nk * t + k],
                             recv_sems.at[nk * t + k], (*chip, c))
                cp.start()
                cps.append(cp)
        for cp in cps:
            cp.wait()
        for lc in lcs:
            lc.wait()

    owns = tuple(jax.ShapeDtypeStruct(g.shard_half_shape(L), BF16) for g in geos)
    recvs = tuple(jax.ShapeDtypeStruct((nk,) + g.shard_half_shape(L), BF16) for g in geos)
    res = pl.pallas_call(
        body, out_shape=owns + recvs, in_specs=[ANY] * n, out_specs=(ANY,) * (2 * n),
        scratch_shapes=[pltpu.SemaphoreType.DMA((nk * n,)), pltpu.SemaphoreType.DMA((nk * n,)),
                        pltpu.SemaphoreType.DMA((n,))],
        name="grad_exchange_shards")(*ps)
    return res[:n], res[n:]


def _join_halves(ghs, geos, L):
    n = len(geos)

    def body(*refs):
        ins, outs = refs[:n], refs[n:2 * n]
        send_sems, recv_sems, local_sems = refs[2 * n:]
        x, y, c, _, _ = _mesh_place()
        sibling = (x, y, 1 - c)
        cps = []
        for t in range(n):
            mine = _at(outs[t], geos[t].in_shard(c))
            lc = pltpu.make_async_copy(ins[t], mine, local_sems.at[t])
            lc.start()
            cp = _remote(ins[t], mine, send_sems.at[t], recv_sems.at[t], sibling)
            cp.start()
            cps.append((lc, cp))
        for t, (lc, cp) in enumerate(cps):
            cp.wait_send()
            other = _at(outs[t], geos[t].in_shard(1 - c))
            _remote(other, other, send_sems.at[t], recv_sems.at[t], sibling).wait_recv()
            lc.wait()

    outs = tuple(jax.ShapeDtypeStruct(g.shard_shape(L), F32) for g in geos)
    return pl.pallas_call(
        body, out_shape=outs, in_specs=[ANY] * n, out_specs=(ANY,) * n,
        scratch_shapes=[pltpu.SemaphoreType.DMA((n,)), pltpu.SemaphoreType.DMA((n,)),
                        pltpu.SemaphoreType.DMA((n,))],
        name="grad_join_halves")(*ghs)


def _allreduce_small(s):
    R, C = s.shape

    def body(s_ref, o_ref, rbuf, send_sems, recv_sems):
        x, y, c = lax.axis_index("x"), lax.axis_index("y"), lax.axis_index("c")
        peers = [(x, y, 1 - c), (1 - x, y, c), (x, 1 - y, c)]
        o_ref[...] = s_ref[...]
        for k, peer in enumerate(peers):
            cp = _remote(o_ref, rbuf.at[k], send_sems.at[k], recv_sems.at[k], peer)
            cp.start()
            cp.wait()
            o_ref[...] = o_ref[...] + rbuf[k]

    vm = pl.BlockSpec(memory_space=pltpu.VMEM)
    return pl.pallas_call(
        body, out_shape=jax.ShapeDtypeStruct((R, C), F32), in_specs=[vm], out_specs=vm,
        scratch_shapes=[pltpu.VMEM((3, R, C), F32), pltpu.SemaphoreType.DMA((3,)), pltpu.SemaphoreType.DMA((3,))],
        name="allreduce_small", compiler_params=pltpu.CompilerParams(vmem_limit_bytes=V7X_VMEM_LIMIT))(s)


def _pack(pieces):
    rows = []
    for p in pieces:
        flat = p.reshape(-1)
        pad = (-flat.shape[0]) % 1024
        rows.append(jnp.pad(flat, (0, pad)).reshape(-1, 128))
    return jnp.concatenate(rows, axis=0)


def _unpack(buf, shapes):
    out, r = [], 0
    for shp in shapes:
        n = math.prod(shp)
        nr = -(-n // 1024) * 8
        out.append(buf[r:r + nr].reshape(-1)[:n].reshape(shp))
        r += nr
    return out


def kernel(x, norm1_g, w_in, gmlp_ln_g, gmlp_ln_b, w_spatial, b_spatial, conv_w, group_norm_g, w_out, norm2_g, w_gate, w_up, w_down, final_norm_g, loss_target, m_norm1_g, m_w_in, m_gmlp_ln_g, m_gmlp_ln_b, m_w_spatial, m_b_spatial, m_conv_w, m_group_norm_g, m_w_out, m_norm2_g, m_w_gate, m_w_up, m_w_down, m_final_norm_g, v_norm1_g, v_w_in, v_gmlp_ln_g, v_gmlp_ln_b, v_w_spatial, v_b_spatial, v_conv_w, v_group_norm_g, v_w_out, v_norm2_g, v_w_gate, v_w_up, v_w_down, v_final_norm_g):
    L, D, n_in = w_in.shape
    T = x.shape[1]
    nin = N_CHIPS * n_in
    A = nin // 5
    H = A // HEAD_DIM
    n_f = w_gate.shape[2]
    F = N_CHIPS * n_f
    n_o = w_out.shape[1]
    cb = conv_w.shape[2]
    assert A == H * HEAD_DIM and T % 256 == 0 and N_CHIPS * n_o == D and N_CHIPS * cb == A

    geoms = {
        "in": _Geom(D, nin, 1, n_in),
        "out": _Geom(D, D, 0, n_o),
        "gate": _Geom(D, 2 * F, 1, n_f, 0),
        "up": _Geom(D, 2 * F, 1, n_f, F),
        "down": _Geom(F, D, 0, n_f),
    }

    shards = [w.astype(BF16) for w in (w_in, w_out, w_gate, w_up, w_down)]
    conv_pad = jnp.pad(conv_w, ((0, 0), (0, 8 - CONV_TAPS), (0, 0)))
    W_in, W_out, W_gu, W_down, conv_full = _gather_weights(shards, conv_pad, geoms, L)

    pos = jnp.arange(HEAD_DIM)
    mask = (pos[None, :] // CHUNK) <= (pos[:, None] // CHUNK)
    row = lambda v: v.reshape(1, -1)

    xs = x[0]
    acts = []
    for l in range(L):
        wm = jnp.where(mask[None], w_spatial[l], 0.0).astype(BF16)
        wmt = jnp.swapaxes(wm, 1, 2)
        bfull = jnp.broadcast_to(b_spatial[l][:, :, None], (H, HEAD_DIM, HEAD_DIM))
        small = dict(lng=row(gmlp_ln_g[l]), lnb=row(gmlp_ln_b[l]), wm=wm, wmt=wmt, bfull=bfull,
                     cw=conv_full[l], gn=row(group_norm_g[l]))
        h = _rms_fwd(xs, row(norm1_g[l]), f"l{l}_rms1")
        z = _mm_nn(h, W_in, l, out_dtype=BF16, name=f"l{l}_mm_in")
        y = _mixer_fwd(z, small["lng"], small["lnb"], wm, bfull, small["cw"], small["gn"], f"l{l}_mixer")
        x1 = _mm_nn(y, W_out, l, res=xs, out_dtype=F32, name=f"l{l}_mm_out")
        h2 = _rms_fwd(x1, row(norm2_g[l]), f"l{l}_rms2")
        gu, act = _mm_swiglu(h2, W_gu, l, name=f"l{l}_mm_swiglu")
        x2 = _mm_nn(act, W_down, l, res=x1, out_dtype=F32, tk=1408, name=f"l{l}_mm_down")
        acts.append(dict(x=xs, h=h, z=z, y=y, x1=x1, h2=h2, gu=gu, act=act, small=small))
        xs = x2

    loss_vec, dx, dxb, dgf = _loss_head(xs, row(final_norm_g), loss_target[0], "loss_head")
    loss = lax.psum(loss_vec[0, 0], ("x", "y", "c"))

    dW_in = dW_out = dW_gu = dW_down = None
    small_grads = [None] * L
    for l in reversed(range(L)):
        a = acts[l]
        sm = a["small"]
        dgu = _mm_nt_swiglu_bwd(dxb, W_down, l, a["gu"], name=f"l{l}_bwd_down")
        dW_down = _mm_tn(a["act"], dxb, l, L, dW_down, name=f"l{l}_dw_down")
        dh2 = _mm_nt_dgu(dgu, W_gu, l, name=f"l{l}_bwd_gu")
        dW_gu = _mm_tn(a["h2"], dgu, l, L, dW_gu, tkw=1024, tnw=512, name=f"l{l}_dw_gu")
        dx1, dx1b, dg2 = _rms_bwd(dh2, a["x1"], row(norm2_g[l]), dx, f"l{l}_rms2_bwd")
        dy = _mm_nt(dx1b, W_out, l, out_dtype=F32, name=f"l{l}_bwd_out")
        dW_out = _mm_tn(a["y"], dx1b, l, L, dW_out, name=f"l{l}_dw_out")
        dz, dgn, dlng, dlnb, dws, dbs, dcw = _mixer_bwd(
            a["z"], dy, sm["lng"], sm["lnb"], sm["wm"], sm["wmt"], sm["bfull"], sm["cw"], sm["gn"],
            f"l{l}_mixer_bwd")
        dh = _mm_nt(dz, W_in, l, out_dtype=F32, tk=1280, name=f"l{l}_bwd_in")
        dW_in = _mm_tn(a["h"], dz, l, L, dW_in, name=f"l{l}_dw_in")
        dx, dxb, dg1 = _rms_bwd(dh, a["x"], row(norm1_g[l]), dx1, f"l{l}_rms1_bwd")
        small_grads[l] = [dg1[0], dlng[0], dlnb[0], dws, dbs[:, 0, :], dcw[:CONV_TAPS], dgn[0], dg2[0]]
    grad_x = dx[None]

    geo4 = [geoms["in"], geoms["out"], _Geom(D, 2 * F, 1, n_f), geoms["down"]]
    own_h, sib_h = _swap_halves([dW_in, dW_out, dW_gu, dW_down], geo4, L)
    p_in, p_out, p_gu, p_down = [_add2_bf16(o, s, f"grad_pair_sum_{i}") for i, (o, s) in enumerate(zip(own_h, sib_h))]
    geo5 = [geoms["in"], geoms["out"], geoms["gate"], geoms["up"], geoms["down"]]
    own_p, recv_p = _exchange_shards([p_in, p_out, p_gu, p_gu, p_down], geo5, L)
    ghs = [_sum4_f32(o, r, f"grad_chip_sum_{i}") for i, (o, r) in enumerate(zip(own_p, recv_p))]
    g_in, g_out, g_gate, g_up, g_down = _join_halves(ghs, geo5, L)
    big = {}
    for nm, w, g, m, v in (("w_in", w_in, g_in, m_w_in, v_w_in), ("w_out", w_out, g_out, m_w_out, v_w_out),
                           ("w_gate", w_gate, g_gate, m_w_gate, v_w_gate), ("w_up", w_up, g_up, m_w_up, v_w_up),
                           ("w_down", w_down, g_down, m_w_down, v_w_down)):
        big[nm] = _adamw(w, g, m, v, f"adamw_{nm}")

    pieces = [p for l in range(L) for p in small_grads[l]] + [dgf[0]]
    red = _allreduce_small(_pack(pieces))
    red_list = _unpack(red, [p.shape for p in pieces])
    per = len(small_grads[0])
    stack = lambda i: jnp.stack([red_list[l * per + i] for l in range(L)])
    g_small = {"norm1_g": stack(0), "gmlp_ln_g": stack(1), "gmlp_ln_b": stack(2), "w_spatial": stack(3),
               "b_spatial": stack(4), "group_norm_g": stack(6), "norm2_g": stack(7),
               "final_norm_g": red_list[L * per]}
    jchip = 2 * lax.axis_index("x") + lax.axis_index("y")
    g_small["conv_w"] = lax.dynamic_slice_in_dim(stack(5), jchip * cb, cb, axis=2)
    small_w = {"norm1_g": (norm1_g, m_norm1_g, v_norm1_g), "gmlp_ln_g": (gmlp_ln_g, m_gmlp_ln_g, v_gmlp_ln_g),
               "gmlp_ln_b": (gmlp_ln_b, m_gmlp_ln_b, v_gmlp_ln_b), "w_spatial": (w_spatial, m_w_spatial, v_w_spatial),
               "b_spatial": (b_spatial, m_b_spatial, v_b_spatial), "conv_w": (conv_w, m_conv_w, v_conv_w),
               "group_norm_g": (group_norm_g, m_group_norm_g, v_group_norm_g),
               "norm2_g": (norm2_g, m_norm2_g, v_norm2_g), "final_norm_g": (final_norm_g, m_final_norm_g, v_final_norm_g)}
    snames = list(small_w)
    sd, smn, svn = _adamw_flat(_pack([small_w[n][0] for n in snames]), _pack([g_small[n] for n in snames]),
                               _pack([small_w[n][1] for n in snames]), _pack([small_w[n][2] for n in snames]),
                               "adamw_small")
    sshapes = [small_w[n][0].shape for n in snames]
    sd, smn, svn = _unpack(sd, sshapes), _unpack(smn, sshapes), _unpack(svn, sshapes)
    small_out = {n: (g_small[n], sd[i], smn[i], svn[i]) for i, n in enumerate(snames)}

    order = ["norm1_g", "w_in", "gmlp_ln_g", "gmlp_ln_b", "w_spatial", "b_spatial", "conv_w", "group_norm_g",
             "w_out", "norm2_g", "w_gate", "w_up", "w_down", "final_norm_g"]
    res = {n: (big[n] if n in big else small_out[n]) for n in order}
    return (loss, grad_x, *[res[n][0] for n in order], *[res[n][1] for n in order],
            *[res[n][2] for n in order], *[res[n][3] for n in order])
```

```python
import functools
import math

import jax
import jax.numpy as jnp
from jax import lax
from jax.experimental import pallas as pl
from jax.experimental.pallas import tpu as pltpu

RMS_EPS = 1e-6
LN_EPS = 1e-5
HEAD_DIM = 128
CHUNK = 64
CONV_TAPS = 3
HALO = 16
ADAM_LR = 0.001
ADAM_B1 = 0.9
ADAM_B2 = 0.999
ADAM_EPS = 1e-08
ADAM_WD = 0.01
ADAM_STEP = 10
V7X_VMEM_LIMIT = 56 * 1024 * 1024
N_CHIPS = 4
DMA_CHUNK_BYTES = 2 * 1024 * 1024
MESH = pl.DeviceIdType.MESH
F32 = jnp.float32
BF16 = jnp.bfloat16


def _tile(n, pref):
    if n <= pref:
        return n
    best = None
    for t in range(128, pref + 1, 128):
        if n % t == 0:
            best = t
    assert best is not None, (n, pref)
    return best


def _params(*sem):
    return pltpu.CompilerParams(dimension_semantics=sem if sem else None,
                                vmem_limit_bytes=V7X_VMEM_LIMIT)


def _gelu(x):
    c = math.sqrt(2.0 / math.pi)
    return 0.5 * x * (1.0 + jnp.tanh(c * (x + 0.044715 * x * x * x)))


def _gelu_and_grad(x):
    c = math.sqrt(2.0 / math.pi)
    x2 = x * x
    th = jnp.tanh(c * (x + 0.044715 * x * x2))
    val = 0.5 * x * (1.0 + th)
    grad = 0.5 * (1.0 + th) + 0.5 * x * (1.0 - th * th) * (c * (1.0 + 3.0 * 0.044715 * x2))
    return val, grad


def _sigmoid(x):
    return 1.0 / (1.0 + jnp.exp(-x))


def _nt(a, b):
    return lax.dot_general(a, b, (((1,), (1,)), ((), ())), preferred_element_type=F32)


def _tn(a, b):
    return lax.dot_general(a, b, (((0,), (0,)), ((), ())), preferred_element_type=F32)


def _rms_fwd(x, g, name):
    T, D = x.shape
    tr = _tile(T, 512)

    def body(x_ref, g_ref, h_ref):
        xv = x_ref[...]
        r = lax.rsqrt(jnp.mean(xv * xv, axis=-1, keepdims=True) + RMS_EPS)
        h_ref[...] = ((xv * r) * g_ref[...]).astype(h_ref.dtype)

    return pl.pallas_call(
        body, out_shape=jax.ShapeDtypeStruct((T, D), BF16), grid=(T // tr,),
        in_specs=[pl.BlockSpec((tr, D), lambda i: (i, 0)), pl.BlockSpec((1, D), lambda i: (0, 0))],
        out_specs=pl.BlockSpec((tr, D), lambda i: (i, 0)),
        name=name, compiler_params=_params("parallel"))(x, g)


def _rms_bwd(dh, x, g, dres, name):
    T, D = x.shape
    tr = _tile(T, 256)

    def body(dh_ref, x_ref, g_ref, dres_ref, dx_ref, dxb_ref, dg_ref):
        i = pl.program_id(0)
        xv = x_ref[...]
        dhv = dh_ref[...]
        r = lax.rsqrt(jnp.mean(xv * xv, axis=-1, keepdims=True) + RMS_EPS)
        xh = xv * r
        q = dhv * g_ref[...]
        dx = dres_ref[...] + r * (q - xh * jnp.mean(q * xh, axis=-1, keepdims=True))
        dx_ref[...] = dx
        dxb_ref[...] = dx.astype(BF16)
        part = jnp.sum(dhv * xh, axis=0, keepdims=True)

        @pl.when(i == 0)
        def _():
            dg_ref[...] = part

        @pl.when(i > 0)
        def _():
            dg_ref[...] += part

    row = pl.BlockSpec((tr, D), lambda i: (i, 0))
    vec = pl.BlockSpec((1, D), lambda i: (0, 0))
    return pl.pallas_call(
        body,
        out_shape=(jax.ShapeDtypeStruct((T, D), F32), jax.ShapeDtypeStruct((T, D), BF16),
                   jax.ShapeDtypeStruct((1, D), F32)),
        grid=(T // tr,), in_specs=[row, row, vec, row], out_specs=(row, row, vec),
        name=name, compiler_params=_params("arbitrary"))(dh, x, g, dres)


def _loss_head(x, g, tgt, name):
    T, D = x.shape
    tr = _tile(T, 256)

    def body(x_ref, g_ref, t_ref, loss_ref, dx_ref, dxb_ref, dg_ref):
        i = pl.program_id(0)
        xv = x_ref[...]
        gv = g_ref[...]
        r = lax.rsqrt(jnp.mean(xv * xv, axis=-1, keepdims=True) + RMS_EPS)
        xh = xv * r
        err = xh * gv - t_ref[...]
        lpart = jnp.full((1, 128), 0.5 * jnp.sum(jnp.mean(err * err, axis=-1, keepdims=True)), F32)
        dy = err * (1.0 / D)
        q = dy * gv
        dx = r * (q - xh * jnp.mean(q * xh, axis=-1, keepdims=True))
        dx_ref[...] = dx
        dxb_ref[...] = dx.astype(BF16)
        gpart = jnp.sum(dy * xh, axis=0, keepdims=True)

        @pl.when(i == 0)
        def _():
            loss_ref[...] = lpart
            dg_ref[...] = gpart

        @pl.when(i > 0)
        def _():
            loss_ref[...] += lpart
            dg_ref[...] += gpart

    row = pl.BlockSpec((tr, D), lambda i: (i, 0))
    vec = pl.BlockSpec((1, D), lambda i: (0, 0))
    return pl.pallas_call(
        body,
        out_shape=(jax.ShapeDtypeStruct((1, 128), F32), jax.ShapeDtypeStruct((T, D), F32),
                   jax.ShapeDtypeStruct((T, D), BF16), jax.ShapeDtypeStruct((1, D), F32)),
        grid=(T // tr,), in_specs=[row, vec, row],
        out_specs=(pl.BlockSpec((1, 128), lambda i: (0, 0)), row, row, vec),
        name=name, compiler_params=_params("arbitrary"))(x, g, tgt)


def _mm_nn(a, w, l, *, res=None, out_dtype, tm=1024, tn=1024, tk=None, name):
    M, K = a.shape
    N = w.shape[2]
    tm, tn = _tile(M, tm), _tile(N, tn)
    tk = K if tk is None else _tile(K, tk)
    nk = K // tk
    has_res = res is not None

    def body(*refs):
        a_ref, w_ref = refs[0], refs[1]
        r_ref = refs[2] if has_res else None
        o_ref = refs[2 + has_res]
        part = jnp.dot(a_ref[...], w_ref[...], preferred_element_type=F32)

        def finish(acc):
            if has_res:
                acc = r_ref[...] + acc
            o_ref[...] = acc.astype(o_ref.dtype)

        if nk == 1:
            finish(part)
        else:
            acc_ref = refs[3 + has_res]
            k = pl.program_id(2)

            @pl.when(k == 0)
            def _():
                acc_ref[...] = part

            @pl.when(jnp.logical_and(k > 0, k < nk - 1))
            def _():
                acc_ref[...] += part

            @pl.when(k == nk - 1)
            def _():
                finish(acc_ref[...] + part)

    in_specs = [pl.BlockSpec((tm, tk), lambda i, j, k: (i, k)),
                pl.BlockSpec((None, tk, tn), lambda i, j, k: (l, k, j))]
    args = [a, w]
    if has_res:
        in_specs.append(pl.BlockSpec((tm, tn), lambda i, j, k: (i, j)))
        args.append(res)
    return pl.pallas_call(
        body, out_shape=jax.ShapeDtypeStruct((M, N), out_dtype), grid=(M // tm, N // tn, nk),
        in_specs=in_specs, out_specs=pl.BlockSpec((tm, tn), lambda i, j, k: (i, j)),
        scratch_shapes=[pltpu.VMEM((tm, tn), F32)] if nk > 1 else [],
        name=name, compiler_params=_params("parallel", "parallel", "arbitrary"))(*args)


def _mm_swiglu(h, wgu, l, *, tm=1024, tn=512, name):
    T, D = h.shape
    F = wgu.shape[2] // 2
    tm, tn = _tile(T, tm), _tile(F, tn)
    nf = F // tn

    def body(h_ref, wg_ref, wu_ref, gu_ref, act_ref):
        hv = h_ref[...]
        g = jnp.dot(hv, wg_ref[...], preferred_element_type=F32)
        u = jnp.dot(hv, wu_ref[...], preferred_element_type=F32)
        act_ref[...] = ((g * _sigmoid(g)) * u).astype(BF16)
        gu_ref[0] = g.astype(BF16)
        gu_ref[1] = u.astype(BF16)

    return pl.pallas_call(
        body,
        out_shape=(jax.ShapeDtypeStruct((2, T, F), BF16), jax.ShapeDtypeStruct((T, F), BF16)),
        grid=(T // tm, nf),
        in_specs=[pl.BlockSpec((tm, D), lambda i, j: (i, 0)),
                  pl.BlockSpec((None, D, tn), lambda i, j: (l, 0, j)),
                  pl.BlockSpec((None, D, tn), lambda i, j: (l, 0, j + nf))],
        out_specs=(pl.BlockSpec((2, tm, tn), lambda i, j: (0, i, j)),
                   pl.BlockSpec((tm, tn), lambda i, j: (i, j))),
        name=name, compiler_params=_params("parallel", "parallel"))(h, wgu, wgu)


def _mm_nt(a, w, l, *, out_dtype, tm=1024, tn=1024, tk=None, name):
    M, K = a.shape
    N = w.shape[1]
    tm, tn = _tile(M, tm), _tile(N, tn)
    tk = K if tk is None else _tile(K, tk)
    nk = K // tk

    def body(*refs):
        a_ref, w_ref, o_ref = refs[0], refs[1], refs[2]
        part = _nt(a_ref[...], w_ref[...])
        if nk == 1:
            o_ref[...] = part.astype(o_ref.dtype)
        else:
            acc_ref = refs[3]
            k = pl.program_id(2)

            @pl.when(k == 0)
            def _():
                acc_ref[...] = part

            @pl.when(jnp.logical_and(k > 0, k < nk - 1))
            def _():
                acc_ref[...] += part

            @pl.when(k == nk - 1)
            def _():
                o_ref[...] = (acc_ref[...] + part).astype(o_ref.dtype)

    return pl.pallas_call(
        body, out_shape=jax.ShapeDtypeStruct((M, N), out_dtype), grid=(M // tm, N // tn, nk),
        in_specs=[pl.BlockSpec((tm, tk), lambda i, j, k: (i, k)),
                  pl.BlockSpec((None, tn, tk), lambda i, j, k: (l, j, k))],
        out_specs=pl.BlockSpec((tm, tn), lambda i, j, k: (i, j)),
        scratch_shapes=[pltpu.VMEM((tm, tn), F32)] if nk > 1 else [],
        name=name, compiler_params=_params("parallel", "parallel", "arbitrary"))(a, w)


def _mm_nt_swiglu_bwd(dxb, wdown, l, gu, *, tm=1024, tn=512, name):
    T, D = dxb.shape
    F = wdown.shape[1]
    tm, tn = _tile(T, tm), _tile(F, tn)

    def body(dx_ref, w_ref, gu_ref, dgu_ref):
        da = _nt(dx_ref[...], w_ref[...])
        g = gu_ref[0].astype(F32)
        u = gu_ref[1].astype(F32)
        s = _sigmoid(g)
        dgu_ref[0] = ((da * u) * (s * (1.0 + g * (1.0 - s)))).astype(BF16)
        dgu_ref[1] = (da * (g * s)).astype(BF16)

    blk3 = pl.BlockSpec((2, tm, tn), lambda i, j: (0, i, j))
    return pl.pallas_call(
        body, out_shape=jax.ShapeDtypeStruct((2, T, F), BF16), grid=(T // tm, F // tn),
        in_specs=[pl.BlockSpec((tm, D), lambda i, j: (i, 0)),
                  pl.BlockSpec((None, tn, D), lambda i, j: (l, j, 0)), blk3],
        out_specs=blk3, name=name, compiler_params=_params("parallel", "parallel"))(dxb, wdown, gu)


def _mm_nt_dgu(dgu, wgu, l, *, tm=1024, tn=1024, tk=1408, name):
    _, T, F = dgu.shape
    D = wgu.shape[1]
    tm, tn, tk = _tile(T, tm), _tile(D, tn), _tile(F, tk)
    nkf = F // tk
    nk = 2 * nkf

    def body(a_ref, w_ref, o_ref, acc_ref):
        k = pl.program_id(2)
        part = _nt(a_ref[...], w_ref[...])

        @pl.when(k == 0)
        def _():
            acc_ref[...] = part

        @pl.when(jnp.logical_and(k > 0, k < nk - 1))
        def _():
            acc_ref[...] += part

        @pl.when(k == nk - 1)
        def _():
            o_ref[...] = acc_ref[...] + part

    return pl.pallas_call(
        body, out_shape=jax.ShapeDtypeStruct((T, D), F32), grid=(T // tm, D // tn, nk),
        in_specs=[pl.BlockSpec((None, tm, tk), lambda i, j, k: (k // nkf, i, k % nkf)),
                  pl.BlockSpec((None, tn, tk), lambda i, j, k: (l, j, k))],
        out_specs=pl.BlockSpec((tm, tn), lambda i, j, k: (i, j)),
        scratch_shapes=[pltpu.VMEM((tm, tn), F32)],
        name=name, compiler_params=_params("parallel", "parallel", "arbitrary"))(dgu, wgu)


def _mm_tn(a, g, l, n_layers, prev, *, tkw=512, tnw=1024, name):
    T, Kw = a.shape
    pair = g.ndim == 3
    Nw = 2 * g.shape[2] if pair else g.shape[1]
    tkw = _tile(Kw, tkw)
    tnw = _tile(g.shape[2] if pair else Nw, tnw)
    nf = (Nw // 2) // tnw if pair else 0

    def body(*refs):
        a_ref, g_ref, o_ref = refs[0], refs[1], refs[-1]
        o_ref[...] = _tn(a_ref[...], g_ref[...]).astype(o_ref.dtype)

    if pair:
        g_spec = pl.BlockSpec((None, T, tnw), lambda i, j: (j // nf, 0, j % nf))
    else:
        g_spec = pl.BlockSpec((T, tnw), lambda i, j: (0, j))
    in_specs = [pl.BlockSpec((T, tkw), lambda i, j: (0, i)), g_spec]
    args = [a, g]
    aliases = {}
    if prev is not None:
        in_specs.append(pl.BlockSpec(memory_space=pl.ANY))
        args.append(prev)
        aliases = {2: 0}
    return pl.pallas_call(
        body, out_shape=jax.ShapeDtypeStruct((n_layers, Kw, Nw), BF16), grid=(Kw // tkw, Nw // tnw),
        in_specs=in_specs, out_specs=pl.BlockSpec((None, tkw, tnw), lambda i, j: (l, i, j)),
        input_output_aliases=aliases,
        name=name, compiler_params=_params("parallel", "parallel"))(*args)


def _mixer_specs(T, A, tr):
    nin = 5 * A
    nb = tr // HALO
    last = T // HALO - 1
    prev = pl.BlockSpec((HALO, nin), lambda i: (jnp.maximum(i * nb - 1, 0), 0))
    cur = pl.BlockSpec((tr, nin), lambda i: (i, 0))
    nxt = pl.BlockSpec((HALO, nin), lambda i: (jnp.minimum((i + 1) * nb, last), 0))
    return prev, cur, nxt


def _conv(p_ext, cw_ref):
    return (cw_ref[2:3, :] * p_ext + cw_ref[1:2, :] * pltpu.roll(p_ext, 1, 0)
            + cw_ref[0:1, :] * pltpu.roll(p_ext, 2, 0))


def _mixer_fwd(z, lng, lnb, wm, bfull, cw, gn, name):
    T, nin = z.shape
    A = nin // 5
    H = A // HEAD_DIM
    tr = _tile(T, 256)
    nblk = tr // HEAD_DIM

    def body(zp_ref, z_ref, lng_ref, lnb_ref, wm_ref, bf_ref, cw_ref, gn_ref, y_ref, vn_ref, mix_ref):
        i = pl.program_id(0)
        u = _gelu(z_ref[:, 0:A].astype(F32))
        vg = _gelu(z_ref[:, A:2 * A].astype(F32))
        xc = vg - jnp.mean(vg, axis=-1, keepdims=True)
        rstd = lax.rsqrt(jnp.mean(xc * xc, axis=-1, keepdims=True) + LN_EPS)
        vn_ref[...] = ((xc * rstd) * lng_ref[...] + lnb_ref[...]).astype(BF16)
        for cb in range(nblk):
            rows = slice(cb * HEAD_DIM, (cb + 1) * HEAD_DIM)
            for h in range(H):
                cols = slice(h * HEAD_DIM, (h + 1) * HEAD_DIM)
                mix_ref[rows, cols] = jnp.dot(wm_ref[h], vn_ref[rows, cols],
                                              preferred_element_type=F32) + bf_ref[h]
        ya = u * mix_ref[...]
        ra = lax.rsqrt(jnp.mean(ya * ya, axis=-1, keepdims=True) + RMS_EPS)
        y_ref[:, 0:A] = ((ya * ra) * gn_ref[:, 0:A]).astype(BF16)

        p_prev = zp_ref[:, 3 * A:4 * A].astype(F32) * zp_ref[:, 4 * A:5 * A].astype(F32)
        p_prev = jnp.where(i > 0, p_prev, 0.0)
        p_cur = z_ref[:, 3 * A:4 * A].astype(F32) * z_ref[:, 4 * A:5 * A].astype(F32)
        cv = _conv(jnp.concatenate([p_prev, p_cur], axis=0), cw_ref)[HALO:]
        yb = z_ref[:, 2 * A:3 * A].astype(F32) * cv
        rb = lax.rsqrt(jnp.mean(yb * yb, axis=-1, keepdims=True) + RMS_EPS)
        y_ref[:, A:2 * A] = ((yb * rb) * gn_ref[:, A:2 * A]).astype(BF16)

    prev, cur, _ = _mixer_specs(T, A, tr)
    full = lambda shape: pl.BlockSpec(shape, lambda i: (0,) * len(shape))
    return pl.pallas_call(
        body, out_shape=jax.ShapeDtypeStruct((T, 2 * A), BF16), grid=(T // tr,),
        in_specs=[prev, cur, full((1, A)), full((1, A)), full((H, HEAD_DIM, HEAD_DIM)),
                  full((H, HEAD_DIM, HEAD_DIM)), full((8, A)), full((1, 2 * A))],
        out_specs=pl.BlockSpec((tr, 2 * A), lambda i: (i, 0)),
        scratch_shapes=[pltpu.VMEM((tr, A), BF16), pltpu.VMEM((tr, A), F32)],
        name=name, compiler_params=_params("parallel"))(z, z, lng, lnb, wm, bfull, cw, gn)


def _mixer_bwd(z, dy, lng, lnb, wm, wmt, bfull, cw, gn, name):
    T, nin = z.shape
    A = nin // 5
    H = A // HEAD_DIM
    tr = _tile(T, 256)
    nblk = tr // HEAD_DIM
    ngrid = T // tr
    next_ = tr + 2 * HALO

    def body(zp_ref, z_ref, zn_ref, dy_ref, dyn_ref, lng_ref, lnb_ref, wm_ref, wmt_ref, bf_ref, cw_ref, gn_ref,
             dz_ref, dgn_ref, dlng_ref, dlnb_ref, dws_ref, dbs_ref, dcw_ref,
             vn_ref, mix_ref, dmix_ref, dvn_ref):
        i = pl.program_id(0)

        @pl.when(i == 0)
        def _():
            dgn_ref[...] = jnp.zeros_like(dgn_ref)
            dlng_ref[...] = jnp.zeros_like(dlng_ref)
            dlnb_ref[...] = jnp.zeros_like(dlnb_ref)
            dws_ref[...] = jnp.zeros_like(dws_ref)
            dbs_ref[...] = jnp.zeros_like(dbs_ref)
            dcw_ref[...] = jnp.zeros_like(dcw_ref)

        u, du_dz = _gelu_and_grad(z_ref[:, 0:A].astype(F32))
        vg, dv_dz = _gelu_and_grad(z_ref[:, A:2 * A].astype(F32))
        xc = vg - jnp.mean(vg, axis=-1, keepdims=True)
        rstd = lax.rsqrt(jnp.mean(xc * xc, axis=-1, keepdims=True) + LN_EPS)
        vhat = xc * rstd
        vn_ref[...] = (vhat * lng_ref[...] + lnb_ref[...]).astype(BF16)
        for cb in range(nblk):
            rows = slice(cb * HEAD_DIM, (cb + 1) * HEAD_DIM)
            for h in range(H):
                cols = slice(h * HEAD_DIM, (h + 1) * HEAD_DIM)
                mix_ref[rows, cols] = jnp.dot(wm_ref[h], vn_ref[rows, cols],
                                              preferred_element_type=F32) + bf_ref[h]
        mixed = mix_ref[...]
        ya = u * mixed
        ra = lax.rsqrt(jnp.mean(ya * ya, axis=-1, keepdims=True) + RMS_EPS)
        yha = ya * ra
        dyan = dy_ref[:, 0:A]
        dgn_ref[:, 0:A] += jnp.sum(dyan * yha, axis=0, keepdims=True)
        qa = dyan * gn_ref[:, 0:A]
        dya = ra * (qa - yha * jnp.mean(qa * yha, axis=-1, keepdims=True))
        dz_ref[:, 0:A] = ((dya * mixed) * du_dz).astype(BF16)
        dmix_ref[...] = (dya * u).astype(BF16)

        ii = lax.broadcasted_iota(jnp.int32, (HEAD_DIM, HEAD_DIM), 0)
        jj = lax.broadcasted_iota(jnp.int32, (HEAD_DIM, HEAD_DIM), 1)
        mask = (jj // CHUNK <= ii // CHUNK).astype(F32)
        ones = jnp.ones((8, HEAD_DIM), BF16)
        for h in range(H):
            cols = slice(h * HEAD_DIM, (h + 1) * HEAD_DIM)
            dws = jnp.zeros((HEAD_DIM, HEAD_DIM), F32)
            dbs = jnp.zeros((8, HEAD_DIM), F32)
            for cb in range(nblk):
                rows = slice(cb * HEAD_DIM, (cb + 1) * HEAD_DIM)
                dm = dmix_ref[rows, cols]
                dws = dws + _nt(dm, vn_ref[rows, cols])
                dbs = dbs + _nt(ones, dm)
                dvn_ref[rows, cols] = jnp.dot(wmt_ref[h], dm, preferred_element_type=F32)
            dws_ref[h] += dws * mask
            dbs_ref[h] += dbs
        dvn = dvn_ref[...]
        dlnb_ref[...] += jnp.sum(dvn, axis=0, keepdims=True)
        dlng_ref[...] += jnp.sum(dvn * vhat, axis=0, keepdims=True)
        dvh = dvn * lng_ref[...]
        dvg = rstd * (dvh - jnp.mean(dvh, axis=-1, keepdims=True)
                      - vhat * jnp.mean(dvh * vhat, axis=-1, keepdims=True))
        dz_ref[:, A:2 * A] = (dvg * dv_dz).astype(BF16)

        def ext(lo):
            mid = z_ref[:, lo:lo + A].astype(F32)
            return jnp.concatenate([zp_ref[:, lo:lo + A].astype(F32), mid, zn_ref[:, lo:lo + A].astype(F32)], axis=0)

        zb, zc, zh = ext(2 * A), ext(3 * A), ext(4 * A)
        row = lax.broadcasted_iota(jnp.int32, (next_, 1), 0)
        p = zc * zh
        p = jnp.where(jnp.logical_and(row < HALO, i == 0), 0.0, p)
        cv = _conv(p, cw_ref)
        yb = zb * cv
        rb = lax.rsqrt(jnp.mean(yb * yb, axis=-1, keepdims=True) + RMS_EPS)
        yhb = yb * rb
        dyn_rows = jnp.where(i < ngrid - 1, dyn_ref[:, A:2 * A], 0.0)
        dybn = jnp.concatenate([jnp.zeros((HALO, A), F32), dy_ref[:, A:2 * A], dyn_rows], axis=0)
        ctr = slice(HALO, HALO + tr)
        dgn_ref[:, A:2 * A] += jnp.sum((dybn * yhb)[ctr], axis=0, keepdims=True)
        qb = dybn * gn_ref[:, A:2 * A]
        dyb = rb * (qb - yhb * jnp.mean(qb * yhb, axis=-1, keepdims=True))
        dcv = dyb * zb
        dp = (cw_ref[2:3, :] * dcv + cw_ref[1:2, :] * pltpu.roll(dcv, next_ - 1, 0)
              + cw_ref[0:1, :] * pltpu.roll(dcv, next_ - 2, 0))
        dz_ref[:, 2 * A:3 * A] = (dyb * cv)[ctr].astype(BF16)
        dz_ref[:, 3 * A:4 * A] = (dp * zh)[ctr].astype(BF16)
        dz_ref[:, 4 * A:5 * A] = (dp * zc)[ctr].astype(BF16)
        dcw_ref[2:3, :] += jnp.sum((dcv * p)[ctr], axis=0, keepdims=True)
        dcw_ref[1:2, :] += jnp.sum((dcv * pltpu.roll(p, 1, 0))[ctr], axis=0, keepdims=True)
        dcw_ref[0:1, :] += jnp.sum((dcv * pltpu.roll(p, 2, 0))[ctr], axis=0, keepdims=True)

    prev, cur, nxt = _mixer_specs(T, A, tr)
    nb = tr // HALO
    dy_cur = pl.BlockSpec((tr, 2 * A), lambda i: (i, 0))
    dy_nxt = pl.BlockSpec((HALO, 2 * A), lambda i: (jnp.minimum((i + 1) * nb, T // HALO - 1), 0))
    full = lambda shape: pl.BlockSpec(shape, lambda i: (0,) * len(shape))
    hh = (H, HEAD_DIM, HEAD_DIM)
    return pl.pallas_call(
        body,
        out_shape=(jax.ShapeDtypeStruct((T, nin), BF16), jax.ShapeDtypeStruct((1, 2 * A), F32),
                   jax.ShapeDtypeStruct((1, A), F32), jax.ShapeDtypeStruct((1, A), F32),
                   jax.ShapeDtypeStruct(hh, F32), jax.ShapeDtypeStruct((H, 8, HEAD_DIM), F32),
                   jax.ShapeDtypeStruct((8, A), F32)),
        grid=(ngrid,),
        in_specs=[prev, cur, nxt, dy_cur, dy_nxt, full((1, A)), full((1, A)), full(hh), full(hh), full(hh),
                  full((8, A)), full((1, 2 * A))],
        out_specs=(pl.BlockSpec((tr, nin), lambda i: (i, 0)), full((1, 2 * A)), full((1, A)), full((1, A)),
                   full(hh), full((H, 8, HEAD_DIM)), full((8, A))),
        scratch_shapes=[pltpu.VMEM((tr, A), BF16), pltpu.VMEM((tr, A), F32), pltpu.VMEM((tr, A), BF16),
                        pltpu.VMEM((tr, A), F32)],
        name=name, compiler_params=_params("arbitrary"))(z, z, z, dy, dy, lng, lnb, wm, wmt, bfull, cw, gn)


def _blocks3(shape, budget_elems=256 * 1024):
    _, R, C = shape
    tr = R
    while tr * C > budget_elems and tr % 2 == 0 and (tr // 2) % 16 == 0:
        tr //= 2
    return tr


def _add2_bf16(a, b, name):
    L, R, C = a.shape
    tr = _blocks3(a.shape, 512 * 1024)

    def body(a_ref, b_ref, o_ref):
        o_ref[...] = (a_ref[...].astype(F32) + b_ref[...].astype(F32)).astype(BF16)

    blk = pl.BlockSpec((None, tr, C), lambda l, i: (l, i, 0))
    return pl.pallas_call(
        body, out_shape=jax.ShapeDtypeStruct(a.shape, BF16), grid=(L, R // tr),
        in_specs=[blk, blk], out_specs=blk, name=name, compiler_params=_params("parallel", "parallel"))(a, b)


def _sum4_f32(own, recv, name):
    L, R, C = own.shape
    tr = _blocks3(own.shape, 512 * 1024)

    def body(o_ref, r_ref, out_ref):
        acc = o_ref[...].astype(F32)
        for k in range(N_CHIPS - 1):
            acc = acc + r_ref[k].astype(F32)
        out_ref[...] = acc

    blk = pl.BlockSpec((None, tr, C), lambda l, i: (l, i, 0))
    return pl.pallas_call(
        body, out_shape=jax.ShapeDtypeStruct(own.shape, F32), grid=(L, R // tr),
        in_specs=[blk, pl.BlockSpec((N_CHIPS - 1, None, tr, C), lambda l, i: (0, l, i, 0))],
        out_specs=blk, name=name, compiler_params=_params("parallel", "parallel"))(own, recv)


def _adamw_math(w, g, m, v):
    m = ADAM_B1 * m + (1.0 - ADAM_B1) * g
    v = ADAM_B2 * v + (1.0 - ADAM_B2) * (g * g)
    m_hat = m / (1.0 - ADAM_B1 ** ADAM_STEP)
    v_hat = v / (1.0 - ADAM_B2 ** ADAM_STEP)
    delta = -ADAM_LR * (m_hat / (jnp.sqrt(v_hat) + ADAM_EPS) + ADAM_WD * w)
    return delta, m, v


def _adamw(w, g, m, v, name):
    L, R, C = w.shape
    tr = _blocks3(w.shape)

    def body(w_ref, g_ref, m_ref, v_ref, go_ref, d_ref, mo_ref, vo_ref):
        gv = g_ref[...]
        d, mn, vn = _adamw_math(w_ref[...], gv, m_ref[...], v_ref[...])
        go_ref[...] = gv
        d_ref[...] = d
        mo_ref[...] = mn
        vo_ref[...] = vn

    blk = pl.BlockSpec((None, tr, C), lambda l, i: (l, i, 0))
    sds = jax.ShapeDtypeStruct(w.shape, F32)
    return pl.pallas_call(
        body, out_shape=(sds, sds, sds, sds), grid=(L, R // tr), in_specs=[blk] * 4, out_specs=(blk,) * 4,
        name=name, compiler_params=_params("parallel", "parallel"))(w, g, m, v)


def _adamw_flat(w, g, m, v, name):
    R, C = w.shape
    tr = _tile(R, 1024) if R % 128 == 0 else R

    def body(w_ref, g_ref, m_ref, v_ref, d_ref, mo_ref, vo_ref):
        d, mn, vn = _adamw_math(w_ref[...], g_ref[...], m_ref[...], v_ref[...])
        d_ref[...] = d
        mo_ref[...] = mn
        vo_ref[...] = vn

    blk = pl.BlockSpec((tr, C), lambda i: (i, 0))
    sds = jax.ShapeDtypeStruct(w.shape, F32)
    return pl.pallas_call(
        body, out_shape=(sds, sds, sds), grid=(R // tr,), in_specs=[blk] * 4, out_specs=(blk,) * 3,
        name=name, compiler_params=_params("parallel"))(w, g, m, v)


class _Geom:
    def __init__(self, rows, cols, axis, size, base=0):
        self.rows, self.cols, self.axis, self.size, self.base = rows, cols, axis, size, base

    def in_full(self, j, h):
        if self.axis == 1:
            r = (0, self.rows) if h is None else (h * (self.rows // 2), self.rows // 2)
            return r, (self.base + j * self.size, self.size)
        c = (0, self.cols) if h is None else (h * (self.cols // 2), self.cols // 2)
        return (self.base + j * self.size, self.size), c

    def in_shard(self, h):
        if self.axis == 1:
            return (h * (self.rows // 2), self.rows // 2), (0, self.size)
        return (0, self.size), (h * (self.cols // 2), self.cols // 2)

    def half_of_full(self, h):
        if self.axis == 1:
            return (h * (self.rows // 2), self.rows // 2), (0, self.cols)
        return (0, self.rows), (h * (self.cols // 2), self.cols // 2)

    def in_half(self, j):
        if self.axis == 1:
            return (0, self.rows // 2), (self.base + j * self.size, self.size)
        return (self.base + j * self.size, self.size), (0, self.cols // 2)

    def half_shape(self, L):
        return (L, self.rows // 2, self.cols) if self.axis == 1 else (L, self.rows, self.cols // 2)

    def shard_half_shape(self, L):
        return (L, self.rows // 2, self.size) if self.axis == 1 else (L, self.size, self.cols // 2)

    def shard_shape(self, L):
        return (L, self.rows, self.size) if self.axis == 1 else (L, self.size, self.cols)


def _at(ref, region):
    (r0, rn), (c0, cn) = region
    if not isinstance(r0, int):
        r0 = pl.multiple_of(r0, 16)
    if not isinstance(c0, int):
        c0 = pl.multiple_of(c0, 128)
    return ref.at[:, pl.ds(r0, rn), pl.ds(c0, cn)]


def _whole(shape):
    return (0, shape[1]), (0, shape[2])


def _split_rows(region, n_layers, itemsize):
    (r0, rn), cols = region
    want = max(1, (n_layers * rn * cols[1] * itemsize) // DMA_CHUNK_BYTES)
    n = 1
    for cand in range(1, want + 1):
        if rn % cand == 0 and (rn // cand) % 16 == 0:
            n = cand
    step = rn // n
    return [((r0 + i * step, step), cols) for i in range(n)]


class _Chunked:
    def __init__(self, make, src, src_reg, dst, dst_reg):
        n_layers, itemsize = src.shape[0], jnp.dtype(src.dtype).itemsize
        self.whole = make(_at(src, src_reg), _at(dst, dst_reg))
        self.parts = [make(_at(src, a), _at(dst, b)) for a, b in
                      zip(_split_rows(src_reg, n_layers, itemsize), _split_rows(dst_reg, n_layers, itemsize))]

    def start(self):
        for p in self.parts:
            p.start()


def _mesh_place():
    x, y, c = lax.axis_index("x"), lax.axis_index("y"), lax.axis_index("c")
    chips = [(1 - x, y), (x, 1 - y), (1 - x, 1 - y)]
    return x, y, c, 2 * x + y, chips


def _remote(ssem, rsem, dev):
    return lambda src, dst: pltpu.make_async_remote_copy(
        src_ref=src, dst_ref=dst, send_sem=ssem, recv_sem=rsem, device_id=dev, device_id_type=MESH)


def _local(sem):
    return lambda src, dst: pltpu.make_async_copy(src, dst, sem)


ANY = pl.BlockSpec(memory_space=pl.ANY)


def _gather_weights(shards, conv, geoms, L):
    names = ["in", "out", "gate", "up", "down"]
    full_of = {"in": 0, "out": 1, "gate": 2, "up": 2, "down": 3}
    g_in, g_out, g_gate, g_down = geoms["in"], geoms["out"], geoms["gate"], geoms["down"]
    full_shapes = [(L, g_in.rows, g_in.cols), (L, g_out.rows, g_out.cols),
                   (L, g_gate.rows, g_gate.cols), (L, g_down.rows, g_down.cols)]
    cb = conv.shape[2]
    nt = len(names)
    nk = N_CHIPS - 1

    def body(s_in, s_out, s_gate, s_up, s_down, s_conv, f_in, f_out, f_gu, f_down, f_conv,
             send_sems, recv_sems, fsend_sems, frecv_sems, local_sems):
        x, y, c, j, chips = _mesh_place()
        sibling = (x, y, 1 - c)
        srcs = [s_in, s_out, s_gate, s_up, s_down]
        fulls = [f_in, f_out, f_gu, f_down]
        sends, locals_ = [], []
        for t, nm in enumerate(names):
            geo, full = geoms[nm], fulls[full_of[nm]]
            lc = _Chunked(_local(local_sems.at[t]), srcs[t], _whole(srcs[t].shape), full, geo.in_full(j, None))
            lc.start()
            locals_.append(lc)
            for k, chip in enumerate(chips):
                cp = _Chunked(_remote(send_sems.at[nk * t + k], recv_sems.at[nk * t + k], (*chip, c)),
                              srcs[t], geo.in_shard(c), full, geo.in_full(j, c))
                cp.start()
                sends.append(cp)
        conv_at = lambda jj: ((0, 8), (jj * cb, cb))
        lc = _Chunked(_local(local_sems.at[nt]), s_conv, _whole(s_conv.shape), f_conv, conv_at(j))
        lc.start()
        locals_.append(lc)
        for k, chip in enumerate(chips):
            cp = _Chunked(_remote(send_sems.at[nk * nt + k], recv_sems.at[nk * nt + k], (*chip, c)),
                          s_conv, _whole(s_conv.shape), f_conv, conv_at(j))
            cp.start()
            sends.append(cp)
        for k, chip in enumerate(chips):
            jk = 2 * chip[0] + chip[1]
            for t, nm in enumerate(names):
                geo, full = geoms[nm], fulls[full_of[nm]]
                reg = geo.in_full(jk, c)
                _Chunked(_remote(send_sems.at[nk * t + k], recv_sems.at[nk * t + k], sibling),
                         full, reg, full, reg).whole.wait_recv()
                fw = _Chunked(_remote(fsend_sems.at[nk * t + k], frecv_sems.at[nk * t + k], sibling),
                              full, reg, full, reg)
                fw.start()
                sends.append(fw)
        for k, chip in enumerate(chips):
            jk = 2 * chip[0] + chip[1]
            for t, nm in enumerate(names):
                geo, full = geoms[nm], fulls[full_of[nm]]
                reg = geo.in_full(jk, 1 - c)
                _Chunked(_remote(fsend_sems.at[nk * t + k], frecv_sems.at[nk * t + k], sibling),
                         full, reg, full, reg).whole.wait_recv()
            _Chunked(_remote(send_sems.at[nk * nt + k], recv_sems.at[nk * nt + k], sibling),
                     f_conv, conv_at(jk), f_conv, conv_at(jk)).whole.wait_recv()
        for cp in sends:
            cp.whole.wait_send()
        for lc in locals_:
            lc.whole.wait()

    out_shape = [jax.ShapeDtypeStruct(s, BF16) for s in full_shapes]
    out_shape.append(jax.ShapeDtypeStruct((L, 8, N_CHIPS * cb), F32))
    return pl.pallas_call(
        body, out_shape=tuple(out_shape), in_specs=[ANY] * 6, out_specs=(ANY,) * 5,
        scratch_shapes=[pltpu.SemaphoreType.DMA((nk * (nt + 1),)), pltpu.SemaphoreType.DMA((nk * (nt + 1),)),
                        pltpu.SemaphoreType.DMA((nk * nt,)), pltpu.SemaphoreType.DMA((nk * nt,)),
                        pltpu.SemaphoreType.DMA((nt + 1,))],
        name="gather_weights")(*shards, conv)


def _swap_halves(dws, geos, L):
    n = len(dws)

    def body(*refs):
        ins, owns, recvs = refs[:n], refs[n:2 * n], refs[2 * n:3 * n]
        send_sems, recv_sems, local_sems = refs[3 * n:]
        x, y, c, _, _ = _mesh_place()
        sibling = (x, y, 1 - c)
        cps = []
        for t in range(n):
            lc = _Chunked(_local(local_sems.at[t]), ins[t], geos[t].half_of_full(c), owns[t], _whole(owns[t].shape))
            lc.start()
            cp = _Chunked(_remote(send_sems.at[t], recv_sems.at[t], sibling),
                          ins[t], geos[t].half_of_full(1 - c), recvs[t], _whole(recvs[t].shape))
            cp.start()
            cps.append((lc, cp))
        for lc, cp in cps:
            cp.whole.wait()
            lc.whole.wait()

    halves = tuple(jax.ShapeDtypeStruct(g.half_shape(L), BF16) for g in geos)
    res = pl.pallas_call(
        body, out_shape=halves + halves, in_specs=[ANY] * n, out_specs=(ANY,) * (2 * n),
        scratch_shapes=[pltpu.SemaphoreType.DMA((n,)), pltpu.SemaphoreType.DMA((n,)),
                        pltpu.SemaphoreType.DMA((n,))],
        name="grad_swap_halves")(*dws)
    return res[:n], res[n:]


def _exchange_shards(ps, geos, L):
    n = len(geos)
    nk = N_CHIPS - 1

    def body(*refs):
        ins, owns, recvs = refs[:n], refs[n:2 * n], refs[2 * n:3 * n]
        send_sems, recv_sems, local_sems = refs[3 * n:]
        x, y, c, j, chips = _mesh_place()
        cps, lcs = [], []
        for t in range(n):
            lc = _Chunked(_local(local_sems.at[t]), ins[t], geos[t].in_half(j), owns[t], _whole(owns[t].shape))
            lc.start()
            lcs.append(lc)
            for k, chip in enumerate(chips):
                jk = 2 * chip[0] + chip[1]
                dst = recvs[t].at[k]
                cp = _Chunked(_remote(send_sems.at[nk * t + k], recv_sems.at[nk * t + k], (*chip, c)),
                              ins[t], geos[t].in_half(jk), dst, _whole(dst.shape))
                cp.start()
                cps.append(cp)
        for cp in cps:
            cp.whole.wait()
        for lc in lcs:
            lc.whole.wait()

    owns = tuple(jax.ShapeDtypeStruct(g.shard_half_shape(L), BF16) for g in geos)
    recvs = tuple(jax.ShapeDtypeStruct((nk,) + g.shard_half_shape(L), BF16) for g in geos)
    res = pl.pallas_call(
        body, out_shape=owns + recvs, in_specs=[ANY] * n, out_specs=(ANY,) * (2 * n),
        scratch_shapes=[pltpu.SemaphoreType.DMA((nk * n,)), pltpu.SemaphoreType.DMA((nk * n,)),
                        pltpu.SemaphoreType.DMA((n,))],
        name="grad_exchange_shards")(*ps)
    return res[:n], res[n:]


def _join_halves(ghs, geos, L):
    n = len(geos)

    def body(*refs):
        ins, outs = refs[:n], refs[n:2 * n]
        send_sems, recv_sems, local_sems = refs[2 * n:]
        x, y, c, _, _ = _mesh_place()
        sibling = (x, y, 1 - c)
        cps = []
        for t in range(n):
            mine = geos[t].in_shard(c)
            lc = _Chunked(_local(local_sems.at[t]), ins[t], _whole(ins[t].shape), outs[t], mine)
            lc.start()
            cp = _Chunked(_remote(send_sems.at[t], recv_sems.at[t], sibling),
                          ins[t], _whole(ins[t].shape), outs[t], mine)
            cp.start()
            cps.append((lc, cp))
        for t, (lc, cp) in enumerate(cps):
            cp.whole.wait_send()
            other = geos[t].in_shard(1 - c)
            _Chunked(_remote(send_sems.at[t], recv_sems.at[t], sibling),
                     outs[t], other, outs[t], other).whole.wait_recv()
            lc.whole.wait()

    outs = tuple(jax.ShapeDtypeStruct(g.shard_shape(L), F32) for g in geos)
    return pl.pallas_call(
        body, out_shape=outs, in_specs=[ANY] * n, out_specs=(ANY,) * n,
        scratch_shapes=[pltpu.SemaphoreType.DMA((n,)), pltpu.SemaphoreType.DMA((n,)),
                        pltpu.SemaphoreType.DMA((n,))],
        name="grad_join_halves")(*ghs)


def _allreduce_small(s):
    R, C = s.shape

    def body(s_ref, o_ref, rbuf, send_sems, recv_sems):
        x, y, c = lax.axis_index("x"), lax.axis_index("y"), lax.axis_index("c")
        peers = [(x, y, 1 - c), (1 - x, y, c), (x, 1 - y, c)]
        o_ref[...] = s_ref[...]
        for k, peer in enumerate(peers):
            cp = _remote(send_sems.at[k], recv_sems.at[k], peer)(o_ref, rbuf.at[k])
            cp.start()
            cp.wait()
            o_ref[...] = o_ref[...] + rbuf[k]

    vm = pl.BlockSpec(memory_space=pltpu.VMEM)
    return pl.pallas_call(
        body, out_shape=jax.ShapeDtypeStruct((R, C), F32), in_specs=[vm], out_specs=vm,
        scratch_shapes=[pltpu.VMEM((3, R, C), F32), pltpu.SemaphoreType.DMA((3,)), pltpu.SemaphoreType.DMA((3,))],
        name="allreduce_small", compiler_params=pltpu.CompilerParams(vmem_limit_bytes=V7X_VMEM_LIMIT))(s)


def _pack(pieces):
    rows = []
    for p in pieces:
        flat = p.reshape(-1)
        pad = (-flat.shape[0]) % 1024
        rows.append(jnp.pad(flat, (0, pad)).reshape(-1, 128))
    return jnp.concatenate(rows, axis=0)


def _unpack(buf, shapes):
    out, r = [], 0
    for shp in shapes:
        n = math.prod(shp)
        nr = -(-n // 1024) * 8
        out.append(buf[r:r + nr].reshape(-1)[:n].reshape(shp))
        r += nr
    return out


def kernel(x, norm1_g, w_in, gmlp_ln_g, gmlp_ln_b, w_spatial, b_spatial, conv_w, group_norm_g, w_out, norm2_g, w_gate, w_up, w_down, final_norm_g, loss_target, m_norm1_g, m_w_in, m_gmlp_ln_g, m_gmlp_ln_b, m_w_spatial, m_b_spatial, m_conv_w, m_group_norm_g, m_w_out, m_norm2_g, m_w_gate, m_w_up, m_w_down, m_final_norm_g, v_norm1_g, v_w_in, v_gmlp_ln_g, v_gmlp_ln_b, v_w_spatial, v_b_spatial, v_conv_w, v_group_norm_g, v_w_out, v_norm2_g, v_w_gate, v_w_up, v_w_down, v_final_norm_g):
    L, D, n_in = w_in.shape
    T = x.shape[1]
    nin = N_CHIPS * n_in
    A = nin // 5
    H = A // HEAD_DIM
    n_f = w_gate.shape[2]
    F = N_CHIPS * n_f
    n_o = w_out.shape[1]
    cb = conv_w.shape[2]
    assert A == H * HEAD_DIM and T % 256 == 0 and N_CHIPS * n_o == D and N_CHIPS * cb == A

    geoms = {
        "in": _Geom(D, nin, 1, n_in),
        "out": _Geom(D, D, 0, n_o),
        "gate": _Geom(D, 2 * F, 1, n_f, 0),
        "up": _Geom(D, 2 * F, 1, n_f, F),
        "down": _Geom(F, D, 0, n_f),
    }

    shards = [w.astype(BF16) for w in (w_in, w_out, w_gate, w_up, w_down)]
    conv_pad = jnp.pad(conv_w, ((0, 0), (0, 8 - CONV_TAPS), (0, 0)))
    W_in, W_out, W_gu, W_down, conv_full = _gather_weights(shards, conv_pad, geoms, L)

    pos = jnp.arange(HEAD_DIM)
    mask = (pos[None, :] // CHUNK) <= (pos[:, None] // CHUNK)
    row = lambda v: v.reshape(1, -1)

    xs = x[0]
    acts = []
    for l in range(L):
        wm = jnp.where(mask[None], w_spatial[l], 0.0).astype(BF16)
        wmt = jnp.swapaxes(wm, 1, 2)
        bfull = jnp.broadcast_to(b_spatial[l][:, :, None], (H, HEAD_DIM, HEAD_DIM))
        small = dict(lng=row(gmlp_ln_g[l]), lnb=row(gmlp_ln_b[l]), wm=wm, wmt=wmt, bfull=bfull,
                     cw=conv_full[l], gn=row(group_norm_g[l]))
        h = _rms_fwd(xs, row(norm1_g[l]), f"l{l}_rms1")
        z = _mm_nn(h, W_in, l, out_dtype=BF16, name=f"l{l}_mm_in")
        y = _mixer_fwd(z, small["lng"], small["lnb"], wm, bfull, small["cw"], small["gn"], f"l{l}_mixer")
        x1 = _mm_nn(y, W_out, l, res=xs, out_dtype=F32, name=f"l{l}_mm_out")
        h2 = _rms_fwd(x1, row(norm2_g[l]), f"l{l}_rms2")
        gu, act = _mm_swiglu(h2, W_gu, l, name=f"l{l}_mm_swiglu")
        x2 = _mm_nn(act, W_down, l, res=x1, out_dtype=F32, tk=1408, name=f"l{l}_mm_down")
        acts.append(dict(x=xs, h=h, z=z, y=y, x1=x1, h2=h2, gu=gu, act=act, small=small))
        xs = x2

    loss_vec, dx, dxb, dgf = _loss_head(xs, row(final_norm_g), loss_target[0], "loss_head")
    loss = lax.psum(loss_vec[0, 0], ("x", "y", "c"))

    dW_in = dW_out = dW_gu = dW_down = None
    small_grads = [None] * L
    for l in reversed(range(L)):
        a = acts[l]
        sm = a["small"]
        dgu = _mm_nt_swiglu_bwd(dxb, W_down, l, a["gu"], name=f"l{l}_bwd_down")
        dW_down = _mm_tn(a["act"], dxb, l, L, dW_down, name=f"l{l}_dw_down")
        dh2 = _mm_nt_dgu(dgu, W_gu, l, name=f"l{l}_bwd_gu")
        dW_gu = _mm_tn(a["h2"], dgu, l, L, dW_gu, tkw=1024, tnw=512, name=f"l{l}_dw_gu")
        dx1, dx1b, dg2 = _rms_bwd(dh2, a["x1"], row(norm2_g[l]), dx, f"l{l}_rms2_bwd")
        dy = _mm_nt(dx1b, W_out, l, out_dtype=F32, name=f"l{l}_bwd_out")
        dW_out = _mm_tn(a["y"], dx1b, l, L, dW_out, name=f"l{l}_dw_out")
        dz, dgn, dlng, dlnb, dws, dbs, dcw = _mixer_bwd(
            a["z"], dy, sm["lng"], sm["lnb"], sm["wm"], sm["wmt"], sm["bfull"], sm["cw"], sm["gn"],
            f"l{l}_mixer_bwd")
        dh = _mm_nt(dz, W_in, l, out_dtype=F32, tk=1280, name=f"l{l}_bwd_in")
        dW_in = _mm_tn(a["h"], dz, l, L, dW_in, name=f"l{l}_dw_in")
        dx, dxb, dg1 = _rms_bwd(dh, a["x"], row(norm1_g[l]), dx1, f"l{l}_rms1_bwd")
        small_grads[l] = [dg1[0], dlng[0], dlnb[0], dws, dbs[:, 0, :], dcw[:CONV_TAPS], dgn[0], dg2[0]]
    grad_x = dx[None]

    geo4 = [geoms["in"], geoms["out"], _Geom(D, 2 * F, 1, n_f), geoms["down"]]
    own_h, sib_h = _swap_halves([dW_in, dW_out, dW_gu, dW_down], geo4, L)
    p_in, p_out, p_gu, p_down = [_add2_bf16(o, s, f"grad_pair_sum_{i}") for i, (o, s) in enumerate(zip(own_h, sib_h))]
    geo5 = [geoms["in"], geoms["out"], geoms["gate"], geoms["up"], geoms["down"]]
    own_p, recv_p = _exchange_shards([p_in, p_out, p_gu, p_gu, p_down], geo5, L)
    ghs = [_sum4_f32(o, r, f"grad_chip_sum_{i}") for i, (o, r) in enumerate(zip(own_p, recv_p))]
    g_in, g_out, g_gate, g_up, g_down = _join_halves(ghs, geo5, L)
    big = {}
    for nm, w, g, m, v in (("w_in", w_in, g_in, m_w_in, v_w_in), ("w_out", w_out, g_out, m_w_out, v_w_out),
                           ("w_gate", w_gate, g_gate, m_w_gate, v_w_gate), ("w_up", w_up, g_up, m_w_up, v_w_up),
                           ("w_down", w_down, g_down, m_w_down, v_w_down)):
        big[nm] = _adamw(w, g, m, v, f"adamw_{nm}")

    pieces = [p for l in range(L) for p in small_grads[l]] + [dgf[0]]
    red = _allreduce_small(_pack(pieces))
    red_list = _unpack(red, [p.shape for p in pieces])
    per = len(small_grads[0])
    stack = lambda i: jnp.stack([red_list[l * per + i] for l in range(L)])
    g_small = {"norm1_g": stack(0), "gmlp_ln_g": stack(1), "gmlp_ln_b": stack(2), "w_spatial": stack(3),
               "b_spatial": stack(4), "group_norm_g": stack(6), "norm2_g": stack(7),
               "final_norm_g": red_list[L * per]}
    jchip = 2 * lax.axis_index("x") + lax.axis_index("y")
    g_small["conv_w"] = lax.dynamic_slice_in_dim(stack(5), jchip * cb, cb, axis=2)
    small_w = {"norm1_g": (norm1_g, m_norm1_g, v_norm1_g), "gmlp_ln_g": (gmlp_ln_g, m_gmlp_ln_g, v_gmlp_ln_g),
               "gmlp_ln_b": (gmlp_ln_b, m_gmlp_ln_b, v_gmlp_ln_b), "w_spatial": (w_spatial, m_w_spatial, v_w_spatial),
               "b_spatial": (b_spatial, m_b_spatial, v_b_spatial), "conv_w": (conv_w, m_conv_w, v_conv_w),
               "group_norm_g": (group_norm_g, m_group_norm_g, v_group_norm_g),
               "norm2_g": (norm2_g, m_norm2_g, v_norm2_g), "final_norm_g": (final_norm_g, m_final_norm_g, v_final_norm_g)}
    snames = list(small_w)
    sd, smn, svn = _adamw_flat(_pack([small_w[n][0] for n in snames]), _pack([g_small[n] for n in snames]),
                               _pack([small_w[n][1] for n in snames]), _pack([small_w[n][2] for n in snames]),
                               "adamw_small")
    sshapes = [small_w[n][0].shape for n in snames]
    sd, smn, svn = _unpack(sd, sshapes), _unpack(smn, sshapes), _unpack(svn, sshapes)
    small_out = {n: (g_small[n], sd[i], smn[i], svn[i]) for i, n in enumerate(snames)}

    order = ["norm1_g", "w_in", "gmlp_ln_g", "gmlp_ln_b", "w_spatial", "b_spatial", "conv_w", "group_norm_g",
             "w_out", "norm2_g", "w_gate", "w_up", "w_down", "final_norm_g"]
    res = {n: (big[n] if n in big else small_out[n]) for n in order}
    return (loss, grad_x, *[res[n][0] for n in order], *[res[n][1] for n in order],
            *[res[n][2] for n in order], *[res[n][3] for n in order])
```

```python
import functools
import math

import jax
import jax.numpy as jnp
from jax import lax
from jax.experimental import pallas as pl
from jax.experimental.pallas import tpu as pltpu

RMS_EPS = 1e-6
LN_EPS = 1e-5
HEAD_DIM = 128
CHUNK = 64
CONV_TAPS = 3
HALO = 16
ADAM_LR = 0.001
ADAM_B1 = 0.9
ADAM_B2 = 0.999
ADAM_EPS = 1e-08
ADAM_WD = 0.01
ADAM_STEP = 10
V7X_VMEM_LIMIT = 56 * 1024 * 1024
N_CHIPS = 4
DMA_CHUNK_BYTES = 2 * 1024 * 1024
MESH = pl.DeviceIdType.MESH
F32 = jnp.float32
BF16 = jnp.bfloat16


def _tile(n, pref):
    if n <= pref:
        return n
    best = None
    for t in range(128, pref + 1, 128):
        if n % t == 0:
            best = t
    assert best is not None, (n, pref)
    return best


def _params(*sem):
    return pltpu.CompilerParams(dimension_semantics=sem if sem else None,
                                vmem_limit_bytes=V7X_VMEM_LIMIT)


def _gelu(x):
    c = math.sqrt(2.0 / math.pi)
    return 0.5 * x * (1.0 + jnp.tanh(c * (x + 0.044715 * x * x * x)))


def _gelu_and_grad(x):
    c = math.sqrt(2.0 / math.pi)
    x2 = x * x
    th = jnp.tanh(c * (x + 0.044715 * x * x2))
    val = 0.5 * x * (1.0 + th)
    grad = 0.5 * (1.0 + th) + 0.5 * x * (1.0 - th * th) * (c * (1.0 + 3.0 * 0.044715 * x2))
    return val, grad


def _sigmoid(x):
    return 1.0 / (1.0 + jnp.exp(-x))


def _nt(a, b):
    return lax.dot_general(a, b, (((1,), (1,)), ((), ())), preferred_element_type=F32)


def _tn(a, b):
    return lax.dot_general(a, b, (((0,), (0,)), ((), ())), preferred_element_type=F32)


def _rms_fwd(x, g, name):
    T, D = x.shape
    tr = _tile(T, 512)

    def body(x_ref, g_ref, h_ref):
        xv = x_ref[...]
        r = lax.rsqrt(jnp.mean(xv * xv, axis=-1, keepdims=True) + RMS_EPS)
        h_ref[...] = ((xv * r) * g_ref[...]).astype(h_ref.dtype)

    return pl.pallas_call(
        body, out_shape=jax.ShapeDtypeStruct((T, D), BF16), grid=(T // tr,),
        in_specs=[pl.BlockSpec((tr, D), lambda i: (i, 0)), pl.BlockSpec((1, D), lambda i: (0, 0))],
        out_specs=pl.BlockSpec((tr, D), lambda i: (i, 0)),
        name=name, compiler_params=_params("parallel"))(x, g)


def _rms_bwd(dh, x, g, dres, name):
    T, D = x.shape
    tr = _tile(T, 256)

    def body(dh_ref, x_ref, g_ref, dres_ref, dx_ref, dxb_ref, dg_ref):
        i = pl.program_id(0)
        xv = x_ref[...]
        dhv = dh_ref[...]
        r = lax.rsqrt(jnp.mean(xv * xv, axis=-1, keepdims=True) + RMS_EPS)
        xh = xv * r
        q = dhv * g_ref[...]
        dx = dres_ref[...] + r * (q - xh * jnp.mean(q * xh, axis=-1, keepdims=True))
        dx_ref[...] = dx
        dxb_ref[...] = dx.astype(BF16)
        part = jnp.sum(dhv * xh, axis=0, keepdims=True)

        @pl.when(i == 0)
        def _():
            dg_ref[...] = part

        @pl.when(i > 0)
        def _():
            dg_ref[...] += part

    row = pl.BlockSpec((tr, D), lambda i: (i, 0))
    vec = pl.BlockSpec((1, D), lambda i: (0, 0))
    return pl.pallas_call(
        body,
        out_shape=(jax.ShapeDtypeStruct((T, D), F32), jax.ShapeDtypeStruct((T, D), BF16),
                   jax.ShapeDtypeStruct((1, D), F32)),
        grid=(T // tr,), in_specs=[row, row, vec, row], out_specs=(row, row, vec),
        name=name, compiler_params=_params("arbitrary"))(dh, x, g, dres)


def _loss_head(x, g, tgt, name):
    T, D = x.shape
    tr = _tile(T, 256)

    def body(x_ref, g_ref, t_ref, loss_ref, dx_ref, dxb_ref, dg_ref):
        i = pl.program_id(0)
        xv = x_ref[...]
        gv = g_ref[...]
        r = lax.rsqrt(jnp.mean(xv * xv, axis=-1, keepdims=True) + RMS_EPS)
        xh = xv * r
        err = xh * gv - t_ref[...]
        lpart = jnp.full((1, 128), 0.5 * jnp.sum(jnp.mean(err * err, axis=-1, keepdims=True)), F32)
        dy = err * (1.0 / D)
        q = dy * gv
        dx = r * (q - xh * jnp.mean(q * xh, axis=-1, keepdims=True))
        dx_ref[...] = dx
        dxb_ref[...] = dx.astype(BF16)
        gpart = jnp.sum(dy * xh, axis=0, keepdims=True)

        @pl.when(i == 0)
        def _():
            loss_ref[...] = lpart
            dg_ref[...] = gpart

        @pl.when(i > 0)
        def _():
            loss_ref[...] += lpart
            dg_ref[...] += gpart

    row = pl.BlockSpec((tr, D), lambda i: (i, 0))
    vec = pl.BlockSpec((1, D), lambda i: (0, 0))
    return pl.pallas_call(
        body,
        out_shape=(jax.ShapeDtypeStruct((1, 128), F32), jax.ShapeDtypeStruct((T, D), F32),
                   jax.ShapeDtypeStruct((T, D), BF16), jax.ShapeDtypeStruct((1, D), F32)),
        grid=(T // tr,), in_specs=[row, vec, row],
        out_specs=(pl.BlockSpec((1, 128), lambda i: (0, 0)), row, row, vec),
        name=name, compiler_params=_params("arbitrary"))(x, g, tgt)


def _mm_nn(a, w, l, *, res=None, out_dtype, tm=1024, tn=1024, tk=None, name):
    M, K = a.shape
    N = w.shape[2]
    tm, tn = _tile(M, tm), _tile(N, tn)
    tk = K if tk is None else _tile(K, tk)
    nk = K // tk
    has_res = res is not None

    def body(*refs):
        a_ref, w_ref = refs[0], refs[1]
        r_ref = refs[2] if has_res else None
        o_ref = refs[2 + has_res]
        part = jnp.dot(a_ref[...], w_ref[...], preferred_element_type=F32)

        def finish(acc):
            if has_res:
                acc = r_ref[...] + acc
            o_ref[...] = acc.astype(o_ref.dtype)

        if nk == 1:
            finish(part)
        else:
            acc_ref = refs[3 + has_res]
            k = pl.program_id(2)

            @pl.when(k == 0)
            def _():
                acc_ref[...] = part

            @pl.when(jnp.logical_and(k > 0, k < nk - 1))
            def _():
                acc_ref[...] += part

            @pl.when(k == nk - 1)
            def _():
                finish(acc_ref[...] + part)

    in_specs = [pl.BlockSpec((tm, tk), lambda i, j, k: (i, k)),
                pl.BlockSpec((None, tk, tn), lambda i, j, k: (l, k, j))]
    args = [a, w]
    if has_res:
        in_specs.append(pl.BlockSpec((tm, tn), lambda i, j, k: (i, j)))
        args.append(res)
    return pl.pallas_call(
        body, out_shape=jax.ShapeDtypeStruct((M, N), out_dtype), grid=(M // tm, N // tn, nk),
        in_specs=in_specs, out_specs=pl.BlockSpec((tm, tn), lambda i, j, k: (i, j)),
        scratch_shapes=[pltpu.VMEM((tm, tn), F32)] if nk > 1 else [],
        name=name, compiler_params=_params("parallel", "parallel", "arbitrary"))(*args)


def _mm_swiglu(h, wgu, l, *, tm=1024, tn=512, name):
    T, D = h.shape
    F = wgu.shape[2] // 2
    tm, tn = _tile(T, tm), _tile(F, tn)
    nf = F // tn

    def body(h_ref, wg_ref, wu_ref, gu_ref, act_ref):
        hv = h_ref[...]
        g = jnp.dot(hv, wg_ref[...], preferred_element_type=F32)
        u = jnp.dot(hv, wu_ref[...], preferred_element_type=F32)
        act_ref[...] = ((g * _sigmoid(g)) * u).astype(BF16)
        gu_ref[0] = g.astype(BF16)
        gu_ref[1] = u.astype(BF16)

    return pl.pallas_call(
        body,
        out_shape=(jax.ShapeDtypeStruct((2, T, F), BF16), jax.ShapeDtypeStruct((T, F), BF16)),
        grid=(T // tm, nf),
        in_specs=[pl.BlockSpec((tm, D), lambda i, j: (i, 0)),
                  pl.BlockSpec((None, D, tn), lambda i, j: (l, 0, j)),
                  pl.BlockSpec((None, D, tn), lambda i, j: (l, 0, j + nf))],
        out_specs=(pl.BlockSpec((2, tm, tn), lambda i, j: (0, i, j)),
                   pl.BlockSpec((tm, tn), lambda i, j: (i, j))),
        name=name, compiler_params=_params("parallel", "parallel"))(h, wgu, wgu)


def _mm_nt(a, w, l, *, out_dtype, tm=1024, tn=1024, tk=None, name):
    M, K = a.shape
    N = w.shape[1]
    tm, tn = _tile(M, tm), _tile(N, tn)
    tk = K if tk is None else _tile(K, tk)
    nk = K // tk

    def body(*refs):
        a_ref, w_ref, o_ref = refs[0], refs[1], refs[2]
        part = _nt(a_ref[...], w_ref[...])
        if nk == 1:
            o_ref[...] = part.astype(o_ref.dtype)
        else:
            acc_ref = refs[3]
            k = pl.program_id(2)

            @pl.when(k == 0)
            def _():
                acc_ref[...] = part

            @pl.when(jnp.logical_and(k > 0, k < nk - 1))
            def _():
                acc_ref[...] += part

            @pl.when(k == nk - 1)
            def _():
                o_ref[...] = (acc_ref[...] + part).astype(o_ref.dtype)

    return pl.pallas_call(
        body, out_shape=jax.ShapeDtypeStruct((M, N), out_dtype), grid=(M // tm, N // tn, nk),
        in_specs=[pl.BlockSpec((tm, tk), lambda i, j, k: (i, k)),
                  pl.BlockSpec((None, tn, tk), lambda i, j, k: (l, j, k))],
        out_specs=pl.BlockSpec((tm, tn), lambda i, j, k: (i, j)),
        scratch_shapes=[pltpu.VMEM((tm, tn), F32)] if nk > 1 else [],
        name=name, compiler_params=_params("parallel", "parallel", "arbitrary"))(a, w)


def _mm_nt_swiglu_bwd(dxb, wdown, l, gu, *, tm=1024, tn=512, name):
    T, D = dxb.shape
    F = wdown.shape[1]
    tm, tn = _tile(T, tm), _tile(F, tn)

    def body(dx_ref, w_ref, gu_ref, dgu_ref):
        da = _nt(dx_ref[...], w_ref[...])
        g = gu_ref[0].astype(F32)
        u = gu_ref[1].astype(F32)
        s = _sigmoid(g)
        dgu_ref[0] = ((da * u) * (s * (1.0 + g * (1.0 - s)))).astype(BF16)
        dgu_ref[1] = (da * (g * s)).astype(BF16)

    blk3 = pl.BlockSpec((2, tm, tn), lambda i, j: (0, i, j))
    return pl.pallas_call(
        body, out_shape=jax.ShapeDtypeStruct((2, T, F), BF16), grid=(T // tm, F // tn),
        in_specs=[pl.BlockSpec((tm, D), lambda i, j: (i, 0)),
                  pl.BlockSpec((None, tn, D), lambda i, j: (l, j, 0)), blk3],
        out_specs=blk3, name=name, compiler_params=_params("parallel", "parallel"))(dxb, wdown, gu)


def _mm_nt_dgu(dgu, wgu, l, *, tm=1024, tn=1024, tk=1408, name):
    _, T, F = dgu.shape
    D = wgu.shape[1]
    tm, tn, tk = _tile(T, tm), _tile(D, tn), _tile(F, tk)
    nkf = F // tk
    nk = 2 * nkf

    def body(a_ref, w_ref, o_ref, acc_ref):
        k = pl.program_id(2)
        part = _nt(a_ref[...], w_ref[...])

        @pl.when(k == 0)
        def _():
            acc_ref[...] = part

        @pl.when(jnp.logical_and(k > 0, k < nk - 1))
        def _():
            acc_ref[...] += part

        @pl.when(k == nk - 1)
        def _():
            o_ref[...] = acc_ref[...] + part

    return pl.pallas_call(
        body, out_shape=jax.ShapeDtypeStruct((T, D), F32), grid=(T // tm, D // tn, nk),
        in_specs=[pl.BlockSpec((None, tm, tk), lambda i, j, k: (k // nkf, i, k % nkf)),
                  pl.BlockSpec((None, tn, tk), lambda i, j, k: (l, j, k))],
        out_specs=pl.BlockSpec((tm, tn), lambda i, j, k: (i, j)),
        scratch_shapes=[pltpu.VMEM((tm, tn), F32)],
        name=name, compiler_params=_params("parallel", "parallel", "arbitrary"))(dgu, wgu)


def _mm_tn(a, g, l, n_layers, prev, *, tkw=512, tnw=1024, name):
    T, Kw = a.shape
    pair = g.ndim == 3
    Nw = 2 * g.shape[2] if pair else g.shape[1]
    tkw = _tile(Kw, tkw)
    tnw = _tile(g.shape[2] if pair else Nw, tnw)
    nf = (Nw // 2) // tnw if pair else 0

    def body(*refs):
        a_ref, g_ref, o_ref = refs[0], refs[1], refs[-1]
        o_ref[...] = _tn(a_ref[...], g_ref[...]).astype(o_ref.dtype)

    if pair:
        g_spec = pl.BlockSpec((None, T, tnw), lambda i, j: (j // nf, 0, j % nf))
    else:
        g_spec = pl.BlockSpec((T, tnw), lambda i, j: (0, j))
    in_specs = [pl.BlockSpec((T, tkw), lambda i, j: (0, i)), g_spec]
    args = [a, g]
    aliases = {}
    if prev is not None:
        in_specs.append(pl.BlockSpec(memory_space=pl.ANY))
        args.append(prev)
        aliases = {2: 0}
    return pl.pallas_call(
        body, out_shape=jax.ShapeDtypeStruct((n_layers, Kw, Nw), BF16), grid=(Kw // tkw, Nw // tnw),
        in_specs=in_specs, out_specs=pl.BlockSpec((None, tkw, tnw), lambda i, j: (l, i, j)),
        input_output_aliases=aliases,
        name=name, compiler_params=_params("parallel", "parallel"))(*args)


def _mixer_specs(T, A, tr):
    nin = 5 * A
    nb = tr // HALO
    last = T // HALO - 1
    prev = pl.BlockSpec((HALO, nin), lambda i: (jnp.maximum(i * nb - 1, 0), 0))
    cur = pl.BlockSpec((tr, nin), lambda i: (i, 0))
    nxt = pl.BlockSpec((HALO, nin), lambda i: (jnp.minimum((i + 1) * nb, last), 0))
    return prev, cur, nxt


def _conv(p_ext, cw_ref):
    return (cw_ref[2:3, :] * p_ext + cw_ref[1:2, :] * pltpu.roll(p_ext, 1, 0)
            + cw_ref[0:1, :] * pltpu.roll(p_ext, 2, 0))


def _mixer_fwd(z, lng, lnb, wm, bfull, cw, gn, name):
    T, nin = z.shape
    A = nin // 5
    H = A // HEAD_DIM
    tr = _tile(T, 256)
    nblk = tr // HEAD_DIM

    def body(zp_ref, z_ref, lng_ref, lnb_ref, wm_ref, bf_ref, cw_ref, gn_ref, y_ref, vn_ref, mix_ref):
        i = pl.program_id(0)
        u = _gelu(z_ref[:, 0:A].astype(F32))
        vg = _gelu(z_ref[:, A:2 * A].astype(F32))
        xc = vg - jnp.mean(vg, axis=-1, keepdims=True)
        rstd = lax.rsqrt(jnp.mean(xc * xc, axis=-1, keepdims=True) + LN_EPS)
        vn_ref[...] = ((xc * rstd) * lng_ref[...] + lnb_ref[...]).astype(BF16)
        for cb in range(nblk):
            rows = slice(cb * HEAD_DIM, (cb + 1) * HEAD_DIM)
            for h in range(H):
                cols = slice(h * HEAD_DIM, (h + 1) * HEAD_DIM)
                mix_ref[rows, cols] = jnp.dot(wm_ref[h], vn_ref[rows, cols],
                                              preferred_element_type=F32) + bf_ref[h]
        ya = u * mix_ref[...]
        ra = lax.rsqrt(jnp.mean(ya * ya, axis=-1, keepdims=True) + RMS_EPS)
        y_ref[:, 0:A] = ((ya * ra) * gn_ref[:, 0:A]).astype(BF16)

        p_prev = zp_ref[:, 3 * A:4 * A].astype(F32) * zp_ref[:, 4 * A:5 * A].astype(F32)
        p_prev = jnp.where(i > 0, p_prev, 0.0)
        p_cur = z_ref[:, 3 * A:4 * A].astype(F32) * z_ref[:, 4 * A:5 * A].astype(F32)
        cv = _conv(jnp.concatenate([p_prev, p_cur], axis=0), cw_ref)[HALO:]
        yb = z_ref[:, 2 * A:3 * A].astype(F32) * cv
        rb = lax.rsqrt(jnp.mean(yb * yb, axis=-1, keepdims=True) + RMS_EPS)
        y_ref[:, A:2 * A] = ((yb * rb) * gn_ref[:, A:2 * A]).astype(BF16)

    prev, cur, _ = _mixer_specs(T, A, tr)
    full = lambda shape: pl.BlockSpec(shape, lambda i: (0,) * len(shape))
    return pl.pallas_call(
        body, out_shape=jax.ShapeDtypeStruct((T, 2 * A), BF16), grid=(T // tr,),
        in_specs=[prev, cur, full((1, A)), full((1, A)), full((H, HEAD_DIM, HEAD_DIM)),
                  full((H, HEAD_DIM, HEAD_DIM)), full((8, A)), full((1, 2 * A))],
        out_specs=pl.BlockSpec((tr, 2 * A), lambda i: (i, 0)),
        scratch_shapes=[pltpu.VMEM((tr, A), BF16), pltpu.VMEM((tr, A), F32)],
        name=name, compiler_params=_params("parallel"))(z, z, lng, lnb, wm, bfull, cw, gn)


def _mixer_bwd(z, dy, lng, lnb, wm, wmt, bfull, cw, gn, name):
    T, nin = z.shape
    A = nin // 5
    H = A // HEAD_DIM
    tr = _tile(T, 256)
    nblk = tr // HEAD_DIM
    ngrid = T // tr
    next_ = tr + 2 * HALO

    def body(zp_ref, z_ref, zn_ref, dy_ref, dyn_ref, lng_ref, lnb_ref, wm_ref, wmt_ref, bf_ref, cw_ref, gn_ref,
             dz_ref, dgn_ref, dlng_ref, dlnb_ref, dws_ref, dbs_ref, dcw_ref,
             vn_ref, mix_ref, dmix_ref, dvn_ref):
        i = pl.program_id(0)

        @pl.when(i == 0)
        def _():
            dgn_ref[...] = jnp.zeros_like(dgn_ref)
            dlng_ref[...] = jnp.zeros_like(dlng_ref)
            dlnb_ref[...] = jnp.zeros_like(dlnb_ref)
            dws_ref[...] = jnp.zeros_like(dws_ref)
            dbs_ref[...] = jnp.zeros_like(dbs_ref)
            dcw_ref[...] = jnp.zeros_like(dcw_ref)

        u, du_dz = _gelu_and_grad(z_ref[:, 0:A].astype(F32))
        vg, dv_dz = _gelu_and_grad(z_ref[:, A:2 * A].astype(F32))
        xc = vg - jnp.mean(vg, axis=-1, keepdims=True)
        rstd = lax.rsqrt(jnp.mean(xc * xc, axis=-1, keepdims=True) + LN_EPS)
        vhat = xc * rstd
        vn_ref[...] = (vhat * lng_ref[...] + lnb_ref[...]).astype(BF16)
        for cb in range(nblk):
            rows = slice(cb * HEAD_DIM, (cb + 1) * HEAD_DIM)
            for h in range(H):
                cols = slice(h * HEAD_DIM, (h + 1) * HEAD_DIM)
                mix_ref[rows, cols] = jnp.dot(wm_ref[h], vn_ref[rows, cols],
                                              preferred_element_type=F32) + bf_ref[h]
        mixed = mix_ref[...]
        ya = u * mixed
        ra = lax.rsqrt(jnp.mean(ya * ya, axis=-1, keepdims=True) + RMS_EPS)
        yha = ya * ra
        dyan = dy_ref[:, 0:A]
        dgn_ref[:, 0:A] += jnp.sum(dyan * yha, axis=0, keepdims=True)
        qa = dyan * gn_ref[:, 0:A]
        dya = ra * (qa - yha * jnp.mean(qa * yha, axis=-1, keepdims=True))
        dz_ref[:, 0:A] = ((dya * mixed) * du_dz).astype(BF16)
        dmix_ref[...] = (dya * u).astype(BF16)

        ii = lax.broadcasted_iota(jnp.int32, (HEAD_DIM, HEAD_DIM), 0)
        jj = lax.broadcasted_iota(jnp.int32, (HEAD_DIM, HEAD_DIM), 1)
        mask = (jj // CHUNK <= ii // CHUNK).astype(F32)
        ones = jnp.ones((8, HEAD_DIM), BF16)
        for h in range(H):
            cols = slice(h * HEAD_DIM, (h + 1) * HEAD_DIM)
            dws = jnp.zeros((HEAD_DIM, HEAD_DIM), F32)
            dbs = jnp.zeros((8, HEAD_DIM), F32)
            for cb in range(nblk):
                rows = slice(cb * HEAD_DIM, (cb + 1) * HEAD_DIM)
                dm = dmix_ref[rows, cols]
                dws = dws + _nt(dm, vn_ref[rows, cols])
                dbs = dbs + _nt(ones, dm)
                dvn_ref[rows, cols] = jnp.dot(wmt_ref[h], dm, preferred_element_type=F32)
            dws_ref[h] += dws * mask
            dbs_ref[h] += dbs
        dvn = dvn_ref[...]
        dlnb_ref[...] += jnp.sum(dvn, axis=0, keepdims=True)
        dlng_ref[...] += jnp.sum(dvn * vhat, axis=0, keepdims=True)
        dvh = dvn * lng_ref[...]
        dvg = rstd * (dvh - jnp.mean(dvh, axis=-1, keepdims=True)
                      - vhat * jnp.mean(dvh * vhat, axis=-1, keepdims=True))
        dz_ref[:, A:2 * A] = (dvg * dv_dz).astype(BF16)

        def ext(lo):
            mid = z_ref[:, lo:lo + A].astype(F32)
            return jnp.concatenate([zp_ref[:, lo:lo + A].astype(F32), mid, zn_ref[:, lo:lo + A].astype(F32)], axis=0)

        zb, zc, zh = ext(2 * A), ext(3 * A), ext(4 * A)
        row = lax.broadcasted_iota(jnp.int32, (next_, 1), 0)
        p = zc * zh
        p = jnp.where(jnp.logical_and(row < HALO, i == 0), 0.0, p)
        cv = _conv(p, cw_ref)
        yb = zb * cv
        rb = lax.rsqrt(jnp.mean(yb * yb, axis=-1, keepdims=True) + RMS_EPS)
        yhb = yb * rb
        dyn_rows = jnp.where(i < ngrid - 1, dyn_ref[:, A:2 * A], 0.0)
        dybn = jnp.concatenate([jnp.zeros((HALO, A), F32), dy_ref[:, A:2 * A], dyn_rows], axis=0)
        ctr = slice(HALO, HALO + tr)
        dgn_ref[:, A:2 * A] += jnp.sum((dybn * yhb)[ctr], axis=0, keepdims=True)
        qb = dybn * gn_ref[:, A:2 * A]
        dyb = rb * (qb - yhb * jnp.mean(qb * yhb, axis=-1, keepdims=True))
        dcv = dyb * zb
        dp = (cw_ref[2:3, :] * dcv + cw_ref[1:2, :] * pltpu.roll(dcv, next_ - 1, 0)
              + cw_ref[0:1, :] * pltpu.roll(dcv, next_ - 2, 0))
        dz_ref[:, 2 * A:3 * A] = (dyb * cv)[ctr].astype(BF16)
        dz_ref[:, 3 * A:4 * A] = (dp * zh)[ctr].astype(BF16)
        dz_ref[:, 4 * A:5 * A] = (dp * zc)[ctr].astype(BF16)
        dcw_ref[2:3, :] += jnp.sum((dcv * p)[ctr], axis=0, keepdims=True)
        dcw_ref[1:2, :] += jnp.sum((dcv * pltpu.roll(p, 1, 0))[ctr], axis=0, keepdims=True)
        dcw_ref[0:1, :] += jnp.sum((dcv * pltpu.roll(p, 2, 0))[ctr], axis=0, keepdims=True)

    prev, cur, nxt = _mixer_specs(T, A, tr)
    nb = tr // HALO
    dy_cur = pl.BlockSpec((tr, 2 * A), lambda i: (i, 0))
    dy_nxt = pl.BlockSpec((HALO, 2 * A), lambda i: (jnp.minimum((i + 1) * nb, T // HALO - 1), 0))
    full = lambda shape: pl.BlockSpec(shape, lambda i: (0,) * len(shape))
    hh = (H, HEAD_DIM, HEAD_DIM)
    return pl.pallas_call(
        body,
        out_shape=(jax.ShapeDtypeStruct((T, nin), BF16), jax.ShapeDtypeStruct((1, 2 * A), F32),
                   jax.ShapeDtypeStruct((1, A), F32), jax.ShapeDtypeStruct((1, A), F32),
                   jax.ShapeDtypeStruct(hh, F32), jax.ShapeDtypeStruct((H, 8, HEAD_DIM), F32),
                   jax.ShapeDtypeStruct((8, A), F32)),
        grid=(ngrid,),
        in_specs=[prev, cur, nxt, dy_cur, dy_nxt, full((1, A)), full((1, A)), full(hh), full(hh), full(hh),
                  full((8, A)), full((1, 2 * A))],
        out_specs=(pl.BlockSpec((tr, nin), lambda i: (i, 0)), full((1, 2 * A)), full((1, A)), full((1, A)),
                   full(hh), full((H, 8, HEAD_DIM)), full((8, A))),
        scratch_shapes=[pltpu.VMEM((tr, A), BF16), pltpu.VMEM((tr, A), F32), pltpu.VMEM((tr, A), BF16),
                        pltpu.VMEM((tr, A), F32)],
        name=name, compiler_params=_params("arbitrary"))(z, z, z, dy, dy, lng, lnb, wm, wmt, bfull, cw, gn)


def _blocks3(shape, budget_elems=256 * 1024):
    _, R, C = shape
    tr = R
    while tr * C > budget_elems and tr % 2 == 0 and (tr // 2) % 16 == 0:
        tr //= 2
    return tr


def _prefetch_call(body, pos, args, *, out_shape, grid, in_specs, out_specs, aliases=None, name):
    return pl.pallas_call(
        body, out_shape=out_shape,
        grid_spec=pltpu.PrefetchScalarGridSpec(num_scalar_prefetch=1, grid=grid, in_specs=in_specs,
                                               out_specs=out_specs),
        input_output_aliases=aliases or {}, name=name,
        compiler_params=_params(*(["parallel"] * len(grid))))(pos, *args)


def _cast_place(w, geo, pos, prev, name):
    L = w.shape[0]
    if geo.axis == 1:
        tr = _blocks3((L, geo.rows, geo.size), 512 * 1024)
        grid = (L, geo.rows // tr)
        blk = (None, tr, geo.size)
        q = geo.base // geo.size
        out_map = lambda l, i, p: (l, i, q + p[0])
    else:
        tr = _blocks3((L, geo.size, geo.cols), 512 * 1024)
        grid = (L, geo.size // tr)
        blk = (None, tr, geo.cols)
        nb = geo.size // tr
        out_map = lambda l, i, p: (l, p[0] * nb + i, 0)

    def body(p_ref, w_ref, *rest):
        rest[-1][...] = w_ref[...].astype(BF16)

    in_specs = [pl.BlockSpec(blk, lambda l, i, p: (l, i, 0))]
    args = [w]
    aliases = None
    if prev is not None:
        in_specs.append(ANY)
        args.append(prev)
        aliases = {2: 0}
    return _prefetch_call(body, pos, args, out_shape=jax.ShapeDtypeStruct((L, geo.rows, geo.cols), BF16), grid=grid,
                          in_specs=in_specs, out_specs=pl.BlockSpec(blk, out_map), aliases=aliases, name=name)


def _pair_sum(dw, recv, geo, pos, name):
    L = dw.shape[0]
    hs = geo.half_shape(L)
    tr = _blocks3(hs, 512 * 1024)
    nb = hs[1] // tr
    blk = (None, tr, hs[2])
    if geo.axis == 1:
        own_map = lambda l, i, p: (l, p[1] * nb + i, 0)
    else:
        own_map = lambda l, i, p: (l, i, p[1])

    def body(p_ref, a_ref, b_ref, o_ref):
        o_ref[...] = (a_ref[...].astype(F32) + b_ref[...].astype(F32)).astype(BF16)

    same = pl.BlockSpec(blk, lambda l, i, p: (l, i, 0))
    return _prefetch_call(body, pos, [dw, recv], out_shape=jax.ShapeDtypeStruct(hs, BF16), grid=(L, nb),
                          in_specs=[pl.BlockSpec(blk, own_map), same], out_specs=same, name=name)


def _chip_sum(p, recv, geo, pos, name):
    L = p.shape[0]
    ss = geo.shard_half_shape(L)
    tr = _blocks3(ss, 256 * 1024)
    nb = ss[1] // tr
    blk = (None, tr, ss[2])
    if geo.axis == 1:
        q = geo.base // geo.size
        own_map = lambda l, i, p_: (l, i, q + p_[0])
        out_map = lambda l, i, p_: (l, p_[1] * nb + i, 0)
    else:
        own_map = lambda l, i, p_: (l, p_[0] * nb + i, 0)
        out_map = lambda l, i, p_: (l, i, p_[1])

    def body(p_ref, o_ref, r_ref, out_ref):
        acc = o_ref[...].astype(F32)
        for k in range(N_CHIPS - 1):
            acc = acc + r_ref[k].astype(F32)
        out_ref[...] = acc

    return _prefetch_call(
        body, pos, [p, recv], out_shape=jax.ShapeDtypeStruct(geo.shard_shape(L), F32), grid=(L, nb),
        in_specs=[pl.BlockSpec(blk, own_map),
                  pl.BlockSpec((N_CHIPS - 1,) + blk, lambda l, i, p_: (0, l, i, 0))],
        out_specs=pl.BlockSpec(blk, out_map), name=name)


def _adamw_math(w, g, m, v):
    m = ADAM_B1 * m + (1.0 - ADAM_B1) * g
    v = ADAM_B2 * v + (1.0 - ADAM_B2) * (g * g)
    m_hat = m / (1.0 - ADAM_B1 ** ADAM_STEP)
    v_hat = v / (1.0 - ADAM_B2 ** ADAM_STEP)
    delta = -ADAM_LR * (m_hat / (jnp.sqrt(v_hat) + ADAM_EPS) + ADAM_WD * w)
    return delta, m, v


def _adamw(w, g, m, v, name):
    L, R, C = w.shape
    tr = _blocks3(w.shape)

    def body(w_ref, g_ref, m_ref, v_ref, go_ref, d_ref, mo_ref, vo_ref):
        gv = g_ref[...]
        d, mn, vn = _adamw_math(w_ref[...], gv, m_ref[...], v_ref[...])
        go_ref[...] = gv
        d_ref[...] = d
        mo_ref[...] = mn
        vo_ref[...] = vn

    blk = pl.BlockSpec((None, tr, C), lambda l, i: (l, i, 0))
    sds = jax.ShapeDtypeStruct(w.shape, F32)
    return pl.pallas_call(
        body, out_shape=(sds, sds, sds, sds), grid=(L, R // tr), in_specs=[blk] * 4, out_specs=(blk,) * 4,
        name=name, compiler_params=_params("parallel", "parallel"))(w, g, m, v)


def _adamw_flat(w, g, m, v, name):
    R, C = w.shape
    tr = _tile(R, 1024) if R % 128 == 0 else R

    def body(w_ref, g_ref, m_ref, v_ref, d_ref, mo_ref, vo_ref):
        d, mn, vn = _adamw_math(w_ref[...], g_ref[...], m_ref[...], v_ref[...])
        d_ref[...] = d
        mo_ref[...] = mn
        vo_ref[...] = vn

    blk = pl.BlockSpec((tr, C), lambda i: (i, 0))
    sds = jax.ShapeDtypeStruct(w.shape, F32)
    return pl.pallas_call(
        body, out_shape=(sds, sds, sds), grid=(R // tr,), in_specs=[blk] * 4, out_specs=(blk,) * 3,
        name=name, compiler_params=_params("parallel"))(w, g, m, v)


class _Geom:
    def __init__(self, rows, cols, axis, size, base=0):
        self.rows, self.cols, self.axis, self.size, self.base = rows, cols, axis, size, base

    def in_full(self, j, h):
        if self.axis == 1:
            r = (0, self.rows) if h is None else (h * (self.rows // 2), self.rows // 2)
            return r, (self.base + j * self.size, self.size)
        c = (0, self.cols) if h is None else (h * (self.cols // 2), self.cols // 2)
        return (self.base + j * self.size, self.size), c

    def in_shard(self, h):
        if self.axis == 1:
            return (h * (self.rows // 2), self.rows // 2), (0, self.size)
        return (0, self.size), (h * (self.cols // 2), self.cols // 2)

    def half_of_full(self, h):
        if self.axis == 1:
            return (h * (self.rows // 2), self.rows // 2), (0, self.cols)
        return (0, self.rows), (h * (self.cols // 2), self.cols // 2)

    def in_half(self, j):
        if self.axis == 1:
            return (0, self.rows // 2), (self.base + j * self.size, self.size)
        return (self.base + j * self.size, self.size), (0, self.cols // 2)

    def half_shape(self, L):
        return (L, self.rows // 2, self.cols) if self.axis == 1 else (L, self.rows, self.cols // 2)

    def shard_half_shape(self, L):
        return (L, self.rows // 2, self.size) if self.axis == 1 else (L, self.size, self.cols // 2)

    def shard_shape(self, L):
        return (L, self.rows, self.size) if self.axis == 1 else (L, self.size, self.cols)


def _at(ref, region):
    (r0, rn), (c0, cn) = region
    if not isinstance(r0, int):
        r0 = pl.multiple_of(r0, 16)
    if not isinstance(c0, int):
        c0 = pl.multiple_of(c0, 128)
    return ref.at[:, pl.ds(r0, rn), pl.ds(c0, cn)]


def _whole(shape):
    return (0, shape[1]), (0, shape[2])


def _split_rows(region, n_layers, itemsize):
    (r0, rn), cols = region
    want = max(1, (n_layers * rn * cols[1] * itemsize) // DMA_CHUNK_BYTES)
    n = 1
    for cand in range(1, want + 1):
        if rn % cand == 0 and (rn // cand) % 16 == 0:
            n = cand
    step = rn // n
    return [((r0 + i * step, step), cols) for i in range(n)]


class _Chunked:
    def __init__(self, make, src, src_reg, dst, dst_reg):
        n_layers, itemsize = src.shape[0], jnp.dtype(src.dtype).itemsize
        self.whole = make(_at(src, src_reg), _at(dst, dst_reg))
        self.parts = [make(_at(src, a), _at(dst, b)) for a, b in
                      zip(_split_rows(src_reg, n_layers, itemsize), _split_rows(dst_reg, n_layers, itemsize))]

    def start(self):
        for p in self.parts:
            p.start()


def _mesh_place():
    x, y, c = lax.axis_index("x"), lax.axis_index("y"), lax.axis_index("c")
    chips = [(1 - x, y), (x, 1 - y), (1 - x, 1 - y)]
    return x, y, c, 2 * x + y, chips


def _remote(ssem, rsem, dev):
    return lambda src, dst: pltpu.make_async_remote_copy(
        src_ref=src, dst_ref=dst, send_sem=ssem, recv_sem=rsem, device_id=dev, device_id_type=MESH)


def _local(sem):
    return lambda src, dst: pltpu.make_async_copy(src, dst, sem)


ANY = pl.BlockSpec(memory_space=pl.ANY)


def _gather_weights(placed, conv, geoms, L):
    names = ["in", "out", "gate", "up", "down"]
    full_of = {"in": 0, "out": 1, "gate": 2, "up": 2, "down": 3}
    g_in, g_out, g_gate, g_down = geoms["in"], geoms["out"], geoms["gate"], geoms["down"]
    full_shapes = [(L, g_in.rows, g_in.cols), (L, g_out.rows, g_out.cols),
                   (L, g_gate.rows, g_gate.cols), (L, g_down.rows, g_down.cols)]
    cb = conv.shape[2]
    nt = len(names)
    nk = N_CHIPS - 1

    def body(i_in, i_out, i_gu, i_down, s_conv, f_in, f_out, f_gu, f_down, f_conv,
             send_sems, recv_sems, fsend_sems, frecv_sems, local_sems):
        x, y, c, j, chips = _mesh_place()
        sibling = (x, y, 1 - c)
        fulls = [f_in, f_out, f_gu, f_down]
        sends, locals_ = [], []
        for t, nm in enumerate(names):
            geo, full = geoms[nm], fulls[full_of[nm]]
            for k, chip in enumerate(chips):
                cp = _Chunked(_remote(send_sems.at[nk * t + k], recv_sems.at[nk * t + k], (*chip, c)),
                              full, geo.in_full(j, c), full, geo.in_full(j, c))
                cp.start()
                sends.append(cp)
        conv_at = lambda jj: ((0, 8), (jj * cb, cb))
        lc = _Chunked(_local(local_sems.at[0]), s_conv, _whole(s_conv.shape), f_conv, conv_at(j))
        lc.start()
        locals_.append(lc)
        for k, chip in enumerate(chips):
            cp = _Chunked(_remote(send_sems.at[nk * nt + k], recv_sems.at[nk * nt + k], (*chip, c)),
                          s_conv, _whole(s_conv.shape), f_conv, conv_at(j))
            cp.start()
            sends.append(cp)
        for k, chip in enumerate(chips):
            jk = 2 * chip[0] + chip[1]
            for t, nm in enumerate(names):
                geo, full = geoms[nm], fulls[full_of[nm]]
                reg = geo.in_full(jk, c)
                _Chunked(_remote(send_sems.at[nk * t + k], recv_sems.at[nk * t + k], sibling),
                         full, reg, full, reg).whole.wait_recv()
                fw = _Chunked(_remote(fsend_sems.at[nk * t + k], frecv_sems.at[nk * t + k], sibling),
                              full, reg, full, reg)
                fw.start()
                sends.append(fw)
        for k, chip in enumerate(chips):
            jk = 2 * chip[0] + chip[1]
            for t, nm in enumerate(names):
                geo, full = geoms[nm], fulls[full_of[nm]]
                reg = geo.in_full(jk, 1 - c)
                _Chunked(_remote(fsend_sems.at[nk * t + k], frecv_sems.at[nk * t + k], sibling),
                         full, reg, full, reg).whole.wait_recv()
            _Chunked(_remote(send_sems.at[nk * nt + k], recv_sems.at[nk * nt + k], sibling),
                     f_conv, conv_at(jk), f_conv, conv_at(jk)).whole.wait_recv()
        for cp in sends:
            cp.whole.wait_send()
        for lc in locals_:
            lc.whole.wait()

    out_shape = [jax.ShapeDtypeStruct(s, BF16) for s in full_shapes]
    out_shape.append(jax.ShapeDtypeStruct((L, 8, N_CHIPS * cb), F32))
    return pl.pallas_call(
        body, out_shape=tuple(out_shape), in_specs=[ANY] * 5, out_specs=(ANY,) * 5,
        input_output_aliases={0: 0, 1: 1, 2: 2, 3: 3},
        scratch_shapes=[pltpu.SemaphoreType.DMA((nk * (nt + 1),)), pltpu.SemaphoreType.DMA((nk * (nt + 1),)),
                        pltpu.SemaphoreType.DMA((nk * nt,)), pltpu.SemaphoreType.DMA((nk * nt,)),
                        pltpu.SemaphoreType.DMA((1,))],
        name="gather_weights")(*placed, conv)


def _swap_halves(dws, geos, L):
    n = len(dws)

    def body(*refs):
        ins, recvs = refs[:n], refs[n:2 * n]
        send_sems, recv_sems = refs[2 * n:]
        x, y, c, _, _ = _mesh_place()
        sibling = (x, y, 1 - c)
        cps = []
        for t in range(n):
            cp = _Chunked(_remote(send_sems.at[t], recv_sems.at[t], sibling),
                          ins[t], geos[t].half_of_full(1 - c), recvs[t], _whole(recvs[t].shape))
            cp.start()
            cps.append(cp)
        for cp in cps:
            cp.whole.wait()

    halves = tuple(jax.ShapeDtypeStruct(g.half_shape(L), BF16) for g in geos)
    return pl.pallas_call(
        body, out_shape=halves, in_specs=[ANY] * n, out_specs=(ANY,) * n,
        scratch_shapes=[pltpu.SemaphoreType.DMA((n,)), pltpu.SemaphoreType.DMA((n,))],
        name="grad_swap_halves")(*dws)


def _exchange_shards(ps, geos, L):
    n = len(geos)
    nk = N_CHIPS - 1

    def body(*refs):
        ins, recvs = refs[:n], refs[n:2 * n]
        send_sems, recv_sems = refs[2 * n:]
        x, y, c, j, chips = _mesh_place()
        cps = []
        for t in range(n):
            for k, chip in enumerate(chips):
                jk = 2 * chip[0] + chip[1]
                dst = recvs[t].at[k]
                cp = _Chunked(_remote(send_sems.at[nk * t + k], recv_sems.at[nk * t + k], (*chip, c)),
                              ins[t], geos[t].in_half(jk), dst, _whole(dst.shape))
                cp.start()
                cps.append(cp)
        for cp in cps:
            cp.whole.wait()

    recvs = tuple(jax.ShapeDtypeStruct((nk,) + g.shard_half_shape(L), BF16) for g in geos)
    return pl.pallas_call(
        body, out_shape=recvs, in_specs=[ANY] * n, out_specs=(ANY,) * n,
        scratch_shapes=[pltpu.SemaphoreType.DMA((nk * n,)), pltpu.SemaphoreType.DMA((nk * n,))],
        name="grad_exchange_shards")(*ps)


def _join_halves(gs, geos, L):
    n = len(geos)

    def body(*refs):
        outs = refs[n:2 * n]
        send_sems, recv_sems = refs[2 * n:]
        x, y, c, _, _ = _mesh_place()
        sibling = (x, y, 1 - c)
        cps = []
        for t in range(n):
            mine = geos[t].in_shard(c)
            cp = _Chunked(_remote(send_sems.at[t], recv_sems.at[t], sibling), outs[t], mine, outs[t], mine)
            cp.start()
            cps.append(cp)
        for t, cp in enumerate(cps):
            cp.whole.wait_send()
            other = geos[t].in_shard(1 - c)
            _Chunked(_remote(send_sems.at[t], recv_sems.at[t], sibling),
                     outs[t], other, outs[t], other).whole.wait_recv()

    outs = tuple(jax.ShapeDtypeStruct(g.shard_shape(L), F32) for g in geos)
    return pl.pallas_call(
        body, out_shape=outs, in_specs=[ANY] * n, out_specs=(ANY,) * n,
        input_output_aliases={t: t for t in range(n)},
        scratch_shapes=[pltpu.SemaphoreType.DMA((n,)), pltpu.SemaphoreType.DMA((n,))],
        name="grad_join_halves")(*gs)


def _allreduce_small(s):
    R, C = s.shape

    def body(s_ref, o_ref, rbuf, send_sems, recv_sems):
        x, y, c = lax.axis_index("x"), lax.axis_index("y"), lax.axis_index("c")
        peers = [(x, y, 1 - c), (1 - x, y, c), (x, 1 - y, c)]
        o_ref[...] = s_ref[...]
        for k, peer in enumerate(peers):
            cp = _remote(send_sems.at[k], recv_sems.at[k], peer)(o_ref, rbuf.at[k])
            cp.start()
            cp.wait()
            o_ref[...] = o_ref[...] + rbuf[k]

    vm = pl.BlockSpec(memory_space=pltpu.VMEM)
    return pl.pallas_call(
        body, out_shape=jax.ShapeDtypeStruct((R, C), F32), in_specs=[vm], out_specs=vm,
        scratch_shapes=[pltpu.VMEM((3, R, C), F32), pltpu.SemaphoreType.DMA((3,)), pltpu.SemaphoreType.DMA((3,))],
        name="allreduce_small", compiler_params=pltpu.CompilerParams(vmem_limit_bytes=V7X_VMEM_LIMIT))(s)


def _pack(pieces):
    rows = []
    for p in pieces:
        flat = p.reshape(-1)
        pad = (-flat.shape[0]) % 1024
        rows.append(jnp.pad(flat, (0, pad)).reshape(-1, 128))
    return jnp.concatenate(rows, axis=0)


def _unpack(buf, shapes):
    out, r = [], 0
    for shp in shapes:
        n = math.prod(shp)
        nr = -(-n // 1024) * 8
        out.append(buf[r:r + nr].reshape(-1)[:n].reshape(shp))
        r += nr
    return out


def kernel(x, norm1_g, w_in, gmlp_ln_g, gmlp_ln_b, w_spatial, b_spatial, conv_w, group_norm_g, w_out, norm2_g, w_gate, w_up, w_down, final_norm_g, loss_target, m_norm1_g, m_w_in, m_gmlp_ln_g, m_gmlp_ln_b, m_w_spatial, m_b_spatial, m_conv_w, m_group_norm_g, m_w_out, m_norm2_g, m_w_gate, m_w_up, m_w_down, m_final_norm_g, v_norm1_g, v_w_in, v_gmlp_ln_g, v_gmlp_ln_b, v_w_spatial, v_b_spatial, v_conv_w, v_group_norm_g, v_w_out, v_norm2_g, v_w_gate, v_w_up, v_w_down, v_final_norm_g):
    L, D, n_in = w_in.shape
    T = x.shape[1]
    nin = N_CHIPS * n_in
    A = nin // 5
    H = A // HEAD_DIM
    n_f = w_gate.shape[2]
    F = N_CHIPS * n_f
    n_o = w_out.shape[1]
    cb = conv_w.shape[2]
    assert A == H * HEAD_DIM and T % 256 == 0 and N_CHIPS * n_o == D and N_CHIPS * cb == A

    geoms = {
        "in": _Geom(D, nin, 1, n_in),
        "out": _Geom(D, D, 0, n_o),
        "gate": _Geom(D, 2 * F, 1, n_f, 0),
        "up": _Geom(D, 2 * F, 1, n_f, F),
        "down": _Geom(F, D, 0, n_f),
    }

    pos = jnp.stack([2 * lax.axis_index("x") + lax.axis_index("y"), lax.axis_index("c")]).astype(jnp.int32)
    placed = [_cast_place(w_in, geoms["in"], pos, None, "place_w_in"),
              _cast_place(w_out, geoms["out"], pos, None, "place_w_out"),
              _cast_place(w_up, geoms["up"], pos,
                          _cast_place(w_gate, geoms["gate"], pos, None, "place_w_gate"), "place_w_up"),
              _cast_place(w_down, geoms["down"], pos, None, "place_w_down")]
    conv_pad = jnp.pad(conv_w, ((0, 0), (0, 8 - CONV_TAPS), (0, 0)))
    W_in, W_out, W_gu, W_down, conv_full = _gather_weights(placed, conv_pad, geoms, L)

    frame = jnp.arange(HEAD_DIM)
    mask = (frame[None, :] // CHUNK) <= (frame[:, None] // CHUNK)
    row = lambda v: v.reshape(1, -1)

    xs = x[0]
    acts = []
    for l in range(L):
        wm = jnp.where(mask[None], w_spatial[l], 0.0).astype(BF16)
        wmt = jnp.swapaxes(wm, 1, 2)
        bfull = jnp.broadcast_to(b_spatial[l][:, :, None], (H, HEAD_DIM, HEAD_DIM))
        small = dict(lng=row(gmlp_ln_g[l]), lnb=row(gmlp_ln_b[l]), wm=wm, wmt=wmt, bfull=bfull,
                     cw=conv_full[l], gn=row(group_norm_g[l]))
        h = _rms_fwd(xs, row(norm1_g[l]), f"l{l}_rms1")
        z = _mm_nn(h, W_in, l, out_dtype=BF16, name=f"l{l}_mm_in")
        y = _mixer_fwd(z, small["lng"], small["lnb"], wm, bfull, small["cw"], small["gn"], f"l{l}_mixer")
        x1 = _mm_nn(y, W_out, l, res=xs, out_dtype=F32, name=f"l{l}_mm_out")
        h2 = _rms_fwd(x1, row(norm2_g[l]), f"l{l}_rms2")
        gu, act = _mm_swiglu(h2, W_gu, l, name=f"l{l}_mm_swiglu")
        x2 = _mm_nn(act, W_down, l, res=x1, out_dtype=F32, tk=1408, name=f"l{l}_mm_down")
        acts.append(dict(x=xs, h=h, z=z, y=y, x1=x1, h2=h2, gu=gu, act=act, small=small))
        xs = x2

    loss_vec, dx, dxb, dgf = _loss_head(xs, row(final_norm_g), loss_target[0], "loss_head")
    loss = lax.psum(loss_vec[0, 0], ("x", "y", "c"))

    dW_in = dW_out = dW_gu = dW_down = None
    small_grads = [None] * L
    for l in reversed(range(L)):
        a = acts[l]
        sm = a["small"]
        dgu = _mm_nt_swiglu_bwd(dxb, W_down, l, a["gu"], name=f"l{l}_bwd_down")
        dW_down = _mm_tn(a["act"], dxb, l, L, dW_down, name=f"l{l}_dw_down")
        dh2 = _mm_nt_dgu(dgu, W_gu, l, name=f"l{l}_bwd_gu")
        dW_gu = _mm_tn(a["h2"], dgu, l, L, dW_gu, tkw=1024, tnw=512, name=f"l{l}_dw_gu")
        dx1, dx1b, dg2 = _rms_bwd(dh2, a["x1"], row(norm2_g[l]), dx, f"l{l}_rms2_bwd")
        dy = _mm_nt(dx1b, W_out, l, out_dtype=F32, name=f"l{l}_bwd_out")
        dW_out = _mm_tn(a["y"], dx1b, l, L, dW_out, name=f"l{l}_dw_out")
        dz, dgn, dlng, dlnb, dws, dbs, dcw = _mixer_bwd(
            a["z"], dy, sm["lng"], sm["lnb"], sm["wm"], sm["wmt"], sm["bfull"], sm["cw"], sm["gn"],
            f"l{l}_mixer_bwd")
        dh = _mm_nt(dz, W_in, l, out_dtype=F32, tk=1280, name=f"l{l}_bwd_in")
        dW_in = _mm_tn(a["h"], dz, l, L, dW_in, name=f"l{l}_dw_in")
        dx, dxb, dg1 = _rms_bwd(dh, a["x"], row(norm1_g[l]), dx1, f"l{l}_rms1_bwd")
        small_grads[l] = [dg1[0], dlng[0], dlnb[0], dws, dbs[:, 0, :], dcw[:CONV_TAPS], dgn[0], dg2[0]]
    grad_x = dx[None]

    geo4 = [geoms["in"], geoms["out"], _Geom(D, 2 * F, 1, n_f), geoms["down"]]
    dws = [dW_in, dW_out, dW_gu, dW_down]
    sib_h = _swap_halves(dws, geo4, L)
    p_in, p_out, p_gu, p_down = [_pair_sum(d, s, g, pos, f"grad_pair_sum_{i}")
                                 for i, (d, s, g) in enumerate(zip(dws, sib_h, geo4))]
    geo5 = [geoms["in"], geoms["out"], geoms["gate"], geoms["up"], geoms["down"]]
    ps = [p_in, p_out, p_gu, p_gu, p_down]
    recv_p = _exchange_shards(ps, geo5, L)
    ghs = [_chip_sum(p, r, g, pos, f"grad_chip_sum_{i}") for i, (p, r, g) in enumerate(zip(ps, recv_p, geo5))]
    g_in, g_out, g_gate, g_up, g_down = _join_halves(ghs, geo5, L)
    big = {}
    for nm, w, g, m, v in (("w_in", w_in, g_in, m_w_in, v_w_in), ("w_out", w_out, g_out, m_w_out, v_w_out),
                           ("w_gate", w_gate, g_gate, m_w_gate, v_w_gate), ("w_up", w_up, g_up, m_w_up, v_w_up),
                           ("w_down", w_down, g_down, m_w_down, v_w_down)):
        big[nm] = _adamw(w, g, m, v, f"adamw_{nm}")

    pieces = [p for l in range(L) for p in small_grads[l]] + [dgf[0]]
    red = _allreduce_small(_pack(pieces))
    red_list = _unpack(red, [p.shape for p in pieces])
    per = len(small_grads[0])
    stack = lambda i: jnp.stack([red_list[l * per + i] for l in range(L)])
    g_small = {"norm1_g": stack(0), "gmlp_ln_g": stack(1), "gmlp_ln_b": stack(2), "w_spatial": stack(3),
               "b_spatial": stack(4), "group_norm_g": stack(6), "norm2_g": stack(7),
               "final_norm_g": red_list[L * per]}
    jchip = 2 * lax.axis_index("x") + lax.axis_index("y")
    g_small["conv_w"] = lax.dynamic_slice_in_dim(stack(5), jchip * cb, cb, axis=2)
    small_w = {"norm1_g": (norm1_g, m_norm1_g, v_norm1_g), "gmlp_ln_g": (gmlp_ln_g, m_gmlp_ln_g, v_gmlp_ln_g),
               "gmlp_ln_b": (gmlp_ln_b, m_gmlp_ln_b, v_gmlp_ln_b), "w_spatial": (w_spatial, m_w_spatial, v_w_spatial),
               "b_spatial": (b_spatial, m_b_spatial, v_b_spatial), "conv_w": (conv_w, m_conv_w, v_conv_w),
               "group_norm_g": (group_norm_g, m_group_norm_g, v_group_norm_g),
               "norm2_g": (norm2_g, m_norm2_g, v_norm2_g), "final_norm_g": (final_norm_g, m_final_norm_g, v_final_norm_g)}
    snames = list(small_w)
    sd, smn, svn = _adamw_flat(_pack([small_w[n][0] for n in snames]), _pack([g_small[n] for n in snames]),
                               _pack([small_w[n][1] for n in snames]), _pack([small_w[n][2] for n in snames]),
                               "adamw_small")
    sshapes = [small_w[n][0].shape for n in snames]
    sd, smn, svn = _unpack(sd, sshapes), _unpack(smn, sshapes), _unpack(svn, sshapes)
    small_out = {n: (g_small[n], sd[i], smn[i], svn[i]) for i, n in enumerate(snames)}

    order = ["norm1_g", "w_in", "gmlp_ln_g", "gmlp_ln_b", "w_spatial", "b_spatial", "conv_w", "group_norm_g",
             "w_out", "norm2_g", "w_gate", "w_up", "w_down", "final_norm_g"]
    res = {n: (big[n] if n in big else small_out[n]) for n in order}
    return (loss, grad_x, *[res[n][0] for n in order], *[res[n][1] for n in order],
            *[res[n][2] for n in order], *[res[n][3] for n in order])
```

```python
import math

import jax
import jax.numpy as jnp
from jax import lax
from jax.experimental import pallas as pl
from jax.experimental.pallas import tpu as pltpu

RMS_EPS = 1e-6
LN_EPS = 1e-5
HEAD_DIM = 128
CHUNK = 64
CONV_TAPS = 3
HALO = 16
ADAM_LR = 0.001
ADAM_B1 = 0.9
ADAM_B2 = 0.999
ADAM_EPS = 1e-08
ADAM_WD = 0.01
ADAM_STEP = 10
V7X_VMEM_LIMIT = 56 * 1024 * 1024
N_CHIPS = 4
DMA_CHUNK_BYTES = 2 * 1024 * 1024
MESH = pl.DeviceIdType.MESH
F32 = jnp.float32
BF16 = jnp.bfloat16
ANY = pl.BlockSpec(memory_space=pl.ANY)
HBM = pl.BlockSpec(memory_space=pltpu.HBM)
SEM = pl.BlockSpec(memory_space=pltpu.SEMAPHORE)
EFFECT = pltpu.SideEffectType.DATAFLOW_SIDE_EFFECTING


def _tile(n, pref):
    if n <= pref:
        return n
    best = None
    for t in range(128, pref + 1, 128):
        if n % t == 0:
            best = t
    assert best is not None, (n, pref)
    return best


def _rows_tile(rows, cols, budget_elems):
    tr = rows
    while tr * cols > budget_elems and tr % 2 == 0 and (tr // 2) % 16 == 0:
        tr //= 2
    return tr


def _params(*sem):
    return pltpu.CompilerParams(dimension_semantics=sem if sem else None,
                                vmem_limit_bytes=V7X_VMEM_LIMIT)


def _call(body, args, after, **kw):
    n, na = len(args), len(after)
    if na:
        inner = body

        def body(*refs):
            inner(*refs[:n], *refs[n + na:])

        kw["in_specs"] = list(kw["in_specs"]) + [ANY] * na
    return pl.pallas_call(body, **kw)(*args, *after)


def _prefetch_call(body, pos, args, after=(), *, out_shape, grid, in_specs, out_specs, aliases=None, name):
    n, na = 1 + len(args), len(after)
    if na:
        inner = body

        def body(*refs):
            inner(*refs[:n], *refs[n + na:])

    return pl.pallas_call(
        body, out_shape=out_shape,
        grid_spec=pltpu.PrefetchScalarGridSpec(num_scalar_prefetch=1, grid=grid,
                                               in_specs=list(in_specs) + [ANY] * na, out_specs=out_specs),
        input_output_aliases=aliases or {}, name=name,
        compiler_params=_params(*(["parallel"] * len(grid))))(pos, *args, *after)


def _gelu(x):
    c = math.sqrt(2.0 / math.pi)
    return 0.5 * x * (1.0 + jnp.tanh(c * (x + 0.044715 * x * x * x)))


def _gelu_and_grad(x):
    c = math.sqrt(2.0 / math.pi)
    x2 = x * x
    th = jnp.tanh(c * (x + 0.044715 * x * x2))
    val = 0.5 * x * (1.0 + th)
    grad = 0.5 * (1.0 + th) + 0.5 * x * (1.0 - th * th) * (c * (1.0 + 3.0 * 0.044715 * x2))
    return val, grad


def _sigmoid(x):
    return 1.0 / (1.0 + jnp.exp(-x))


def _nt(a, b):
    return lax.dot_general(a, b, (((1,), (1,)), ((), ())), preferred_element_type=F32)


def _tn(a, b):
    return lax.dot_general(a, b, (((0,), (0,)), ((), ())), preferred_element_type=F32)


def _rms_fwd(x, g, name, after=()):
    T, D = x.shape
    tr = _tile(T, 512)

    def body(x_ref, g_ref, h_ref):
        xv = x_ref[...]
        r = lax.rsqrt(jnp.mean(xv * xv, axis=-1, keepdims=True) + RMS_EPS)
        h_ref[...] = ((xv * r) * g_ref[...]).astype(h_ref.dtype)

    return _call(
        body, [x, g], after, out_shape=jax.ShapeDtypeStruct((T, D), BF16), grid=(T // tr,),
        in_specs=[pl.BlockSpec((tr, D), lambda i: (i, 0)), pl.BlockSpec((1, D), lambda i: (0, 0))],
        out_specs=pl.BlockSpec((tr, D), lambda i: (i, 0)),
        name=name, compiler_params=_params("parallel"))


def _rms_bwd(dh, x, g, dres, name, after=()):
    T, D = x.shape
    tr = _tile(T, 256)

    def body(dh_ref, x_ref, g_ref, dres_ref, dx_ref, dxb_ref, dg_ref):
        i = pl.program_id(0)
        xv = x_ref[...]
        dhv = dh_ref[...]
        r = lax.rsqrt(jnp.mean(xv * xv, axis=-1, keepdims=True) + RMS_EPS)
        xh = xv * r
        q = dhv * g_ref[...]
        dx = dres_ref[...] + r * (q - xh * jnp.mean(q * xh, axis=-1, keepdims=True))
        dx_ref[...] = dx
        dxb_ref[...] = dx.astype(BF16)
        part = jnp.sum(dhv * xh, axis=0, keepdims=True)

        @pl.when(i == 0)
        def _():
            dg_ref[...] = part

        @pl.when(i > 0)
        def _():
            dg_ref[...] += part

    row = pl.BlockSpec((tr, D), lambda i: (i, 0))
    vec = pl.BlockSpec((1, D), lambda i: (0, 0))
    return _call(
        body, [dh, x, g, dres], after,
        out_shape=(jax.ShapeDtypeStruct((T, D), F32), jax.ShapeDtypeStruct((T, D), BF16),
                   jax.ShapeDtypeStruct((1, D), F32)),
        grid=(T // tr,), in_specs=[row, row, vec, row], out_specs=(row, row, vec),
        name=name, compiler_params=_params("arbitrary"))


def _loss_head(x, g, tgt, name):
    T, D = x.shape
    tr = _tile(T, 256)

    def body(x_ref, g_ref, t_ref, loss_ref, dx_ref, dxb_ref, dg_ref):
        i = pl.program_id(0)
        xv = x_ref[...]
        gv = g_ref[...]
        r = lax.rsqrt(jnp.mean(xv * xv, axis=-1, keepdims=True) + RMS_EPS)
        xh = xv * r
        err = xh * gv - t_ref[...]
        lpart = jnp.full((1, 128), 0.5 * jnp.sum(jnp.mean(err * err, axis=-1, keepdims=True)), F32)
        dy = err * (1.0 / D)
        q = dy * gv
        dx = r * (q - xh * jnp.mean(q * xh, axis=-1, keepdims=True))
        dx_ref[...] = dx
        dxb_ref[...] = dx.astype(BF16)
        gpart = jnp.sum(dy * xh, axis=0, keepdims=True)

        @pl.when(i == 0)
        def _():
            loss_ref[...] = lpart
            dg_ref[...] = gpart

        @pl.when(i > 0)
        def _():
            loss_ref[...] += lpart
            dg_ref[...] += gpart

    row = pl.BlockSpec((tr, D), lambda i: (i, 0))
    vec = pl.BlockSpec((1, D), lambda i: (0, 0))
    return pl.pallas_call(
        body,
        out_shape=(jax.ShapeDtypeStruct((1, 128), F32), jax.ShapeDtypeStruct((T, D), F32),
                   jax.ShapeDtypeStruct((T, D), BF16), jax.ShapeDtypeStruct((1, D), F32)),
        grid=(T // tr,), in_specs=[row, vec, row],
        out_specs=(pl.BlockSpec((1, 128), lambda i: (0, 0)), row, row, vec),
        name=name, compiler_params=_params("arbitrary"))(x, g, tgt)


def _mm_nn(a, w, *, res=None, out_dtype, tm=1024, tn=1024, tk=None, name, after=()):
    M, K = a.shape
    N = w.shape[1]
    tm, tn = _tile(M, tm), _tile(N, tn)
    tk = K if tk is None else _tile(K, tk)
    nk = K // tk
    has_res = res is not None

    def body(*refs):
        a_ref, w_ref = refs[0], refs[1]
        r_ref = refs[2] if has_res else None
        o_ref = refs[2 + has_res]
        part = jnp.dot(a_ref[...], w_ref[...], preferred_element_type=F32)

        def finish(acc):
            if has_res:
                acc = r_ref[...] + acc
            o_ref[...] = acc.astype(o_ref.dtype)

        if nk == 1:
            finish(part)
        else:
            acc_ref = refs[3 + has_res]
            k = pl.program_id(2)

            @pl.when(k == 0)
            def _():
                acc_ref[...] = part

            @pl.when(jnp.logical_and(k > 0, k < nk - 1))
            def _():
                acc_ref[...] += part

            @pl.when(k == nk - 1)
            def _():
                finish(acc_ref[...] + part)

    in_specs = [pl.BlockSpec((tm, tk), lambda i, j, k: (i, k)),
                pl.BlockSpec((tk, tn), lambda i, j, k: (k, j))]
    args = [a, w]
    if has_res:
        in_specs.append(pl.BlockSpec((tm, tn), lambda i, j, k: (i, j)))
        args.append(res)
    return _call(
        body, args, after, out_shape=jax.ShapeDtypeStruct((M, N), out_dtype), grid=(M // tm, N // tn, nk),
        in_specs=in_specs, out_specs=pl.BlockSpec((tm, tn), lambda i, j, k: (i, j)),
        scratch_shapes=[pltpu.VMEM((tm, tn), F32)] if nk > 1 else [],
        name=name, compiler_params=_params("parallel", "parallel", "arbitrary"))


def _mm_swiglu(h, wgu, *, tm=1024, tn=512, name):
    T, D = h.shape
    F = wgu.shape[1] // 2
    tm, tn = _tile(T, tm), _tile(F, tn)
    nf = F // tn

    def body(h_ref, wg_ref, wu_ref, gu_ref, act_ref):
        hv = h_ref[...]
        g = jnp.dot(hv, wg_ref[...], preferred_element_type=F32)
        u = jnp.dot(hv, wu_ref[...], preferred_element_type=F32)
        act_ref[...] = ((g * _sigmoid(g)) * u).astype(BF16)
        gu_ref[0] = g.astype(BF16)
        gu_ref[1] = u.astype(BF16)

    return pl.pallas_call(
        body,
        out_shape=(jax.ShapeDtypeStruct((2, T, F), BF16), jax.ShapeDtypeStruct((T, F), BF16)),
        grid=(T // tm, nf),
        in_specs=[pl.BlockSpec((tm, D), lambda i, j: (i, 0)),
                  pl.BlockSpec((D, tn), lambda i, j: (0, j)),
                  pl.BlockSpec((D, tn), lambda i, j: (0, j + nf))],
        out_specs=(pl.BlockSpec((2, tm, tn), lambda i, j: (0, i, j)),
                   pl.BlockSpec((tm, tn), lambda i, j: (i, j))),
        name=name, compiler_params=_params("parallel", "parallel"))(h, wgu, wgu)


def _mm_nt(a, w, *, out_dtype, tm=1024, tn=1024, tk=None, name, after=()):
    M, K = a.shape
    N = w.shape[0]
    tm, tn = _tile(M, tm), _tile(N, tn)
    tk = K if tk is None else _tile(K, tk)
    nk = K // tk

    def body(*refs):
        a_ref, w_ref, o_ref = refs[0], refs[1], refs[2]
        part = _nt(a_ref[...], w_ref[...])
        if nk == 1:
            o_ref[...] = part.astype(o_ref.dtype)
        else:
            acc_ref = refs[3]
            k = pl.program_id(2)

            @pl.when(k == 0)
            def _():
                acc_ref[...] = part

            @pl.when(jnp.logical_and(k > 0, k < nk - 1))
            def _():
                acc_ref[...] += part

            @pl.when(k == nk - 1)
            def _():
                o_ref[...] = (acc_ref[...] + part).astype(o_ref.dtype)

    return _call(
        body, [a, w], after, out_shape=jax.ShapeDtypeStruct((M, N), out_dtype), grid=(M // tm, N // tn, nk),
        in_specs=[pl.BlockSpec((tm, tk), lambda i, j, k: (i, k)),
                  pl.BlockSpec((tn, tk), lambda i, j, k: (j, k))],
        out_specs=pl.BlockSpec((tm, tn), lambda i, j, k: (i, j)),
        scratch_shapes=[pltpu.VMEM((tm, tn), F32)] if nk > 1 else [],
        name=name, compiler_params=_params("parallel", "parallel", "arbitrary"))


def _mm_nt_swiglu_bwd(dxb, wdown, gu, *, tm=1024, tn=512, name, after=()):
    T, D = dxb.shape
    F = wdown.shape[0]
    tm, tn = _tile(T, tm), _tile(F, tn)

    def body(dx_ref, w_ref, gu_ref, dgu_ref):
        da = _nt(dx_ref[...], w_ref[...])
        g = gu_ref[0].astype(F32)
        u = gu_ref[1].astype(F32)
        s = _sigmoid(g)
        dgu_ref[0] = ((da * u) * (s * (1.0 + g * (1.0 - s)))).astype(BF16)
        dgu_ref[1] = (da * (g * s)).astype(BF16)

    blk3 = pl.BlockSpec((2, tm, tn), lambda i, j: (0, i, j))
    return _call(
        body, [dxb, wdown, gu], after, out_shape=jax.ShapeDtypeStruct((2, T, F), BF16), grid=(T // tm, F // tn),
        in_specs=[pl.BlockSpec((tm, D), lambda i, j: (i, 0)),
                  pl.BlockSpec((tn, D), lambda i, j: (j, 0)), blk3],
        out_specs=blk3, name=name, compiler_params=_params("parallel", "parallel"))


def _mm_nt_dgu(dgu, wgu, *, tm=1024, tn=1024, tk=1408, name, after=()):
    _, T, F = dgu.shape
    D = wgu.shape[0]
    tm, tn, tk = _tile(T, tm), _tile(D, tn), _tile(F, tk)
    nkf = F // tk
    nk = 2 * nkf

    def body(a_ref, w_ref, o_ref, acc_ref):
        k = pl.program_id(2)
        part = _nt(a_ref[...], w_ref[...])

        @pl.when(k == 0)
        def _():
            acc_ref[...] = part

        @pl.when(jnp.logical_and(k > 0, k < nk - 1))
        def _():
            acc_ref[...] += part

        @pl.when(k == nk - 1)
        def _():
            o_ref[...] = acc_ref[...] + part

    return _call(
        body, [dgu, wgu], after, out_shape=jax.ShapeDtypeStruct((T, D), F32), grid=(T // tm, D // tn, nk),
        in_specs=[pl.BlockSpec((None, tm, tk), lambda i, j, k: (k // nkf, i, k % nkf)),
                  pl.BlockSpec((tn, tk), lambda i, j, k: (j, k))],
        out_specs=pl.BlockSpec((tm, tn), lambda i, j, k: (i, j)),
        scratch_shapes=[pltpu.VMEM((tm, tn), F32)],
        name=name, compiler_params=_params("parallel", "parallel", "arbitrary"))


def _mm_tn(a, g, *, tkw=512, tnw=1024, name, after=()):
    T, Kw = a.shape
    pair = g.ndim == 3
    Nw = 2 * g.shape[2] if pair else g.shape[1]
    tkw = _tile(Kw, tkw)
    tnw = _tile(g.shape[2] if pair else Nw, tnw)
    nf = (Nw // 2) // tnw if pair else 0

    def body(a_ref, g_ref, o_ref):
        o_ref[...] = _tn(a_ref[...], g_ref[...]).astype(o_ref.dtype)

    if pair:
        g_spec = pl.BlockSpec((None, T, tnw), lambda i, j: (j // nf, 0, j % nf))
    else:
        g_spec = pl.BlockSpec((T, tnw), lambda i, j: (0, j))
    return _call(
        body, [a, g], after, out_shape=jax.ShapeDtypeStruct((Kw, Nw), BF16), grid=(Kw // tkw, Nw // tnw),
        in_specs=[pl.BlockSpec((T, tkw), lambda i, j: (0, i)), g_spec],
        out_specs=pl.BlockSpec((tkw, tnw), lambda i, j: (i, j)),
        name=name, compiler_params=_params("parallel", "parallel"))


def _mixer_specs(T, A, tr):
    nin = 5 * A
    nb = tr // HALO
    last = T // HALO - 1
    prev = pl.BlockSpec((HALO, nin), lambda i: (jnp.maximum(i * nb - 1, 0), 0))
    cur = pl.BlockSpec((tr, nin), lambda i: (i, 0))
    nxt = pl.BlockSpec((HALO, nin), lambda i: (jnp.minimum((i + 1) * nb, last), 0))
    return prev, cur, nxt


def _conv(p_ext, cw_ref):
    return (cw_ref[2:3, :] * p_ext + cw_ref[1:2, :] * pltpu.roll(p_ext, 1, 0)
            + cw_ref[0:1, :] * pltpu.roll(p_ext, 2, 0))


def _mixer_fwd(z, lng, lnb, wm, bfull, cw, gn, name, after=()):
    T, nin = z.shape
    A = nin // 5
    H = A // HEAD_DIM
    tr = _tile(T, 256)
    nblk = tr // HEAD_DIM

    def body(zp_ref, z_ref, lng_ref, lnb_ref, wm_ref, bf_ref, cw_ref, gn_ref, y_ref, vn_ref, mix_ref):
        i = pl.program_id(0)
        u = _gelu(z_ref[:, 0:A].astype(F32))
        vg = _gelu(z_ref[:, A:2 * A].astype(F32))
        xc = vg - jnp.mean(vg, axis=-1, keepdims=True)
        rstd = lax.rsqrt(jnp.mean(xc * xc, axis=-1, keepdims=True) + LN_EPS)
        vn_ref[...] = ((xc * rstd) * lng_ref[...] + lnb_ref[...]).astype(BF16)
        for cb in range(nblk):
            rows = slice(cb * HEAD_DIM, (cb + 1) * HEAD_DIM)
            for h in range(H):
                cols = slice(h * HEAD_DIM, (h + 1) * HEAD_DIM)
                mix_ref[rows, cols] = jnp.dot(wm_ref[h], vn_ref[rows, cols],
                                              preferred_element_type=F32) + bf_ref[h]
        ya = u * mix_ref[...]
        ra = lax.rsqrt(jnp.mean(ya * ya, axis=-1, keepdims=True) + RMS_EPS)
        y_ref[:, 0:A] = ((ya * ra) * gn_ref[:, 0:A]).astype(BF16)

        p_prev = zp_ref[:, 3 * A:4 * A].astype(F32) * zp_ref[:, 4 * A:5 * A].astype(F32)
        p_prev = jnp.where(i > 0, p_prev, 0.0)
        p_cur = z_ref[:, 3 * A:4 * A].astype(F32) * z_ref[:, 4 * A:5 * A].astype(F32)
        cv = _conv(jnp.concatenate([p_prev, p_cur], axis=0), cw_ref)[HALO:]
        yb = z_ref[:, 2 * A:3 * A].astype(F32) * cv
        rb = lax.rsqrt(jnp.mean(yb * yb, axis=-1, keepdims=True) + RMS_EPS)
        y_ref[:, A:2 * A] = ((yb * rb) * gn_ref[:, A:2 * A]).astype(BF16)

    prev, cur, _ = _mixer_specs(T, A, tr)
    full = lambda shape: pl.BlockSpec(shape, lambda i: (0,) * len(shape))
    return _call(
        body, [z, z, lng, lnb, wm, bfull, cw, gn], after,
        out_shape=jax.ShapeDtypeStruct((T, 2 * A), BF16), grid=(T // tr,),
        in_specs=[prev, cur, full((1, A)), full((1, A)), full((H, HEAD_DIM, HEAD_DIM)),
                  full((H, HEAD_DIM, HEAD_DIM)), full((8, A)), full((1, 2 * A))],
        out_specs=pl.BlockSpec((tr, 2 * A), lambda i: (i, 0)),
        scratch_shapes=[pltpu.VMEM((tr, A), BF16), pltpu.VMEM((tr, A), F32)],
        name=name, compiler_params=_params("parallel"))


def _mixer_bwd(z, dy, lng, lnb, wm, wmt, bfull, cw, gn, name, after=()):
    T, nin = z.shape
    A = nin // 5
    H = A // HEAD_DIM
    tr = _tile(T, 256)
    nblk = tr // HEAD_DIM
    ngrid = T // tr
    next_ = tr + 2 * HALO

    def body(zp_ref, z_ref, zn_ref, dy_ref, dyn_ref, lng_ref, lnb_ref, wm_ref, wmt_ref, bf_ref, cw_ref, gn_ref,
             dz_ref, dgn_ref, dlng_ref, dlnb_ref, dws_ref, dbs_ref, dcw_ref,
             vn_ref, mix_ref, dmix_ref, dvn_ref):
        i = pl.program_id(0)

        @pl.when(i == 0)
        def _():
            dgn_ref[...] = jnp.zeros_like(dgn_ref)
            dlng_ref[...] = jnp.zeros_like(dlng_ref)
            dlnb_ref[...] = jnp.zeros_like(dlnb_ref)
            dws_ref[...] = jnp.zeros_like(dws_ref)
            dbs_ref[...] = jnp.zeros_like(dbs_ref)
            dcw_ref[...] = jnp.zeros_like(dcw_ref)

        u, du_dz = _gelu_and_grad(z_ref[:, 0:A].astype(F32))
        vg, dv_dz = _gelu_and_grad(z_ref[:, A:2 * A].astype(F32))
        xc = vg - jnp.mean(vg, axis=-1, keepdims=True)
        rstd = lax.rsqrt(jnp.mean(xc * xc, axis=-1, keepdims=True) + LN_EPS)
        vhat = xc * rstd
        vn_ref[...] = (vhat * lng_ref[...] + lnb_ref[...]).astype(BF16)
        for cb in range(nblk):
            rows = slice(cb * HEAD_DIM, (cb + 1) * HEAD_DIM)
            for h in range(H):
                cols = slice(h * HEAD_DIM, (h + 1) * HEAD_DIM)
                mix_ref[rows, cols] = jnp.dot(wm_ref[h], vn_ref[rows, cols],
                                              preferred_element_type=F32) + bf_ref[h]
        mixed = mix_ref[...]
        ya = u * mixed
        ra = lax.rsqrt(jnp.mean(ya * ya, axis=-1, keepdims=True) + RMS_EPS)
        yha = ya * ra
        dyan = dy_ref[:, 0:A]
        dgn_ref[:, 0:A] += jnp.sum(dyan * yha, axis=0, keepdims=True)
        qa = dyan * gn_ref[:, 0:A]
        dya = ra * (qa - yha * jnp.mean(qa * yha, axis=-1, keepdims=True))
        dz_ref[:, 0:A] = ((dya * mixed) * du_dz).astype(BF16)
        dmix_ref[...] = (dya * u).astype(BF16)

        ii = lax.broadcasted_iota(jnp.int32, (HEAD_DIM, HEAD_DIM), 0)
        jj = lax.broadcasted_iota(jnp.int32, (HEAD_DIM, HEAD_DIM), 1)
        mask = (jj // CHUNK <= ii // CHUNK).astype(F32)
        ones = jnp.ones((8, HEAD_DIM), BF16)
        for h in range(H):
            cols = slice(h * HEAD_DIM, (h + 1) * HEAD_DIM)
            dws = jnp.zeros((HEAD_DIM, HEAD_DIM), F32)
            dbs = jnp.zeros((8, HEAD_DIM), F32)
            for cb in range(nblk):
                rows = slice(cb * HEAD_DIM, (cb + 1) * HEAD_DIM)
                dm = dmix_ref[rows, cols]
                dws = dws + _nt(dm, vn_ref[rows, cols])
                dbs = dbs + _nt(ones, dm)
                dvn_ref[rows, cols] = jnp.dot(wmt_ref[h], dm, preferred_element_type=F32)
            dws_ref[h] += dws * mask
            dbs_ref[h] += dbs
        dvn = dvn_ref[...]
        dlnb_ref[...] += jnp.sum(dvn, axis=0, keepdims=True)
        dlng_ref[...] += jnp.sum(dvn * vhat, axis=0, keepdims=True)
        dvh = dvn * lng_ref[...]
        dvg = rstd * (dvh - jnp.mean(dvh, axis=-1, keepdims=True)
                      - vhat * jnp.mean(dvh * vhat, axis=-1, keepdims=True))
        dz_ref[:, A:2 * A] = (dvg * dv_dz).astype(BF16)

        def ext(lo):
            mid = z_ref[:, lo:lo + A].astype(F32)
            return jnp.concatenate([zp_ref[:, lo:lo + A].astype(F32), mid, zn_ref[:, lo:lo + A].astype(F32)], axis=0)

        zb, zc, zh = ext(2 * A), ext(3 * A), ext(4 * A)
        row = lax.broadcasted_iota(jnp.int32, (next_, 1), 0)
        p = zc * zh
        p = jnp.where(jnp.logical_and(row < HALO, i == 0), 0.0, p)
        cv = _conv(p, cw_ref)
        yb = zb * cv
        rb = lax.rsqrt(jnp.mean(yb * yb, axis=-1, keepdims=True) + RMS_EPS)
        yhb = yb * rb
        dyn_rows = jnp.where(i < ngrid - 1, dyn_ref[:, A:2 * A], 0.0)
        dybn = jnp.concatenate([jnp.zeros((HALO, A), F32), dy_ref[:, A:2 * A], dyn_rows], axis=0)
        ctr = slice(HALO, HALO + tr)
        dgn_ref[:, A:2 * A] += jnp.sum((dybn * yhb)[ctr], axis=0, keepdims=True)
        qb = dybn * gn_ref[:, A:2 * A]
        dyb = rb * (qb - yhb * jnp.mean(qb * yhb, axis=-1, keepdims=True))
        dcv = dyb * zb
        dp = (cw_ref[2:3, :] * dcv + cw_ref[1:2, :] * pltpu.roll(dcv, next_ - 1, 0)
              + cw_ref[0:1, :] * pltpu.roll(dcv, next_ - 2, 0))
        dz_ref[:, 2 * A:3 * A] = (dyb * cv)[ctr].astype(BF16)
        dz_ref[:, 3 * A:4 * A] = (dp * zh)[ctr].astype(BF16)
        dz_ref[:, 4 * A:5 * A] = (dp * zc)[ctr].astype(BF16)
        dcw_ref[2:3, :] += jnp.sum((dcv * p)[ctr], axis=0, keepdims=True)
        dcw_ref[1:2, :] += jnp.sum((dcv * pltpu.roll(p, 1, 0))[ctr], axis=0, keepdims=True)
        dcw_ref[0:1, :] += jnp.sum((dcv * pltpu.roll(p, 2, 0))[ctr], axis=0, keepdims=True)

    prev, cur, nxt = _mixer_specs(T, A, tr)
    nb = tr // HALO
    dy_cur = pl.BlockSpec((tr, 2 * A), lambda i: (i, 0))
    dy_nxt = pl.BlockSpec((HALO, 2 * A), lambda i: (jnp.minimum((i + 1) * nb, T // HALO - 1), 0))
    full = lambda shape: pl.BlockSpec(shape, lambda i: (0,) * len(shape))
    hh = (H, HEAD_DIM, HEAD_DIM)
    return _call(
        body, [z, z, z, dy, dy, lng, lnb, wm, wmt, bfull, cw, gn], after,
        out_shape=(jax.ShapeDtypeStruct((T, nin), BF16), jax.ShapeDtypeStruct((1, 2 * A), F32),
                   jax.ShapeDtypeStruct((1, A), F32), jax.ShapeDtypeStruct((1, A), F32),
                   jax.ShapeDtypeStruct(hh, F32), jax.ShapeDtypeStruct((H, 8, HEAD_DIM), F32),
                   jax.ShapeDtypeStruct((8, A), F32)),
        grid=(ngrid,),
        in_specs=[prev, cur, nxt, dy_cur, dy_nxt, full((1, A)), full((1, A)), full(hh), full(hh), full(hh),
                  full((8, A)), full((1, 2 * A))],
        out_specs=(pl.BlockSpec((tr, nin), lambda i: (i, 0)), full((1, 2 * A)), full((1, A)), full((1, A)),
                   full(hh), full((H, 8, HEAD_DIM)), full((8, A))),
        scratch_shapes=[pltpu.VMEM((tr, A), BF16), pltpu.VMEM((tr, A), F32), pltpu.VMEM((tr, A), BF16),
                        pltpu.VMEM((tr, A), F32)],
        name=name, compiler_params=_params("arbitrary"))


class _Geom:
    def __init__(self, rows, cols, axis, size, base=0):
        self.rows, self.cols, self.axis, self.size, self.base = rows, cols, axis, size, base

    def in_full(self, j, h):
        if self.axis == 1:
            return (h * (self.rows // 2), self.rows // 2), (self.base + j * self.size, self.size)
        return (self.base + j * self.size, self.size), (h * (self.cols // 2), self.cols // 2)

    def in_shard(self, h):
        if self.axis == 1:
            return (h * (self.rows // 2), self.rows // 2), (0, self.size)
        return (0, self.size), (h * (self.cols // 2), self.cols // 2)

    def half_of_full(self, h):
        if self.axis == 1:
            return (h * (self.rows // 2), self.rows // 2), (0, self.cols)
        return (0, self.rows), (h * (self.cols // 2), self.cols // 2)

    def in_half(self, j):
        if self.axis == 1:
            return (0, self.rows // 2), (self.base + j * self.size, self.size)
        return (self.base + j * self.size, self.size), (0, self.cols // 2)

    @property
    def half_shape(self):
        return (self.rows // 2, self.cols) if self.axis == 1 else (self.rows, self.cols // 2)

    @property
    def shard_half_shape(self):
        return (self.rows // 2, self.size) if self.axis == 1 else (self.size, self.cols // 2)

    @property
    def shard_shape(self):
        return (self.rows, self.size) if self.axis == 1 else (self.size, self.cols)


def _at(ref, region):
    (r0, rn), (c0, cn) = region
    if not isinstance(r0, int):
        r0 = pl.multiple_of(r0, 16)
    if not isinstance(c0, int):
        c0 = pl.multiple_of(c0, 128)
    return ref.at[pl.ds(r0, rn), pl.ds(c0, cn)]


def _whole(shape):
    return (0, shape[-2]), (0, shape[-1])


def _split_rows(region, itemsize):
    (r0, rn), cols = region
    want = max(1, (rn * cols[1] * itemsize) // DMA_CHUNK_BYTES)
    n = 1
    for cand in range(1, want + 1):
        if rn % cand == 0 and (rn // cand) % 16 == 0:
            n = cand
    step = rn // n
    return [((r0 + i * step, step), cols) for i in range(n)]


class _Chunked:
    def __init__(self, make, src, src_reg, dst, dst_reg):
        itemsize = jnp.dtype(src.dtype).itemsize
        self.whole = make(_at(src, src_reg), _at(dst, dst_reg))
        self.parts = [make(_at(src, a), _at(dst, b)) for a, b in
                      zip(_split_rows(src_reg, itemsize), _split_rows(dst_reg, itemsize))]

    def start(self):
        for p in self.parts:
            p.start()


def _mesh_place():
    x, y, c = lax.axis_index("x"), lax.axis_index("y"), lax.axis_index("c")
    chips = [(1 - x, y), (x, 1 - y), (1 - x, 1 - y)]
    return x, y, c, 2 * x + y, chips


def _remote(ssem, rsem, dev):
    return lambda src, dst: pltpu.make_async_remote_copy(
        src_ref=src, dst_ref=dst, send_sem=ssem, recv_sem=rsem, device_id=dev, device_id_type=MESH)


def _comm_call(name, body, arrays, sems_in=(), after=(), sems_out=(), new=()):
    na, ns, nf, no, nn = len(arrays), len(sems_in), len(after), len(sems_out), len(new)

    def kern(*refs):
        sin = refs[na:na + ns]
        outs = refs[na + ns + nf:]
        body(outs[no:no + na], outs[no + na:no + na + nn], sin, outs[:no])
        outs[-1][...] = jnp.zeros_like(outs[-1])

    out_shape = (tuple(pltpu.SemaphoreType.DMA((n,)) for n in sems_out)
                 + tuple(pltpu.HBM(a.shape, a.dtype) for a in arrays)
                 + tuple(pltpu.HBM(s, d) for s, d in new)
                 + (jax.ShapeDtypeStruct((8, 128), F32),))
    res = pl.pallas_call(
        kern, out_shape=out_shape, in_specs=[HBM] * na + [SEM] * ns + [ANY] * nf,
        out_specs=(SEM,) * no + (HBM,) * (na + nn) + (pl.BlockSpec(memory_space=pltpu.VMEM),),
        input_output_aliases={i: no + i for i in range(na)}, name=name,
        compiler_params=pltpu.CompilerParams(has_side_effects=EFFECT),
    )(*[pltpu.with_memory_space_constraint(a, pltpu.HBM) for a in arrays], *sems_in, *after)
    return list(res[:no]), list(res[no:no + na]), list(res[no + na:no + na + nn]), res[-1]


def _gather_start(arrays, entries):
    def body(arrs, news, sin, sout):
        x, y, c, j, chips = _mesh_place()
        for e, (ai, geo) in enumerate(entries):
            reg = geo.in_full(j, c)
            for k, chip in enumerate(chips):
                _Chunked(_remote(sout[2 * e].at[k], sout[2 * e + 1].at[k], (*chip, c)),
                         arrs[ai], reg, arrs[ai], reg).start()

    sems, arrs, _, tok = _comm_call("gather_start", body, arrays, sems_out=[N_CHIPS - 1] * (2 * len(entries)))
    return [sems[2 * e:2 * e + 2] for e in range(len(entries))], arrs, tok


def _gather_forward(name, arr, members, sems, after):
    def body(arrs, news, sin, sout):
        x, y, c, j, chips = _mesh_place()
        sibling = (x, y, 1 - c)
        full = arrs[0]
        for m, geo in enumerate(members):
            mine = geo.in_full(j, c)
            for k, chip in enumerate(chips):
                got = geo.in_full(2 * chip[0] + chip[1], c)
                sent = _Chunked(_remote(sin[2 * m].at[k], sin[2 * m + 1].at[k], sibling), full, mine, full, got)
                sent.whole.wait_send()
                sent.whole.wait_recv()
                _Chunked(_remote(sout[2 * m].at[k], sout[2 * m + 1].at[k], sibling), full, got, full, got).start()

    flat = [s for pair in sems for s in pair]
    fs, arrs, _, tok = _comm_call(name, body, [arr], sems_in=flat, after=after,
                                  sems_out=[N_CHIPS - 1] * (2 * len(members)))
    return fs, arrs[0], tok


def _gather_wait(name, arr, members, fsems, after):
    def body(arrs, news, sin, sout):
        x, y, c, j, chips = _mesh_place()
        sibling = (x, y, 1 - c)
        full = arrs[0]
        for m, geo in enumerate(members):
            for k, chip in enumerate(chips):
                jk = 2 * chip[0] + chip[1]
                cp = _Chunked(_remote(sin[2 * m].at[k], sin[2 * m + 1].at[k], sibling),
                              full, geo.in_full(jk, c), full, geo.in_full(jk, 1 - c))
                cp.whole.wait_send()
                cp.whole.wait_recv()

    _, arrs, _, _ = _comm_call(name, body, [arr], sems_in=fsems, after=after)
    return arrs[0]


def _gather_conv(conv):
    L, _, cb = conv.shape
    nk = N_CHIPS - 1

    def body(s_ref, f_ref, send_sems, recv_sems, local_sem):
        x, y, c, j, chips = _mesh_place()
        at = lambda jj: f_ref.at[:, :, pl.ds(pl.multiple_of(jj * cb, 128), cb)]
        lc = pltpu.make_async_copy(s_ref, at(j), local_sem.at[0])
        lc.start()
        cps = []
        for k, chip in enumerate(chips):
            cp = _remote(send_sems.at[k], recv_sems.at[k], (*chip, c))(s_ref, at(j))
            cp.start()
            cps.append(cp)
        for k, chip in enumerate(chips):
            jk = 2 * chip[0] + chip[1]
            cps[k].wait_send()
            _remote(send_sems.at[k], recv_sems.at[k], (*chip, c))(at(jk), at(jk)).wait_recv()
        lc.wait()

    return pl.pallas_call(
        body, out_shape=jax.ShapeDtypeStruct((L, 8, N_CHIPS * cb), F32), in_specs=[ANY], out_specs=ANY,
        scratch_shapes=[pltpu.SemaphoreType.DMA((nk,)), pltpu.SemaphoreType.DMA((nk,)),
                        pltpu.SemaphoreType.DMA((1,))],
        name="gather_conv")(conv)


def _swap_start(name, dws, geos, after):
    n = len(dws)

    def body(arrs, news, sin, sout):
        x, y, c, _, _ = _mesh_place()
        for t in range(n):
            _Chunked(_remote(sout[0].at[t], sout[1].at[t], (x, y, 1 - c)),
                     arrs[t], geos[t].half_of_full(1 - c), news[t], _whole(news[t].shape)).start()

    return _comm_call(name, body, dws, after=after, sems_out=[n, n], new=[(g.half_shape, BF16) for g in geos])


def _swap_wait(name, dws, lands, sems, geos, after):
    n = len(dws)

    def body(arrs, news, sin, sout):
        x, y, c, _, _ = _mesh_place()
        for t in range(n):
            cp = _Chunked(_remote(sin[0].at[t], sin[1].at[t], (x, y, 1 - c)),
                          arrs[t], geos[t].half_of_full(1 - c), arrs[n + t], _whole(arrs[n + t].shape))
            cp.whole.wait_send()
            cp.whole.wait_recv()

    _, arrs, _, _ = _comm_call(name, body, list(dws) + list(lands), sems_in=sems, after=after)
    return arrs[:n], arrs[n:]


def _exchange_start(name, ps, entries, after):
    nk = N_CHIPS - 1

    def body(arrs, news, sin, sout):
        x, y, c, j, chips = _mesh_place()
        for e, (pi, geo) in enumerate(entries):
            for k, chip in enumerate(chips):
                dst = news[e].at[k]
                _Chunked(_remote(sout[0].at[nk * e + k], sout[1].at[nk * e + k], (*chip, c)),
                         arrs[pi], geo.in_half(2 * chip[0] + chip[1]), dst, _whole(dst.shape)).start()

    ne = len(entries)
    return _comm_call(name, body, ps, after=after, sems_out=[nk * ne, nk * ne],
                      new=[((nk,) + g.shard_half_shape, BF16) for _, g in entries])


def _exchange_wait(name, ps, lands, sems, entries, after):
    nk = N_CHIPS - 1
    n = len(ps)

    def body(arrs, news, sin, sout):
        x, y, c, j, chips = _mesh_place()
        for e, (pi, geo) in enumerate(entries):
            for k, chip in enumerate(chips):
                dst = arrs[n + e].at[k]
                cp = _Chunked(_remote(sin[0].at[nk * e + k], sin[1].at[nk * e + k], (*chip, c)),
                              arrs[pi], geo.in_half(2 * chip[0] + chip[1]), dst, _whole(dst.shape))
                cp.whole.wait_send()
                cp.whole.wait_recv()

    _, arrs, _, _ = _comm_call(name, body, list(ps) + list(lands), sems_in=sems, after=after)
    return arrs[:n], arrs[n:]


def _join_start(name, gs, geos, after):
    n = len(gs)

    def body(arrs, news, sin, sout):
        x, y, c, _, _ = _mesh_place()
        for t in range(n):
            mine = geos[t].in_shard(c)
            _Chunked(_remote(sout[0].at[t], sout[1].at[t], (x, y, 1 - c)), arrs[t], mine, arrs[t], mine).start()

    return _comm_call(name, body, gs, after=after, sems_out=[n, n])


def _join_wait(name, gs, sems, geos, after):
    n = len(gs)

    def body(arrs, news, sin, sout):
        x, y, c, _, _ = _mesh_place()
        for t in range(n):
            cp = _Chunked(_remote(sin[0].at[t], sin[1].at[t], (x, y, 1 - c)),
                          arrs[t], geos[t].in_shard(c), arrs[t], geos[t].in_shard(1 - c))
            cp.whole.wait_send()
            cp.whole.wait_recv()

    _, arrs, _, _ = _comm_call(name, body, gs, sems_in=sems, after=after)
    return arrs


def _cast_place(w, l, geo, pos, prev, name):
    if geo.axis == 1:
        tr = _rows_tile(geo.rows, geo.size, 512 * 1024)
        grid = (geo.rows // tr,)
        blk = (tr, geo.size)
        q = geo.base // geo.size
        out_map = lambda i, p: (i, q + p[0])
    else:
        tr = _rows_tile(geo.size, geo.cols, 512 * 1024)
        grid = (geo.size // tr,)
        blk = (tr, geo.cols)
        nb = geo.size // tr
        out_map = lambda i, p: (p[0] * nb + i, 0)

    def body(p_ref, w_ref, *rest):
        rest[-1][...] = w_ref[...].astype(BF16)

    in_specs = [pl.BlockSpec((None,) + blk, lambda i, p: (l, i, 0))]
    args = [w]
    aliases = None
    if prev is not None:
        in_specs.append(ANY)
        args.append(prev)
        aliases = {2: 0}
    return _prefetch_call(body, pos, args, out_shape=jax.ShapeDtypeStruct((geo.rows, geo.cols), BF16), grid=grid,
                          in_specs=in_specs, out_specs=pl.BlockSpec(blk, out_map), aliases=aliases, name=name)


def _pair_sum(dw, recv, geo, pos, name):
    hs = geo.half_shape
    tr = _rows_tile(hs[0], hs[1], 512 * 1024)
    nb = hs[0] // tr
    blk = (tr, hs[1])
    if geo.axis == 1:
        own_map = lambda i, p: (p[1] * nb + i, 0)
    else:
        own_map = lambda i, p: (i, p[1])

    def body(p_ref, a_ref, b_ref, o_ref):
        o_ref[...] = (a_ref[...].astype(F32) + b_ref[...].astype(F32)).astype(BF16)

    same = pl.BlockSpec(blk, lambda i, p: (i, 0))
    return _prefetch_call(body, pos, [dw, recv], out_shape=jax.ShapeDtypeStruct(hs, BF16), grid=(nb,),
                          in_specs=[pl.BlockSpec(blk, own_map), same], out_specs=same, name=name)


def _chip_sum(p, recv, geo, pos, name):
    ss = geo.shard_half_shape
    tr = _rows_tile(ss[0], ss[1], 256 * 1024)
    nb = ss[0] // tr
    blk = (tr, ss[1])
    if geo.axis == 1:
        q = geo.base // geo.size
        own_map = lambda i, p_: (i, q + p_[0])
        out_map = lambda i, p_: (p_[1] * nb + i, 0)
    else:
        own_map = lambda i, p_: (p_[0] * nb + i, 0)
        out_map = lambda i, p_: (i, p_[1])

    def body(p_ref, o_ref, r_ref, out_ref):
        acc = o_ref[...].astype(F32)
        for k in range(N_CHIPS - 1):
            acc = acc + r_ref[k].astype(F32)
        out_ref[...] = acc

    return _prefetch_call(
        body, pos, [p, recv], out_shape=jax.ShapeDtypeStruct(geo.shard_shape, F32), grid=(nb,),
        in_specs=[pl.BlockSpec(blk, own_map), pl.BlockSpec((N_CHIPS - 1,) + blk, lambda i, p_: (0, i, 0))],
        out_specs=pl.BlockSpec(blk, out_map), name=name)


def _adamw_math(w, g, m, v):
    m = ADAM_B1 * m + (1.0 - ADAM_B1) * g
    v = ADAM_B2 * v + (1.0 - ADAM_B2) * (g * g)
    m_hat = m / (1.0 - ADAM_B1 ** ADAM_STEP)
    v_hat = v / (1.0 - ADAM_B2 ** ADAM_STEP)
    delta = -ADAM_LR * (m_hat / (jnp.sqrt(v_hat) + ADAM_EPS) + ADAM_WD * w)
    return delta, m, v


def _adamw_layer(l, w, g, m, v, prev, name):
    L, R, C = w.shape
    tr = _rows_tile(R, C, 256 * 1024)

    def body(w_ref, g_ref, m_ref, v_ref, *rest):
        go_ref, d_ref, mo_ref, vo_ref = rest[-4:]
        gv = g_ref[...]
        d, mn, vn = _adamw_math(w_ref[...], gv, m_ref[...], v_ref[...])
        go_ref[...] = gv
        d_ref[...] = d
        mo_ref[...] = mn
        vo_ref[...] = vn

    blk = pl.BlockSpec((None, tr, C), lambda i: (l, i, 0))
    sds = jax.ShapeDtypeStruct(w.shape, F32)
    in_specs = [blk, pl.BlockSpec((tr, C), lambda i: (i, 0)), blk, blk]
    args = [w, g, m, v]
    aliases = {}
    if prev is not None:
        in_specs += [ANY] * 4
        args += list(prev)
        aliases = {4 + i: i for i in range(4)}
    return pl.pallas_call(
        body, out_shape=(sds, sds, sds, sds), grid=(R // tr,), in_specs=in_specs, out_specs=(blk,) * 4,
        input_output_aliases=aliases, name=name, compiler_params=_params("parallel"))(*args)


def _adamw_flat(w, g, m, v, name):
    R, C = w.shape
    tr = _tile(R, 1024) if R % 128 == 0 else R

    def body(w_ref, g_ref, m_ref, v_ref, d_ref, mo_ref, vo_ref):
        d, mn, vn = _adamw_math(w_ref[...], g_ref[...], m_ref[...], v_ref[...])
        d_ref[...] = d
        mo_ref[...] = mn
        vo_ref[...] = vn

    blk = pl.BlockSpec((tr, C), lambda i: (i, 0))
    sds = jax.ShapeDtypeStruct(w.shape, F32)
    return pl.pallas_call(
        body, out_shape=(sds, sds, sds), grid=(R // tr,), in_specs=[blk] * 4, out_specs=(blk,) * 3,
        name=name, compiler_params=_params("parallel"))(w, g, m, v)


def _allreduce_small(s, after=()):
    R, C = s.shape

    def body(s_ref, o_ref, rbuf, send_sems, recv_sems):
        x, y, c = lax.axis_index("x"), lax.axis_index("y"), lax.axis_index("c")
        peers = [(x, y, 1 - c), (1 - x, y, c), (x, 1 - y, c)]
        o_ref[...] = s_ref[...]
        for k, peer in enumerate(peers):
            cp = _remote(send_sems.at[k], recv_sems.at[k], peer)(o_ref, rbuf.at[k])
            cp.start()
            cp.wait()
            o_ref[...] = o_ref[...] + rbuf[k]

    vm = pl.BlockSpec(memory_space=pltpu.VMEM)
    return _call(
        body, [s], after, out_shape=jax.ShapeDtypeStruct((R, C), F32), in_specs=[vm], out_specs=vm,
        scratch_shapes=[pltpu.VMEM((3, R, C), F32), pltpu.SemaphoreType.DMA((3,)), pltpu.SemaphoreType.DMA((3,))],
        name="allreduce_small", compiler_params=pltpu.CompilerParams(vmem_limit_bytes=V7X_VMEM_LIMIT))


class _GradBatch:
    def __init__(self, tag, dws, geos4, entries, pos):
        self.tag, self.dws, self.geos4, self.entries, self.pos = tag, dws, geos4, entries, pos

    def start_swap(self, after):
        self.s1, self.dws, self.land1, tok = _swap_start(f"{self.tag}_swap_start", self.dws, self.geos4, after)
        return tok

    def swap_to_exchange(self, after):
        dws, lands = _swap_wait(f"{self.tag}_swap_wait", self.dws, self.land1, self.s1, self.geos4, after)
        ps = [_pair_sum(d, r, g, self.pos, f"{self.tag}_pair_sum_{i}")
              for i, (d, r, g) in enumerate(zip(dws, lands, self.geos4))]
        self.s2, self.ps, self.land2, tok = _exchange_start(f"{self.tag}_exch_start", ps, self.entries, ())
        return tok

    def exchange_to_join(self, after):
        ps, lands = _exchange_wait(f"{self.tag}_exch_wait", self.ps, self.land2, self.s2, self.entries, after)
        self.geos5 = [g for _, g in self.entries]
        gs = [_chip_sum(ps[pi], r, g, self.pos, f"{self.tag}_chip_sum_{e}")
              for e, ((pi, g), r) in enumerate(zip(self.entries, lands))]
        self.s3, self.gs, _, tok = _join_start(f"{self.tag}_join_start", gs, self.geos5, ())
        return tok

    def finish(self, after):
        return _join_wait(f"{self.tag}_join_wait", self.gs, self.s3, self.geos5, after)


def _pack(pieces):
    rows = []
    for p in pieces:
        flat = p.reshape(-1)
        pad = (-flat.shape[0]) % 1024
        rows.append(jnp.pad(flat, (0, pad)).reshape(-1, 128))
    return jnp.concatenate(rows, axis=0)


def _unpack(buf, shapes):
    out, r = [], 0
    for shp in shapes:
        n = math.prod(shp)
        nr = -(-n // 1024) * 8
        out.append(buf[r:r + nr].reshape(-1)[:n].reshape(shp))
        r += nr
    return out


def kernel(x, norm1_g, w_in, gmlp_ln_g, gmlp_ln_b, w_spatial, b_spatial, conv_w, group_norm_g, w_out, norm2_g, w_gate, w_up, w_down, final_norm_g, loss_target, m_norm1_g, m_w_in, m_gmlp_ln_g, m_gmlp_ln_b, m_w_spatial, m_b_spatial, m_conv_w, m_group_norm_g, m_w_out, m_norm2_g, m_w_gate, m_w_up, m_w_down, m_final_norm_g, v_norm1_g, v_w_in, v_gmlp_ln_g, v_gmlp_ln_b, v_w_spatial, v_b_spatial, v_conv_w, v_group_norm_g, v_w_out, v_norm2_g, v_w_gate, v_w_up, v_w_down, v_final_norm_g):
    L, D, n_in = w_in.shape
    T = x.shape[1]
    nin = N_CHIPS * n_in
    A = nin // 5
    H = A // HEAD_DIM
    n_f = w_gate.shape[2]
    F = N_CHIPS * n_f
    n_o = w_out.shape[1]
    cb = conv_w.shape[2]
    assert A == H * HEAD_DIM and T % 256 == 0 and N_CHIPS * n_o == D and N_CHIPS * cb == A

    g_in, g_out, g_down = _Geom(D, nin, 1, n_in), _Geom(D, D, 0, n_o), _Geom(F, D, 0, n_f)
    g_gate, g_up, g_gu = _Geom(D, 2 * F, 1, n_f, 0), _Geom(D, 2 * F, 1, n_f, F), _Geom(D, 2 * F, 1, n_f)
    pos = jnp.stack([2 * lax.axis_index("x") + lax.axis_index("y"), lax.axis_index("c")]).astype(jnp.int32)
    row = lambda v: v.reshape(1, -1)

    placed = []
    for l in range(L):
        gate = _cast_place(w_gate, l, g_gate, pos, None, f"l{l}_place_gate")
        placed += [_cast_place(w_in, l, g_in, pos, None, f"l{l}_place_in"),
                   _cast_place(w_out, l, g_out, pos, None, f"l{l}_place_out"),
                   _cast_place(w_up, l, g_up, pos, gate, f"l{l}_place_up"),
                   _cast_place(w_down, l, g_down, pos, None, f"l{l}_place_down")]
    conv_full = _gather_conv(jnp.pad(conv_w, ((0, 0), (0, 8 - CONV_TAPS), (0, 0))))
    members = [[g_in], [g_out], [g_gate, g_up], [g_down]]
    entries = [(4 * l + a, geo) for l in range(L) for a in range(4) for geo in members[a]]
    ag_sems, placed, tok = _gather_start(placed, entries)
    sems_of = {}
    e = 0
    for l in range(L):
        for a in range(4):
            sems_of[l, a] = ag_sems[e:e + len(members[a])]
            e += len(members[a])

    def arrive(l, a, after):
        return _gather_forward(f"l{l}_gather_fwd_{a}", placed[4 * l + a], members[a], sems_of[l, a], after)

    def landed(l, a, fs, arr, after):
        return _gather_wait(f"l{l}_gather_wait_{a}", arr, members[a], fs, after)

    frame = jnp.arange(HEAD_DIM)
    mask = (frame[None, :] // CHUNK) <= (frame[:, None] // CHUNK)

    xs = x[0]
    acts = []
    for l in range(L):
        wm = jnp.where(mask[None], w_spatial[l], 0.0).astype(BF16)
        wmt = jnp.swapaxes(wm, 1, 2)
        bfull = jnp.broadcast_to(b_spatial[l][:, :, None], (H, HEAD_DIM, HEAD_DIM))
        sm = dict(lng=row(gmlp_ln_g[l]), lnb=row(gmlp_ln_b[l]), wm=wm, wmt=wmt, bfull=bfull,
                  cw=conv_full[l], gn=row(group_norm_g[l]))
        if l == 0:
            h = _rms_fwd(xs, row(norm1_g[l]), f"l{l}_rms1", after=(tok,))
            fs, arr, tok = arrive(l, 0, (h,))
            W_in = landed(l, 0, fs, arr, (tok,))
        else:
            fs, arr, tok = arrive(l, 0, (xs,))
            h = _rms_fwd(xs, row(norm1_g[l]), f"l{l}_rms1", after=(tok,))
            W_in = landed(l, 0, fs, arr, (h,))
        z = _mm_nn(h, W_in, out_dtype=BF16, name=f"l{l}_mm_in")
        fs, arr, tok = arrive(l, 1, (z,))
        y = _mixer_fwd(z, sm["lng"], sm["lnb"], wm, bfull, sm["cw"], sm["gn"], f"l{l}_mixer", after=(tok,))
        W_out = landed(l, 1, fs, arr, (y,))
        x1 = _mm_nn(y, W_out, res=xs, out_dtype=F32, name=f"l{l}_mm_out")
        fs, arr, tok = arrive(l, 2, (x1,))
        h2 = _rms_fwd(x1, row(norm2_g[l]), f"l{l}_rms2", after=(tok,))
        W_gu = landed(l, 2, fs, arr, (h2,))
        gu, act = _mm_swiglu(h2, W_gu, name=f"l{l}_mm_swiglu")
        fs, arr, tok = arrive(l, 3, (act,))
        W_down = landed(l, 3, fs, arr, (tok,))
        x2 = _mm_nn(act, W_down, res=x1, out_dtype=F32, tk=1408, name=f"l{l}_mm_down")
        acts.append(dict(x=xs, h=h, z=z, y=y, x1=x1, h2=h2, gu=gu, act=act, sm=sm,
                         W_in=W_in, W_out=W_out, W_gu=W_gu, W_down=W_down))
        xs = x2

    loss_vec, dx, dxb, dgf = _loss_head(xs, row(final_norm_g), loss_target[0], "loss_head")
    loss = lax.psum(loss_vec[0, 0], ("x", "y", "c"))

    big_w = {"in": (w_in, m_w_in, v_w_in), "out": (w_out, m_w_out, v_w_out), "gate": (w_gate, m_w_gate, v_w_gate),
             "up": (w_up, m_w_up, v_w_up), "down": (w_down, m_w_down, v_w_down)}
    big = {nm: None for nm in big_w}

    def adamw(l, names, gs):
        outs = None
        for nm, g in zip(names, gs):
            w, m, v = big_w[nm]
            big[nm] = outs = _adamw_layer(l, w, g, m, v, big[nm], f"l{l}_adamw_{nm}")
        return outs[1]

    small_grads = [None] * L
    pend_dg = pend_oi = None
    tok = ()
    for l in reversed(range(L)):
        a = acts[l]
        sm = a["sm"]
        dgu = _mm_nt_swiglu_bwd(dxb, a["W_down"], a["gu"], name=f"l{l}_bwd_down", after=tok)
        if pend_dg:
            tok = (pend_dg[0].exchange_to_join((dgu,)),)
        dW_down = _mm_tn(a["act"], dxb, name=f"l{l}_dw_down", after=tok)
        if pend_dg:
            tok = (adamw(pend_dg[1], ["down", "gate", "up"], pend_dg[0].finish((dW_down,))),)
        dh2 = _mm_nt_dgu(dgu, a["W_gu"], name=f"l{l}_bwd_gu", after=tok)
        if pend_oi:
            tok = (pend_oi[0].exchange_to_join((dh2,)),)
        dW_gu = _mm_tn(a["h2"], dgu, tkw=1024, tnw=512, name=f"l{l}_dw_gu", after=tok)
        if pend_oi:
            tok = (adamw(pend_oi[1], ["out", "in"], pend_oi[0].finish((dW_gu,))),)
        dg_batch = _GradBatch(f"l{l}_dg", [dW_down, dW_gu], [g_down, g_gu], [(0, g_down), (1, g_gate), (1, g_up)], pos)
        tok = (dg_batch.start_swap((dW_gu,) + tuple(tok)),)
        dx1, dx1b, dg2 = _rms_bwd(dh2, a["x1"], row(norm2_g[l]), dx, f"l{l}_rms2_bwd", after=tok)
        dy = _mm_nt(dx1b, a["W_out"], out_dtype=F32, name=f"l{l}_bwd_out")
        dW_out = _mm_tn(a["y"], dx1b, name=f"l{l}_dw_out")
        tok = (dg_batch.swap_to_exchange((dW_out, dy)),)
        dz, dgn, dlng, dlnb, dws, dbs, dcw = _mixer_bwd(
            a["z"], dy, sm["lng"], sm["lnb"], sm["wm"], sm["wmt"], sm["bfull"], sm["cw"], sm["gn"],
            f"l{l}_mixer_bwd", after=tok)
        dW_in = _mm_tn(a["h"], dz, name=f"l{l}_dw_in")
        oi_batch = _GradBatch(f"l{l}_oi", [dW_out, dW_in], [g_out, g_in], [(0, g_out), (1, g_in)], pos)
        tok = (oi_batch.start_swap((dW_in,)),)
        dh = _mm_nt(dz, a["W_in"], out_dtype=F32, tk=1280, name=f"l{l}_bwd_in", after=tok)
        dx, dxb, dg1 = _rms_bwd(dh, a["x"], row(norm1_g[l]), dx1, f"l{l}_rms1_bwd")
        tok = (oi_batch.swap_to_exchange((dx,)),)
        small_grads[l] = [dg1[0], dlng[0], dlnb[0], dws, dbs[:, 0, :], dcw[:CONV_TAPS], dgn[0], dg2[0]]
        pend_dg, pend_oi = (dg_batch, l), (oi_batch, l)
    grad_x = dx[None]

    tok = (pend_dg[0].exchange_to_join(tok),)
    pieces = [p for l in range(L) for p in small_grads[l]] + [dgf[0]]
    red = _allreduce_small(_pack(pieces), after=tok)
    tok = (adamw(pend_dg[1], ["down", "gate", "up"], pend_dg[0].finish((red,))),)
    tok = (pend_oi[0].exchange_to_join(tok),)
    adamw(pend_oi[1], ["out", "in"], pend_oi[0].finish(tok))

    red_list = _unpack(red, [p.shape for p in pieces])
    per = len(small_grads[0])
    stack = lambda i: jnp.stack([red_list[l * per + i] for l in range(L)])
    g_small = {"norm1_g": stack(0), "gmlp_ln_g": stack(1), "gmlp_ln_b": stack(2), "w_spatial": stack(3),
               "b_spatial": stack(4), "group_norm_g": stack(6), "norm2_g": stack(7),
               "final_norm_g": red_list[L * per]}
    g_small["conv_w"] = lax.dynamic_slice_in_dim(stack(5), pos[0] * cb, cb, axis=2)
    small_w = {"norm1_g": (norm1_g, m_norm1_g, v_norm1_g), "gmlp_ln_g": (gmlp_ln_g, m_gmlp_ln_g, v_gmlp_ln_g),
               "gmlp_ln_b": (gmlp_ln_b, m_gmlp_ln_b, v_gmlp_ln_b), "w_spatial": (w_spatial, m_w_spatial, v_w_spatial),
               "b_spatial": (b_spatial, m_b_spatial, v_b_spatial), "conv_w": (conv_w, m_conv_w, v_conv_w),
               "group_norm_g": (group_norm_g, m_group_norm_g, v_group_norm_g),
               "norm2_g": (norm2_g, m_norm2_g, v_norm2_g), "final_norm_g": (final_norm_g, m_final_norm_g, v_final_norm_g)}
    snames = list(small_w)
    sd, smn, svn = _adamw_flat(_pack([small_w[n][0] for n in snames]), _pack([g_small[n] for n in snames]),
                               _pack([small_w[n][1] for n in snames]), _pack([small_w[n][2] for n in snames]),
                               "adamw_small")
    sshapes = [small_w[n][0].shape for n in snames]
    sd, smn, svn = _unpack(sd, sshapes), _unpack(smn, sshapes), _unpack(svn, sshapes)
    small_out = {n: (g_small[n], sd[i], smn[i], svn[i]) for i, n in enumerate(snames)}

    order = ["norm1_g", "w_in", "gmlp_ln_g", "gmlp_ln_b", "w_spatial", "b_spatial", "conv_w", "group_norm_g",
             "w_out", "norm2_g", "w_gate", "w_up", "w_down", "final_norm_g"]
    res = {n: (big[n[2:]] if n[2:] in big else small_out[n]) for n in order}
    return (loss, grad_x, *[res[n][0] for n in order], *[res[n][1] for n in order],
            *[res[n][2] for n in order], *[res[n][3] for n in order])
```

```python
import math

import jax
import jax.numpy as jnp
from jax import lax
from jax.experimental import pallas as pl
from jax.experimental.pallas import tpu as pltpu

RMS_EPS = 1e-6
LN_EPS = 1e-5
HEAD_DIM = 128
CHUNK = 64
CONV_TAPS = 3
HALO = 16
ADAM_LR = 0.001
ADAM_B1 = 0.9
ADAM_B2 = 0.999
ADAM_EPS = 1e-08
ADAM_WD = 0.01
ADAM_STEP = 10
V7X_VMEM_LIMIT = 56 * 1024 * 1024
N_CHIPS = 4
DMA_CHUNK_BYTES = 2 * 1024 * 1024
MESH = pl.DeviceIdType.MESH
F32 = jnp.float32
BF16 = jnp.bfloat16
ANY = pl.BlockSpec(memory_space=pl.ANY)
HBM = pl.BlockSpec(memory_space=pltpu.HBM)
SEM = pl.BlockSpec(memory_space=pltpu.SEMAPHORE)
EFFECT = pltpu.SideEffectType.DATAFLOW_SIDE_EFFECTING


def _tile(n, pref):
    if n <= pref:
        return n
    best = None
    for t in range(128, pref + 1, 128):
        if n % t == 0:
            best = t
    assert best is not None, (n, pref)
    return best


def _rows_tile(rows, cols, budget_elems):
    tr = rows
    while tr * cols > budget_elems and tr % 2 == 0 and (tr // 2) % 16 == 0:
        tr //= 2
    return tr


def _params(*sem):
    return pltpu.CompilerParams(dimension_semantics=sem if sem else None,
                                vmem_limit_bytes=V7X_VMEM_LIMIT)


def _call(body, args, after, **kw):
    n, na = len(args), len(after)
    if na:
        inner = body

        def body(*refs):
            inner(*refs[:n], *refs[n + na:])

        kw["in_specs"] = list(kw["in_specs"]) + [ANY] * na
    return pl.pallas_call(body, **kw)(*args, *after)


def _prefetch_call(body, pos, args, after=(), *, out_shape, grid, in_specs, out_specs, aliases=None, name):
    n, na = 1 + len(args), len(after)
    if na:
        inner = body

        def body(*refs):
            inner(*refs[:n], *refs[n + na:])

    return pl.pallas_call(
        body, out_shape=out_shape,
        grid_spec=pltpu.PrefetchScalarGridSpec(num_scalar_prefetch=1, grid=grid,
                                               in_specs=list(in_specs) + [ANY] * na, out_specs=out_specs),
        input_output_aliases=aliases or {}, name=name,
        compiler_params=_params(*(["parallel"] * len(grid))))(pos, *args, *after)


def _gelu(x):
    c = math.sqrt(2.0 / math.pi)
    return 0.5 * x * (1.0 + jnp.tanh(c * (x + 0.044715 * x * x * x)))


def _gelu_and_grad(x):
    c = math.sqrt(2.0 / math.pi)
    x2 = x * x
    th = jnp.tanh(c * (x + 0.044715 * x * x2))
    val = 0.5 * x * (1.0 + th)
    grad = 0.5 * (1.0 + th) + 0.5 * x * (1.0 - th * th) * (c * (1.0 + 3.0 * 0.044715 * x2))
    return val, grad


def _sigmoid(x):
    return 1.0 / (1.0 + jnp.exp(-x))


def _nt(a, b):
    return lax.dot_general(a, b, (((1,), (1,)), ((), ())), preferred_element_type=F32)


def _tn(a, b):
    return lax.dot_general(a, b, (((0,), (0,)), ((), ())), preferred_element_type=F32)


def _rms_fwd(x, g, name, after=()):
    T, D = x.shape
    tr = _tile(T, 512)

    def body(x_ref, g_ref, h_ref):
        xv = x_ref[...]
        r = lax.rsqrt(jnp.mean(xv * xv, axis=-1, keepdims=True) + RMS_EPS)
        h_ref[...] = ((xv * r) * g_ref[...]).astype(h_ref.dtype)

    return _call(
        body, [x, g], after, out_shape=jax.ShapeDtypeStruct((T, D), BF16), grid=(T // tr,),
        in_specs=[pl.BlockSpec((tr, D), lambda i: (i, 0)), pl.BlockSpec((1, D), lambda i: (0, 0))],
        out_specs=pl.BlockSpec((tr, D), lambda i: (i, 0)),
        name=name, compiler_params=_params("parallel"))


def _rms_bwd(dh, x, g, dres, name, after=()):
    T, D = x.shape
    tr = _tile(T, 256)

    def body(dh_ref, x_ref, g_ref, dres_ref, dx_ref, dxb_ref, dg_ref):
        i = pl.program_id(0)
        xv = x_ref[...]
        dhv = dh_ref[...]
        r = lax.rsqrt(jnp.mean(xv * xv, axis=-1, keepdims=True) + RMS_EPS)
        xh = xv * r
        q = dhv * g_ref[...]
        dx = dres_ref[...] + r * (q - xh * jnp.mean(q * xh, axis=-1, keepdims=True))
        dx_ref[...] = dx
        dxb_ref[...] = dx.astype(BF16)
        part = jnp.sum(dhv * xh, axis=0, keepdims=True)

        @pl.when(i == 0)
        def _():
            dg_ref[...] = part

        @pl.when(i > 0)
        def _():
            dg_ref[...] += part

    row = pl.BlockSpec((tr, D), lambda i: (i, 0))
    vec = pl.BlockSpec((1, D), lambda i: (0, 0))
    return _call(
        body, [dh, x, g, dres], after,
        out_shape=(jax.ShapeDtypeStruct((T, D), F32), jax.ShapeDtypeStruct((T, D), BF16),
                   jax.ShapeDtypeStruct((1, D), F32)),
        grid=(T // tr,), in_specs=[row, row, vec, row], out_specs=(row, row, vec),
        name=name, compiler_params=_params("arbitrary"))


def _loss_head(x, g, tgt, name):
    T, D = x.shape
    tr = _tile(T, 256)

    def body(x_ref, g_ref, t_ref, loss_ref, dx_ref, dxb_ref, dg_ref):
        i = pl.program_id(0)
        xv = x_ref[...]
        gv = g_ref[...]
        r = lax.rsqrt(jnp.mean(xv * xv, axis=-1, keepdims=True) + RMS_EPS)
        xh = xv * r
        err = xh * gv - t_ref[...]
        lpart = jnp.full((1, 128), 0.5 * jnp.sum(jnp.mean(err * err, axis=-1, keepdims=True)), F32)
        dy = err * (1.0 / D)
        q = dy * gv
        dx = r * (q - xh * jnp.mean(q * xh, axis=-1, keepdims=True))
        dx_ref[...] = dx
        dxb_ref[...] = dx.astype(BF16)
        gpart = jnp.sum(dy * xh, axis=0, keepdims=True)

        @pl.when(i == 0)
        def _():
            loss_ref[...] = lpart
            dg_ref[...] = gpart

        @pl.when(i > 0)
        def _():
            loss_ref[...] += lpart
            dg_ref[...] += gpart

    row = pl.BlockSpec((tr, D), lambda i: (i, 0))
    vec = pl.BlockSpec((1, D), lambda i: (0, 0))
    return pl.pallas_call(
        body,
        out_shape=(jax.ShapeDtypeStruct((1, 128), F32), jax.ShapeDtypeStruct((T, D), F32),
                   jax.ShapeDtypeStruct((T, D), BF16), jax.ShapeDtypeStruct((1, D), F32)),
        grid=(T // tr,), in_specs=[row, vec, row],
        out_specs=(pl.BlockSpec((1, 128), lambda i: (0, 0)), row, row, vec),
        name=name, compiler_params=_params("arbitrary"))(x, g, tgt)


def _mm_nn(a, w, *, res=None, out_dtype, tm=1024, tn=1024, tk=None, name, after=()):
    M, K = a.shape
    N = w.shape[1]
    tm, tn = _tile(M, tm), _tile(N, tn)
    tk = K if tk is None else _tile(K, tk)
    nk = K // tk
    has_res = res is not None

    def body(*refs):
        a_ref, w_ref = refs[0], refs[1]
        r_ref = refs[2] if has_res else None
        o_ref = refs[2 + has_res]
        part = jnp.dot(a_ref[...], w_ref[...], preferred_element_type=F32)

        def finish(acc):
            if has_res:
                acc = r_ref[...] + acc
            o_ref[...] = acc.astype(o_ref.dtype)

        if nk == 1:
            finish(part)
        else:
            acc_ref = refs[3 + has_res]
            k = pl.program_id(2)

            @pl.when(k == 0)
            def _():
                acc_ref[...] = part

            @pl.when(jnp.logical_and(k > 0, k < nk - 1))
            def _():
                acc_ref[...] += part

            @pl.when(k == nk - 1)
            def _():
                finish(acc_ref[...] + part)

    in_specs = [pl.BlockSpec((tm, tk), lambda i, j, k: (i, k)),
                pl.BlockSpec((tk, tn), lambda i, j, k: (k, j))]
    args = [a, w]
    if has_res:
        in_specs.append(pl.BlockSpec((tm, tn), lambda i, j, k: (i, j)))
        args.append(res)
    return _call(
        body, args, after, out_shape=jax.ShapeDtypeStruct((M, N), out_dtype), grid=(M // tm, N // tn, nk),
        in_specs=in_specs, out_specs=pl.BlockSpec((tm, tn), lambda i, j, k: (i, j)),
        scratch_shapes=[pltpu.VMEM((tm, tn), F32)] if nk > 1 else [],
        name=name, compiler_params=_params("parallel", "parallel", "arbitrary"))


def _mm_swiglu(h, wgu, *, tm=1024, tn=512, name):
    T, D = h.shape
    F = wgu.shape[1] // 2
    tm, tn = _tile(T, tm), _tile(F, tn)
    nf = F // tn

    def body(h_ref, wg_ref, wu_ref, gu_ref, act_ref):
        hv = h_ref[...]
        g = jnp.dot(hv, wg_ref[...], preferred_element_type=F32)
        u = jnp.dot(hv, wu_ref[...], preferred_element_type=F32)
        act_ref[...] = ((g * _sigmoid(g)) * u).astype(BF16)
        gu_ref[0] = g.astype(BF16)
        gu_ref[1] = u.astype(BF16)

    return pl.pallas_call(
        body,
        out_shape=(jax.ShapeDtypeStruct((2, T, F), BF16), jax.ShapeDtypeStruct((T, F), BF16)),
        grid=(T // tm, nf),
        in_specs=[pl.BlockSpec((tm, D), lambda i, j: (i, 0)),
                  pl.BlockSpec((D, tn), lambda i, j: (0, j)),
                  pl.BlockSpec((D, tn), lambda i, j: (0, j + nf))],
        out_specs=(pl.BlockSpec((2, tm, tn), lambda i, j: (0, i, j)),
                   pl.BlockSpec((tm, tn), lambda i, j: (i, j))),
        name=name, compiler_params=_params("parallel", "parallel"))(h, wgu, wgu)


def _mm_nt(a, w, *, out_dtype, tm=1024, tn=1024, tk=None, name, after=()):
    M, K = a.shape
    N = w.shape[0]
    tm, tn = _tile(M, tm), _tile(N, tn)
    tk = K if tk is None else _tile(K, tk)
    nk = K // tk

    def body(*refs):
        a_ref, w_ref, o_ref = refs[0], refs[1], refs[2]
        part = _nt(a_ref[...], w_ref[...])
        if nk == 1:
            o_ref[...] = part.astype(o_ref.dtype)
        else:
            acc_ref = refs[3]
            k = pl.program_id(2)

            @pl.when(k == 0)
            def _():
                acc_ref[...] = part

            @pl.when(jnp.logical_and(k > 0, k < nk - 1))
            def _():
                acc_ref[...] += part

            @pl.when(k == nk - 1)
            def _():
                o_ref[...] = (acc_ref[...] + part).astype(o_ref.dtype)

    return _call(
        body, [a, w], after, out_shape=jax.ShapeDtypeStruct((M, N), out_dtype), grid=(M // tm, N // tn, nk),
        in_specs=[pl.BlockSpec((tm, tk), lambda i, j, k: (i, k)),
                  pl.BlockSpec((tn, tk), lambda i, j, k: (j, k))],
        out_specs=pl.BlockSpec((tm, tn), lambda i, j, k: (i, j)),
        scratch_shapes=[pltpu.VMEM((tm, tn), F32)] if nk > 1 else [],
        name=name, compiler_params=_params("parallel", "parallel", "arbitrary"))


def _mm_nt_swiglu_bwd(dxb, wdown, gu, *, tm=1024, tn=512, name, after=()):
    T, D = dxb.shape
    F = wdown.shape[0]
    tm, tn = _tile(T, tm), _tile(F, tn)

    def body(dx_ref, w_ref, gu_ref, dgu_ref):
        da = _nt(dx_ref[...], w_ref[...])
        g = gu_ref[0].astype(F32)
        u = gu_ref[1].astype(F32)
        s = _sigmoid(g)
        dgu_ref[0] = ((da * u) * (s * (1.0 + g * (1.0 - s)))).astype(BF16)
        dgu_ref[1] = (da * (g * s)).astype(BF16)

    blk3 = pl.BlockSpec((2, tm, tn), lambda i, j: (0, i, j))
    return _call(
        body, [dxb, wdown, gu], after, out_shape=jax.ShapeDtypeStruct((2, T, F), BF16), grid=(T // tm, F // tn),
        in_specs=[pl.BlockSpec((tm, D), lambda i, j: (i, 0)),
                  pl.BlockSpec((tn, D), lambda i, j: (j, 0)), blk3],
        out_specs=blk3, name=name, compiler_params=_params("parallel", "parallel"))


def _mm_nt_dgu(dgu, wgu, *, tm=1024, tn=1024, tk=1408, name, after=()):
    _, T, F = dgu.shape
    D = wgu.shape[0]
    tm, tn, tk = _tile(T, tm), _tile(D, tn), _tile(F, tk)
    nkf = F // tk
    nk = 2 * nkf

    def body(a_ref, w_ref, o_ref, acc_ref):
        k = pl.program_id(2)
        part = _nt(a_ref[...], w_ref[...])

        @pl.when(k == 0)
        def _():
            acc_ref[...] = part

        @pl.when(jnp.logical_and(k > 0, k < nk - 1))
        def _():
            acc_ref[...] += part

        @pl.when(k == nk - 1)
        def _():
            o_ref[...] = acc_ref[...] + part

    return _call(
        body, [dgu, wgu], after, out_shape=jax.ShapeDtypeStruct((T, D), F32), grid=(T // tm, D // tn, nk),
        in_specs=[pl.BlockSpec((None, tm, tk), lambda i, j, k: (k // nkf, i, k % nkf)),
                  pl.BlockSpec((tn, tk), lambda i, j, k: (j, k))],
        out_specs=pl.BlockSpec((tm, tn), lambda i, j, k: (i, j)),
        scratch_shapes=[pltpu.VMEM((tm, tn), F32)],
        name=name, compiler_params=_params("parallel", "parallel", "arbitrary"))


def _mm_tn(a, g, *, tkw=512, tnw=1024, name, after=()):
    T, Kw = a.shape
    pair = g.ndim == 3
    Nw = 2 * g.shape[2] if pair else g.shape[1]
    tkw = _tile(Kw, tkw)
    tnw = _tile(g.shape[2] if pair else Nw, tnw)
    nf = (Nw // 2) // tnw if pair else 0

    def body(a_ref, g_ref, o_ref):
        o_ref[...] = _tn(a_ref[...], g_ref[...]).astype(o_ref.dtype)

    if pair:
        g_spec = pl.BlockSpec((None, T, tnw), lambda i, j: (j // nf, 0, j % nf))
    else:
        g_spec = pl.BlockSpec((T, tnw), lambda i, j: (0, j))
    return _call(
        body, [a, g], after, out_shape=jax.ShapeDtypeStruct((Kw, Nw), BF16), grid=(Kw // tkw, Nw // tnw),
        in_specs=[pl.BlockSpec((T, tkw), lambda i, j: (0, i)), g_spec],
        out_specs=pl.BlockSpec((tkw, tnw), lambda i, j: (i, j)),
        name=name, compiler_params=_params("parallel", "parallel"))


def _mixer_specs(T, A, tr):
    nin = 5 * A
    nb = tr // HALO
    last = T // HALO - 1
    prev = pl.BlockSpec((HALO, nin), lambda i: (jnp.maximum(i * nb - 1, 0), 0))
    cur = pl.BlockSpec((tr, nin), lambda i: (i, 0))
    nxt = pl.BlockSpec((HALO, nin), lambda i: (jnp.minimum((i + 1) * nb, last), 0))
    return prev, cur, nxt


def _conv(p_ext, cw_ref):
    return (cw_ref[2:3, :] * p_ext + cw_ref[1:2, :] * pltpu.roll(p_ext, 1, 0)
            + cw_ref[0:1, :] * pltpu.roll(p_ext, 2, 0))


def _mixer_fwd(z, lng, lnb, wm, bfull, cw, gn, name, after=()):
    T, nin = z.shape
    A = nin // 5
    H = A // HEAD_DIM
    tr = _tile(T, 256)
    nblk = tr // HEAD_DIM

    def body(zp_ref, z_ref, lng_ref, lnb_ref, wm_ref, bf_ref, cw_ref, gn_ref, y_ref, vn_ref, mix_ref):
        i = pl.program_id(0)
        u = _gelu(z_ref[:, 0:A].astype(F32))
        vg = _gelu(z_ref[:, A:2 * A].astype(F32))
        xc = vg - jnp.mean(vg, axis=-1, keepdims=True)
        rstd = lax.rsqrt(jnp.mean(xc * xc, axis=-1, keepdims=True) + LN_EPS)
        vn_ref[...] = ((xc * rstd) * lng_ref[...] + lnb_ref[...]).astype(BF16)
        for cb in range(nblk):
            rows = slice(cb * HEAD_DIM, (cb + 1) * HEAD_DIM)
            for h in range(H):
                cols = slice(h * HEAD_DIM, (h + 1) * HEAD_DIM)
                mix_ref[rows, cols] = jnp.dot(wm_ref[h], vn_ref[rows, cols],
                                              preferred_element_type=F32) + bf_ref[h]
        ya = u * mix_ref[...]
        ra = lax.rsqrt(jnp.mean(ya * ya, axis=-1, keepdims=True) + RMS_EPS)
        y_ref[:, 0:A] = ((ya * ra) * gn_ref[:, 0:A]).astype(BF16)

        p_prev = zp_ref[:, 3 * A:4 * A].astype(F32) * zp_ref[:, 4 * A:5 * A].astype(F32)
        p_prev = jnp.where(i > 0, p_prev, 0.0)
        p_cur = z_ref[:, 3 * A:4 * A].astype(F32) * z_ref[:, 4 * A:5 * A].astype(F32)
        cv = _conv(jnp.concatenate([p_prev, p_cur], axis=0), cw_ref)[HALO:]
        yb = z_ref[:, 2 * A:3 * A].astype(F32) * cv
        rb = lax.rsqrt(jnp.mean(yb * yb, axis=-1, keepdims=True) + RMS_EPS)
        y_ref[:, A:2 * A] = ((yb * rb) * gn_ref[:, A:2 * A]).astype(BF16)

    prev, cur, _ = _mixer_specs(T, A, tr)
    full = lambda shape: pl.BlockSpec(shape, lambda i: (0,) * len(shape))
    return _call(
        body, [z, z, lng, lnb, wm, bfull, cw, gn], after,
        out_shape=jax.ShapeDtypeStruct((T, 2 * A), BF16), grid=(T // tr,),
        in_specs=[prev, cur, full((1, A)), full((1, A)), full((H, HEAD_DIM, HEAD_DIM)),
                  full((H, HEAD_DIM, HEAD_DIM)), full((8, A)), full((1, 2 * A))],
        out_specs=pl.BlockSpec((tr, 2 * A), lambda i: (i, 0)),
        scratch_shapes=[pltpu.VMEM((tr, A), BF16), pltpu.VMEM((tr, A), F32)],
        name=name, compiler_params=_params("parallel"))


def _mixer_bwd(z, dy, lng, lnb, wm, wmt, bfull, cw, gn, name, after=()):
    T, nin = z.shape
    A = nin // 5
    H = A // HEAD_DIM
    tr = _tile(T, 256)
    nblk = tr // HEAD_DIM
    ngrid = T // tr
    next_ = tr + 2 * HALO

    def body(zp_ref, z_ref, zn_ref, dy_ref, dyn_ref, lng_ref, lnb_ref, wm_ref, wmt_ref, bf_ref, cw_ref, gn_ref,
             dz_ref, dgn_ref, dlng_ref, dlnb_ref, dws_ref, dbs_ref, dcw_ref,
             vn_ref, mix_ref, dmix_ref, dvn_ref):
        i = pl.program_id(0)

        @pl.when(i == 0)
        def _():
            dgn_ref[...] = jnp.zeros_like(dgn_ref)
            dlng_ref[...] = jnp.zeros_like(dlng_ref)
            dlnb_ref[...] = jnp.zeros_like(dlnb_ref)
            dws_ref[...] = jnp.zeros_like(dws_ref)
            dbs_ref[...] = jnp.zeros_like(dbs_ref)
            dcw_ref[...] = jnp.zeros_like(dcw_ref)

        u, du_dz = _gelu_and_grad(z_ref[:, 0:A].astype(F32))
        vg, dv_dz = _gelu_and_grad(z_ref[:, A:2 * A].astype(F32))
        xc = vg - jnp.mean(vg, axis=-1, keepdims=True)
        rstd = lax.rsqrt(jnp.mean(xc * xc, axis=-1, keepdims=True) + LN_EPS)
        vhat = xc * rstd
        vn_ref[...] = (vhat * lng_ref[...] + lnb_ref[...]).astype(BF16)
        for cb in range(nblk):
            rows = slice(cb * HEAD_DIM, (cb + 1) * HEAD_DIM)
            for h in range(H):
                cols = slice(h * HEAD_DIM, (h + 1) * HEAD_DIM)
                mix_ref[rows, cols] = jnp.dot(wm_ref[h], vn_ref[rows, cols],
                                              preferred_element_type=F32) + bf_ref[h]
        mixed = mix_ref[...]
        ya = u * mixed
        ra = lax.rsqrt(jnp.mean(ya * ya, axis=-1, keepdims=True) + RMS_EPS)
        yha = ya * ra
        dyan = dy_ref[:, 0:A]
        dgn_ref[:, 0:A] += jnp.sum(dyan * yha, axis=0, keepdims=True)
        qa = dyan * gn_ref[:, 0:A]
        dya = ra * (qa - yha * jnp.mean(qa * yha, axis=-1, keepdims=True))
        dz_ref[:, 0:A] = ((dya * mixed) * du_dz).astype(BF16)
        dmix_ref[...] = (dya * u).astype(BF16)

        ii = lax.broadcasted_iota(jnp.int32, (HEAD_DIM, HEAD_DIM), 0)
        jj = lax.broadcasted_iota(jnp.int32, (HEAD_DIM, HEAD_DIM), 1)
        mask = (jj // CHUNK <= ii // CHUNK).astype(F32)
        ones = jnp.ones((8, HEAD_DIM), BF16)
        for h in range(H):
            cols = slice(h * HEAD_DIM, (h + 1) * HEAD_DIM)
            dws = jnp.zeros((HEAD_DIM, HEAD_DIM), F32)
            dbs = jnp.zeros((8, HEAD_DIM), F32)
            for cb in range(nblk):
                rows = slice(cb * HEAD_DIM, (cb + 1) * HEAD_DIM)
                dm = dmix_ref[rows, cols]
                dws = dws + _nt(dm, vn_ref[rows, cols])
                dbs = dbs + _nt(ones, dm)
                dvn_ref[rows, cols] = jnp.dot(wmt_ref[h], dm, preferred_element_type=F32)
            dws_ref[h] += dws * mask
            dbs_ref[h] += dbs
        dvn = dvn_ref[...]
        dlnb_ref[...] += jnp.sum(dvn, axis=0, keepdims=True)
        dlng_ref[...] += jnp.sum(dvn * vhat, axis=0, keepdims=True)
        dvh = dvn * lng_ref[...]
        dvg = rstd * (dvh - jnp.mean(dvh, axis=-1, keepdims=True)
                      - vhat * jnp.mean(dvh * vhat, axis=-1, keepdims=True))
        dz_ref[:, A:2 * A] = (dvg * dv_dz).astype(BF16)

        def ext(lo):
            mid = z_ref[:, lo:lo + A].astype(F32)
            return jnp.concatenate([zp_ref[:, lo:lo + A].astype(F32), mid, zn_ref[:, lo:lo + A].astype(F32)], axis=0)

        zb, zc, zh = ext(2 * A), ext(3 * A), ext(4 * A)
        row = lax.broadcasted_iota(jnp.int32, (next_, 1), 0)
        p = zc * zh
        p = jnp.where(jnp.logical_and(row < HALO, i == 0), 0.0, p)
        cv = _conv(p, cw_ref)
        yb = zb * cv
        rb = lax.rsqrt(jnp.mean(yb * yb, axis=-1, keepdims=True) + RMS_EPS)
        yhb = yb * rb
        dyn_rows = jnp.where(i < ngrid - 1, dyn_ref[:, A:2 * A], 0.0)
        dybn = jnp.concatenate([jnp.zeros((HALO, A), F32), dy_ref[:, A:2 * A], dyn_rows], axis=0)
        ctr = slice(HALO, HALO + tr)
        dgn_ref[:, A:2 * A] += jnp.sum((dybn * yhb)[ctr], axis=0, keepdims=True)
        qb = dybn * gn_ref[:, A:2 * A]
        dyb = rb * (qb - yhb * jnp.mean(qb * yhb, axis=-1, keepdims=True))
        dcv = dyb * zb
        dp = (cw_ref[2:3, :] * dcv + cw_ref[1:2, :] * pltpu.roll(dcv, next_ - 1, 0)
              + cw_ref[0:1, :] * pltpu.roll(dcv, next_ - 2, 0))
        dz_ref[:, 2 * A:3 * A] = (dyb * cv)[ctr].astype(BF16)
        dz_ref[:, 3 * A:4 * A] = (dp * zh)[ctr].astype(BF16)
        dz_ref[:, 4 * A:5 * A] = (dp * zc)[ctr].astype(BF16)
        dcw_ref[2:3, :] += jnp.sum((dcv * p)[ctr], axis=0, keepdims=True)
        dcw_ref[1:2, :] += jnp.sum((dcv * pltpu.roll(p, 1, 0))[ctr], axis=0, keepdims=True)
        dcw_ref[0:1, :] += jnp.sum((dcv * pltpu.roll(p, 2, 0))[ctr], axis=0, keepdims=True)

    prev, cur, nxt = _mixer_specs(T, A, tr)
    nb = tr // HALO
    dy_cur = pl.BlockSpec((tr, 2 * A), lambda i: (i, 0))
    dy_nxt = pl.BlockSpec((HALO, 2 * A), lambda i: (jnp.minimum((i + 1) * nb, T // HALO - 1), 0))
    full = lambda shape: pl.BlockSpec(shape, lambda i: (0,) * len(shape))
    hh = (H, HEAD_DIM, HEAD_DIM)
    return _call(
        body, [z, z, z, dy, dy, lng, lnb, wm, wmt, bfull, cw, gn], after,
        out_shape=(jax.ShapeDtypeStruct((T, nin), BF16), jax.ShapeDtypeStruct((1, 2 * A), F32),
                   jax.ShapeDtypeStruct((1, A), F32), jax.ShapeDtypeStruct((1, A), F32),
                   jax.ShapeDtypeStruct(hh, F32), jax.ShapeDtypeStruct((H, 8, HEAD_DIM), F32),
                   jax.ShapeDtypeStruct((8, A), F32)),
        grid=(ngrid,),
        in_specs=[prev, cur, nxt, dy_cur, dy_nxt, full((1, A)), full((1, A)), full(hh), full(hh), full(hh),
                  full((8, A)), full((1, 2 * A))],
        out_specs=(pl.BlockSpec((tr, nin), lambda i: (i, 0)), full((1, 2 * A)), full((1, A)), full((1, A)),
                   full(hh), full((H, 8, HEAD_DIM)), full((8, A))),
        scratch_shapes=[pltpu.VMEM((tr, A), BF16), pltpu.VMEM((tr, A), F32), pltpu.VMEM((tr, A), BF16),
                        pltpu.VMEM((tr, A), F32)],
        name=name, compiler_params=_params("arbitrary"))


class _Geom:
    def __init__(self, rows, cols, axis, size, base=0):
        self.rows, self.cols, self.axis, self.size, self.base = rows, cols, axis, size, base

    def in_full(self, j, h):
        if self.axis == 1:
            return (h * (self.rows // 2), self.rows // 2), (self.base + j * self.size, self.size)
        return (self.base + j * self.size, self.size), (h * (self.cols // 2), self.cols // 2)

    def in_shard(self, h):
        if self.axis == 1:
            return (h * (self.rows // 2), self.rows // 2), (0, self.size)
        return (0, self.size), (h * (self.cols // 2), self.cols // 2)

    def half_of_full(self, h):
        if self.axis == 1:
            return (h * (self.rows // 2), self.rows // 2), (0, self.cols)
        return (0, self.rows), (h * (self.cols // 2), self.cols // 2)

    def in_half(self, j):
        if self.axis == 1:
            return (0, self.rows // 2), (self.base + j * self.size, self.size)
        return (self.base + j * self.size, self.size), (0, self.cols // 2)

    @property
    def half_shape(self):
        return (self.rows // 2, self.cols) if self.axis == 1 else (self.rows, self.cols // 2)

    @property
    def shard_half_shape(self):
        return (self.rows // 2, self.size) if self.axis == 1 else (self.size, self.cols // 2)

    @property
    def shard_shape(self):
        return (self.rows, self.size) if self.axis == 1 else (self.size, self.cols)


def _at(ref, region):
    (r0, rn), (c0, cn) = region
    if not isinstance(r0, int):
        r0 = pl.multiple_of(r0, 16)
    if not isinstance(c0, int):
        c0 = pl.multiple_of(c0, 128)
    return ref.at[pl.ds(r0, rn), pl.ds(c0, cn)]


def _whole(shape):
    return (0, shape[-2]), (0, shape[-1])


def _split_rows(region, itemsize):
    (r0, rn), cols = region
    want = max(1, (rn * cols[1] * itemsize) // DMA_CHUNK_BYTES)
    n = 1
    for cand in range(1, want + 1):
        if rn % cand == 0 and (rn // cand) % 16 == 0:
            n = cand
    step = rn // n
    return [((r0 + i * step, step), cols) for i in range(n)]


class _Chunked:
    def __init__(self, make, src, src_reg, dst, dst_reg):
        itemsize = jnp.dtype(src.dtype).itemsize
        self.whole = make(_at(src, src_reg), _at(dst, dst_reg))
        self.parts = [make(_at(src, a), _at(dst, b)) for a, b in
                      zip(_split_rows(src_reg, itemsize), _split_rows(dst_reg, itemsize))]

    def start(self):
        for p in self.parts:
            p.start()


def _mesh_place():
    x, y, c = lax.axis_index("x"), lax.axis_index("y"), lax.axis_index("c")
    chips = [(1 - x, y), (x, 1 - y), (1 - x, 1 - y)]
    return x, y, c, 2 * x + y, chips


def _remote(ssem, rsem, dev):
    return lambda src, dst: pltpu.make_async_remote_copy(
        src_ref=src, dst_ref=dst, send_sem=ssem, recv_sem=rsem, device_id=dev, device_id_type=MESH)


def _comm_call(name, body, arrays, sems_in=(), after=(), sems_out=(), new=()):
    na, ns, nf, no, nn = len(arrays), len(sems_in), len(after), len(sems_out), len(new)

    def kern(*refs):
        sin = refs[na:na + ns]
        outs = refs[na + ns + nf:]
        body(outs[no:no + na], outs[no + na:no + na + nn], sin, outs[:no])
        outs[-1][...] = jnp.zeros_like(outs[-1])

    out_shape = (tuple(pltpu.SemaphoreType.DMA((n,)) for n in sems_out)
                 + tuple(pltpu.HBM(a.shape, a.dtype) for a in arrays)
                 + tuple(pltpu.HBM(s, d) for s, d in new)
                 + (jax.ShapeDtypeStruct((8, 128), F32),))
    res = pl.pallas_call(
        kern, out_shape=out_shape, in_specs=[HBM] * na + [SEM] * ns + [ANY] * nf,
        out_specs=(SEM,) * no + (HBM,) * (na + nn) + (pl.BlockSpec(memory_space=pltpu.VMEM),),
        input_output_aliases={i: no + i for i in range(na)}, name=name,
        compiler_params=pltpu.CompilerParams(has_side_effects=EFFECT),
    )(*[pltpu.with_memory_space_constraint(a, pltpu.HBM) for a in arrays], *sems_in, *after)
    return list(res[:no]), list(res[no:no + na]), list(res[no + na:no + na + nn]), res[-1]


def _gather_start(name, arr, members, after):
    def body(arrs, news, sin, sout):
        x, y, c, j, chips = _mesh_place()
        for m, geo in enumerate(members):
            reg = geo.in_full(j, c)
            for k, chip in enumerate(chips):
                _Chunked(_remote(sout[2 * m].at[k], sout[2 * m + 1].at[k], (*chip, c)),
                         arrs[0], reg, arrs[0], reg).start()

    sems, arrs, _, tok = _comm_call(name, body, [arr], after=after, sems_out=[N_CHIPS - 1] * (2 * len(members)))
    return [sems[2 * m:2 * m + 2] for m in range(len(members))], arrs[0], tok


def _gather_forward(name, arr, members, sems, after):
    def body(arrs, news, sin, sout):
        x, y, c, j, chips = _mesh_place()
        sibling = (x, y, 1 - c)
        full = arrs[0]
        for m, geo in enumerate(members):
            mine = geo.in_full(j, c)
            for k, chip in enumerate(chips):
                got = geo.in_full(2 * chip[0] + chip[1], c)
                sent = _Chunked(_remote(sin[2 * m].at[k], sin[2 * m + 1].at[k], sibling), full, mine, full, got)
                sent.whole.wait_send()
                sent.whole.wait_recv()
                _Chunked(_remote(sout[2 * m].at[k], sout[2 * m + 1].at[k], sibling), full, got, full, got).start()

    flat = [s for pair in sems for s in pair]
    fs, arrs, _, tok = _comm_call(name, body, [arr], sems_in=flat, after=after,
                                  sems_out=[N_CHIPS - 1] * (2 * len(members)))
    return fs, arrs[0], tok


def _gather_wait(name, arr, members, fsems, after):
    def body(arrs, news, sin, sout):
        x, y, c, j, chips = _mesh_place()
        sibling = (x, y, 1 - c)
        full = arrs[0]
        for m, geo in enumerate(members):
            for k, chip in enumerate(chips):
                jk = 2 * chip[0] + chip[1]
                cp = _Chunked(_remote(sin[2 * m].at[k], sin[2 * m + 1].at[k], sibling),
                              full, geo.in_full(jk, c), full, geo.in_full(jk, 1 - c))
                cp.whole.wait_send()
                cp.whole.wait_recv()

    _, arrs, _, _ = _comm_call(name, body, [arr], sems_in=fsems, after=after)
    return arrs[0]


def _gather_conv(conv):
    L, _, cb = conv.shape
    nk = N_CHIPS - 1

    def body(s_ref, f_ref, send_sems, recv_sems, local_sem):
        x, y, c, j, chips = _mesh_place()
        at = lambda jj: f_ref.at[:, :, pl.ds(pl.multiple_of(jj * cb, 128), cb)]
        lc = pltpu.make_async_copy(s_ref, at(j), local_sem.at[0])
        lc.start()
        cps = []
        for k, chip in enumerate(chips):
            cp = _remote(send_sems.at[k], recv_sems.at[k], (*chip, c))(s_ref, at(j))
            cp.start()
            cps.append(cp)
        for k, chip in enumerate(chips):
            jk = 2 * chip[0] + chip[1]
            cps[k].wait_send()
            _remote(send_sems.at[k], recv_sems.at[k], (*chip, c))(at(jk), at(jk)).wait_recv()
        lc.wait()

    return pl.pallas_call(
        body, out_shape=jax.ShapeDtypeStruct((L, 8, N_CHIPS * cb), F32), in_specs=[ANY], out_specs=ANY,
        scratch_shapes=[pltpu.SemaphoreType.DMA((nk,)), pltpu.SemaphoreType.DMA((nk,)),
                        pltpu.SemaphoreType.DMA((1,))],
        name="gather_conv")(conv)


def _swap_start(name, dws, geos, after):
    n = len(dws)

    def body(arrs, news, sin, sout):
        x, y, c, _, _ = _mesh_place()
        for t in range(n):
            _Chunked(_remote(sout[0].at[t], sout[1].at[t], (x, y, 1 - c)),
                     arrs[t], geos[t].half_of_full(1 - c), news[t], _whole(news[t].shape)).start()

    return _comm_call(name, body, dws, after=after, sems_out=[n, n], new=[(g.half_shape, BF16) for g in geos])


def _swap_wait(name, dws, lands, sems, geos, after):
    n = len(dws)

    def body(arrs, news, sin, sout):
        x, y, c, _, _ = _mesh_place()
        for t in range(n):
            cp = _Chunked(_remote(sin[0].at[t], sin[1].at[t], (x, y, 1 - c)),
                          arrs[t], geos[t].half_of_full(1 - c), arrs[n + t], _whole(arrs[n + t].shape))
            cp.whole.wait_send()
            cp.whole.wait_recv()

    _, arrs, _, _ = _comm_call(name, body, list(dws) + list(lands), sems_in=sems, after=after)
    return arrs[:n], arrs[n:]


def _exchange_start(name, ps, entries, after):
    nk = N_CHIPS - 1

    def body(arrs, news, sin, sout):
        x, y, c, j, chips = _mesh_place()
        for e, (pi, geo) in enumerate(entries):
            for k, chip in enumerate(chips):
                dst = news[e].at[k]
                _Chunked(_remote(sout[0].at[nk * e + k], sout[1].at[nk * e + k], (*chip, c)),
                         arrs[pi], geo.in_half(2 * chip[0] + chip[1]), dst, _whole(dst.shape)).start()

    ne = len(entries)
    return _comm_call(name, body, ps, after=after, sems_out=[nk * ne, nk * ne],
                      new=[((nk,) + g.shard_half_shape, BF16) for _, g in entries])


def _exchange_wait(name, ps, lands, sems, entries, after):
    nk = N_CHIPS - 1
    n = len(ps)

    def body(arrs, news, sin, sout):
        x, y, c, j, chips = _mesh_place()
        for e, (pi, geo) in enumerate(entries):
            for k, chip in enumerate(chips):
                dst = arrs[n + e].at[k]
                cp = _Chunked(_remote(sin[0].at[nk * e + k], sin[1].at[nk * e + k], (*chip, c)),
                              arrs[pi], geo.in_half(2 * chip[0] + chip[1]), dst, _whole(dst.shape))
                cp.whole.wait_send()
                cp.whole.wait_recv()

    _, arrs, _, _ = _comm_call(name, body, list(ps) + list(lands), sems_in=sems, after=after)
    return arrs[:n], arrs[n:]


def _join_start(name, gs, geos, after):
    n = len(gs)

    def body(arrs, news, sin, sout):
        x, y, c, _, _ = _mesh_place()
        for t in range(n):
            mine = geos[t].in_shard(c)
            _Chunked(_remote(sout[0].at[t], sout[1].at[t], (x, y, 1 - c)), arrs[t], mine, arrs[t], mine).start()

    return _comm_call(name, body, gs, after=after, sems_out=[n, n])


def _join_wait(name, gs, sems, geos, after):
    n = len(gs)

    def body(arrs, news, sin, sout):
        x, y, c, _, _ = _mesh_place()
        for t in range(n):
            cp = _Chunked(_remote(sin[0].at[t], sin[1].at[t], (x, y, 1 - c)),
                          arrs[t], geos[t].in_shard(c), arrs[t], geos[t].in_shard(1 - c))
            cp.whole.wait_send()
            cp.whole.wait_recv()

    _, arrs, _, _ = _comm_call(name, body, gs, sems_in=sems, after=after)
    return arrs


def _cast_place(w, l, geo, pos, prev, name, after=()):
    if geo.axis == 1:
        tr = _rows_tile(geo.rows, geo.size, 512 * 1024)
        grid = (geo.rows // tr,)
        blk = (tr, geo.size)
        q = geo.base // geo.size
        out_map = lambda i, p: (i, q + p[0])
    else:
        tr = _rows_tile(geo.size, geo.cols, 512 * 1024)
        grid = (geo.size // tr,)
        blk = (tr, geo.cols)
        nb = geo.size // tr
        out_map = lambda i, p: (p[0] * nb + i, 0)

    def body(p_ref, w_ref, *rest):
        rest[-1][...] = w_ref[...].astype(BF16)

    in_specs = [pl.BlockSpec((None,) + blk, lambda i, p: (l, i, 0))]
    args = [w]
    aliases = None
    if prev is not None:
        in_specs.append(ANY)
        args.append(prev)
        aliases = {2: 0}
    return _prefetch_call(body, pos, args, after, out_shape=jax.ShapeDtypeStruct((geo.rows, geo.cols), BF16),
                          grid=grid, in_specs=in_specs, out_specs=pl.BlockSpec(blk, out_map), aliases=aliases,
                          name=name)


def _pair_sum(dw, recv, geo, pos, name):
    hs = geo.half_shape
    tr = _rows_tile(hs[0], hs[1], 512 * 1024)
    nb = hs[0] // tr
    blk = (tr, hs[1])
    if geo.axis == 1:
        own_map = lambda i, p: (p[1] * nb + i, 0)
    else:
        own_map = lambda i, p: (i, p[1])

    def body(p_ref, a_ref, b_ref, o_ref):
        o_ref[...] = (a_ref[...].astype(F32) + b_ref[...].astype(F32)).astype(BF16)

    same = pl.BlockSpec(blk, lambda i, p: (i, 0))
    return _prefetch_call(body, pos, [dw, recv], out_shape=jax.ShapeDtypeStruct(hs, BF16), grid=(nb,),
                          in_specs=[pl.BlockSpec(blk, own_map), same], out_specs=same, name=name)


def _chip_sum(p, recv, geo, pos, name):
    ss = geo.shard_half_shape
    tr = _rows_tile(ss[0], ss[1], 256 * 1024)
    nb = ss[0] // tr
    blk = (tr, ss[1])
    if geo.axis == 1:
        q = geo.base // geo.size
        own_map = lambda i, p_: (i, q + p_[0])
        out_map = lambda i, p_: (p_[1] * nb + i, 0)
    else:
        own_map = lambda i, p_: (p_[0] * nb + i, 0)
        out_map = lambda i, p_: (i, p_[1])

    def body(p_ref, o_ref, r_ref, out_ref):
        acc = o_ref[...].astype(F32)
        for k in range(N_CHIPS - 1):
            acc = acc + r_ref[k].astype(F32)
        out_ref[...] = acc

    return _prefetch_call(
        body, pos, [p, recv], out_shape=jax.ShapeDtypeStruct(geo.shard_shape, F32), grid=(nb,),
        in_specs=[pl.BlockSpec(blk, own_map), pl.BlockSpec((N_CHIPS - 1,) + blk, lambda i, p_: (0, i, 0))],
        out_specs=pl.BlockSpec(blk, out_map), name=name)


def _adamw_math(w, g, m, v):
    m = ADAM_B1 * m + (1.0 - ADAM_B1) * g
    v = ADAM_B2 * v + (1.0 - ADAM_B2) * (g * g)
    m_hat = m / (1.0 - ADAM_B1 ** ADAM_STEP)
    v_hat = v / (1.0 - ADAM_B2 ** ADAM_STEP)
    delta = -ADAM_LR * (m_hat / (jnp.sqrt(v_hat) + ADAM_EPS) + ADAM_WD * w)
    return delta, m, v


def _adamw_layer(l, w, g, m, v, prev, name):
    L, R, C = w.shape
    tr = _rows_tile(R, C, 256 * 1024)

    def body(w_ref, g_ref, m_ref, v_ref, *rest):
        go_ref, d_ref, mo_ref, vo_ref = rest[-4:]
        gv = g_ref[...]
        d, mn, vn = _adamw_math(w_ref[...], gv, m_ref[...], v_ref[...])
        go_ref[...] = gv
        d_ref[...] = d
        mo_ref[...] = mn
        vo_ref[...] = vn

    blk = pl.BlockSpec((None, tr, C), lambda i: (l, i, 0))
    sds = jax.ShapeDtypeStruct(w.shape, F32)
    in_specs = [blk, pl.BlockSpec((tr, C), lambda i: (i, 0)), blk, blk]
    args = [w, g, m, v]
    aliases = {}
    if prev is not None:
        in_specs += [ANY] * 4
        args += list(prev)
        aliases = {4 + i: i for i in range(4)}
    return pl.pallas_call(
        body, out_shape=(sds, sds, sds, sds), grid=(R // tr,), in_specs=in_specs, out_specs=(blk,) * 4,
        input_output_aliases=aliases, name=name, compiler_params=_params("parallel"))(*args)


def _adamw_flat(w, g, m, v, name):
    R, C = w.shape
    tr = _tile(R, 1024) if R % 128 == 0 else R

    def body(w_ref, g_ref, m_ref, v_ref, d_ref, mo_ref, vo_ref):
        d, mn, vn = _adamw_math(w_ref[...], g_ref[...], m_ref[...], v_ref[...])
        d_ref[...] = d
        mo_ref[...] = mn
        vo_ref[...] = vn

    blk = pl.BlockSpec((tr, C), lambda i: (i, 0))
    sds = jax.ShapeDtypeStruct(w.shape, F32)
    return pl.pallas_call(
        body, out_shape=(sds, sds, sds), grid=(R // tr,), in_specs=[blk] * 4, out_specs=(blk,) * 3,
        name=name, compiler_params=_params("parallel"))(w, g, m, v)


def _allreduce_small(s, after=()):
    R, C = s.shape

    def body(s_ref, o_ref, rbuf, send_sems, recv_sems):
        x, y, c = lax.axis_index("x"), lax.axis_index("y"), lax.axis_index("c")
        peers = [(x, y, 1 - c), (1 - x, y, c), (x, 1 - y, c)]
        o_ref[...] = s_ref[...]
        for k, peer in enumerate(peers):
            cp = _remote(send_sems.at[k], recv_sems.at[k], peer)(o_ref, rbuf.at[k])
            cp.start()
            cp.wait()
            o_ref[...] = o_ref[...] + rbuf[k]

    vm = pl.BlockSpec(memory_space=pltpu.VMEM)
    return _call(
        body, [s], after, out_shape=jax.ShapeDtypeStruct((R, C), F32), in_specs=[vm], out_specs=vm,
        scratch_shapes=[pltpu.VMEM((3, R, C), F32), pltpu.SemaphoreType.DMA((3,)), pltpu.SemaphoreType.DMA((3,))],
        name="allreduce_small", compiler_params=pltpu.CompilerParams(vmem_limit_bytes=V7X_VMEM_LIMIT))


class _GradBatch:
    def __init__(self, tag, dws, geos4, entries, pos):
        self.tag, self.dws, self.geos4, self.entries, self.pos = tag, dws, geos4, entries, pos

    def start_swap(self, after):
        self.s1, self.dws, self.land1, tok = _swap_start(f"{self.tag}_swap_start", self.dws, self.geos4, after)
        return tok

    def swap_to_exchange(self, after):
        dws, lands = _swap_wait(f"{self.tag}_swap_wait", self.dws, self.land1, self.s1, self.geos4, after)
        ps = [_pair_sum(d, r, g, self.pos, f"{self.tag}_pair_sum_{i}")
              for i, (d, r, g) in enumerate(zip(dws, lands, self.geos4))]
        self.s2, self.ps, self.land2, tok = _exchange_start(f"{self.tag}_exch_start", ps, self.entries, ())
        return tok

    def exchange_to_join(self, after):
        ps, lands = _exchange_wait(f"{self.tag}_exch_wait", self.ps, self.land2, self.s2, self.entries, after)
        self.geos5 = [g for _, g in self.entries]
        gs = [_chip_sum(ps[pi], r, g, self.pos, f"{self.tag}_chip_sum_{e}")
              for e, ((pi, g), r) in enumerate(zip(self.entries, lands))]
        self.s3, self.gs, _, tok = _join_start(f"{self.tag}_join_start", gs, self.geos5, ())
        return tok

    def finish(self, after):
        return _join_wait(f"{self.tag}_join_wait", self.gs, self.s3, self.geos5, after)


def _pack(pieces):
    rows = []
    for p in pieces:
        flat = p.reshape(-1)
        pad = (-flat.shape[0]) % 1024
        rows.append(jnp.pad(flat, (0, pad)).reshape(-1, 128))
    return jnp.concatenate(rows, axis=0)


def _unpack(buf, shapes):
    out, r = [], 0
    for shp in shapes:
        n = math.prod(shp)
        nr = -(-n // 1024) * 8
        out.append(buf[r:r + nr].reshape(-1)[:n].reshape(shp))
        r += nr
    return out


def kernel(x, norm1_g, w_in, gmlp_ln_g, gmlp_ln_b, w_spatial, b_spatial, conv_w, group_norm_g, w_out, norm2_g, w_gate, w_up, w_down, final_norm_g, loss_target, m_norm1_g, m_w_in, m_gmlp_ln_g, m_gmlp_ln_b, m_w_spatial, m_b_spatial, m_conv_w, m_group_norm_g, m_w_out, m_norm2_g, m_w_gate, m_w_up, m_w_down, m_final_norm_g, v_norm1_g, v_w_in, v_gmlp_ln_g, v_gmlp_ln_b, v_w_spatial, v_b_spatial, v_conv_w, v_group_norm_g, v_w_out, v_norm2_g, v_w_gate, v_w_up, v_w_down, v_final_norm_g):
    L, D, n_in = w_in.shape
    T = x.shape[1]
    nin = N_CHIPS * n_in
    A = nin // 5
    H = A // HEAD_DIM
    n_f = w_gate.shape[2]
    F = N_CHIPS * n_f
    n_o = w_out.shape[1]
    cb = conv_w.shape[2]
    assert A == H * HEAD_DIM and T % 256 == 0 and N_CHIPS * n_o == D and N_CHIPS * cb == A

    g_in, g_out, g_down = _Geom(D, nin, 1, n_in), _Geom(D, D, 0, n_o), _Geom(F, D, 0, n_f)
    g_gate, g_up, g_gu = _Geom(D, 2 * F, 1, n_f, 0), _Geom(D, 2 * F, 1, n_f, F), _Geom(D, 2 * F, 1, n_f)
    pos = jnp.stack([2 * lax.axis_index("x") + lax.axis_index("y"), lax.axis_index("c")]).astype(jnp.int32)
    row = lambda v: v.reshape(1, -1)

    conv_full = _gather_conv(jnp.pad(conv_w, ((0, 0), (0, 8 - CONV_TAPS), (0, 0))))
    members = [[g_in], [g_out], [g_gate, g_up], [g_down]]
    sources = [[w_in], [w_out], [w_gate, w_up], [w_down]]
    placed, sems_of = {}, {}
    tok = conv_full
    for l in range(L):
        for a in range(4):
            arr = None
            for m, (w, geo) in enumerate(zip(sources[a], members[a])):
                arr = _cast_place(w, l, geo, pos, arr, f"l{l}_place_{a}_{m}", after=(tok,))
            sems_of[l, a], placed[l, a], tok = _gather_start(f"l{l}_gather_start_{a}", arr, members[a], (tok,))

    def arrive(l, a, after):
        return _gather_forward(f"l{l}_gather_fwd_{a}", placed[l, a], members[a], sems_of[l, a], after)

    def landed(l, a, fs, arr, after):
        return _gather_wait(f"l{l}_gather_wait_{a}", arr, members[a], fs, after)

    frame = jnp.arange(HEAD_DIM)
    mask = (frame[None, :] // CHUNK) <= (frame[:, None] // CHUNK)

    xs = x[0]
    acts = []
    for l in range(L):
        wm = jnp.where(mask[None], w_spatial[l], 0.0).astype(BF16)
        wmt = jnp.swapaxes(wm, 1, 2)
        bfull = jnp.broadcast_to(b_spatial[l][:, :, None], (H, HEAD_DIM, HEAD_DIM))
        sm = dict(lng=row(gmlp_ln_g[l]), lnb=row(gmlp_ln_b[l]), wm=wm, wmt=wmt, bfull=bfull,
                  cw=conv_full[l], gn=row(group_norm_g[l]))
        if l == 0:
            h = _rms_fwd(xs, row(norm1_g[l]), f"l{l}_rms1", after=(tok,))
            fs, arr, tok = arrive(l, 0, (h,))
            W_in = landed(l, 0, fs, arr, (tok,))
        else:
            fs, arr, tok = arrive(l, 0, (xs,))
            h = _rms_fwd(xs, row(norm1_g[l]), f"l{l}_rms1", after=(tok,))
            W_in = landed(l, 0, fs, arr, (h,))
        z = _mm_nn(h, W_in, out_dtype=BF16, name=f"l{l}_mm_in")
        fs, arr, tok = arrive(l, 1, (z,))
        y = _mixer_fwd(z, sm["lng"], sm["lnb"], wm, bfull, sm["cw"], sm["gn"], f"l{l}_mixer", after=(tok,))
        W_out = landed(l, 1, fs, arr, (y,))
        x1 = _mm_nn(y, W_out, res=xs, out_dtype=F32, name=f"l{l}_mm_out")
        fs, arr, tok = arrive(l, 2, (x1,))
        h2 = _rms_fwd(x1, row(norm2_g[l]), f"l{l}_rms2", after=(tok,))
        W_gu = landed(l, 2, fs, arr, (h2,))
        gu, act = _mm_swiglu(h2, W_gu, name=f"l{l}_mm_swiglu")
        fs, arr, tok = arrive(l, 3, (act,))
        W_down = landed(l, 3, fs, arr, (tok,))
        x2 = _mm_nn(act, W_down, res=x1, out_dtype=F32, tk=1408, name=f"l{l}_mm_down")
        acts.append(dict(x=xs, h=h, z=z, y=y, x1=x1, h2=h2, gu=gu, act=act, sm=sm,
                         W_in=W_in, W_out=W_out, W_gu=W_gu, W_down=W_down))
        xs = x2

    loss_vec, dx, dxb, dgf = _loss_head(xs, row(final_norm_g), loss_target[0], "loss_head")
    loss = lax.psum(loss_vec[0, 0], ("x", "y", "c"))

    big_w = {"in": (w_in, m_w_in, v_w_in), "out": (w_out, m_w_out, v_w_out), "gate": (w_gate, m_w_gate, v_w_gate),
             "up": (w_up, m_w_up, v_w_up), "down": (w_down, m_w_down, v_w_down)}
    big = {nm: None for nm in big_w}

    def adamw(l, names, gs):
        outs = None
        for nm, g in zip(names, gs):
            w, m, v = big_w[nm]
            big[nm] = outs = _adamw_layer(l, w, g, m, v, big[nm], f"l{l}_adamw_{nm}")
        return outs[1]

    small_grads = [None] * L
    pend_dg = pend_oi = None
    tok = ()
    for l in reversed(range(L)):
        a = acts[l]
        sm = a["sm"]
        dgu = _mm_nt_swiglu_bwd(dxb, a["W_down"], a["gu"], name=f"l{l}_bwd_down", after=tok)
        if pend_dg:
            tok = (pend_dg[0].exchange_to_join((dgu,)),)
        dW_down = _mm_tn(a["act"], dxb, name=f"l{l}_dw_down", after=tok)
        if pend_dg:
            tok = (adamw(pend_dg[1], ["down", "gate", "up"], pend_dg[0].finish((dW_down,))),)
        dh2 = _mm_nt_dgu(dgu, a["W_gu"], name=f"l{l}_bwd_gu", after=tok)
        if pend_oi:
            tok = (pend_oi[0].exchange_to_join((dh2,)),)
        dW_gu = _mm_tn(a["h2"], dgu, tkw=1024, tnw=512, name=f"l{l}_dw_gu", after=tok)
        if pend_oi:
            tok = (adamw(pend_oi[1], ["out", "in"], pend_oi[0].finish((dW_gu,))),)
        dg_batch = _GradBatch(f"l{l}_dg", [dW_down, dW_gu], [g_down, g_gu], [(0, g_down), (1, g_gate), (1, g_up)], pos)
        tok = (dg_batch.start_swap(tok),)
        dx1, dx1b, dg2 = _rms_bwd(dh2, a["x1"], row(norm2_g[l]), dx, f"l{l}_rms2_bwd", after=tok)
        dy = _mm_nt(dx1b, a["W_out"], out_dtype=F32, name=f"l{l}_bwd_out")
        dW_out = _mm_tn(a["y"], dx1b, name=f"l{l}_dw_out")
        tok = (dg_batch.swap_to_exchange((dW_out, dy)),)
        dz, dgn, dlng, dlnb, dws, dbs, dcw = _mixer_bwd(
            a["z"], dy, sm["lng"], sm["lnb"], sm["wm"], sm["wmt"], sm["bfull"], sm["cw"], sm["gn"],
            f"l{l}_mixer_bwd", after=tok)
        dW_in = _mm_tn(a["h"], dz, name=f"l{l}_dw_in")
        oi_batch = _GradBatch(f"l{l}_oi", [dW_out, dW_in], [g_out, g_in], [(0, g_out), (1, g_in)], pos)
        tok = (oi_batch.start_swap(()),)
        dh = _mm_nt(dz, a["W_in"], out_dtype=F32, tk=1280, name=f"l{l}_bwd_in", after=tok)
        dx, dxb, dg1 = _rms_bwd(dh, a["x"], row(norm1_g[l]), dx1, f"l{l}_rms1_bwd")
        tok = (oi_batch.swap_to_exchange((dx,)),)
        small_grads[l] = [dg1[0], dlng[0], dlnb[0], dws, dbs[:, 0, :], dcw[:CONV_TAPS], dgn[0], dg2[0]]
        pend_dg, pend_oi = (dg_batch, l), (oi_batch, l)
    grad_x = dx[None]

    tok = (pend_dg[0].exchange_to_join(tok),)
    pieces = [p for l in range(L) for p in small_grads[l]] + [dgf[0]]
    red = _allreduce_small(_pack(pieces), after=tok)
    tok = (adamw(pend_dg[1], ["down", "gate", "up"], pend_dg[0].finish((red,))),)
    tok = (pend_oi[0].exchange_to_join(tok),)
    adamw(pend_oi[1], ["out", "in"], pend_oi[0].finish(tok))

    red_list = _unpack(red, [p.shape for p in pieces])
    per = len(small_grads[0])
    stack = lambda i: jnp.stack([red_list[l * per + i] for l in range(L)])
    g_small = {"norm1_g": stack(0), "gmlp_ln_g": stack(1), "gmlp_ln_b": stack(2), "w_spatial": stack(3),
               "b_spatial": stack(4), "group_norm_g": stack(6), "norm2_g": stack(7),
               "final_norm_g": red_list[L * per]}
    g_small["conv_w"] = lax.dynamic_slice_in_dim(stack(5), pos[0] * cb, cb, axis=2)
    small_w = {"norm1_g": (norm1_g, m_norm1_g, v_norm1_g), "gmlp_ln_g": (gmlp_ln_g, m_gmlp_ln_g, v_gmlp_ln_g),
               "gmlp_ln_b": (gmlp_ln_b, m_gmlp_ln_b, v_gmlp_ln_b), "w_spatial": (w_spatial, m_w_spatial, v_w_spatial),
               "b_spatial": (b_spatial, m_b_spatial, v_b_spatial), "conv_w": (conv_w, m_conv_w, v_conv_w),
               "group_norm_g": (group_norm_g, m_group_norm_g, v_group_norm_g),
               "norm2_g": (norm2_g, m_norm2_g, v_norm2_g), "final_norm_g": (final_norm_g, m_final_norm_g, v_final_norm_g)}
    snames = list(small_w)
    sd, smn, svn = _adamw_flat(_pack([small_w[n][0] for n in snames]), _pack([g_small[n] for n in snames]),
                               _pack([small_w[n][1] for n in snames]), _pack([small_w[n][2] for n in snames]),
                               "adamw_small")
    sshapes = [small_w[n][0].shape for n in snames]
    sd, smn, svn = _unpack(sd, sshapes), _unpack(smn, sshapes), _unpack(svn, sshapes)
    small_out = {n: (g_small[n], sd[i], smn[i], svn[i]) for i, n in enumerate(snames)}

    order = ["norm1_g", "w_in", "gmlp_ln_g", "gmlp_ln_b", "w_spatial", "b_spatial", "conv_w", "group_norm_g",
             "w_out", "norm2_g", "w_gate", "w_up", "w_down", "final_norm_g"]
    res = {n: (big[n[2:]] if n[2:] in big else small_out[n]) for n in order}
    return (loss, grad_x, *[res[n][0] for n in order], *[res[n][1] for n in order],
            *[res[n][2] for n in order], *[res[n][3] for n in order])
```

```python
import math

import jax
import jax.numpy as jnp
from jax import lax
from jax.experimental import pallas as pl
from jax.experimental.pallas import tpu as pltpu

RMS_EPS = 1e-6
LN_EPS = 1e-5
HEAD_DIM = 128
CHUNK = 64
CONV_TAPS = 3
HALO = 16
ADAM_LR = 0.001
ADAM_B1 = 0.9
ADAM_B2 = 0.999
ADAM_EPS = 1e-08
ADAM_WD = 0.01
ADAM_STEP = 10
V7X_VMEM_LIMIT = 56 * 1024 * 1024
N_CHIPS = 4
DMA_CHUNK_BYTES = 2 * 1024 * 1024
EPILOGUE_ROWS = 256
MESH = pl.DeviceIdType.MESH
F32 = jnp.float32
BF16 = jnp.bfloat16
ANY = pl.BlockSpec(memory_space=pl.ANY)
HBM = pl.BlockSpec(memory_space=pltpu.HBM)
SEM = pl.BlockSpec(memory_space=pltpu.SEMAPHORE)
EFFECT = pltpu.SideEffectType.DATAFLOW_SIDE_EFFECTING


def _tile(n, pref):
    if n <= pref:
        return n
    best = None
    for t in range(128, pref + 1, 128):
        if n % t == 0:
            best = t
    assert best is not None, (n, pref)
    return best


def _rows_tile(rows, cols, budget_elems):
    tr = rows
    while tr * cols > budget_elems and tr % 2 == 0 and (tr // 2) % 16 == 0:
        tr //= 2
    return tr


def _params(*sem):
    return pltpu.CompilerParams(dimension_semantics=sem if sem else None,
                                vmem_limit_bytes=V7X_VMEM_LIMIT)


def _call(body, args, after, **kw):
    n, na = len(args), len(after)
    if na:
        inner = body

        def body(*refs):
            inner(*refs[:n], *refs[n + na:])

        kw["in_specs"] = list(kw["in_specs"]) + [ANY] * na
    return pl.pallas_call(body, **kw)(*args, *after)


def _prefetch_call(body, pos, args, after=(), *, out_shape, grid, in_specs, out_specs, aliases=None, name):
    n, na = 1 + len(args), len(after)
    if na:
        inner = body

        def body(*refs):
            inner(*refs[:n], *refs[n + na:])

    return pl.pallas_call(
        body, out_shape=out_shape,
        grid_spec=pltpu.PrefetchScalarGridSpec(num_scalar_prefetch=1, grid=grid,
                                               in_specs=list(in_specs) + [ANY] * na, out_specs=out_specs),
        input_output_aliases=aliases or {}, name=name,
        compiler_params=_params(*(["parallel"] * len(grid))))(pos, *args, *after)


def _gelu(x):
    c = math.sqrt(2.0 / math.pi)
    return 0.5 * x * (1.0 + jnp.tanh(c * (x + 0.044715 * x * x * x)))


def _gelu_and_grad(x):
    c = math.sqrt(2.0 / math.pi)
    x2 = x * x
    th = jnp.tanh(c * (x + 0.044715 * x * x2))
    val = 0.5 * x * (1.0 + th)
    grad = 0.5 * (1.0 + th) + 0.5 * x * (1.0 - th * th) * (c * (1.0 + 3.0 * 0.044715 * x2))
    return val, grad


def _sigmoid(x):
    return 1.0 / (1.0 + jnp.exp(-x))


def _nt(a, b):
    return lax.dot_general(a, b, (((1,), (1,)), ((), ())), preferred_element_type=F32)


def _tn(a, b):
    return lax.dot_general(a, b, (((0,), (0,)), ((), ())), preferred_element_type=F32)


def _rms_fwd(x, g, name, after=()):
    T, D = x.shape
    tr = _tile(T, 512)

    def body(x_ref, g_ref, h_ref):
        xv = x_ref[...]
        r = lax.rsqrt(jnp.mean(xv * xv, axis=-1, keepdims=True) + RMS_EPS)
        h_ref[...] = ((xv * r) * g_ref[...]).astype(h_ref.dtype)

    return _call(
        body, [x, g], after, out_shape=jax.ShapeDtypeStruct((T, D), BF16), grid=(T // tr,),
        in_specs=[pl.BlockSpec((tr, D), lambda i: (i, 0)), pl.BlockSpec((1, D), lambda i: (0, 0))],
        out_specs=pl.BlockSpec((tr, D), lambda i: (i, 0)),
        name=name, compiler_params=_params("parallel"))


def _rms_bwd(dh, x, g, dres, name, after=()):
    T, D = x.shape
    tr = _tile(T, 256)

    def body(dh_ref, x_ref, g_ref, dres_ref, dx_ref, dxb_ref, dg_ref):
        i = pl.program_id(0)
        xv = x_ref[...]
        dhv = dh_ref[...]
        r = lax.rsqrt(jnp.mean(xv * xv, axis=-1, keepdims=True) + RMS_EPS)
        xh = xv * r
        q = dhv * g_ref[...]
        dx = dres_ref[...] + r * (q - xh * jnp.mean(q * xh, axis=-1, keepdims=True))
        dx_ref[...] = dx
        dxb_ref[...] = dx.astype(BF16)
        part = jnp.sum(dhv * xh, axis=0, keepdims=True)

        @pl.when(i == 0)
        def _():
            dg_ref[...] = part

        @pl.when(i > 0)
        def _():
            dg_ref[...] += part

    row = pl.BlockSpec((tr, D), lambda i: (i, 0))
    vec = pl.BlockSpec((1, D), lambda i: (0, 0))
    return _call(
        body, [dh, x, g, dres], after,
        out_shape=(jax.ShapeDtypeStruct((T, D), F32), jax.ShapeDtypeStruct((T, D), BF16),
                   jax.ShapeDtypeStruct((1, D), F32)),
        grid=(T // tr,), in_specs=[row, row, vec, row], out_specs=(row, row, vec),
        name=name, compiler_params=_params("arbitrary"))


def _loss_head(x, g, tgt, name):
    T, D = x.shape
    tr = _tile(T, 256)

    def body(x_ref, g_ref, t_ref, loss_ref, dx_ref, dxb_ref, dg_ref):
        i = pl.program_id(0)
        xv = x_ref[...]
        gv = g_ref[...]
        r = lax.rsqrt(jnp.mean(xv * xv, axis=-1, keepdims=True) + RMS_EPS)
        xh = xv * r
        err = xh * gv - t_ref[...]
        lpart = jnp.full((1, 128), 0.5 * jnp.sum(jnp.mean(err * err, axis=-1, keepdims=True)), F32)
        dy = err * (1.0 / D)
        q = dy * gv
        dx = r * (q - xh * jnp.mean(q * xh, axis=-1, keepdims=True))
        dx_ref[...] = dx
        dxb_ref[...] = dx.astype(BF16)
        gpart = jnp.sum(dy * xh, axis=0, keepdims=True)

        @pl.when(i == 0)
        def _():
            loss_ref[...] = lpart
            dg_ref[...] = gpart

        @pl.when(i > 0)
        def _():
            loss_ref[...] += lpart
            dg_ref[...] += gpart

    row = pl.BlockSpec((tr, D), lambda i: (i, 0))
    vec = pl.BlockSpec((1, D), lambda i: (0, 0))
    return pl.pallas_call(
        body,
        out_shape=(jax.ShapeDtypeStruct((1, 128), F32), jax.ShapeDtypeStruct((T, D), F32),
                   jax.ShapeDtypeStruct((T, D), BF16), jax.ShapeDtypeStruct((1, D), F32)),
        grid=(T // tr,), in_specs=[row, vec, row],
        out_specs=(pl.BlockSpec((1, 128), lambda i: (0, 0)), row, row, vec),
        name=name, compiler_params=_params("arbitrary"))(x, g, tgt)


def _mm_nn(a, w, *, res=None, out_dtype, tm=1024, tn=1024, tk=None, name, after=()):
    M, K = a.shape
    N = w.shape[1]
    tm, tn = _tile(M, tm), _tile(N, tn)
    tk = K if tk is None else _tile(K, tk)
    nk = K // tk
    has_res = res is not None

    def body(*refs):
        a_ref, w_ref = refs[0], refs[1]
        r_ref = refs[2] if has_res else None
        o_ref = refs[2 + has_res]
        part = jnp.dot(a_ref[...], w_ref[...], preferred_element_type=F32)

        def finish(acc):
            if has_res:
                acc = r_ref[...] + acc
            o_ref[...] = acc.astype(o_ref.dtype)

        if nk == 1:
            finish(part)
        else:
            acc_ref = refs[3 + has_res]
            k = pl.program_id(2)

            @pl.when(k == 0)
            def _():
                acc_ref[...] = part

            @pl.when(jnp.logical_and(k > 0, k < nk - 1))
            def _():
                acc_ref[...] += part

            @pl.when(k == nk - 1)
            def _():
                finish(acc_ref[...] + part)

    in_specs = [pl.BlockSpec((tm, tk), lambda i, j, k: (i, k)),
                pl.BlockSpec((tk, tn), lambda i, j, k: (k, j))]
    args = [a, w]
    if has_res:
        in_specs.append(pl.BlockSpec((tm, tn), lambda i, j, k: (i, j)))
        args.append(res)
    return _call(
        body, args, after, out_shape=jax.ShapeDtypeStruct((M, N), out_dtype), grid=(M // tm, N // tn, nk),
        in_specs=in_specs, out_specs=pl.BlockSpec((tm, tn), lambda i, j, k: (i, j)),
        scratch_shapes=[pltpu.VMEM((tm, tn), F32)] if nk > 1 else [],
        name=name, compiler_params=_params("parallel", "parallel", "arbitrary"))


def _mm_swiglu(h, wgu, *, tm=1024, tn=512, name):
    T, D = h.shape
    F = wgu.shape[1] // 2
    tm, tn = _tile(T, tm), _tile(F, tn)
    nf = F // tn

    rc = _tile(tm, EPILOGUE_ROWS)

    def body(h_ref, wg_ref, wu_ref, gu_ref, act_ref):
        for r in range(tm // rc):
            rows = slice(r * rc, (r + 1) * rc)
            hv = h_ref[rows, :]
            g = jnp.dot(hv, wg_ref[...], preferred_element_type=F32)
            u = jnp.dot(hv, wu_ref[...], preferred_element_type=F32)
            act_ref[rows, :] = ((g * _sigmoid(g)) * u).astype(BF16)
            gu_ref[0, rows, :] = g.astype(BF16)
            gu_ref[1, rows, :] = u.astype(BF16)

    return pl.pallas_call(
        body,
        out_shape=(jax.ShapeDtypeStruct((2, T, F), BF16), jax.ShapeDtypeStruct((T, F), BF16)),
        grid=(T // tm, nf),
        in_specs=[pl.BlockSpec((tm, D), lambda i, j: (i, 0)),
                  pl.BlockSpec((D, tn), lambda i, j: (0, j)),
                  pl.BlockSpec((D, tn), lambda i, j: (0, j + nf))],
        out_specs=(pl.BlockSpec((2, tm, tn), lambda i, j: (0, i, j)),
                   pl.BlockSpec((tm, tn), lambda i, j: (i, j))),
        name=name, compiler_params=_params("parallel", "parallel"))(h, wgu, wgu)


def _mm_nt(a, w, *, out_dtype, tm=1024, tn=1024, tk=None, name, after=()):
    M, K = a.shape
    N = w.shape[0]
    tm, tn = _tile(M, tm), _tile(N, tn)
    tk = K if tk is None else _tile(K, tk)
    nk = K // tk

    def body(*refs):
        a_ref, w_ref, o_ref = refs[0], refs[1], refs[2]
        part = _nt(a_ref[...], w_ref[...])
        if nk == 1:
            o_ref[...] = part.astype(o_ref.dtype)
        else:
            acc_ref = refs[3]
            k = pl.program_id(2)

            @pl.when(k == 0)
            def _():
                acc_ref[...] = part

            @pl.when(jnp.logical_and(k > 0, k < nk - 1))
            def _():
                acc_ref[...] += part

            @pl.when(k == nk - 1)
            def _():
                o_ref[...] = (acc_ref[...] + part).astype(o_ref.dtype)

    return _call(
        body, [a, w], after, out_shape=jax.ShapeDtypeStruct((M, N), out_dtype), grid=(M // tm, N // tn, nk),
        in_specs=[pl.BlockSpec((tm, tk), lambda i, j, k: (i, k)),
                  pl.BlockSpec((tn, tk), lambda i, j, k: (j, k))],
        out_specs=pl.BlockSpec((tm, tn), lambda i, j, k: (i, j)),
        scratch_shapes=[pltpu.VMEM((tm, tn), F32)] if nk > 1 else [],
        name=name, compiler_params=_params("parallel", "parallel", "arbitrary"))


def _mm_nt_swiglu_bwd(dxb, wdown, gu, *, tm=1024, tn=512, name, after=()):
    T, D = dxb.shape
    F = wdown.shape[0]
    tm, tn = _tile(T, tm), _tile(F, tn)

    rc = _tile(tm, EPILOGUE_ROWS)

    def body(dx_ref, w_ref, gu_ref, dgu_ref):
        for r in range(tm // rc):
            rows = slice(r * rc, (r + 1) * rc)
            da = _nt(dx_ref[rows, :], w_ref[...])
            g = gu_ref[0, rows, :].astype(F32)
            u = gu_ref[1, rows, :].astype(F32)
            s = _sigmoid(g)
            dgu_ref[0, rows, :] = ((da * u) * (s * (1.0 + g * (1.0 - s)))).astype(BF16)
            dgu_ref[1, rows, :] = (da * (g * s)).astype(BF16)

    blk3 = pl.BlockSpec((2, tm, tn), lambda i, j: (0, i, j))
    return _call(
        body, [dxb, wdown, gu], after, out_shape=jax.ShapeDtypeStruct((2, T, F), BF16), grid=(T // tm, F // tn),
        in_specs=[pl.BlockSpec((tm, D), lambda i, j: (i, 0)),
                  pl.BlockSpec((tn, D), lambda i, j: (j, 0)), blk3],
        out_specs=blk3, name=name, compiler_params=_params("parallel", "parallel"))


def _mm_nt_dgu(dgu, wgu, *, tm=1024, tn=512, tk=5632, name, after=()):
    _, T, F = dgu.shape
    D = wgu.shape[0]
    tm, tn, tk = _tile(T, tm), _tile(D, tn), _tile(F, tk)
    nkf = F // tk
    nk = 2 * nkf

    def body(a_ref, w_ref, o_ref, acc_ref):
        k = pl.program_id(2)
        part = _nt(a_ref[...], w_ref[...])

        @pl.when(k == 0)
        def _():
            acc_ref[...] = part

        @pl.when(jnp.logical_and(k > 0, k < nk - 1))
        def _():
            acc_ref[...] += part

        @pl.when(k == nk - 1)
        def _():
            o_ref[...] = acc_ref[...] + part

    return _call(
        body, [dgu, wgu], after, out_shape=jax.ShapeDtypeStruct((T, D), F32), grid=(T // tm, D // tn, nk),
        in_specs=[pl.BlockSpec((None, tm, tk), lambda i, j, k: (k // nkf, i, k % nkf)),
                  pl.BlockSpec((tn, tk), lambda i, j, k: (j, k))],
        out_specs=pl.BlockSpec((tm, tn), lambda i, j, k: (i, j)),
        scratch_shapes=[pltpu.VMEM((tm, tn), F32)],
        name=name, compiler_params=_params("parallel", "parallel", "arbitrary"))


def _mm_tn(a, g, *, tkw=512, tnw=1024, name, after=()):
    T, Kw = a.shape
    pair = g.ndim == 3
    Nw = 2 * g.shape[2] if pair else g.shape[1]
    tkw = _tile(Kw, tkw)
    tnw = _tile(g.shape[2] if pair else Nw, tnw)
    nf = (Nw // 2) // tnw if pair else 0

    def body(a_ref, g_ref, o_ref):
        o_ref[...] = _tn(a_ref[...], g_ref[...]).astype(o_ref.dtype)

    if pair:
        g_spec = pl.BlockSpec((None, T, tnw), lambda i, j: (j // nf, 0, j % nf))
    else:
        g_spec = pl.BlockSpec((T, tnw), lambda i, j: (0, j))
    return _call(
        body, [a, g], after, out_shape=jax.ShapeDtypeStruct((Kw, Nw), BF16), grid=(Kw // tkw, Nw // tnw),
        in_specs=[pl.BlockSpec((T, tkw), lambda i, j: (0, i)), g_spec],
        out_specs=pl.BlockSpec((tkw, tnw), lambda i, j: (i, j)),
        name=name, compiler_params=_params("parallel", "parallel"))


def _mixer_specs(T, A, tr):
    nin = 5 * A
    nb = tr // HALO
    last = T // HALO - 1
    prev = pl.BlockSpec((HALO, nin), lambda i: (jnp.maximum(i * nb - 1, 0), 0))
    cur = pl.BlockSpec((tr, nin), lambda i: (i, 0))
    nxt = pl.BlockSpec((HALO, nin), lambda i: (jnp.minimum((i + 1) * nb, last), 0))
    return prev, cur, nxt


def _conv(p_ext, cw_ref):
    return (cw_ref[2:3, :] * p_ext + cw_ref[1:2, :] * pltpu.roll(p_ext, 1, 0)
            + cw_ref[0:1, :] * pltpu.roll(p_ext, 2, 0))


def _mixer_fwd(z, lng, lnb, wm, bfull, cw, gn, name, after=()):
    T, nin = z.shape
    A = nin // 5
    H = A // HEAD_DIM
    tr = _tile(T, 256)
    nblk = tr // HEAD_DIM

    def body(zp_ref, z_ref, lng_ref, lnb_ref, wm_ref, bf_ref, cw_ref, gn_ref, y_ref, vn_ref, mix_ref):
        i = pl.program_id(0)
        u = _gelu(z_ref[:, 0:A].astype(F32))
        vg = _gelu(z_ref[:, A:2 * A].astype(F32))
        xc = vg - jnp.mean(vg, axis=-1, keepdims=True)
        rstd = lax.rsqrt(jnp.mean(xc * xc, axis=-1, keepdims=True) + LN_EPS)
        vn_ref[...] = ((xc * rstd) * lng_ref[...] + lnb_ref[...]).astype(BF16)
        for cb in range(nblk):
            rows = slice(cb * HEAD_DIM, (cb + 1) * HEAD_DIM)
            for h in range(H):
                cols = slice(h * HEAD_DIM, (h + 1) * HEAD_DIM)
                mix_ref[rows, cols] = jnp.dot(wm_ref[h], vn_ref[rows, cols],
                                              preferred_element_type=F32) + bf_ref[h]
        ya = u * mix_ref[...]
        ra = lax.rsqrt(jnp.mean(ya * ya, axis=-1, keepdims=True) + RMS_EPS)
        y_ref[:, 0:A] = ((ya * ra) * gn_ref[:, 0:A]).astype(BF16)

        p_prev = zp_ref[:, 3 * A:4 * A].astype(F32) * zp_ref[:, 4 * A:5 * A].astype(F32)
        p_prev = jnp.where(i > 0, p_prev, 0.0)
        p_cur = z_ref[:, 3 * A:4 * A].astype(F32) * z_ref[:, 4 * A:5 * A].astype(F32)
        cv = _conv(jnp.concatenate([p_prev, p_cur], axis=0), cw_ref)[HALO:]
        yb = z_ref[:, 2 * A:3 * A].astype(F32) * cv
        rb = lax.rsqrt(jnp.mean(yb * yb, axis=-1, keepdims=True) + RMS_EPS)
        y_ref[:, A:2 * A] = ((yb * rb) * gn_ref[:, A:2 * A]).astype(BF16)

    prev, cur, _ = _mixer_specs(T, A, tr)
    full = lambda shape: pl.BlockSpec(shape, lambda i: (0,) * len(shape))
    return _call(
        body, [z, z, lng, lnb, wm, bfull, cw, gn], after,
        out_shape=jax.ShapeDtypeStruct((T, 2 * A), BF16), grid=(T // tr,),
        in_specs=[prev, cur, full((1, A)), full((1, A)), full((H, HEAD_DIM, HEAD_DIM)),
                  full((H, HEAD_DIM, HEAD_DIM)), full((8, A)), full((1, 2 * A))],
        out_specs=pl.BlockSpec((tr, 2 * A), lambda i: (i, 0)),
        scratch_shapes=[pltpu.VMEM((tr, A), BF16), pltpu.VMEM((tr, A), F32)],
        name=name, compiler_params=_params("parallel"))


def _mixer_bwd(z, dy, lng, lnb, wm, wmt, bfull, cw, gn, name, after=()):
    T, nin = z.shape
    A = nin // 5
    H = A // HEAD_DIM
    tr = _tile(T, 256)
    nblk = tr // HEAD_DIM
    ngrid = T // tr
    next_ = tr + 2 * HALO

    def body(zp_ref, z_ref, zn_ref, dy_ref, dyn_ref, lng_ref, lnb_ref, wm_ref, wmt_ref, bf_ref, cw_ref, gn_ref,
             dz_ref, dgn_ref, dlng_ref, dlnb_ref, dws_ref, dbs_ref, dcw_ref,
             vn_ref, mix_ref, dmix_ref, dvn_ref):
        i = pl.program_id(0)

        @pl.when(i == 0)
        def _():
            dgn_ref[...] = jnp.zeros_like(dgn_ref)
            dlng_ref[...] = jnp.zeros_like(dlng_ref)
            dlnb_ref[...] = jnp.zeros_like(dlnb_ref)
            dws_ref[...] = jnp.zeros_like(dws_ref)
            dbs_ref[...] = jnp.zeros_like(dbs_ref)
            dcw_ref[...] = jnp.zeros_like(dcw_ref)

        u, du_dz = _gelu_and_grad(z_ref[:, 0:A].astype(F32))
        vg, dv_dz = _gelu_and_grad(z_ref[:, A:2 * A].astype(F32))
        xc = vg - jnp.mean(vg, axis=-1, keepdims=True)
        rstd = lax.rsqrt(jnp.mean(xc * xc, axis=-1, keepdims=True) + LN_EPS)
        vhat = xc * rstd
        vn_ref[...] = (vhat * lng_ref[...] + lnb_ref[...]).astype(BF16)
        for cb in range(nblk):
            rows = slice(cb * HEAD_DIM, (cb + 1) * HEAD_DIM)
            for h in range(H):
                cols = slice(h * HEAD_DIM, (h + 1) * HEAD_DIM)
                mix_ref[rows, cols] = jnp.dot(wm_ref[h], vn_ref[rows, cols],
                                              preferred_element_type=F32) + bf_ref[h]
        mixed = mix_ref[...]
        ya = u * mixed
        ra = lax.rsqrt(jnp.mean(ya * ya, axis=-1, keepdims=True) + RMS_EPS)
        yha = ya * ra
        dyan = dy_ref[:, 0:A]
        dgn_ref[:, 0:A] += jnp.sum(dyan * yha, axis=0, keepdims=True)
        qa = dyan * gn_ref[:, 0:A]
        dya = ra * (qa - yha * jnp.mean(qa * yha, axis=-1, keepdims=True))
        dz_ref[:, 0:A] = ((dya * mixed) * du_dz).astype(BF16)
        dmix_ref[...] = (dya * u).astype(BF16)

        ii = lax.broadcasted_iota(jnp.int32, (HEAD_DIM, HEAD_DIM), 0)
        jj = lax.broadcasted_iota(jnp.int32, (HEAD_DIM, HEAD_DIM), 1)
        mask = (jj // CHUNK <= ii // CHUNK).astype(F32)
        ones = jnp.ones((8, HEAD_DIM), BF16)
        for h in range(H):
            cols = slice(h * HEAD_DIM, (h + 1) * HEAD_DIM)
            dws = jnp.zeros((HEAD_DIM, HEAD_DIM), F32)
            dbs = jnp.zeros((8, HEAD_DIM), F32)
            for cb in range(nblk):
                rows = slice(cb * HEAD_DIM, (cb + 1) * HEAD_DIM)
                dm = dmix_ref[rows, cols]
                dws = dws + _nt(dm, vn_ref[rows, cols])
                dbs = dbs + _nt(ones, dm)
                dvn_ref[rows, cols] = jnp.dot(wmt_ref[h], dm, preferred_element_type=F32)
            dws_ref[h] += dws * mask
            dbs_ref[h] += dbs
        dvn = dvn_ref[...]
        dlnb_ref[...] += jnp.sum(dvn, axis=0, keepdims=True)
        dlng_ref[...] += jnp.sum(dvn * vhat, axis=0, keepdims=True)
        dvh = dvn * lng_ref[...]
        dvg = rstd * (dvh - jnp.mean(dvh, axis=-1, keepdims=True)
                      - vhat * jnp.mean(dvh * vhat, axis=-1, keepdims=True))
        dz_ref[:, A:2 * A] = (dvg * dv_dz).astype(BF16)

        def ext(lo):
            mid = z_ref[:, lo:lo + A].astype(F32)
            return jnp.concatenate([zp_ref[:, lo:lo + A].astype(F32), mid, zn_ref[:, lo:lo + A].astype(F32)], axis=0)

        zb, zc, zh = ext(2 * A), ext(3 * A), ext(4 * A)
        row = lax.broadcasted_iota(jnp.int32, (next_, 1), 0)
        p = zc * zh
        p = jnp.where(jnp.logical_and(row < HALO, i == 0), 0.0, p)
        cv = _conv(p, cw_ref)
        yb = zb * cv
        rb = lax.rsqrt(jnp.mean(yb * yb, axis=-1, keepdims=True) + RMS_EPS)
        yhb = yb * rb
        dyn_rows = jnp.where(i < ngrid - 1, dyn_ref[:, A:2 * A], 0.0)
        dybn = jnp.concatenate([jnp.zeros((HALO, A), F32), dy_ref[:, A:2 * A], dyn_rows], axis=0)
        ctr = slice(HALO, HALO + tr)
        dgn_ref[:, A:2 * A] += jnp.sum((dybn * yhb)[ctr], axis=0, keepdims=True)
        qb = dybn * gn_ref[:, A:2 * A]
        dyb = rb * (qb - yhb * jnp.mean(qb * yhb, axis=-1, keepdims=True))
        dcv = dyb * zb
        dp = (cw_ref[2:3, :] * dcv + cw_ref[1:2, :] * pltpu.roll(dcv, next_ - 1, 0)
              + cw_ref[0:1, :] * pltpu.roll(dcv, next_ - 2, 0))
        dz_ref[:, 2 * A:3 * A] = (dyb * cv)[ctr].astype(BF16)
        dz_ref[:, 3 * A:4 * A] = (dp * zh)[ctr].astype(BF16)
        dz_ref[:, 4 * A:5 * A] = (dp * zc)[ctr].astype(BF16)
        dcw_ref[2:3, :] += jnp.sum((dcv * p)[ctr], axis=0, keepdims=True)
        dcw_ref[1:2, :] += jnp.sum((dcv * pltpu.roll(p, 1, 0))[ctr], axis=0, keepdims=True)
        dcw_ref[0:1, :] += jnp.sum((dcv * pltpu.roll(p, 2, 0))[ctr], axis=0, keepdims=True)

    prev, cur, nxt = _mixer_specs(T, A, tr)
    nb = tr // HALO
    dy_cur = pl.BlockSpec((tr, 2 * A), lambda i: (i, 0))
    dy_nxt = pl.BlockSpec((HALO, 2 * A), lambda i: (jnp.minimum((i + 1) * nb, T // HALO - 1), 0))
    full = lambda shape: pl.BlockSpec(shape, lambda i: (0,) * len(shape))
    hh = (H, HEAD_DIM, HEAD_DIM)
    return _call(
        body, [z, z, z, dy, dy, lng, lnb, wm, wmt, bfull, cw, gn], after,
        out_shape=(jax.ShapeDtypeStruct((T, nin), BF16), jax.ShapeDtypeStruct((1, 2 * A), F32),
                   jax.ShapeDtypeStruct((1, A), F32), jax.ShapeDtypeStruct((1, A), F32),
                   jax.ShapeDtypeStruct(hh, F32), jax.ShapeDtypeStruct((H, 8, HEAD_DIM), F32),
                   jax.ShapeDtypeStruct((8, A), F32)),
        grid=(ngrid,),
        in_specs=[prev, cur, nxt, dy_cur, dy_nxt, full((1, A)), full((1, A)), full(hh), full(hh), full(hh),
                  full((8, A)), full((1, 2 * A))],
        out_specs=(pl.BlockSpec((tr, nin), lambda i: (i, 0)), full((1, 2 * A)), full((1, A)), full((1, A)),
                   full(hh), full((H, 8, HEAD_DIM)), full((8, A))),
        scratch_shapes=[pltpu.VMEM((tr, A), BF16), pltpu.VMEM((tr, A), F32), pltpu.VMEM((tr, A), BF16),
                        pltpu.VMEM((tr, A), F32)],
        name=name, compiler_params=_params("arbitrary"))


class _Geom:
    def __init__(self, rows, cols, axis, size, base=0):
        self.rows, self.cols, self.axis, self.size, self.base = rows, cols, axis, size, base

    def in_full(self, j, h):
        if self.axis == 1:
            return (h * (self.rows // 2), self.rows // 2), (self.base + j * self.size, self.size)
        return (self.base + j * self.size, self.size), (h * (self.cols // 2), self.cols // 2)

    def in_shard(self, h):
        if self.axis == 1:
            return (h * (self.rows // 2), self.rows // 2), (0, self.size)
        return (0, self.size), (h * (self.cols // 2), self.cols // 2)

    def half_of_full(self, h):
        if self.axis == 1:
            return (h * (self.rows // 2), self.rows // 2), (0, self.cols)
        return (0, self.rows), (h * (self.cols // 2), self.cols // 2)

    def in_half(self, j):
        if self.axis == 1:
            return (0, self.rows // 2), (self.base + j * self.size, self.size)
        return (self.base + j * self.size, self.size), (0, self.cols // 2)

    @property
    def half_shape(self):
        return (self.rows // 2, self.cols) if self.axis == 1 else (self.rows, self.cols // 2)

    @property
    def shard_half_shape(self):
        return (self.rows // 2, self.size) if self.axis == 1 else (self.size, self.cols // 2)

    @property
    def shard_shape(self):
        return (self.rows, self.size) if self.axis == 1 else (self.size, self.cols)


def _at(ref, region):
    (r0, rn), (c0, cn) = region
    if not isinstance(r0, int):
        r0 = pl.multiple_of(r0, 16)
    if not isinstance(c0, int):
        c0 = pl.multiple_of(c0, 128)
    return ref.at[pl.ds(r0, rn), pl.ds(c0, cn)]


def _whole(shape):
    return (0, shape[-2]), (0, shape[-1])


def _split_rows(region, itemsize):
    (r0, rn), cols = region
    want = max(1, (rn * cols[1] * itemsize) // DMA_CHUNK_BYTES)
    n = 1
    for cand in range(1, want + 1):
        if rn % cand == 0 and (rn // cand) % 16 == 0:
            n = cand
    step = rn // n
    return [((r0 + i * step, step), cols) for i in range(n)]


class _Chunked:
    def __init__(self, make, src, src_reg, dst, dst_reg):
        self.make, self.src, self.src_reg, self.dst, self.dst_reg = make, src, src_reg, dst, dst_reg

    @property
    def whole(self):
        return self.make(_at(self.src, self.src_reg), _at(self.dst, self.dst_reg))

    def start(self):
        itemsize = jnp.dtype(self.src.dtype).itemsize
        for a, b in zip(_split_rows(self.src_reg, itemsize), _split_rows(self.dst_reg, itemsize)):
            self.make(_at(self.src, a), _at(self.dst, b)).start()


def _mesh_place():
    x, y, c = lax.axis_index("x"), lax.axis_index("y"), lax.axis_index("c")
    chips = [(1 - x, y), (x, 1 - y), (1 - x, 1 - y)]
    return x, y, c, 2 * x + y, chips


def _remote(ssem, rsem, dev):
    return lambda src, dst: pltpu.make_async_remote_copy(
        src_ref=src, dst_ref=dst, send_sem=ssem, recv_sem=rsem, device_id=dev, device_id_type=MESH)


def _comm_call(name, body, arrays, sems_in=(), after=(), sems_out=(), new=()):
    na, ns, nf, no, nn = len(arrays), len(sems_in), len(after), len(sems_out), len(new)

    def kern(*refs):
        sin = refs[na:na + ns]
        outs = refs[na + ns + nf:]
        body(outs[no:no + na], outs[no + na:no + na + nn], sin, outs[:no])
        outs[-1][...] = jnp.zeros_like(outs[-1])

    out_shape = (tuple(pltpu.SemaphoreType.DMA((n,)) for n in sems_out)
                 + tuple(pltpu.HBM(a.shape, a.dtype) for a in arrays)
                 + tuple(pltpu.HBM(s, d) for s, d in new)
                 + (jax.ShapeDtypeStruct((8, 128), F32),))
    res = pl.pallas_call(
        kern, out_shape=out_shape, in_specs=[HBM] * na + [SEM] * ns + [ANY] * nf,
        out_specs=(SEM,) * no + (HBM,) * (na + nn) + (pl.BlockSpec(memory_space=pltpu.VMEM),),
        input_output_aliases={i: no + i for i in range(na)}, name=name,
        compiler_params=pltpu.CompilerParams(has_side_effects=EFFECT),
    )(*[pltpu.with_memory_space_constraint(a, pltpu.HBM) for a in arrays], *sems_in, *after)
    return list(res[:no]), list(res[no:no + na]), list(res[no + na:no + na + nn]), res[-1]


def _gather_start(name, arr, members, after):
    def body(arrs, news, sin, sout):
        x, y, c, j, chips = _mesh_place()
        for m, geo in enumerate(members):
            reg = geo.in_full(j, c)
            for k, chip in enumerate(chips):
                _Chunked(_remote(sout[2 * m].at[k], sout[2 * m + 1].at[k], (*chip, c)),
                         arrs[0], reg, arrs[0], reg).start()

    sems, arrs, _, tok = _comm_call(name, body, [arr], after=after, sems_out=[N_CHIPS - 1] * (2 * len(members)))
    return [sems[2 * m:2 * m + 2] for m in range(len(members))], arrs[0], tok


def _gather_forward(name, arr, members, sems, after):
    def body(arrs, news, sin, sout):
        x, y, c, j, chips = _mesh_place()
        sibling = (x, y, 1 - c)
        full = arrs[0]
        for m, geo in enumerate(members):
            mine = geo.in_full(j, c)
            for k, chip in enumerate(chips):
                got = geo.in_full(2 * chip[0] + chip[1], c)
                sent = _Chunked(_remote(sin[2 * m].at[k], sin[2 * m + 1].at[k], sibling), full, mine, full, got)
                sent.whole.wait_send()
                sent.whole.wait_recv()
                _Chunked(_remote(sout[2 * m].at[k], sout[2 * m + 1].at[k], sibling), full, got, full, got).start()

    flat = [s for pair in sems for s in pair]
    fs, arrs, _, tok = _comm_call(name, body, [arr], sems_in=flat, after=after,
                                  sems_out=[N_CHIPS - 1] * (2 * len(members)))
    return fs, arrs[0], tok


def _gather_wait(name, arr, members, fsems, after):
    def body(arrs, news, sin, sout):
        x, y, c, j, chips = _mesh_place()
        sibling = (x, y, 1 - c)
        full = arrs[0]
        for m, geo in enumerate(members):
            for k, chip in enumerate(chips):
                jk = 2 * chip[0] + chip[1]
                cp = _Chunked(_remote(sin[2 * m].at[k], sin[2 * m + 1].at[k], sibling),
                              full, geo.in_full(jk, c), full, geo.in_full(jk, 1 - c))
                cp.whole.wait_send()
                cp.whole.wait_recv()

    _, arrs, _, _ = _comm_call(name, body, [arr], sems_in=fsems, after=after)
    return arrs[0]


def _gather_conv(conv):
    L, _, cb = conv.shape
    nk = N_CHIPS - 1

    def body(s_ref, f_ref, send_sems, recv_sems, local_sem):
        x, y, c, j, chips = _mesh_place()
        at = lambda jj: f_ref.at[:, :, pl.ds(pl.multiple_of(jj * cb, 128), cb)]
        lc = pltpu.make_async_copy(s_ref, at(j), local_sem.at[0])
        lc.start()
        cps = []
        for k, chip in enumerate(chips):
            cp = _remote(send_sems.at[k], recv_sems.at[k], (*chip, c))(s_ref, at(j))
            cp.start()
            cps.append(cp)
        for k, chip in enumerate(chips):
            jk = 2 * chip[0] + chip[1]
            cps[k].wait_send()
            _remote(send_sems.at[k], recv_sems.at[k], (*chip, c))(at(jk), at(jk)).wait_recv()
        lc.wait()

    return pl.pallas_call(
        body, out_shape=jax.ShapeDtypeStruct((L, 8, N_CHIPS * cb), F32), in_specs=[ANY], out_specs=ANY,
        scratch_shapes=[pltpu.SemaphoreType.DMA((nk,)), pltpu.SemaphoreType.DMA((nk,)),
                        pltpu.SemaphoreType.DMA((1,))],
        name="gather_conv")(conv)


def _swap_start(name, dws, geos, after):
    n = len(dws)

    def body(arrs, news, sin, sout):
        x, y, c, _, _ = _mesh_place()
        for t in range(n):
            _Chunked(_remote(sout[0].at[t], sout[1].at[t], (x, y, 1 - c)),
                     arrs[t], geos[t].half_of_full(1 - c), news[t], _whole(news[t].shape)).start()

    return _comm_call(name, body, dws, after=after, sems_out=[n, n], new=[(g.half_shape, BF16) for g in geos])


def _swap_wait(name, dws, lands, sems, geos, after):
    n = len(dws)

    def body(arrs, news, sin, sout):
        x, y, c, _, _ = _mesh_place()
        for t in range(n):
            cp = _Chunked(_remote(sin[0].at[t], sin[1].at[t], (x, y, 1 - c)),
                          arrs[t], geos[t].half_of_full(1 - c), arrs[n + t], _whole(arrs[n + t].shape))
            cp.whole.wait_send()
            cp.whole.wait_recv()

    _, arrs, _, _ = _comm_call(name, body, list(dws) + list(lands), sems_in=sems, after=after)
    return arrs[:n], arrs[n:]


def _exchange_start(name, ps, entries, after):
    nk = N_CHIPS - 1

    def body(arrs, news, sin, sout):
        x, y, c, j, chips = _mesh_place()
        for e, (pi, geo) in enumerate(entries):
            for k, chip in enumerate(chips):
                dst = news[e].at[k]
                _Chunked(_remote(sout[0].at[nk * e + k], sout[1].at[nk * e + k], (*chip, c)),
                         arrs[pi], geo.in_half(2 * chip[0] + chip[1]), dst, _whole(dst.shape)).start()

    ne = len(entries)
    return _comm_call(name, body, ps, after=after, sems_out=[nk * ne, nk * ne],
                      new=[((nk,) + g.shard_half_shape, BF16) for _, g in entries])


def _exchange_wait(name, ps, lands, sems, entries, after):
    nk = N_CHIPS - 1
    n = len(ps)

    def body(arrs, news, sin, sout):
        x, y, c, j, chips = _mesh_place()
        for e, (pi, geo) in enumerate(entries):
            for k, chip in enumerate(chips):
                dst = arrs[n + e].at[k]
                cp = _Chunked(_remote(sin[0].at[nk * e + k], sin[1].at[nk * e + k], (*chip, c)),
                              arrs[pi], geo.in_half(2 * chip[0] + chip[1]), dst, _whole(dst.shape))
                cp.whole.wait_send()
                cp.whole.wait_recv()

    _, arrs, _, _ = _comm_call(name, body, list(ps) + list(lands), sems_in=sems, after=after)
    return arrs[:n], arrs[n:]


def _join_start(name, gs, geos, after):
    n = len(gs)

    def body(arrs, news, sin, sout):
        x, y, c, _, _ = _mesh_place()
        for t in range(n):
            mine = geos[t].in_shard(c)
            _Chunked(_remote(sout[0].at[t], sout[1].at[t], (x, y, 1 - c)), arrs[t], mine, arrs[t], mine).start()

    return _comm_call(name, body, gs, after=after, sems_out=[n, n])


def _join_wait(name, gs, sems, geos, after):
    n = len(gs)

    def body(arrs, news, sin, sout):
        x, y, c, _, _ = _mesh_place()
        for t in range(n):
            cp = _Chunked(_remote(sin[0].at[t], sin[1].at[t], (x, y, 1 - c)),
                          arrs[t], geos[t].in_shard(c), arrs[t], geos[t].in_shard(1 - c))
            cp.whole.wait_send()
            cp.whole.wait_recv()

    _, arrs, _, _ = _comm_call(name, body, gs, sems_in=sems, after=after)
    return arrs


def _cast_place(w, l, geo, pos, prev, name, after=()):
    if geo.axis == 1:
        tr = _rows_tile(geo.rows, geo.size, 512 * 1024)
        grid = (geo.rows // tr,)
        blk = (tr, geo.size)
        q = geo.base // geo.size
        out_map = lambda i, p: (i, q + p[0])
    else:
        tr = _rows_tile(geo.size, geo.cols, 512 * 1024)
        grid = (geo.size // tr,)
        blk = (tr, geo.cols)
        nb = geo.size // tr
        out_map = lambda i, p: (p[0] * nb + i, 0)

    def body(p_ref, w_ref, *rest):
        rest[-1][...] = w_ref[...].astype(BF16)

    in_specs = [pl.BlockSpec((None,) + blk, lambda i, p: (l, i, 0))]
    args = [w]
    aliases = None
    if prev is not None:
        in_specs.append(ANY)
        args.append(prev)
        aliases = {2: 0}
    return _prefetch_call(body, pos, args, after, out_shape=jax.ShapeDtypeStruct((geo.rows, geo.cols), BF16),
                          grid=grid, in_specs=in_specs, out_specs=pl.BlockSpec(blk, out_map), aliases=aliases,
                          name=name)


def _pair_sum(dw, recv, geo, pos, name):
    hs = geo.half_shape
    tr = _rows_tile(hs[0], hs[1], 512 * 1024)
    nb = hs[0] // tr
    blk = (tr, hs[1])
    if geo.axis == 1:
        own_map = lambda i, p: (p[1] * nb + i, 0)
    else:
        own_map = lambda i, p: (i, p[1])

    def body(p_ref, a_ref, b_ref, o_ref):
        o_ref[...] = (a_ref[...].astype(F32) + b_ref[...].astype(F32)).astype(BF16)

    same = pl.BlockSpec(blk, lambda i, p: (i, 0))
    return _prefetch_call(body, pos, [dw, recv], out_shape=jax.ShapeDtypeStruct(hs, BF16), grid=(nb,),
                          in_specs=[pl.BlockSpec(blk, own_map), same], out_specs=same, name=name)


def _chip_sum(p, recv, geo, pos, name):
    ss = geo.shard_half_shape
    tr = _rows_tile(ss[0], ss[1], 256 * 1024)
    nb = ss[0] // tr
    blk = (tr, ss[1])
    if geo.axis == 1:
        q = geo.base // geo.size
        own_map = lambda i, p_: (i, q + p_[0])
        out_map = lambda i, p_: (p_[1] * nb + i, 0)
    else:
        own_map = lambda i, p_: (p_[0] * nb + i, 0)
        out_map = lambda i, p_: (i, p_[1])

    def body(p_ref, o_ref, r_ref, out_ref):
        acc = o_ref[...].astype(F32)
        for k in range(N_CHIPS - 1):
            acc = acc + r_ref[k].astype(F32)
        out_ref[...] = acc

    return _prefetch_call(
        body, pos, [p, recv], out_shape=jax.ShapeDtypeStruct(geo.shard_shape, F32), grid=(nb,),
        in_specs=[pl.BlockSpec(blk, own_map), pl.BlockSpec((N_CHIPS - 1,) + blk, lambda i, p_: (0, i, 0))],
        out_specs=pl.BlockSpec(blk, out_map), name=name)


def _adamw_math(w, g, m, v):
    m = ADAM_B1 * m + (1.0 - ADAM_B1) * g
    v = ADAM_B2 * v + (1.0 - ADAM_B2) * (g * g)
    m_hat = m / (1.0 - ADAM_B1 ** ADAM_STEP)
    v_hat = v / (1.0 - ADAM_B2 ** ADAM_STEP)
    delta = -ADAM_LR * (m_hat / (jnp.sqrt(v_hat) + ADAM_EPS) + ADAM_WD * w)
    return delta, m, v


def _adamw_layer(l, w, g, m, v, prev, name, after=()):
    L, R, C = w.shape
    tr = _rows_tile(R, C, 256 * 1024)

    def body(w_ref, g_ref, m_ref, v_ref, *rest):
        go_ref, d_ref, mo_ref, vo_ref = rest[-4:]
        gv = g_ref[...]
        d, mn, vn = _adamw_math(w_ref[...], gv, m_ref[...], v_ref[...])
        go_ref[...] = gv
        d_ref[...] = d
        mo_ref[...] = mn
        vo_ref[...] = vn

    blk = pl.BlockSpec((None, tr, C), lambda i: (l, i, 0))
    sds = jax.ShapeDtypeStruct(w.shape, F32)
    in_specs = [blk, pl.BlockSpec((tr, C), lambda i: (i, 0)), blk, blk]
    args = [w, g, m, v]
    aliases = {}
    if prev is not None:
        in_specs += [ANY] * 4
        args += list(prev)
        aliases = {4 + i: i for i in range(4)}
    return _call(
        body, args, after, out_shape=(sds, sds, sds, sds), grid=(R // tr,), in_specs=in_specs,
        out_specs=(blk,) * 4, input_output_aliases=aliases, name=name, compiler_params=_params("parallel"))


def _adamw_flat(w, g, m, v, name):
    R, C = w.shape
    tr = _tile(R, 1024) if R % 128 == 0 else R

    def body(w_ref, g_ref, m_ref, v_ref, d_ref, mo_ref, vo_ref):
        d, mn, vn = _adamw_math(w_ref[...], g_ref[...], m_ref[...], v_ref[...])
        d_ref[...] = d
        mo_ref[...] = mn
        vo_ref[...] = vn

    blk = pl.BlockSpec((tr, C), lambda i: (i, 0))
    sds = jax.ShapeDtypeStruct(w.shape, F32)
    return pl.pallas_call(
        body, out_shape=(sds, sds, sds), grid=(R // tr,), in_specs=[blk] * 4, out_specs=(blk,) * 3,
        name=name, compiler_params=_params("parallel"))(w, g, m, v)


def _allreduce_small(s, after=()):
    R, C = s.shape

    def body(s_ref, o_ref, rbuf, send_sems, recv_sems):
        x, y, c = lax.axis_index("x"), lax.axis_index("y"), lax.axis_index("c")
        peers = [(x, y, 1 - c), (1 - x, y, c), (x, 1 - y, c)]
        o_ref[...] = s_ref[...]
        for k, peer in enumerate(peers):
            cp = _remote(send_sems.at[k], recv_sems.at[k], peer)(o_ref, rbuf.at[k])
            cp.start()
            cp.wait()
            o_ref[...] = o_ref[...] + rbuf[k]

    vm = pl.BlockSpec(memory_space=pltpu.VMEM)
    return _call(
        body, [s], after, out_shape=jax.ShapeDtypeStruct((R, C), F32), in_specs=[vm], out_specs=vm,
        scratch_shapes=[pltpu.VMEM((3, R, C), F32), pltpu.SemaphoreType.DMA((3,)), pltpu.SemaphoreType.DMA((3,))],
        name="allreduce_small", compiler_params=pltpu.CompilerParams(vmem_limit_bytes=V7X_VMEM_LIMIT))


class _GradBatch:
    def __init__(self, tag, dws, geos4, entries, pos):
        self.tag, self.dws, self.geos4, self.entries, self.pos = tag, dws, geos4, entries, pos

    def start_swap(self, after):
        self.s1, self.dws, self.land1, tok = _swap_start(f"{self.tag}_swap_start", self.dws, self.geos4, after)
        return tok

    def swap_to_exchange(self, after):
        dws, lands = _swap_wait(f"{self.tag}_swap_wait", self.dws, self.land1, self.s1, self.geos4, after)
        ps = [_pair_sum(d, r, g, self.pos, f"{self.tag}_pair_sum_{i}")
              for i, (d, r, g) in enumerate(zip(dws, lands, self.geos4))]
        self.s2, self.ps, self.land2, tok = _exchange_start(f"{self.tag}_exch_start", ps, self.entries, ())
        return tok

    def exchange_to_join(self, after):
        ps, lands = _exchange_wait(f"{self.tag}_exch_wait", self.ps, self.land2, self.s2, self.entries, after)
        self.geos5 = [g for _, g in self.entries]
        gs = [_chip_sum(ps[pi], r, g, self.pos, f"{self.tag}_chip_sum_{e}")
              for e, ((pi, g), r) in enumerate(zip(self.entries, lands))]
        self.s3, self.gs, _, tok = _join_start(f"{self.tag}_join_start", gs, self.geos5, ())
        return tok

    def finish(self, after):
        return _join_wait(f"{self.tag}_join_wait", self.gs, self.s3, self.geos5, after)


def _pack(pieces):
    rows = []
    for p in pieces:
        flat = p.reshape(-1)
        pad = (-flat.shape[0]) % 1024
        rows.append(jnp.pad(flat, (0, pad)).reshape(-1, 128))
    return jnp.concatenate(rows, axis=0)


def _unpack(buf, shapes):
    out, r = [], 0
    for shp in shapes:
        n = math.prod(shp)
        nr = -(-n // 1024) * 8
        out.append(buf[r:r + nr].reshape(-1)[:n].reshape(shp))
        r += nr
    return out


def kernel(x, norm1_g, w_in, gmlp_ln_g, gmlp_ln_b, w_spatial, b_spatial, conv_w, group_norm_g, w_out, norm2_g, w_gate, w_up, w_down, final_norm_g, loss_target, m_norm1_g, m_w_in, m_gmlp_ln_g, m_gmlp_ln_b, m_w_spatial, m_b_spatial, m_conv_w, m_group_norm_g, m_w_out, m_norm2_g, m_w_gate, m_w_up, m_w_down, m_final_norm_g, v_norm1_g, v_w_in, v_gmlp_ln_g, v_gmlp_ln_b, v_w_spatial, v_b_spatial, v_conv_w, v_group_norm_g, v_w_out, v_norm2_g, v_w_gate, v_w_up, v_w_down, v_final_norm_g):
    L, D, n_in = w_in.shape
    T = x.shape[1]
    nin = N_CHIPS * n_in
    A = nin // 5
    H = A // HEAD_DIM
    n_f = w_gate.shape[2]
    F = N_CHIPS * n_f
    n_o = w_out.shape[1]
    cb = conv_w.shape[2]
    assert A == H * HEAD_DIM and T % 256 == 0 and N_CHIPS * n_o == D and N_CHIPS * cb == A

    g_in, g_out, g_down = _Geom(D, nin, 1, n_in), _Geom(D, D, 0, n_o), _Geom(F, D, 0, n_f)
    g_gate, g_up, g_gu = _Geom(D, 2 * F, 1, n_f, 0), _Geom(D, 2 * F, 1, n_f, F), _Geom(D, 2 * F, 1, n_f)
    pos = jnp.stack([2 * lax.axis_index("x") + lax.axis_index("y"), lax.axis_index("c")]).astype(jnp.int32)
    row = lambda v: v.reshape(1, -1)

    conv_full = _gather_conv(jnp.pad(conv_w, ((0, 0), (0, 8 - CONV_TAPS), (0, 0))))
    members = [[g_in], [g_out], [g_gate, g_up], [g_down]]
    sources = [[w_in], [w_out], [w_gate, w_up], [w_down]]
    placed, sems_of = {}, {}
    tok = conv_full
    for l in range(L):
        for a in range(4):
            arr = None
            for m, (w, geo) in enumerate(zip(sources[a], members[a])):
                arr = _cast_place(w, l, geo, pos, arr, f"l{l}_place_{a}_{m}", after=(tok,))
            sems_of[l, a], placed[l, a], tok = _gather_start(f"l{l}_gather_start_{a}", arr, members[a], (tok,))

    def arrive(l, a, after):
        return _gather_forward(f"l{l}_gather_fwd_{a}", placed[l, a], members[a], sems_of[l, a], after)

    def landed(l, a, fs, arr, after):
        return _gather_wait(f"l{l}_gather_wait_{a}", arr, members[a], fs, after)

    frame = jnp.arange(HEAD_DIM)
    mask = (frame[None, :] // CHUNK) <= (frame[:, None] // CHUNK)

    xs = x[0]
    acts = []
    for l in range(L):
        wm = jnp.where(mask[None], w_spatial[l], 0.0).astype(BF16)
        wmt = jnp.swapaxes(wm, 1, 2)
        bfull = jnp.broadcast_to(b_spatial[l][:, :, None], (H, HEAD_DIM, HEAD_DIM))
        sm = dict(lng=row(gmlp_ln_g[l]), lnb=row(gmlp_ln_b[l]), wm=wm, wmt=wmt, bfull=bfull,
                  cw=conv_full[l], gn=row(group_norm_g[l]))
        if l == 0:
            h = _rms_fwd(xs, row(norm1_g[l]), f"l{l}_rms1", after=(tok,))
            fs, arr, tok = arrive(l, 0, (h,))
            W_in = landed(l, 0, fs, arr, (tok,))
        else:
            fs, arr, tok = arrive(l, 0, (xs,))
            h = _rms_fwd(xs, row(norm1_g[l]), f"l{l}_rms1", after=(tok,))
            W_in = landed(l, 0, fs, arr, (h,))
        z = _mm_nn(h, W_in, out_dtype=BF16, name=f"l{l}_mm_in")
        fs, arr, tok = arrive(l, 1, (z,))
        y = _mixer_fwd(z, sm["lng"], sm["lnb"], wm, bfull, sm["cw"], sm["gn"], f"l{l}_mixer", after=(tok,))
        W_out = landed(l, 1, fs, arr, (y,))
        x1 = _mm_nn(y, W_out, res=xs, out_dtype=F32, name=f"l{l}_mm_out")
        fs, arr, tok = arrive(l, 2, (x1,))
        h2 = _rms_fwd(x1, row(norm2_g[l]), f"l{l}_rms2", after=(tok,))
        W_gu = landed(l, 2, fs, arr, (h2,))
        gu, act = _mm_swiglu(h2, W_gu, name=f"l{l}_mm_swiglu")
        fs, arr, tok = arrive(l, 3, (act,))
        W_down = landed(l, 3, fs, arr, (tok,))
        x2 = _mm_nn(act, W_down, res=x1, out_dtype=F32, tm=512, name=f"l{l}_mm_down")
        acts.append(dict(x=xs, h=h, z=z, y=y, x1=x1, h2=h2, gu=gu, act=act, sm=sm,
                         W_in=W_in, W_out=W_out, W_gu=W_gu, W_down=W_down))
        xs = x2

    loss_vec, dx, dxb, dgf = _loss_head(xs, row(final_norm_g), loss_target[0], "loss_head")
    loss = lax.psum(loss_vec[0, 0], ("x", "y", "c"))

    big_w = {"in": (w_in, m_w_in, v_w_in), "out": (w_out, m_w_out, v_w_out), "gate": (w_gate, m_w_gate, v_w_gate),
             "up": (w_up, m_w_up, v_w_up), "down": (w_down, m_w_down, v_w_down)}
    big = {nm: None for nm in big_w}

    def adamw(l, names, gs):
        after = ()
        for nm, g in zip(names, gs):
            w, m, v = big_w[nm]
            big[nm] = _adamw_layer(l, w, g, m, v, big[nm], f"l{l}_adamw_{nm}", after=after)
            after = (big[nm][1],)
        return after[0]

    small_grads = [None] * L
    pend_dg = pend_oi = None
    tok = ()
    for l in reversed(range(L)):
        a = acts[l]
        sm = a["sm"]
        dgu = _mm_nt_swiglu_bwd(dxb, a["W_down"], a["gu"], name=f"l{l}_bwd_down", after=tok)
        if pend_dg:
            tok = (pend_dg[0].exchange_to_join((dgu,)),)
        dW_down = _mm_tn(a["act"], dxb, name=f"l{l}_dw_down", after=tok)
        if pend_dg:
            tok = (adamw(pend_dg[1], ["down", "gate", "up"], pend_dg[0].finish((dW_down,))),)
        dh2 = _mm_nt_dgu(dgu, a["W_gu"], name=f"l{l}_bwd_gu", after=tok)
        if pend_oi:
            tok = (pend_oi[0].exchange_to_join((dh2,)),)
        dW_gu = _mm_tn(a["h2"], dgu, tkw=1024, tnw=512, name=f"l{l}_dw_gu", after=tok)
        if pend_oi:
            tok = (adamw(pend_oi[1], ["out", "in"], pend_oi[0].finish((dW_gu,))),)
        dg_batch = _GradBatch(f"l{l}_dg", [dW_down, dW_gu], [g_down, g_gu], [(0, g_down), (1, g_gate), (1, g_up)], pos)
        tok = (dg_batch.start_swap(tok),)
        dx1, dx1b, dg2 = _rms_bwd(dh2, a["x1"], row(norm2_g[l]), dx, f"l{l}_rms2_bwd", after=tok)
        dy = _mm_nt(dx1b, a["W_out"], out_dtype=F32, name=f"l{l}_bwd_out")
        dW_out = _mm_tn(a["y"], dx1b, name=f"l{l}_dw_out")
        tok = (dg_batch.swap_to_exchange((dW_out, dy)),)
        dz, dgn, dlng, dlnb, dws, dbs, dcw = _mixer_bwd(
            a["z"], dy, sm["lng"], sm["lnb"], sm["wm"], sm["wmt"], sm["bfull"], sm["cw"], sm["gn"],
            f"l{l}_mixer_bwd", after=tok)
        dW_in = _mm_tn(a["h"], dz, name=f"l{l}_dw_in")
        oi_batch = _GradBatch(f"l{l}_oi", [dW_out, dW_in], [g_out, g_in], [(0, g_out), (1, g_in)], pos)
        tok = (oi_batch.start_swap(()),)
        dh = _mm_nt(dz, a["W_in"], out_dtype=F32, tm=512, name=f"l{l}_bwd_in", after=tok)
        dx, dxb, dg1 = _rms_bwd(dh, a["x"], row(norm1_g[l]), dx1, f"l{l}_rms1_bwd")
        tok = (oi_batch.swap_to_exchange((dx,)),)
        small_grads[l] = [dg1[0], dlng[0], dlnb[0], dws, dbs[:, 0, :], dcw[:CONV_TAPS], dgn[0], dg2[0]]
        pend_dg, pend_oi = (dg_batch, l), (oi_batch, l)
    grad_x = dx[None]

    tok = (pend_dg[0].exchange_to_join(tok),)
    tok = (adamw(pend_dg[1], ["down", "gate", "up"], pend_dg[0].finish(tok)),)
    tok = (pend_oi[0].exchange_to_join(tok),)
    pieces = [p for l in range(L) for p in small_grads[l]] + [dgf[0]]
    red = _allreduce_small(_pack(pieces), after=tok)
    adamw(pend_oi[1], ["out", "in"], pend_oi[0].finish((red,)))

    red_list = _unpack(red, [p.shape for p in pieces])
    per = len(small_grads[0])
    stack = lambda i: jnp.stack([red_list[l * per + i] for l in range(L)])
    g_small = {"norm1_g": stack(0), "gmlp_ln_g": stack(1), "gmlp_ln_b": stack(2), "w_spatial": stack(3),
               "b_spatial": stack(4), "group_norm_g": stack(6), "norm2_g": stack(7),
               "final_norm_g": red_list[L * per]}
    g_small["conv_w"] = lax.dynamic_slice_in_dim(stack(5), pos[0] * cb, cb, axis=2)
    small_w = {"norm1_g": (norm1_g, m_norm1_g, v_norm1_g), "gmlp_ln_g": (gmlp_ln_g, m_gmlp_ln_g, v_gmlp_ln_g),
               "gmlp_ln_b": (gmlp_ln_b, m_gmlp_ln_b, v_gmlp_ln_b), "w_spatial": (w_spatial, m_w_spatial, v_w_spatial),
               "b_spatial": (b_spatial, m_b_spatial, v_b_spatial), "conv_w": (conv_w, m_conv_w, v_conv_w),
               "group_norm_g": (group_norm_g, m_group_norm_g, v_group_norm_g),
               "norm2_g": (norm2_g, m_norm2_g, v_norm2_g), "final_norm_g": (final_norm_g, m_final_norm_g, v_final_norm_g)}
    snames = list(small_w)
    sd, smn, svn = _adamw_flat(_pack([small_w[n][0] for n in snames]), _pack([g_small[n] for n in snames]),
                               _pack([small_w[n][1] for n in snames]), _pack([small_w[n][2] for n in snames]),
                               "adamw_small")
    sshapes = [small_w[n][0].shape for n in snames]
    sd, smn, svn = _unpack(sd, sshapes), _unpack(smn, sshapes), _unpack(svn, sshapes)
    small_out = {n: (g_small[n], sd[i], smn[i], svn[i]) for i, n in enumerate(snames)}

    order = ["norm1_g", "w_in", "gmlp_ln_g", "gmlp_ln_b", "w_spatial", "b_spatial", "conv_w", "group_norm_g",
             "w_out", "norm2_g", "w_gate", "w_up", "w_down", "final_norm_g"]
    res = {n: (big[n[2:]] if n[2:] in big else small_out[n]) for n in order}
    return (loss, grad_x, *[res[n][0] for n in order], *[res[n][1] for n in order],
            *[res[n][2] for n in order], *[res[n][3] for n in order])
```

```python
import math

import jax
import jax.numpy as jnp
from jax import lax
from jax.experimental import pallas as pl
from jax.experimental.pallas import tpu as pltpu

RMS_EPS = 1e-6
LN_EPS = 1e-5
HEAD_DIM = 128
CHUNK = 64
CONV_TAPS = 3
HALO = 16
ADAM_LR = 0.001
ADAM_B1 = 0.9
ADAM_B2 = 0.999
ADAM_EPS = 1e-08
ADAM_WD = 0.01
ADAM_STEP = 10
V7X_VMEM_LIMIT = 56 * 1024 * 1024
N_CHIPS = 4
DMA_CHUNK_BYTES = 2 * 1024 * 1024
EPILOGUE_ROWS = 256
MESH = pl.DeviceIdType.MESH
F32 = jnp.float32
BF16 = jnp.bfloat16
ANY = pl.BlockSpec(memory_space=pl.ANY)
HBM = pl.BlockSpec(memory_space=pltpu.HBM)
SEM = pl.BlockSpec(memory_space=pltpu.SEMAPHORE)
EFFECT = pltpu.SideEffectType.DATAFLOW_SIDE_EFFECTING


def _tile(n, pref):
    if n <= pref:
        return n
    best = None
    for t in range(128, pref + 1, 128):
        if n % t == 0:
            best = t
    assert best is not None, (n, pref)
    return best


def _rows_tile(rows, cols, budget_elems):
    tr = rows
    while tr * cols > budget_elems and tr % 2 == 0 and (tr // 2) % 16 == 0:
        tr //= 2
    return tr


def _params(*sem):
    return pltpu.CompilerParams(dimension_semantics=sem if sem else None,
                                vmem_limit_bytes=V7X_VMEM_LIMIT)


def _call(body, args, after, **kw):
    n, na = len(args), len(after)
    if na:
        inner = body

        def body(*refs):
            inner(*refs[:n], *refs[n + na:])

        kw["in_specs"] = list(kw["in_specs"]) + [ANY] * na
    return pl.pallas_call(body, **kw)(*args, *after)


def _prefetch_call(body, pos, args, after=(), *, out_shape, grid, in_specs, out_specs, aliases=None, name):
    n, na = 1 + len(args), len(after)
    if na:
        inner = body

        def body(*refs):
            inner(*refs[:n], *refs[n + na:])

    return pl.pallas_call(
        body, out_shape=out_shape,
        grid_spec=pltpu.PrefetchScalarGridSpec(num_scalar_prefetch=1, grid=grid,
                                               in_specs=list(in_specs) + [ANY] * na, out_specs=out_specs),
        input_output_aliases=aliases or {}, name=name,
        compiler_params=_params(*(["parallel"] * len(grid))))(pos, *args, *after)


def _gelu(x):
    c = math.sqrt(2.0 / math.pi)
    return 0.5 * x * (1.0 + jnp.tanh(c * (x + 0.044715 * x * x * x)))


def _gelu_and_grad(x):
    c = math.sqrt(2.0 / math.pi)
    x2 = x * x
    th = jnp.tanh(c * (x + 0.044715 * x * x2))
    val = 0.5 * x * (1.0 + th)
    grad = 0.5 * (1.0 + th) + 0.5 * x * (1.0 - th * th) * (c * (1.0 + 3.0 * 0.044715 * x2))
    return val, grad


def _sigmoid(x):
    return 1.0 / (1.0 + jnp.exp(-x))


def _nt(a, b):
    return lax.dot_general(a, b, (((1,), (1,)), ((), ())), preferred_element_type=F32)


def _tn(a, b):
    return lax.dot_general(a, b, (((0,), (0,)), ((), ())), preferred_element_type=F32)


def _rms_fwd(x, g, name, after=()):
    T, D = x.shape
    tr = _tile(T, 512)

    def body(x_ref, g_ref, h_ref):
        xv = x_ref[...]
        r = lax.rsqrt(jnp.mean(xv * xv, axis=-1, keepdims=True) + RMS_EPS)
        h_ref[...] = ((xv * r) * g_ref[...]).astype(h_ref.dtype)

    return _call(
        body, [x, g], after, out_shape=jax.ShapeDtypeStruct((T, D), BF16), grid=(T // tr,),
        in_specs=[pl.BlockSpec((tr, D), lambda i: (i, 0)), pl.BlockSpec((1, D), lambda i: (0, 0))],
        out_specs=pl.BlockSpec((tr, D), lambda i: (i, 0)),
        name=name, compiler_params=_params("parallel"))


def _rms_bwd(dh, x, g, dres, name, after=()):
    T, D = x.shape
    tr = _tile(T, 256)

    def body(dh_ref, x_ref, g_ref, dres_ref, dx_ref, dxb_ref, dg_ref):
        i = pl.program_id(0)
        xv = x_ref[...]
        dhv = dh_ref[...]
        r = lax.rsqrt(jnp.mean(xv * xv, axis=-1, keepdims=True) + RMS_EPS)
        xh = xv * r
        q = dhv * g_ref[...]
        dx = dres_ref[...] + r * (q - xh * jnp.mean(q * xh, axis=-1, keepdims=True))
        dx_ref[...] = dx
        dxb_ref[...] = dx.astype(BF16)
        part = jnp.sum(dhv * xh, axis=0, keepdims=True)

        @pl.when(i == 0)
        def _():
            dg_ref[...] = part

        @pl.when(i > 0)
        def _():
            dg_ref[...] += part

    row = pl.BlockSpec((tr, D), lambda i: (i, 0))
    vec = pl.BlockSpec((1, D), lambda i: (0, 0))
    return _call(
        body, [dh, x, g, dres], after,
        out_shape=(jax.ShapeDtypeStruct((T, D), F32), jax.ShapeDtypeStruct((T, D), BF16),
                   jax.ShapeDtypeStruct((1, D), F32)),
        grid=(T // tr,), in_specs=[row, row, vec, row], out_specs=(row, row, vec),
        name=name, compiler_params=_params("arbitrary"))


def _loss_head(x, g, tgt, name):
    T, D = x.shape
    tr = _tile(T, 256)

    def body(x_ref, g_ref, t_ref, loss_ref, dx_ref, dxb_ref, dg_ref):
        i = pl.program_id(0)
        xv = x_ref[...]
        gv = g_ref[...]
        r = lax.rsqrt(jnp.mean(xv * xv, axis=-1, keepdims=True) + RMS_EPS)
        xh = xv * r
        err = xh * gv - t_ref[...]
        lpart = jnp.full((1, 128), 0.5 * jnp.sum(jnp.mean(err * err, axis=-1, keepdims=True)), F32)
        dy = err * (1.0 / D)
        q = dy * gv
        dx = r * (q - xh * jnp.mean(q * xh, axis=-1, keepdims=True))
        dx_ref[...] = dx
        dxb_ref[...] = dx.astype(BF16)
        gpart = jnp.sum(dy * xh, axis=0, keepdims=True)

        @pl.when(i == 0)
        def _():
            loss_ref[...] = lpart
            dg_ref[...] = gpart

        @pl.when(i > 0)
        def _():
            loss_ref[...] += lpart
            dg_ref[...] += gpart

    row = pl.BlockSpec((tr, D), lambda i: (i, 0))
    vec = pl.BlockSpec((1, D), lambda i: (0, 0))
    return pl.pallas_call(
        body,
        out_shape=(jax.ShapeDtypeStruct((1, 128), F32), jax.ShapeDtypeStruct((T, D), F32),
                   jax.ShapeDtypeStruct((T, D), BF16), jax.ShapeDtypeStruct((1, D), F32)),
        grid=(T // tr,), in_specs=[row, vec, row],
        out_specs=(pl.BlockSpec((1, 128), lambda i: (0, 0)), row, row, vec),
        name=name, compiler_params=_params("arbitrary"))(x, g, tgt)


def _mm_nn(a, w, *, res=None, out_dtype, tm=1024, tn=1024, tk=None, name, after=()):
    M, K = a.shape
    N = w.shape[1]
    tm, tn = _tile(M, tm), _tile(N, tn)
    tk = K if tk is None else _tile(K, tk)
    nk = K // tk
    has_res = res is not None

    def body(*refs):
        a_ref, w_ref = refs[0], refs[1]
        r_ref = refs[2] if has_res else None
        o_ref = refs[2 + has_res]
        part = jnp.dot(a_ref[...], w_ref[...], preferred_element_type=F32)

        def finish(acc):
            if has_res:
                acc = r_ref[...] + acc
            o_ref[...] = acc.astype(o_ref.dtype)

        if nk == 1:
            finish(part)
        else:
            acc_ref = refs[3 + has_res]
            k = pl.program_id(2)

            @pl.when(k == 0)
            def _():
                acc_ref[...] = part

            @pl.when(jnp.logical_and(k > 0, k < nk - 1))
            def _():
                acc_ref[...] += part

            @pl.when(k == nk - 1)
            def _():
                finish(acc_ref[...] + part)

    in_specs = [pl.BlockSpec((tm, tk), lambda i, j, k: (i, k)),
                pl.BlockSpec((tk, tn), lambda i, j, k: (k, j))]
    args = [a, w]
    if has_res:
        in_specs.append(pl.BlockSpec((tm, tn), lambda i, j, k: (i, j)))
        args.append(res)
    return _call(
        body, args, after, out_shape=jax.ShapeDtypeStruct((M, N), out_dtype), grid=(M // tm, N // tn, nk),
        in_specs=in_specs, out_specs=pl.BlockSpec((tm, tn), lambda i, j, k: (i, j)),
        scratch_shapes=[pltpu.VMEM((tm, tn), F32)] if nk > 1 else [],
        name=name, compiler_params=_params("parallel", "parallel", "arbitrary"))


def _mm_res_rms(a, w, res, g, *, tm=512, name, after=()):
    M, K = a.shape
    N = w.shape[1]
    tm = _tile(M, tm)

    def body(a_ref, w_ref, r_ref, g_ref, x_ref, h_ref):
        xv = r_ref[...] + jnp.dot(a_ref[...], w_ref[...], preferred_element_type=F32)
        x_ref[...] = xv
        r = lax.rsqrt(jnp.mean(xv * xv, axis=-1, keepdims=True) + RMS_EPS)
        h_ref[...] = ((xv * r) * g_ref[...]).astype(BF16)

    row = lambda n: pl.BlockSpec((tm, n), lambda i: (i, 0))
    return _call(
        body, [a, w, res, g], after,
        out_shape=(jax.ShapeDtypeStruct((M, N), F32), jax.ShapeDtypeStruct((M, N), BF16)), grid=(M // tm,),
        in_specs=[row(K), pl.BlockSpec((K, N), lambda i: (0, 0)), row(N), pl.BlockSpec((1, N), lambda i: (0, 0))],
        out_specs=(row(N), row(N)), name=name, compiler_params=_params("parallel"))


def _mm_swiglu(h, wgu, *, tm=1024, tn=512, name):
    T, D = h.shape
    F = wgu.shape[1] // 2
    tm, tn = _tile(T, tm), _tile(F, tn)
    nf = F // tn

    rc = _tile(tm, EPILOGUE_ROWS)

    def body(h_ref, wg_ref, wu_ref, fac_ref, act_ref):
        for r in range(tm // rc):
            rows = slice(r * rc, (r + 1) * rc)
            hv = h_ref[rows, :]
            g = jnp.dot(hv, wg_ref[...], preferred_element_type=F32)
            u = jnp.dot(hv, wu_ref[...], preferred_element_type=F32)
            s = _sigmoid(g)
            t = g * s
            act_ref[rows, :] = (t * u).astype(BF16)
            fac_ref[0, rows, :] = (u * (s + t * (1.0 - s))).astype(BF16)
            fac_ref[1, rows, :] = t.astype(BF16)

    return pl.pallas_call(
        body,
        out_shape=(jax.ShapeDtypeStruct((2, T, F), BF16), jax.ShapeDtypeStruct((T, F), BF16)),
        grid=(T // tm, nf),
        in_specs=[pl.BlockSpec((tm, D), lambda i, j: (i, 0)),
                  pl.BlockSpec((D, tn), lambda i, j: (0, j)),
                  pl.BlockSpec((D, tn), lambda i, j: (0, j + nf))],
        out_specs=(pl.BlockSpec((2, tm, tn), lambda i, j: (0, i, j)),
                   pl.BlockSpec((tm, tn), lambda i, j: (i, j))),
        name=name, compiler_params=_params("parallel", "parallel"))(h, wgu, wgu)


def _mm_nt(a, w, *, out_dtype, tm=1024, tn=1024, tk=None, name, after=()):
    M, K = a.shape
    N = w.shape[0]
    tm, tn = _tile(M, tm), _tile(N, tn)
    tk = K if tk is None else _tile(K, tk)
    nk = K // tk

    def body(*refs):
        a_ref, w_ref, o_ref = refs[0], refs[1], refs[2]
        part = _nt(a_ref[...], w_ref[...])
        if nk == 1:
            o_ref[...] = part.astype(o_ref.dtype)
        else:
            acc_ref = refs[3]
            k = pl.program_id(2)

            @pl.when(k == 0)
            def _():
                acc_ref[...] = part

            @pl.when(jnp.logical_and(k > 0, k < nk - 1))
            def _():
                acc_ref[...] += part

            @pl.when(k == nk - 1)
            def _():
                o_ref[...] = (acc_ref[...] + part).astype(o_ref.dtype)

    return _call(
        body, [a, w], after, out_shape=jax.ShapeDtypeStruct((M, N), out_dtype), grid=(M // tm, N // tn, nk),
        in_specs=[pl.BlockSpec((tm, tk), lambda i, j, k: (i, k)),
                  pl.BlockSpec((tn, tk), lambda i, j, k: (j, k))],
        out_specs=pl.BlockSpec((tm, tn), lambda i, j, k: (i, j)),
        scratch_shapes=[pltpu.VMEM((tm, tn), F32)] if nk > 1 else [],
        name=name, compiler_params=_params("parallel", "parallel", "arbitrary"))


def _mm_nt_swiglu_bwd(dxb, wdown, gu, *, tm=1024, tn=512, name, after=()):
    T, D = dxb.shape
    F = wdown.shape[0]
    tm, tn = _tile(T, tm), _tile(F, tn)

    def body(dx_ref, w_ref, fac_ref, dgu_ref):
        da = _nt(dx_ref[...], w_ref[...])
        dgu_ref[0] = (da * fac_ref[0].astype(F32)).astype(BF16)
        dgu_ref[1] = (da * fac_ref[1].astype(F32)).astype(BF16)

    blk3 = pl.BlockSpec((2, tm, tn), lambda i, j: (0, i, j))
    return _call(
        body, [dxb, wdown, gu], after, out_shape=jax.ShapeDtypeStruct((2, T, F), BF16), grid=(T // tm, F // tn),
        in_specs=[pl.BlockSpec((tm, D), lambda i, j: (i, 0)),
                  pl.BlockSpec((tn, D), lambda i, j: (j, 0)), blk3],
        out_specs=blk3, name=name, compiler_params=_params("parallel", "parallel"))


def _mm_nt_dgu(dgu, wgu, *, tm=1024, tn=512, tk=5632, name, after=()):
    _, T, F = dgu.shape
    D = wgu.shape[0]
    tm, tn, tk = _tile(T, tm), _tile(D, tn), _tile(F, tk)
    nkf = F // tk
    nk = 2 * nkf

    def body(a_ref, w_ref, o_ref, acc_ref):
        k = pl.program_id(2)
        part = _nt(a_ref[...], w_ref[...])

        @pl.when(k == 0)
        def _():
            acc_ref[...] = part

        @pl.when(jnp.logical_and(k > 0, k < nk - 1))
        def _():
            acc_ref[...] += part

        @pl.when(k == nk - 1)
        def _():
            o_ref[...] = acc_ref[...] + part

    return _call(
        body, [dgu, wgu], after, out_shape=jax.ShapeDtypeStruct((T, D), F32), grid=(T // tm, D // tn, nk),
        in_specs=[pl.BlockSpec((None, tm, tk), lambda i, j, k: (k // nkf, i, k % nkf)),
                  pl.BlockSpec((tn, tk), lambda i, j, k: (j, k))],
        out_specs=pl.BlockSpec((tm, tn), lambda i, j, k: (i, j)),
        scratch_shapes=[pltpu.VMEM((tm, tn), F32)],
        name=name, compiler_params=_params("parallel", "parallel", "arbitrary"))


def _mm_tn(a, g, *, tkw=512, tnw=1024, name, after=()):
    T, Kw = a.shape
    pair = g.ndim == 3
    Nw = 2 * g.shape[2] if pair else g.shape[1]
    tkw = _tile(Kw, tkw)
    tnw = _tile(g.shape[2] if pair else Nw, tnw)
    nf = (Nw // 2) // tnw if pair else 0

    def body(a_ref, g_ref, o_ref):
        o_ref[...] = _tn(a_ref[...], g_ref[...]).astype(o_ref.dtype)

    if pair:
        g_spec = pl.BlockSpec((None, T, tnw), lambda i, j: (j // nf, 0, j % nf))
    else:
        g_spec = pl.BlockSpec((T, tnw), lambda i, j: (0, j))
    return _call(
        body, [a, g], after, out_shape=jax.ShapeDtypeStruct((Kw, Nw), BF16), grid=(Kw // tkw, Nw // tnw),
        in_specs=[pl.BlockSpec((T, tkw), lambda i, j: (0, i)), g_spec],
        out_specs=pl.BlockSpec((tkw, tnw), lambda i, j: (i, j)),
        name=name, compiler_params=_params("parallel", "parallel"))


def _mixer_specs(T, A, tr):
    nin = 5 * A
    nb = tr // HALO
    last = T // HALO - 1
    prev = pl.BlockSpec((HALO, nin), lambda i: (jnp.maximum(i * nb - 1, 0), 0))
    cur = pl.BlockSpec((tr, nin), lambda i: (i, 0))
    nxt = pl.BlockSpec((HALO, nin), lambda i: (jnp.minimum((i + 1) * nb, last), 0))
    return prev, cur, nxt


def _conv(p_ext, cw_ref):
    return (cw_ref[2:3, :] * p_ext + cw_ref[1:2, :] * pltpu.roll(p_ext, 1, 0)
            + cw_ref[0:1, :] * pltpu.roll(p_ext, 2, 0))


def _mixer_fwd(z, lng, lnb, wm, bfull, cw, gn, name, after=()):
    T, nin = z.shape
    A = nin // 5
    H = A // HEAD_DIM
    tr = _tile(T, 256)
    nblk = tr // HEAD_DIM

    def body(zp_ref, z_ref, lng_ref, lnb_ref, wm_ref, bf_ref, cw_ref, gn_ref, y_ref, vn_ref, mix_ref):
        i = pl.program_id(0)
        u = _gelu(z_ref[:, 0:A].astype(F32))
        vg = _gelu(z_ref[:, A:2 * A].astype(F32))
        xc = vg - jnp.mean(vg, axis=-1, keepdims=True)
        rstd = lax.rsqrt(jnp.mean(xc * xc, axis=-1, keepdims=True) + LN_EPS)
        vn_ref[...] = ((xc * rstd) * lng_ref[...] + lnb_ref[...]).astype(BF16)
        for cb in range(nblk):
            rows = slice(cb * HEAD_DIM, (cb + 1) * HEAD_DIM)
            for h in range(H):
                cols = slice(h * HEAD_DIM, (h + 1) * HEAD_DIM)
                mix_ref[rows, cols] = jnp.dot(wm_ref[h], vn_ref[rows, cols],
                                              preferred_element_type=F32) + bf_ref[h]
        ya = u * mix_ref[...]
        ra = lax.rsqrt(jnp.mean(ya * ya, axis=-1, keepdims=True) + RMS_EPS)
        y_ref[:, 0:A] = ((ya * ra) * gn_ref[:, 0:A]).astype(BF16)

        p_prev = zp_ref[:, 3 * A:4 * A].astype(F32) * zp_ref[:, 4 * A:5 * A].astype(F32)
        p_prev = jnp.where(i > 0, p_prev, 0.0)
        p_cur = z_ref[:, 3 * A:4 * A].astype(F32) * z_ref[:, 4 * A:5 * A].astype(F32)
        cv = _conv(jnp.concatenate([p_prev, p_cur], axis=0), cw_ref)[HALO:]
        yb = z_ref[:, 2 * A:3 * A].astype(F32) * cv
        rb = lax.rsqrt(jnp.mean(yb * yb, axis=-1, keepdims=True) + RMS_EPS)
        y_ref[:, A:2 * A] = ((yb * rb) * gn_ref[:, A:2 * A]).astype(BF16)

    prev, cur, _ = _mixer_specs(T, A, tr)
    full = lambda shape: pl.BlockSpec(shape, lambda i: (0,) * len(shape))
    return _call(
        body, [z, z, lng, lnb, wm, bfull, cw, gn], after,
        out_shape=jax.ShapeDtypeStruct((T, 2 * A), BF16), grid=(T // tr,),
        in_specs=[prev, cur, full((1, A)), full((1, A)), full((H, HEAD_DIM, HEAD_DIM)),
                  full((H, HEAD_DIM, HEAD_DIM)), full((8, A)), full((1, 2 * A))],
        out_specs=pl.BlockSpec((tr, 2 * A), lambda i: (i, 0)),
        scratch_shapes=[pltpu.VMEM((tr, A), BF16), pltpu.VMEM((tr, A), F32)],
        name=name, compiler_params=_params("parallel"))


def _mixer_bwd(z, dy, lng, lnb, wm, wmt, bfull, cw, gn, name, after=()):
    T, nin = z.shape
    A = nin // 5
    H = A // HEAD_DIM
    tr = _tile(T, 256)
    nblk = tr // HEAD_DIM
    ngrid = T // tr
    next_ = tr + 2 * HALO

    def body(zp_ref, z_ref, zn_ref, dy_ref, dyn_ref, lng_ref, lnb_ref, wm_ref, wmt_ref, bf_ref, cw_ref, gn_ref,
             dz_ref, dgn_ref, dlng_ref, dlnb_ref, dws_ref, dbs_ref, dcw_ref,
             vn_ref, mix_ref, dmix_ref, dvn_ref):
        i = pl.program_id(0)

        @pl.when(i == 0)
        def _():
            dgn_ref[...] = jnp.zeros_like(dgn_ref)
            dlng_ref[...] = jnp.zeros_like(dlng_ref)
            dlnb_ref[...] = jnp.zeros_like(dlnb_ref)
            dws_ref[...] = jnp.zeros_like(dws_ref)
            dbs_ref[...] = jnp.zeros_like(dbs_ref)
            dcw_ref[...] = jnp.zeros_like(dcw_ref)

        u, du_dz = _gelu_and_grad(z_ref[:, 0:A].astype(F32))
        vg, dv_dz = _gelu_and_grad(z_ref[:, A:2 * A].astype(F32))
        xc = vg - jnp.mean(vg, axis=-1, keepdims=True)
        rstd = lax.rsqrt(jnp.mean(xc * xc, axis=-1, keepdims=True) + LN_EPS)
        vhat = xc * rstd
        vn_ref[...] = (vhat * lng_ref[...] + lnb_ref[...]).astype(BF16)
        for cb in range(nblk):
            rows = slice(cb * HEAD_DIM, (cb + 1) * HEAD_DIM)
            for h in range(H):
                cols = slice(h * HEAD_DIM, (h + 1) * HEAD_DIM)
                mix_ref[rows, cols] = jnp.dot(wm_ref[h], vn_ref[rows, cols],
                                              preferred_element_type=F32) + bf_ref[h]
        mixed = mix_ref[...]
        ya = u * mixed
        ra = lax.rsqrt(jnp.mean(ya * ya, axis=-1, keepdims=True) + RMS_EPS)
        yha = ya * ra
        dyan = dy_ref[:, 0:A]
        dgn_ref[:, 0:A] += jnp.sum(dyan * yha, axis=0, keepdims=True)
        qa = dyan * gn_ref[:, 0:A]
        dya = ra * (qa - yha * jnp.mean(qa * yha, axis=-1, keepdims=True))
        dz_ref[:, 0:A] = ((dya * mixed) * du_dz).astype(BF16)
        dmix_ref[...] = (dya * u).astype(BF16)

        ii = lax.broadcasted_iota(jnp.int32, (HEAD_DIM, HEAD_DIM), 0)
        jj = lax.broadcasted_iota(jnp.int32, (HEAD_DIM, HEAD_DIM), 1)
        mask = (jj // CHUNK <= ii // CHUNK).astype(F32)
        ones = jnp.ones((8, HEAD_DIM), BF16)
        for h in range(H):
            cols = slice(h * HEAD_DIM, (h + 1) * HEAD_DIM)
            dws = jnp.zeros((HEAD_DIM, HEAD_DIM), F32)
            dbs = jnp.zeros((8, HEAD_DIM), F32)
            for cb in range(nblk):
                rows = slice(cb * HEAD_DIM, (cb + 1) * HEAD_DIM)
                dm = dmix_ref[rows, cols]
                dws = dws + _nt(dm, vn_ref[rows, cols])
                dbs = dbs + _nt(ones, dm)
                dvn_ref[rows, cols] = jnp.dot(wmt_ref[h], dm, preferred_element_type=F32)
            dws_ref[h] += dws * mask
            dbs_ref[h] += dbs
        dvn = dvn_ref[...]
        dlnb_ref[...] += jnp.sum(dvn, axis=0, keepdims=True)
        dlng_ref[...] += jnp.sum(dvn * vhat, axis=0, keepdims=True)
        dvh = dvn * lng_ref[...]
        dvg = rstd * (dvh - jnp.mean(dvh, axis=-1, keepdims=True)
                      - vhat * jnp.mean(dvh * vhat, axis=-1, keepdims=True))
        dz_ref[:, A:2 * A] = (dvg * dv_dz).astype(BF16)

        def ext(lo):
            mid = z_ref[:, lo:lo + A].astype(F32)
            return jnp.concatenate([zp_ref[:, lo:lo + A].astype(F32), mid, zn_ref[:, lo:lo + A].astype(F32)], axis=0)

        zb, zc, zh = ext(2 * A), ext(3 * A), ext(4 * A)
        row = lax.broadcasted_iota(jnp.int32, (next_, 1), 0)
        p = zc * zh
        p = jnp.where(jnp.logical_and(row < HALO, i == 0), 0.0, p)
        cv = _conv(p, cw_ref)
        yb = zb * cv
        rb = lax.rsqrt(jnp.mean(yb * yb, axis=-1, keepdims=True) + RMS_EPS)
        yhb = yb * rb
        dyn_rows = jnp.where(i < ngrid - 1, dyn_ref[:, A:2 * A], 0.0)
        dybn = jnp.concatenate([jnp.zeros((HALO, A), F32), dy_ref[:, A:2 * A], dyn_rows], axis=0)
        ctr = slice(HALO, HALO + tr)
        dgn_ref[:, A:2 * A] += jnp.sum((dybn * yhb)[ctr], axis=0, keepdims=True)
        qb = dybn * gn_ref[:, A:2 * A]
        dyb = rb * (qb - yhb * jnp.mean(qb * yhb, axis=-1, keepdims=True))
        dcv = dyb * zb
        dp = (cw_ref[2:3, :] * dcv + cw_ref[1:2, :] * pltpu.roll(dcv, next_ - 1, 0)
              + cw_ref[0:1, :] * pltpu.roll(dcv, next_ - 2, 0))
        dz_ref[:, 2 * A:3 * A] = (dyb * cv)[ctr].astype(BF16)
        dz_ref[:, 3 * A:4 * A] = (dp * zh)[ctr].astype(BF16)
        dz_ref[:, 4 * A:5 * A] = (dp * zc)[ctr].astype(BF16)
        dcw_ref[2:3, :] += jnp.sum((dcv * p)[ctr], axis=0, keepdims=True)
        dcw_ref[1:2, :] += jnp.sum((dcv * pltpu.roll(p, 1, 0))[ctr], axis=0, keepdims=True)
        dcw_ref[0:1, :] += jnp.sum((dcv * pltpu.roll(p, 2, 0))[ctr], axis=0, keepdims=True)

    prev, cur, nxt = _mixer_specs(T, A, tr)
    nb = tr // HALO
    dy_cur = pl.BlockSpec((tr, 2 * A), lambda i: (i, 0))
    dy_nxt = pl.BlockSpec((HALO, 2 * A), lambda i: (jnp.minimum((i + 1) * nb, T // HALO - 1), 0))
    full = lambda shape: pl.BlockSpec(shape, lambda i: (0,) * len(shape))
    hh = (H, HEAD_DIM, HEAD_DIM)
    return _call(
        body, [z, z, z, dy, dy, lng, lnb, wm, wmt, bfull, cw, gn], after,
        out_shape=(jax.ShapeDtypeStruct((T, nin), BF16), jax.ShapeDtypeStruct((1, 2 * A), F32),
                   jax.ShapeDtypeStruct((1, A), F32), jax.ShapeDtypeStruct((1, A), F32),
                   jax.ShapeDtypeStruct(hh, F32), jax.ShapeDtypeStruct((H, 8, HEAD_DIM), F32),
                   jax.ShapeDtypeStruct((8, A), F32)),
        grid=(ngrid,),
        in_specs=[prev, cur, nxt, dy_cur, dy_nxt, full((1, A)), full((1, A)), full(hh), full(hh), full(hh),
                  full((8, A)), full((1, 2 * A))],
        out_specs=(pl.BlockSpec((tr, nin), lambda i: (i, 0)), full((1, 2 * A)), full((1, A)), full((1, A)),
                   full(hh), full((H, 8, HEAD_DIM)), full((8, A))),
        scratch_shapes=[pltpu.VMEM((tr, A), BF16), pltpu.VMEM((tr, A), F32), pltpu.VMEM((tr, A), BF16),
                        pltpu.VMEM((tr, A), F32)],
        name=name, compiler_params=_params("arbitrary"))


class _Geom:
    def __init__(self, rows, cols, axis, size, base=0):
        self.rows, self.cols, self.axis, self.size, self.base = rows, cols, axis, size, base

    def in_full(self, j, h):
        if self.axis == 1:
            return (h * (self.rows // 2), self.rows // 2), (self.base + j * self.size, self.size)
        return (self.base + j * self.size, self.size), (h * (self.cols // 2), self.cols // 2)

    def in_shard(self, h):
        if self.axis == 1:
            return (h * (self.rows // 2), self.rows // 2), (0, self.size)
        return (0, self.size), (h * (self.cols // 2), self.cols // 2)

    def half_of_full(self, h):
        if self.axis == 1:
            return (h * (self.rows // 2), self.rows // 2), (0, self.cols)
        return (0, self.rows), (h * (self.cols // 2), self.cols // 2)

    def in_half(self, j):
        if self.axis == 1:
            return (0, self.rows // 2), (self.base + j * self.size, self.size)
        return (self.base + j * self.size, self.size), (0, self.cols // 2)

    @property
    def half_shape(self):
        return (self.rows // 2, self.cols) if self.axis == 1 else (self.rows, self.cols // 2)

    @property
    def shard_half_shape(self):
        return (self.rows // 2, self.size) if self.axis == 1 else (self.size, self.cols // 2)

    @property
    def shard_shape(self):
        return (self.rows, self.size) if self.axis == 1 else (self.size, self.cols)


def _at(ref, region):
    (r0, rn), (c0, cn) = region
    if not isinstance(r0, int):
        r0 = pl.multiple_of(r0, 16)
    if not isinstance(c0, int):
        c0 = pl.multiple_of(c0, 128)
    return ref.at[pl.ds(r0, rn), pl.ds(c0, cn)]


def _whole(shape):
    return (0, shape[-2]), (0, shape[-1])


def _split_rows(region, itemsize):
    (r0, rn), cols = region
    want = max(1, (rn * cols[1] * itemsize) // DMA_CHUNK_BYTES)
    n = 1
    for cand in range(1, want + 1):
        if rn % cand == 0 and (rn // cand) % 16 == 0:
            n = cand
    step = rn // n
    return [((r0 + i * step, step), cols) for i in range(n)]


class _Chunked:
    def __init__(self, make, src, src_reg, dst, dst_reg):
        self.make, self.src, self.src_reg, self.dst, self.dst_reg = make, src, src_reg, dst, dst_reg

    @property
    def whole(self):
        return self.make(_at(self.src, self.src_reg), _at(self.dst, self.dst_reg))

    def start(self):
        itemsize = jnp.dtype(self.src.dtype).itemsize
        for a, b in zip(_split_rows(self.src_reg, itemsize), _split_rows(self.dst_reg, itemsize)):
            self.make(_at(self.src, a), _at(self.dst, b)).start()


def _mesh_place():
    x, y, c = lax.axis_index("x"), lax.axis_index("y"), lax.axis_index("c")
    chips = [(1 - x, y), (x, 1 - y), (1 - x, 1 - y)]
    return x, y, c, 2 * x + y, chips


def _remote(ssem, rsem, dev):
    return lambda src, dst: pltpu.make_async_remote_copy(
        src_ref=src, dst_ref=dst, send_sem=ssem, recv_sem=rsem, device_id=dev, device_id_type=MESH)


def _comm_call(name, body, arrays, sems_in=(), after=(), sems_out=(), new=()):
    na, ns, nf, no, nn = len(arrays), len(sems_in), len(after), len(sems_out), len(new)

    def kern(*refs):
        sin = refs[na:na + ns]
        outs = refs[na + ns + nf:]
        body(outs[no:no + na], outs[no + na:no + na + nn], sin, outs[:no])
        outs[-1][...] = jnp.zeros_like(outs[-1])

    out_shape = (tuple(pltpu.SemaphoreType.DMA((n,)) for n in sems_out)
                 + tuple(pltpu.HBM(a.shape, a.dtype) for a in arrays)
                 + tuple(pltpu.HBM(s, d) for s, d in new)
                 + (jax.ShapeDtypeStruct((8, 128), F32),))
    res = pl.pallas_call(
        kern, out_shape=out_shape, in_specs=[HBM] * na + [SEM] * ns + [ANY] * nf,
        out_specs=(SEM,) * no + (HBM,) * (na + nn) + (pl.BlockSpec(memory_space=pltpu.VMEM),),
        input_output_aliases={i: no + i for i in range(na)}, name=name,
        compiler_params=pltpu.CompilerParams(has_side_effects=EFFECT),
    )(*[pltpu.with_memory_space_constraint(a, pltpu.HBM) for a in arrays], *sems_in, *after)
    return list(res[:no]), list(res[no:no + na]), list(res[no + na:no + na + nn]), res[-1]


def _gather_start(name, arr, members, after):
    def body(arrs, news, sin, sout):
        x, y, c, j, chips = _mesh_place()
        for m, geo in enumerate(members):
            reg = geo.in_full(j, c)
            for k, chip in enumerate(chips):
                _Chunked(_remote(sout[2 * m].at[k], sout[2 * m + 1].at[k], (*chip, c)),
                         arrs[0], reg, arrs[0], reg).start()

    sems, arrs, _, tok = _comm_call(name, body, [arr], after=after, sems_out=[N_CHIPS - 1] * (2 * len(members)))
    return [sems[2 * m:2 * m + 2] for m in range(len(members))], arrs[0], tok


def _gather_forward(name, arr, members, sems, after):
    def body(arrs, news, sin, sout):
        x, y, c, j, chips = _mesh_place()
        sibling = (x, y, 1 - c)
        full = arrs[0]
        for m, geo in enumerate(members):
            mine = geo.in_full(j, c)
            for k, chip in enumerate(chips):
                got = geo.in_full(2 * chip[0] + chip[1], c)
                sent = _Chunked(_remote(sin[2 * m].at[k], sin[2 * m + 1].at[k], sibling), full, mine, full, got)
                sent.whole.wait_send()
                sent.whole.wait_recv()
                _Chunked(_remote(sout[2 * m].at[k], sout[2 * m + 1].at[k], sibling), full, got, full, got).start()

    flat = [s for pair in sems for s in pair]
    fs, arrs, _, tok = _comm_call(name, body, [arr], sems_in=flat, after=after,
                                  sems_out=[N_CHIPS - 1] * (2 * len(members)))
    return fs, arrs[0], tok


def _gather_wait(name, arr, members, fsems, after):
    def body(arrs, news, sin, sout):
        x, y, c, j, chips = _mesh_place()
        sibling = (x, y, 1 - c)
        full = arrs[0]
        for m, geo in enumerate(members):
            for k, chip in enumerate(chips):
                jk = 2 * chip[0] + chip[1]
                cp = _Chunked(_remote(sin[2 * m].at[k], sin[2 * m + 1].at[k], sibling),
                              full, geo.in_full(jk, c), full, geo.in_full(jk, 1 - c))
                cp.whole.wait_send()
                cp.whole.wait_recv()

    _, arrs, _, _ = _comm_call(name, body, [arr], sems_in=fsems, after=after)
    return arrs[0]


def _gather_conv(conv):
    L, _, cb = conv.shape
    nk = N_CHIPS - 1

    def body(s_ref, f_ref, send_sems, recv_sems, local_sem):
        x, y, c, j, chips = _mesh_place()
        at = lambda jj: f_ref.at[:, :, pl.ds(pl.multiple_of(jj * cb, 128), cb)]
        lc = pltpu.make_async_copy(s_ref, at(j), local_sem.at[0])
        lc.start()
        cps = []
        for k, chip in enumerate(chips):
            cp = _remote(send_sems.at[k], recv_sems.at[k], (*chip, c))(s_ref, at(j))
            cp.start()
            cps.append(cp)
        for k, chip in enumerate(chips):
            jk = 2 * chip[0] + chip[1]
            cps[k].wait_send()
            _remote(send_sems.at[k], recv_sems.at[k], (*chip, c))(at(jk), at(jk)).wait_recv()
        lc.wait()

    return pl.pallas_call(
        body, out_shape=jax.ShapeDtypeStruct((L, 8, N_CHIPS * cb), F32), in_specs=[ANY], out_specs=ANY,
        scratch_shapes=[pltpu.SemaphoreType.DMA((nk,)), pltpu.SemaphoreType.DMA((nk,)),
                        pltpu.SemaphoreType.DMA((1,))],
        name="gather_conv")(conv)


def _swap_start(name, dws, geos, after):
    n = len(dws)

    def body(arrs, news, sin, sout):
        x, y, c, _, _ = _mesh_place()
        for t in range(n):
            _Chunked(_remote(sout[0].at[t], sout[1].at[t], (x, y, 1 - c)),
                     arrs[t], geos[t].half_of_full(1 - c), news[t], _whole(news[t].shape)).start()

    return _comm_call(name, body, dws, after=after, sems_out=[n, n], new=[(g.half_shape, BF16) for g in geos])


def _swap_wait(name, dws, lands, sems, geos, after):
    n = len(dws)

    def body(arrs, news, sin, sout):
        x, y, c, _, _ = _mesh_place()
        for t in range(n):
            cp = _Chunked(_remote(sin[0].at[t], sin[1].at[t], (x, y, 1 - c)),
                          arrs[t], geos[t].half_of_full(1 - c), arrs[n + t], _whole(arrs[n + t].shape))
            cp.whole.wait_send()
            cp.whole.wait_recv()

    _, arrs, _, _ = _comm_call(name, body, list(dws) + list(lands), sems_in=sems, after=after)
    return arrs[:n], arrs[n:]


def _exchange_start(name, ps, entries, after):
    nk = N_CHIPS - 1

    def body(arrs, news, sin, sout):
        x, y, c, j, chips = _mesh_place()
        for e, (pi, geo) in enumerate(entries):
            for k, chip in enumerate(chips):
                dst = news[e].at[k]
                _Chunked(_remote(sout[0].at[nk * e + k], sout[1].at[nk * e + k], (*chip, c)),
                         arrs[pi], geo.in_half(2 * chip[0] + chip[1]), dst, _whole(dst.shape)).start()

    ne = len(entries)
    return _comm_call(name, body, ps, after=after, sems_out=[nk * ne, nk * ne],
                      new=[((nk,) + g.shard_half_shape, BF16) for _, g in entries])


def _exchange_wait(name, ps, lands, sems, entries, after):
    nk = N_CHIPS - 1
    n = len(ps)

    def body(arrs, news, sin, sout):
        x, y, c, j, chips = _mesh_place()
        for e, (pi, geo) in enumerate(entries):
            for k, chip in enumerate(chips):
                dst = arrs[n + e].at[k]
                cp = _Chunked(_remote(sin[0].at[nk * e + k], sin[1].at[nk * e + k], (*chip, c)),
                              arrs[pi], geo.in_half(2 * chip[0] + chip[1]), dst, _whole(dst.shape))
                cp.whole.wait_send()
                cp.whole.wait_recv()

    _, arrs, _, _ = _comm_call(name, body, list(ps) + list(lands), sems_in=sems, after=after)
    return arrs[:n], arrs[n:]


def _join_start(name, gs, geos, after):
    n = len(gs)

    def body(arrs, news, sin, sout):
        x, y, c, _, _ = _mesh_place()
        for t in range(n):
            mine = geos[t].in_shard(c)
            _Chunked(_remote(sout[0].at[t], sout[1].at[t], (x, y, 1 - c)), arrs[t], mine, arrs[t], mine).start()

    return _comm_call(name, body, gs, after=after, sems_out=[n, n])


def _join_wait(name, gs, sems, geos, after):
    n = len(gs)

    def body(arrs, news, sin, sout):
        x, y, c, _, _ = _mesh_place()
        for t in range(n):
            cp = _Chunked(_remote(sin[0].at[t], sin[1].at[t], (x, y, 1 - c)),
                          arrs[t], geos[t].in_shard(c), arrs[t], geos[t].in_shard(1 - c))
            cp.whole.wait_send()
            cp.whole.wait_recv()

    _, arrs, _, _ = _comm_call(name, body, gs, sems_in=sems, after=after)
    return arrs


def _cast_place(w, l, geo, pos, prev, name, after=()):
    if geo.axis == 1:
        tr = _rows_tile(geo.rows, geo.size, 1024 * 1024)
        grid = (geo.rows // tr,)
        blk = (tr, geo.size)
        q = geo.base // geo.size
        out_map = lambda i, p: (i, q + p[0])
    else:
        tr = _rows_tile(geo.size, geo.cols, 1024 * 1024)
        grid = (geo.size // tr,)
        blk = (tr, geo.cols)
        nb = geo.size // tr
        out_map = lambda i, p: (p[0] * nb + i, 0)

    def body(p_ref, w_ref, *rest):
        rest[-1][...] = w_ref[...].astype(BF16)

    in_specs = [pl.BlockSpec((None,) + blk, lambda i, p: (l, i, 0))]
    args = [w]
    aliases = None
    if prev is not None:
        in_specs.append(ANY)
        args.append(prev)
        aliases = {2: 0}
    return _prefetch_call(body, pos, args, after, out_shape=jax.ShapeDtypeStruct((geo.rows, geo.cols), BF16),
                          grid=grid, in_specs=in_specs, out_specs=pl.BlockSpec(blk, out_map), aliases=aliases,
                          name=name)


def _pair_sum(dw, recv, geo, pos, name):
    hs = geo.half_shape
    tr = _rows_tile(hs[0], hs[1], 2048 * 1024)
    nb = hs[0] // tr
    blk = (tr, hs[1])
    if geo.axis == 1:
        own_map = lambda i, p: (p[1] * nb + i, 0)
    else:
        own_map = lambda i, p: (i, p[1])

    def body(p_ref, a_ref, b_ref, o_ref):
        o_ref[...] = (a_ref[...].astype(F32) + b_ref[...].astype(F32)).astype(BF16)

    same = pl.BlockSpec(blk, lambda i, p: (i, 0))
    return _prefetch_call(body, pos, [dw, recv], out_shape=jax.ShapeDtypeStruct(hs, BF16), grid=(nb,),
                          in_specs=[pl.BlockSpec(blk, own_map), same], out_specs=same, name=name)


def _chip_sum(p, recv, geo, pos, name):
    ss = geo.shard_half_shape
    tr = _rows_tile(ss[0], ss[1], 1024 * 1024)
    nb = ss[0] // tr
    blk = (tr, ss[1])
    if geo.axis == 1:
        q = geo.base // geo.size
        own_map = lambda i, p_: (i, q + p_[0])
        out_map = lambda i, p_: (p_[1] * nb + i, 0)
    else:
        own_map = lambda i, p_: (p_[0] * nb + i, 0)
        out_map = lambda i, p_: (i, p_[1])

    def body(p_ref, o_ref, r_ref, out_ref):
        acc = o_ref[...].astype(F32)
        for k in range(N_CHIPS - 1):
            acc = acc + r_ref[k].astype(F32)
        out_ref[...] = acc

    return _prefetch_call(
        body, pos, [p, recv], out_shape=jax.ShapeDtypeStruct(geo.shard_shape, F32), grid=(nb,),
        in_specs=[pl.BlockSpec(blk, own_map), pl.BlockSpec((N_CHIPS - 1,) + blk, lambda i, p_: (0, i, 0))],
        out_specs=pl.BlockSpec(blk, out_map), name=name)


def _adamw_math(w, g, m, v):
    m = ADAM_B1 * m + (1.0 - ADAM_B1) * g
    v = ADAM_B2 * v + (1.0 - ADAM_B2) * (g * g)
    m_hat = m / (1.0 - ADAM_B1 ** ADAM_STEP)
    v_hat = v / (1.0 - ADAM_B2 ** ADAM_STEP)
    delta = -ADAM_LR * (m_hat / (jnp.sqrt(v_hat) + ADAM_EPS) + ADAM_WD * w)
    return delta, m, v


def _adamw_layer(l, w, g, m, v, prev, name, after=()):
    L, R, C = w.shape
    tr = _rows_tile(R, C, 512 * 1024)

    def body(w_ref, g_ref, m_ref, v_ref, *rest):
        go_ref, d_ref, mo_ref, vo_ref = rest[-4:]
        gv = g_ref[...]
        d, mn, vn = _adamw_math(w_ref[...], gv, m_ref[...], v_ref[...])
        go_ref[...] = gv
        d_ref[...] = d
        mo_ref[...] = mn
        vo_ref[...] = vn

    blk = pl.BlockSpec((None, tr, C), lambda i: (l, i, 0))
    sds = jax.ShapeDtypeStruct(w.shape, F32)
    in_specs = [blk, pl.BlockSpec((tr, C), lambda i: (i, 0)), blk, blk]
    args = [w, g, m, v]
    aliases = {}
    if prev is not None:
        in_specs += [ANY] * 4
        args += list(prev)
        aliases = {4 + i: i for i in range(4)}
    return _call(
        body, args, after, out_shape=(sds, sds, sds, sds), grid=(R // tr,), in_specs=in_specs,
        out_specs=(blk,) * 4, input_output_aliases=aliases, name=name, compiler_params=_params("parallel"))


def _adamw_flat(w, g, m, v, name):
    R, C = w.shape
    tr = _tile(R, 1024) if R % 128 == 0 else R

    def body(w_ref, g_ref, m_ref, v_ref, d_ref, mo_ref, vo_ref):
        d, mn, vn = _adamw_math(w_ref[...], g_ref[...], m_ref[...], v_ref[...])
        d_ref[...] = d
        mo_ref[...] = mn
        vo_ref[...] = vn

    blk = pl.BlockSpec((tr, C), lambda i: (i, 0))
    sds = jax.ShapeDtypeStruct(w.shape, F32)
    return pl.pallas_call(
        body, out_shape=(sds, sds, sds), grid=(R // tr,), in_specs=[blk] * 4, out_specs=(blk,) * 3,
        name=name, compiler_params=_params("parallel"))(w, g, m, v)


def _allreduce_small(s, after=()):
    R, C = s.shape

    def body(s_ref, o_ref, rbuf, send_sems, recv_sems):
        x, y, c = lax.axis_index("x"), lax.axis_index("y"), lax.axis_index("c")
        peers = [(x, y, 1 - c), (1 - x, y, c), (x, 1 - y, c)]
        o_ref[...] = s_ref[...]
        for k, peer in enumerate(peers):
            cp = _remote(send_sems.at[k], recv_sems.at[k], peer)(o_ref, rbuf.at[k])
            cp.start()
            cp.wait()
            o_ref[...] = o_ref[...] + rbuf[k]

    vm = pl.BlockSpec(memory_space=pltpu.VMEM)
    return _call(
        body, [s], after, out_shape=jax.ShapeDtypeStruct((R, C), F32), in_specs=[vm], out_specs=vm,
        scratch_shapes=[pltpu.VMEM((3, R, C), F32), pltpu.SemaphoreType.DMA((3,)), pltpu.SemaphoreType.DMA((3,))],
        name="allreduce_small", compiler_params=pltpu.CompilerParams(vmem_limit_bytes=V7X_VMEM_LIMIT))


class _GradBatch:
    def __init__(self, tag, dws, geos4, entries, pos):
        self.tag, self.dws, self.geos4, self.entries, self.pos = tag, dws, geos4, entries, pos

    def start_swap(self, after):
        self.s1, self.dws, self.land1, tok = _swap_start(f"{self.tag}_swap_start", self.dws, self.geos4, after)
        return tok

    def swap_to_exchange(self, after):
        dws, lands = _swap_wait(f"{self.tag}_swap_wait", self.dws, self.land1, self.s1, self.geos4, after)
        ps = [_pair_sum(d, r, g, self.pos, f"{self.tag}_pair_sum_{i}")
              for i, (d, r, g) in enumerate(zip(dws, lands, self.geos4))]
        self.s2, self.ps, self.land2, tok = _exchange_start(f"{self.tag}_exch_start", ps, self.entries, ())
        return tok

    def exchange_to_join(self, after):
        ps, lands = _exchange_wait(f"{self.tag}_exch_wait", self.ps, self.land2, self.s2, self.entries, after)
        self.geos5 = [g for _, g in self.entries]
        gs = [_chip_sum(ps[pi], r, g, self.pos, f"{self.tag}_chip_sum_{e}")
              for e, ((pi, g), r) in enumerate(zip(self.entries, lands))]
        self.s3, self.gs, _, tok = _join_start(f"{self.tag}_join_start", gs, self.geos5, ())
        return tok

    def finish(self, after):
        return _join_wait(f"{self.tag}_join_wait", self.gs, self.s3, self.geos5, after)


def _pack(pieces):
    rows = []
    for p in pieces:
        flat = p.reshape(-1)
        pad = (-flat.shape[0]) % 1024
        rows.append(jnp.pad(flat, (0, pad)).reshape(-1, 128))
    return jnp.concatenate(rows, axis=0)


def _unpack(buf, shapes):
    out, r = [], 0
    for shp in shapes:
        n = math.prod(shp)
        nr = -(-n // 1024) * 8
        out.append(buf[r:r + nr].reshape(-1)[:n].reshape(shp))
        r += nr
    return out


def kernel(x, norm1_g, w_in, gmlp_ln_g, gmlp_ln_b, w_spatial, b_spatial, conv_w, group_norm_g, w_out, norm2_g, w_gate, w_up, w_down, final_norm_g, loss_target, m_norm1_g, m_w_in, m_gmlp_ln_g, m_gmlp_ln_b, m_w_spatial, m_b_spatial, m_conv_w, m_group_norm_g, m_w_out, m_norm2_g, m_w_gate, m_w_up, m_w_down, m_final_norm_g, v_norm1_g, v_w_in, v_gmlp_ln_g, v_gmlp_ln_b, v_w_spatial, v_b_spatial, v_conv_w, v_group_norm_g, v_w_out, v_norm2_g, v_w_gate, v_w_up, v_w_down, v_final_norm_g):
    L, D, n_in = w_in.shape
    T = x.shape[1]
    nin = N_CHIPS * n_in
    A = nin // 5
    H = A // HEAD_DIM
    n_f = w_gate.shape[2]
    F = N_CHIPS * n_f
    n_o = w_out.shape[1]
    cb = conv_w.shape[2]
    assert A == H * HEAD_DIM and T % 256 == 0 and N_CHIPS * n_o == D and N_CHIPS * cb == A

    g_in, g_out, g_down = _Geom(D, nin, 1, n_in), _Geom(D, D, 0, n_o), _Geom(F, D, 0, n_f)
    g_gate, g_up, g_gu = _Geom(D, 2 * F, 1, n_f, 0), _Geom(D, 2 * F, 1, n_f, F), _Geom(D, 2 * F, 1, n_f)
    pos = jnp.stack([2 * lax.axis_index("x") + lax.axis_index("y"), lax.axis_index("c")]).astype(jnp.int32)
    row = lambda v: v.reshape(1, -1)

    conv_full = _gather_conv(jnp.pad(conv_w, ((0, 0), (0, 8 - CONV_TAPS), (0, 0))))
    members = [[g_in], [g_out], [g_gate, g_up], [g_down]]
    sources = [[w_in], [w_out], [w_gate, w_up], [w_down]]
    placed, sems_of = {}, {}
    tok = conv_full
    for l in range(L):
        for a in range(4):
            arr = None
            for m, (w, geo) in enumerate(zip(sources[a], members[a])):
                arr = _cast_place(w, l, geo, pos, arr, f"l{l}_place_{a}_{m}", after=(tok,))
            sems_of[l, a], placed[l, a], tok = _gather_start(f"l{l}_gather_start_{a}", arr, members[a], (tok,))

    def arrive(l, a, after):
        return _gather_forward(f"l{l}_gather_fwd_{a}", placed[l, a], members[a], sems_of[l, a], after)

    def landed(l, a, fs, arr, after):
        return _gather_wait(f"l{l}_gather_wait_{a}", arr, members[a], fs, after)

    frame = jnp.arange(HEAD_DIM)
    mask = (frame[None, :] // CHUNK) <= (frame[:, None] // CHUNK)

    xs = x[0]
    acts = []
    for l in range(L):
        wm = jnp.where(mask[None], w_spatial[l], 0.0).astype(BF16)
        wmt = jnp.swapaxes(wm, 1, 2)
        bfull = jnp.broadcast_to(b_spatial[l][:, :, None], (H, HEAD_DIM, HEAD_DIM))
        sm = dict(lng=row(gmlp_ln_g[l]), lnb=row(gmlp_ln_b[l]), wm=wm, wmt=wmt, bfull=bfull,
                  cw=conv_full[l], gn=row(group_norm_g[l]))
        if l == 0:
            h = _rms_fwd(xs, row(norm1_g[l]), f"l{l}_rms1", after=(tok,))
            fs, arr, tok = arrive(l, 0, (h,))
            W_in = landed(l, 0, fs, arr, (tok,))
        else:
            fs, arr, tok = arrive(l, 0, (xs,))
            h = _rms_fwd(xs, row(norm1_g[l]), f"l{l}_rms1", after=(tok,))
            W_in = landed(l, 0, fs, arr, (h,))
        z = _mm_nn(h, W_in, out_dtype=BF16, name=f"l{l}_mm_in")
        fs, arr, tok = arrive(l, 1, (z,))
        y = _mixer_fwd(z, sm["lng"], sm["lnb"], wm, bfull, sm["cw"], sm["gn"], f"l{l}_mixer", after=(tok,))
        W_out = landed(l, 1, fs, arr, (y,))
        if l == 0:
            x1, h2 = _mm_res_rms(y, W_out, xs, row(norm2_g[l]), name=f"l{l}_mm_out")
            fs, arr, tok = arrive(l, 2, (x1,))
            W_gu = landed(l, 2, fs, arr, (tok,))
        else:
            fs, arr, tok = arrive(l, 2, (W_out,))
            x1, h2 = _mm_res_rms(y, W_out, xs, row(norm2_g[l]), name=f"l{l}_mm_out", after=(tok,))
            W_gu = landed(l, 2, fs, arr, (x1,))
        gu, act = _mm_swiglu(h2, W_gu, name=f"l{l}_mm_swiglu")
        fs, arr, tok = arrive(l, 3, (act,))
        W_down = landed(l, 3, fs, arr, (tok,))
        x2 = _mm_nn(act, W_down, res=x1, out_dtype=F32, tm=512, name=f"l{l}_mm_down")
        acts.append(dict(x=xs, h=h, z=z, y=y, x1=x1, h2=h2, gu=gu, act=act, sm=sm,
                         W_in=W_in, W_out=W_out, W_gu=W_gu, W_down=W_down))
        xs = x2

    loss_vec, dx, dxb, dgf = _loss_head(xs, row(final_norm_g), loss_target[0], "loss_head")
    loss = lax.psum(loss_vec[0, 0], ("x", "y", "c"))

    big_w = {"in": (w_in, m_w_in, v_w_in), "out": (w_out, m_w_out, v_w_out), "gate": (w_gate, m_w_gate, v_w_gate),
             "up": (w_up, m_w_up, v_w_up), "down": (w_down, m_w_down, v_w_down)}
    big = {nm: None for nm in big_w}

    def adamw(l, names, gs):
        after = ()
        for nm, g in zip(names, gs):
            w, m, v = big_w[nm]
            big[nm] = _adamw_layer(l, w, g, m, v, big[nm], f"l{l}_adamw_{nm}", after=after)
            after = (big[nm][1],)
        return after[0]

    small_grads = [None] * L
    pend_dg = pend_oi = None
    tok = ()
    for l in reversed(range(L)):
        a = acts[l]
        sm = a["sm"]
        dgu = _mm_nt_swiglu_bwd(dxb, a["W_down"], a["gu"], name=f"l{l}_bwd_down", after=tok)
        if pend_dg:
            tok = (pend_dg[0].exchange_to_join((dgu,)),)
        dW_down = _mm_tn(a["act"], dxb, name=f"l{l}_dw_down", after=tok)
        if pend_dg:
            tok = (adamw(pend_dg[1], ["down", "gate", "up"], pend_dg[0].finish((dW_down,))),)
        dh2 = _mm_nt_dgu(dgu, a["W_gu"], name=f"l{l}_bwd_gu", after=tok)
        if pend_oi:
            tok = (pend_oi[0].exchange_to_join((dh2,)),)
        dW_gu = _mm_tn(a["h2"], dgu, tkw=1024, tnw=512, name=f"l{l}_dw_gu", after=tok)
        if pend_oi:
            tok = (adamw(pend_oi[1], ["out", "in"], pend_oi[0].finish((dW_gu,))),)
        dg_batch = _GradBatch(f"l{l}_dg", [dW_down, dW_gu], [g_down, g_gu], [(0, g_down), (1, g_gate), (1, g_up)], pos)
        tok = (dg_batch.start_swap(tok),)
        dx1, dx1b, dg2 = _rms_bwd(dh2, a["x1"], row(norm2_g[l]), dx, f"l{l}_rms2_bwd", after=tok)
        dy = _mm_nt(dx1b, a["W_out"], out_dtype=F32, name=f"l{l}_bwd_out")
        dW_out = _mm_tn(a["y"], dx1b, name=f"l{l}_dw_out")
        tok = (dg_batch.swap_to_exchange((dW_out, dy)),)
        dz, dgn, dlng, dlnb, dws, dbs, dcw = _mixer_bwd(
            a["z"], dy, sm["lng"], sm["lnb"], sm["wm"], sm["wmt"], sm["bfull"], sm["cw"], sm["gn"],
            f"l{l}_mixer_bwd", after=tok)
        dW_in = _mm_tn(a["h"], dz, name=f"l{l}_dw_in")
        oi_batch = _GradBatch(f"l{l}_oi", [dW_out, dW_in], [g_out, g_in], [(0, g_out), (1, g_in)], pos)
        tok = (oi_batch.start_swap(()),)
        dh = _mm_nt(dz, a["W_in"], out_dtype=F32, tm=512, name=f"l{l}_bwd_in", after=tok)
        dx, dxb, dg1 = _rms_bwd(dh, a["x"], row(norm1_g[l]), dx1, f"l{l}_rms1_bwd")
        tok = (oi_batch.swap_to_exchange((dx,)),)
        small_grads[l] = [dg1[0], dlng[0], dlnb[0], dws, dbs[:, 0, :], dcw[:CONV_TAPS], dgn[0], dg2[0]]
        pend_dg, pend_oi = (dg_batch, l), (oi_batch, l)
    grad_x = dx[None]

    tok = (pend_dg[0].exchange_to_join(tok),)
    tok = (adamw(pend_dg[1], ["down", "gate", "up"], pend_dg[0].finish(tok)),)
    tok = (pend_oi[0].exchange_to_join(tok),)
    pieces = [p for l in range(L) for p in small_grads[l]] + [dgf[0]]
    red = _allreduce_small(_pack(pieces), after=tok)
    adamw(pend_oi[1], ["out", "in"], pend_oi[0].finish((red,)))

    red_list = _unpack(red, [p.shape for p in pieces])
    per = len(small_grads[0])
    stack = lambda i: jnp.stack([red_list[l * per + i] for l in range(L)])
    g_small = {"norm1_g": stack(0), "gmlp_ln_g": stack(1), "gmlp_ln_b": stack(2), "w_spatial": stack(3),
               "b_spatial": stack(4), "group_norm_g": stack(6), "norm2_g": stack(7),
               "final_norm_g": red_list[L * per]}
    g_small["conv_w"] = lax.dynamic_slice_in_dim(stack(5), pos[0] * cb, cb, axis=2)
    small_w = {"norm1_g": (norm1_g, m_norm1_g, v_norm1_g), "gmlp_ln_g": (gmlp_ln_g, m_gmlp_ln_g, v_gmlp_ln_g),
               "gmlp_ln_b": (gmlp_ln_b, m_gmlp_ln_b, v_gmlp_ln_b), "w_spatial": (w_spatial, m_w_spatial, v_w_spatial),
               "b_spatial": (b_spatial, m_b_spatial, v_b_spatial), "conv_w": (conv_w, m_conv_w, v_conv_w),
               "group_norm_g": (group_norm_g, m_group_norm_g, v_group_norm_g),
               "norm2_g": (norm2_g, m_norm2_g, v_norm2_g), "final_norm_g": (final_norm_g, m_final_norm_g, v_final_norm_g)}
    snames = list(small_w)
    sd, smn, svn = _adamw_flat(_pack([small_w[n][0] for n in snames]), _pack([g_small[n] for n in snames]),
                               _pack([small_w[n][1] for n in snames]), _pack([small_w[n][2] for n in snames]),
                               "adamw_small")
    sshapes = [small_w[n][0].shape for n in snames]
    sd, smn, svn = _unpack(sd, sshapes), _unpack(smn, sshapes), _unpack(svn, sshapes)
    small_out = {n: (g_small[n], sd[i], smn[i], svn[i]) for i, n in enumerate(snames)}

    order = ["norm1_g", "w_in", "gmlp_ln_g", "gmlp_ln_b", "w_spatial", "b_spatial", "conv_w", "group_norm_g",
             "w_out", "norm2_g", "w_gate", "w_up", "w_down", "final_norm_g"]
    res = {n: (big[n[2:]] if n[2:] in big else small_out[n]) for n in order}
    return (loss, grad_x, *[res[n][0] for n in order], *[res[n][1] for n in order],
            *[res[n][2] for n in order], *[res[n][3] for n in order])
```

```python
import math

import jax
import jax.numpy as jnp
from jax import lax
from jax.experimental import pallas as pl
from jax.experimental.pallas import tpu as pltpu
from jax.experimental.pallas import tpu_sc as plsc

RMS_EPS = 1e-6
LN_EPS = 1e-5
HEAD_DIM = 128
CHUNK = 64
CONV_TAPS = 3
HALO = 16
ADAM_LR = 0.001
ADAM_B1 = 0.9
ADAM_B2 = 0.999
ADAM_EPS = 1e-08
ADAM_WD = 0.01
ADAM_STEP = 10
V7X_VMEM_LIMIT = 56 * 1024 * 1024
N_CHIPS = 4
DMA_CHUNK_BYTES = 2 * 1024 * 1024
EPILOGUE_ROWS = 256
SC_LANES = 16
SC_BLOCK_ELEMS = 5120
MESH = pl.DeviceIdType.MESH
F32 = jnp.float32
BF16 = jnp.bfloat16
ANY = pl.BlockSpec(memory_space=pl.ANY)
HBM = pl.BlockSpec(memory_space=pltpu.HBM)
SEM = pl.BlockSpec(memory_space=pltpu.SEMAPHORE)
EFFECT = pltpu.SideEffectType.DATAFLOW_SIDE_EFFECTING


def _tile(n, pref):
    if n <= pref:
        return n
    best = None
    for t in range(128, pref + 1, 128):
        if n % t == 0:
            best = t
    assert best is not None, (n, pref)
    return best


def _rows_tile(rows, cols, budget_elems):
    tr = rows
    while tr * cols > budget_elems and tr % 2 == 0 and (tr // 2) % 16 == 0:
        tr //= 2
    return tr


def _params(*sem):
    return pltpu.CompilerParams(dimension_semantics=sem if sem else None,
                                vmem_limit_bytes=V7X_VMEM_LIMIT)


def _call(body, args, after, **kw):
    n, na = len(args), len(after)
    if na:
        inner = body

        def body(*refs):
            inner(*refs[:n], *refs[n + na:])

        kw["in_specs"] = list(kw["in_specs"]) + [ANY] * na
    return pl.pallas_call(body, **kw)(*args, *after)


def _prefetch_call(body, pos, args, after=(), *, out_shape, grid, in_specs, out_specs, aliases=None, name):
    n, na = 1 + len(args), len(after)
    if na:
        inner = body

        def body(*refs):
            inner(*refs[:n], *refs[n + na:])

    return pl.pallas_call(
        body, out_shape=out_shape,
        grid_spec=pltpu.PrefetchScalarGridSpec(num_scalar_prefetch=1, grid=grid,
                                               in_specs=list(in_specs) + [ANY] * na, out_specs=out_specs),
        input_output_aliases=aliases or {}, name=name,
        compiler_params=_params(*(["parallel"] * len(grid))))(pos, *args, *after)


def _gelu(x):
    c = math.sqrt(2.0 / math.pi)
    return 0.5 * x * (1.0 + jnp.tanh(c * (x + 0.044715 * x * x * x)))


def _gelu_and_grad(x):
    c = math.sqrt(2.0 / math.pi)
    x2 = x * x
    th = jnp.tanh(c * (x + 0.044715 * x * x2))
    val = 0.5 * x * (1.0 + th)
    grad = 0.5 * (1.0 + th) + 0.5 * x * (1.0 - th * th) * (c * (1.0 + 3.0 * 0.044715 * x2))
    return val, grad


def _sigmoid(x):
    return 1.0 / (1.0 + jnp.exp(-x))


def _nt(a, b):
    return lax.dot_general(a, b, (((1,), (1,)), ((), ())), preferred_element_type=F32)


def _tn(a, b):
    return lax.dot_general(a, b, (((0,), (0,)), ((), ())), preferred_element_type=F32)


def _rms_fwd(x, g, name, after=()):
    T, D = x.shape
    tr = _tile(T, 512)

    def body(x_ref, g_ref, h_ref):
        xv = x_ref[...]
        r = lax.rsqrt(jnp.mean(xv * xv, axis=-1, keepdims=True) + RMS_EPS)
        h_ref[...] = ((xv * r) * g_ref[...]).astype(h_ref.dtype)

    return _call(
        body, [x, g], after, out_shape=jax.ShapeDtypeStruct((T, D), BF16), grid=(T // tr,),
        in_specs=[pl.BlockSpec((tr, D), lambda i: (i, 0)), pl.BlockSpec((1, D), lambda i: (0, 0))],
        out_specs=pl.BlockSpec((tr, D), lambda i: (i, 0)),
        name=name, compiler_params=_params("parallel"))


def _rms_bwd(dh, x, g, dres, name, after=()):
    T, D = x.shape
    tr = _tile(T, 256)

    def body(dh_ref, x_ref, g_ref, dres_ref, dx_ref, dxb_ref, dg_ref):
        i = pl.program_id(0)
        xv = x_ref[...]
        dhv = dh_ref[...].astype(F32)
        r = lax.rsqrt(jnp.mean(xv * xv, axis=-1, keepdims=True) + RMS_EPS)
        xh = xv * r
        q = dhv * g_ref[...]
        dx = dres_ref[...] + r * (q - xh * jnp.mean(q * xh, axis=-1, keepdims=True))
        dx_ref[...] = dx
        dxb_ref[...] = dx.astype(BF16)
        part = jnp.sum(dhv * xh, axis=0, keepdims=True)

        @pl.when(i == 0)
        def _():
            dg_ref[...] = part

        @pl.when(i > 0)
        def _():
            dg_ref[...] += part

    row = pl.BlockSpec((tr, D), lambda i: (i, 0))
    vec = pl.BlockSpec((1, D), lambda i: (0, 0))
    return _call(
        body, [dh, x, g, dres], after,
        out_shape=(jax.ShapeDtypeStruct((T, D), F32), jax.ShapeDtypeStruct((T, D), BF16),
                   jax.ShapeDtypeStruct((1, D), F32)),
        grid=(T // tr,), in_specs=[row, row, vec, row], out_specs=(row, row, vec),
        name=name, compiler_params=_params("arbitrary"))


def _loss_head(x, g, tgt, name):
    T, D = x.shape
    tr = _tile(T, 256)

    def body(x_ref, g_ref, t_ref, loss_ref, dx_ref, dxb_ref, dg_ref):
        i = pl.program_id(0)
        xv = x_ref[...]
        gv = g_ref[...]
        r = lax.rsqrt(jnp.mean(xv * xv, axis=-1, keepdims=True) + RMS_EPS)
        xh = xv * r
        err = xh * gv - t_ref[...]
        lpart = jnp.full((1, 128), 0.5 * jnp.sum(jnp.mean(err * err, axis=-1, keepdims=True)), F32)
        dy = err * (1.0 / D)
        q = dy * gv
        dx = r * (q - xh * jnp.mean(q * xh, axis=-1, keepdims=True))
        dx_ref[...] = dx
        dxb_ref[...] = dx.astype(BF16)
        gpart = jnp.sum(dy * xh, axis=0, keepdims=True)

        @pl.when(i == 0)
        def _():
            loss_ref[...] = lpart
            dg_ref[...] = gpart

        @pl.when(i > 0)
        def _():
            loss_ref[...] += lpart
            dg_ref[...] += gpart

    row = pl.BlockSpec((tr, D), lambda i: (i, 0))
    vec = pl.BlockSpec((1, D), lambda i: (0, 0))
    return pl.pallas_call(
        body,
        out_shape=(jax.ShapeDtypeStruct((1, 128), F32), jax.ShapeDtypeStruct((T, D), F32),
                   jax.ShapeDtypeStruct((T, D), BF16), jax.ShapeDtypeStruct((1, D), F32)),
        grid=(T // tr,), in_specs=[row, vec, row],
        out_specs=(pl.BlockSpec((1, 128), lambda i: (0, 0)), row, row, vec),
        name=name, compiler_params=_params("arbitrary"))(x, g, tgt)


def _mm_nn(a, w, *, res=None, out_dtype, tm=1024, tn=1024, tk=None, name, after=()):
    M, K = a.shape
    N = w.shape[1]
    tm, tn = _tile(M, tm), _tile(N, tn)
    tk = K if tk is None else _tile(K, tk)
    nk = K // tk
    has_res = res is not None

    def body(*refs):
        a_ref, w_ref = refs[0], refs[1]
        r_ref = refs[2] if has_res else None
        o_ref = refs[2 + has_res]
        part = jnp.dot(a_ref[...], w_ref[...], preferred_element_type=F32)

        def finish(acc):
            if has_res:
                acc = r_ref[...] + acc
            o_ref[...] = acc.astype(o_ref.dtype)

        if nk == 1:
            finish(part)
        else:
            acc_ref = refs[3 + has_res]
            k = pl.program_id(2)

            @pl.when(k == 0)
            def _():
                acc_ref[...] = part

            @pl.when(jnp.logical_and(k > 0, k < nk - 1))
            def _():
                acc_ref[...] += part

            @pl.when(k == nk - 1)
            def _():
                finish(acc_ref[...] + part)

    in_specs = [pl.BlockSpec((tm, tk), lambda i, j, k: (i, k)),
                pl.BlockSpec((tk, tn), lambda i, j, k: (k, j))]
    args = [a, w]
    if has_res:
        in_specs.append(pl.BlockSpec((tm, tn), lambda i, j, k: (i, j)))
        args.append(res)
    return _call(
        body, args, after, out_shape=jax.ShapeDtypeStruct((M, N), out_dtype), grid=(M // tm, N // tn, nk),
        in_specs=in_specs, out_specs=pl.BlockSpec((tm, tn), lambda i, j, k: (i, j)),
        scratch_shapes=[pltpu.VMEM((tm, tn), F32)] if nk > 1 else [],
        name=name, compiler_params=_params("parallel", "parallel", "arbitrary"))


def _mm_res_rms(a, w, res, g, *, tm=512, name, after=()):
    M, K = a.shape
    N = w.shape[1]
    tm = _tile(M, tm)

    def body(a_ref, w_ref, r_ref, g_ref, x_ref, h_ref):
        xv = r_ref[...] + jnp.dot(a_ref[...], w_ref[...], preferred_element_type=F32)
        x_ref[...] = xv
        r = lax.rsqrt(jnp.mean(xv * xv, axis=-1, keepdims=True) + RMS_EPS)
        h_ref[...] = ((xv * r) * g_ref[...]).astype(BF16)

    row = lambda n: pl.BlockSpec((tm, n), lambda i: (i, 0))
    return _call(
        body, [a, w, res, g], after,
        out_shape=(jax.ShapeDtypeStruct((M, N), F32), jax.ShapeDtypeStruct((M, N), BF16)), grid=(M // tm,),
        in_specs=[row(K), pl.BlockSpec((K, N), lambda i: (0, 0)), row(N), pl.BlockSpec((1, N), lambda i: (0, 0))],
        out_specs=(row(N), row(N)), name=name, compiler_params=_params("parallel"))


def _mm_swiglu(h, wgu, *, tm=1024, tn=512, name):
    T, D = h.shape
    F = wgu.shape[1] // 2
    tm, tn = _tile(T, tm), _tile(F, tn)
    nf = F // tn

    rc = _tile(tm, EPILOGUE_ROWS)

    def body(h_ref, wg_ref, wu_ref, fac_ref, act_ref):
        for r in range(tm // rc):
            rows = slice(r * rc, (r + 1) * rc)
            hv = h_ref[rows, :]
            g = jnp.dot(hv, wg_ref[...], preferred_element_type=F32)
            u = jnp.dot(hv, wu_ref[...], preferred_element_type=F32)
            s = _sigmoid(g)
            t = g * s
            act_ref[rows, :] = (t * u).astype(BF16)
            fac_ref[0, rows, :] = (u * (s + t * (1.0 - s))).astype(BF16)
            fac_ref[1, rows, :] = t.astype(BF16)

    return pl.pallas_call(
        body,
        out_shape=(jax.ShapeDtypeStruct((2, T, F), BF16), jax.ShapeDtypeStruct((T, F), BF16)),
        grid=(T // tm, nf),
        in_specs=[pl.BlockSpec((tm, D), lambda i, j: (i, 0)),
                  pl.BlockSpec((D, tn), lambda i, j: (0, j)),
                  pl.BlockSpec((D, tn), lambda i, j: (0, j + nf))],
        out_specs=(pl.BlockSpec((2, tm, tn), lambda i, j: (0, i, j)),
                   pl.BlockSpec((tm, tn), lambda i, j: (i, j))),
        name=name, compiler_params=_params("parallel", "parallel"))(h, wgu, wgu)


def _mm_nt(a, w, *, out_dtype, tm=1024, tn=1024, tk=None, name, after=()):
    M, K = a.shape
    N = w.shape[0]
    tm, tn = _tile(M, tm), _tile(N, tn)
    tk = K if tk is None else _tile(K, tk)
    nk = K // tk

    def body(*refs):
        a_ref, w_ref, o_ref = refs[0], refs[1], refs[2]
        part = _nt(a_ref[...], w_ref[...])
        if nk == 1:
            o_ref[...] = part.astype(o_ref.dtype)
        else:
            acc_ref = refs[3]
            k = pl.program_id(2)

            @pl.when(k == 0)
            def _():
                acc_ref[...] = part

            @pl.when(jnp.logical_and(k > 0, k < nk - 1))
            def _():
                acc_ref[...] += part

            @pl.when(k == nk - 1)
            def _():
                o_ref[...] = (acc_ref[...] + part).astype(o_ref.dtype)

    return _call(
        body, [a, w], after, out_shape=jax.ShapeDtypeStruct((M, N), out_dtype), grid=(M // tm, N // tn, nk),
        in_specs=[pl.BlockSpec((tm, tk), lambda i, j, k: (i, k)),
                  pl.BlockSpec((tn, tk), lambda i, j, k: (j, k))],
        out_specs=pl.BlockSpec((tm, tn), lambda i, j, k: (i, j)),
        scratch_shapes=[pltpu.VMEM((tm, tn), F32)] if nk > 1 else [],
        name=name, compiler_params=_params("parallel", "parallel", "arbitrary"))


def _mm_nt_swiglu_bwd(dxb, wdown, gu, *, tm=1024, tn=512, name, after=()):
    T, D = dxb.shape
    F = wdown.shape[0]
    tm, tn = _tile(T, tm), _tile(F, tn)

    def body(dx_ref, w_ref, fac_ref, dgu_ref):
        da = _nt(dx_ref[...], w_ref[...])
        dgu_ref[0] = (da * fac_ref[0].astype(F32)).astype(BF16)
        dgu_ref[1] = (da * fac_ref[1].astype(F32)).astype(BF16)

    blk3 = pl.BlockSpec((2, tm, tn), lambda i, j: (0, i, j))
    return _call(
        body, [dxb, wdown, gu], after, out_shape=jax.ShapeDtypeStruct((2, T, F), BF16), grid=(T // tm, F // tn),
        in_specs=[pl.BlockSpec((tm, D), lambda i, j: (i, 0)),
                  pl.BlockSpec((tn, D), lambda i, j: (j, 0)), blk3],
        out_specs=blk3, name=name, compiler_params=_params("parallel", "parallel"))


def _mm_nt_dgu(dgu, wgu, *, tm=1024, tn=512, tk=5632, name, after=()):
    _, T, F = dgu.shape
    D = wgu.shape[0]
    tm, tn, tk = _tile(T, tm), _tile(D, tn), _tile(F, tk)
    nkf = F // tk
    nk = 2 * nkf

    def body(a_ref, w_ref, o_ref, acc_ref):
        k = pl.program_id(2)
        part = _nt(a_ref[...], w_ref[...])

        @pl.when(k == 0)
        def _():
            acc_ref[...] = part

        @pl.when(jnp.logical_and(k > 0, k < nk - 1))
        def _():
            acc_ref[...] += part

        @pl.when(k == nk - 1)
        def _():
            o_ref[...] = (acc_ref[...] + part).astype(o_ref.dtype)

    return _call(
        body, [dgu, wgu], after, out_shape=jax.ShapeDtypeStruct((T, D), BF16), grid=(T // tm, D // tn, nk),
        in_specs=[pl.BlockSpec((None, tm, tk), lambda i, j, k: (k // nkf, i, k % nkf)),
                  pl.BlockSpec((tn, tk), lambda i, j, k: (j, k))],
        out_specs=pl.BlockSpec((tm, tn), lambda i, j, k: (i, j)),
        scratch_shapes=[pltpu.VMEM((tm, tn), F32)],
        name=name, compiler_params=_params("parallel", "parallel", "arbitrary"))


def _mm_tn(a, g, *, tkw=512, tnw=1024, name, after=()):
    T, Kw = a.shape
    pair = g.ndim == 3
    Nw = 2 * g.shape[2] if pair else g.shape[1]
    tkw = _tile(Kw, tkw)
    tnw = _tile(g.shape[2] if pair else Nw, tnw)
    nf = (Nw // 2) // tnw if pair else 0

    def body(a_ref, g_ref, o_ref):
        o_ref[...] = _tn(a_ref[...], g_ref[...]).astype(o_ref.dtype)

    if pair:
        g_spec = pl.BlockSpec((None, T, tnw), lambda i, j: (j // nf, 0, j % nf))
    else:
        g_spec = pl.BlockSpec((T, tnw), lambda i, j: (0, j))
    return _call(
        body, [a, g], after, out_shape=jax.ShapeDtypeStruct((Kw, Nw), BF16), grid=(Kw // tkw, Nw // tnw),
        in_specs=[pl.BlockSpec((T, tkw), lambda i, j: (0, i)), g_spec],
        out_specs=pl.BlockSpec((tkw, tnw), lambda i, j: (i, j)),
        name=name, compiler_params=_params("parallel", "parallel"))


def _mixer_specs(T, A, tr):
    nin = 5 * A
    nb = tr // HALO
    last = T // HALO - 1
    prev = pl.BlockSpec((HALO, nin), lambda i: (jnp.maximum(i * nb - 1, 0), 0))
    cur = pl.BlockSpec((tr, nin), lambda i: (i, 0))
    nxt = pl.BlockSpec((HALO, nin), lambda i: (jnp.minimum((i + 1) * nb, last), 0))
    return prev, cur, nxt


def _conv(p_ext, cw_ref):
    return (cw_ref[2:3, :] * p_ext + cw_ref[1:2, :] * pltpu.roll(p_ext, 1, 0)
            + cw_ref[0:1, :] * pltpu.roll(p_ext, 2, 0))


def _mixer_fwd(z, lng, lnb, wm, bfull, cw, gn, name, after=()):
    T, nin = z.shape
    A = nin // 5
    H = A // HEAD_DIM
    tr = _tile(T, 256)
    nblk = tr // HEAD_DIM

    def body(zp_ref, z_ref, lng_ref, lnb_ref, wm_ref, bf_ref, cw_ref, gn_ref, y_ref, vn_ref, mix_ref):
        i = pl.program_id(0)
        u = _gelu(z_ref[:, 0:A].astype(F32))
        vg = _gelu(z_ref[:, A:2 * A].astype(F32))
        xc = vg - jnp.mean(vg, axis=-1, keepdims=True)
        rstd = lax.rsqrt(jnp.mean(xc * xc, axis=-1, keepdims=True) + LN_EPS)
        vn_ref[...] = ((xc * rstd) * lng_ref[...] + lnb_ref[...]).astype(BF16)
        for cb in range(nblk):
            rows = slice(cb * HEAD_DIM, (cb + 1) * HEAD_DIM)
            for h in range(H):
                cols = slice(h * HEAD_DIM, (h + 1) * HEAD_DIM)
                mix_ref[rows, cols] = jnp.dot(wm_ref[h], vn_ref[rows, cols],
                                              preferred_element_type=F32) + bf_ref[h]
        ya = u * mix_ref[...]
        ra = lax.rsqrt(jnp.mean(ya * ya, axis=-1, keepdims=True) + RMS_EPS)
        y_ref[:, 0:A] = ((ya * ra) * gn_ref[:, 0:A]).astype(BF16)

        p_prev = zp_ref[:, 3 * A:4 * A].astype(F32) * zp_ref[:, 4 * A:5 * A].astype(F32)
        p_prev = jnp.where(i > 0, p_prev, 0.0)
        p_cur = z_ref[:, 3 * A:4 * A].astype(F32) * z_ref[:, 4 * A:5 * A].astype(F32)
        cv = _conv(jnp.concatenate([p_prev, p_cur], axis=0), cw_ref)[HALO:]
        yb = z_ref[:, 2 * A:3 * A].astype(F32) * cv
        rb = lax.rsqrt(jnp.mean(yb * yb, axis=-1, keepdims=True) + RMS_EPS)
        y_ref[:, A:2 * A] = ((yb * rb) * gn_ref[:, A:2 * A]).astype(BF16)

    prev, cur, _ = _mixer_specs(T, A, tr)
    full = lambda shape: pl.BlockSpec(shape, lambda i: (0,) * len(shape))
    return _call(
        body, [z, z, lng, lnb, wm, bfull, cw, gn], after,
        out_shape=jax.ShapeDtypeStruct((T, 2 * A), BF16), grid=(T // tr,),
        in_specs=[prev, cur, full((1, A)), full((1, A)), full((H, HEAD_DIM, HEAD_DIM)),
                  full((H, HEAD_DIM, HEAD_DIM)), full((8, A)), full((1, 2 * A))],
        out_specs=pl.BlockSpec((tr, 2 * A), lambda i: (i, 0)),
        scratch_shapes=[pltpu.VMEM((tr, A), BF16), pltpu.VMEM((tr, A), F32)],
        name=name, compiler_params=_params("parallel"))


def _mixer_bwd(z, dy, lng, lnb, wm, wmt, bfull, cw, gn, name, after=()):
    T, nin = z.shape
    A = nin // 5
    H = A // HEAD_DIM
    tr = _tile(T, 256)
    nblk = tr // HEAD_DIM
    ngrid = T // tr
    next_ = tr + 2 * HALO

    def body(zp_ref, z_ref, zn_ref, dy_ref, dyn_ref, lng_ref, lnb_ref, wm_ref, wmt_ref, bf_ref, cw_ref, gn_ref,
             dz_ref, dgn_ref, dlng_ref, dlnb_ref, dws_ref, dbs_ref, dcw_ref,
             vn_ref, mix_ref, dmix_ref, dvn_ref):
        i = pl.program_id(0)

        @pl.when(i == 0)
        def _():
            dgn_ref[...] = jnp.zeros_like(dgn_ref)
            dlng_ref[...] = jnp.zeros_like(dlng_ref)
            dlnb_ref[...] = jnp.zeros_like(dlnb_ref)
            dws_ref[...] = jnp.zeros_like(dws_ref)
            dbs_ref[...] = jnp.zeros_like(dbs_ref)
            dcw_ref[...] = jnp.zeros_like(dcw_ref)

        u, du_dz = _gelu_and_grad(z_ref[:, 0:A].astype(F32))
        vg, dv_dz = _gelu_and_grad(z_ref[:, A:2 * A].astype(F32))
        xc = vg - jnp.mean(vg, axis=-1, keepdims=True)
        rstd = lax.rsqrt(jnp.mean(xc * xc, axis=-1, keepdims=True) + LN_EPS)
        vhat = xc * rstd
        vn_ref[...] = (vhat * lng_ref[...] + lnb_ref[...]).astype(BF16)
        for cb in range(nblk):
            rows = slice(cb * HEAD_DIM, (cb + 1) * HEAD_DIM)
            for h in range(H):
                cols = slice(h * HEAD_DIM, (h + 1) * HEAD_DIM)
                mix_ref[rows, cols] = jnp.dot(wm_ref[h], vn_ref[rows, cols],
                                              preferred_element_type=F32) + bf_ref[h]
        mixed = mix_ref[...]
        ya = u * mixed
        ra = lax.rsqrt(jnp.mean(ya * ya, axis=-1, keepdims=True) + RMS_EPS)
        yha = ya * ra
        dyan = dy_ref[:, 0:A]
        dgn_ref[:, 0:A] += jnp.sum(dyan * yha, axis=0, keepdims=True)
        qa = dyan * gn_ref[:, 0:A]
        dya = ra * (qa - yha * jnp.mean(qa * yha, axis=-1, keepdims=True))
        dz_ref[:, 0:A] = ((dya * mixed) * du_dz).astype(BF16)
        dmix_ref[...] = (dya * u).astype(BF16)

        ii = lax.broadcasted_iota(jnp.int32, (HEAD_DIM, HEAD_DIM), 0)
        jj = lax.broadcasted_iota(jnp.int32, (HEAD_DIM, HEAD_DIM), 1)
        mask = (jj // CHUNK <= ii // CHUNK).astype(F32)
        ones = jnp.ones((8, HEAD_DIM), BF16)
        for h in range(H):
            cols = slice(h * HEAD_DIM, (h + 1) * HEAD_DIM)
            dws = jnp.zeros((HEAD_DIM, HEAD_DIM), F32)
            dbs = jnp.zeros((8, HEAD_DIM), F32)
            for cb in range(nblk):
                rows = slice(cb * HEAD_DIM, (cb + 1) * HEAD_DIM)
                dm = dmix_ref[rows, cols]
                dws = dws + _nt(dm, vn_ref[rows, cols])
                dbs = dbs + _nt(ones, dm)
                dvn_ref[rows, cols] = jnp.dot(wmt_ref[h], dm, preferred_element_type=F32)
            dws_ref[h] += dws * mask
            dbs_ref[h] += dbs
        dvn = dvn_ref[...]
        dlnb_ref[...] += jnp.sum(dvn, axis=0, keepdims=True)
        dlng_ref[...] += jnp.sum(dvn * vhat, axis=0, keepdims=True)
        dvh = dvn * lng_ref[...]
        dvg = rstd * (dvh - jnp.mean(dvh, axis=-1, keepdims=True)
                      - vhat * jnp.mean(dvh * vhat, axis=-1, keepdims=True))
        dz_ref[:, A:2 * A] = (dvg * dv_dz).astype(BF16)

        def ext(lo):
            mid = z_ref[:, lo:lo + A].astype(F32)
            return jnp.concatenate([zp_ref[:, lo:lo + A].astype(F32), mid, zn_ref[:, lo:lo + A].astype(F32)], axis=0)

        zb, zc, zh = ext(2 * A), ext(3 * A), ext(4 * A)
        row = lax.broadcasted_iota(jnp.int32, (next_, 1), 0)
        p = zc * zh
        p = jnp.where(jnp.logical_and(row < HALO, i == 0), 0.0, p)
        cv = _conv(p, cw_ref)
        yb = zb * cv
        rb = lax.rsqrt(jnp.mean(yb * yb, axis=-1, keepdims=True) + RMS_EPS)
        yhb = yb * rb
        dyn_rows = jnp.where(i < ngrid - 1, dyn_ref[:, A:2 * A], 0.0)
        dybn = jnp.concatenate([jnp.zeros((HALO, A), F32), dy_ref[:, A:2 * A], dyn_rows], axis=0)
        ctr = slice(HALO, HALO + tr)
        dgn_ref[:, A:2 * A] += jnp.sum((dybn * yhb)[ctr], axis=0, keepdims=True)
        qb = dybn * gn_ref[:, A:2 * A]
        dyb = rb * (qb - yhb * jnp.mean(qb * yhb, axis=-1, keepdims=True))
        dcv = dyb * zb
        dp = (cw_ref[2:3, :] * dcv + cw_ref[1:2, :] * pltpu.roll(dcv, next_ - 1, 0)
              + cw_ref[0:1, :] * pltpu.roll(dcv, next_ - 2, 0))
        dz_ref[:, 2 * A:3 * A] = (dyb * cv)[ctr].astype(BF16)
        dz_ref[:, 3 * A:4 * A] = (dp * zh)[ctr].astype(BF16)
        dz_ref[:, 4 * A:5 * A] = (dp * zc)[ctr].astype(BF16)
        dcw_ref[2:3, :] += jnp.sum((dcv * p)[ctr], axis=0, keepdims=True)
        dcw_ref[1:2, :] += jnp.sum((dcv * pltpu.roll(p, 1, 0))[ctr], axis=0, keepdims=True)
        dcw_ref[0:1, :] += jnp.sum((dcv * pltpu.roll(p, 2, 0))[ctr], axis=0, keepdims=True)

    prev, cur, nxt = _mixer_specs(T, A, tr)
    nb = tr // HALO
    dy_cur = pl.BlockSpec((tr, 2 * A), lambda i: (i, 0))
    dy_nxt = pl.BlockSpec((HALO, 2 * A), lambda i: (jnp.minimum((i + 1) * nb, T // HALO - 1), 0))
    full = lambda shape: pl.BlockSpec(shape, lambda i: (0,) * len(shape))
    hh = (H, HEAD_DIM, HEAD_DIM)
    return _call(
        body, [z, z, z, dy, dy, lng, lnb, wm, wmt, bfull, cw, gn], after,
        out_shape=(jax.ShapeDtypeStruct((T, nin), BF16), jax.ShapeDtypeStruct((1, 2 * A), F32),
                   jax.ShapeDtypeStruct((1, A), F32), jax.ShapeDtypeStruct((1, A), F32),
                   jax.ShapeDtypeStruct(hh, F32), jax.ShapeDtypeStruct((H, 8, HEAD_DIM), F32),
                   jax.ShapeDtypeStruct((8, A), F32)),
        grid=(ngrid,),
        in_specs=[prev, cur, nxt, dy_cur, dy_nxt, full((1, A)), full((1, A)), full(hh), full(hh), full(hh),
                  full((8, A)), full((1, 2 * A))],
        out_specs=(pl.BlockSpec((tr, nin), lambda i: (i, 0)), full((1, 2 * A)), full((1, A)), full((1, A)),
                   full(hh), full((H, 8, HEAD_DIM)), full((8, A))),
        scratch_shapes=[pltpu.VMEM((tr, A), BF16), pltpu.VMEM((tr, A), F32), pltpu.VMEM((tr, A), BF16),
                        pltpu.VMEM((tr, A), F32)],
        name=name, compiler_params=_params("arbitrary"))


class _Geom:
    def __init__(self, rows, cols, axis, size, base=0):
        self.rows, self.cols, self.axis, self.size, self.base = rows, cols, axis, size, base

    def in_full(self, j, h):
        if self.axis == 1:
            return (h * (self.rows // 2), self.rows // 2), (self.base + j * self.size, self.size)
        return (self.base + j * self.size, self.size), (h * (self.cols // 2), self.cols // 2)

    def in_shard(self, h):
        if self.axis == 1:
            return (h * (self.rows // 2), self.rows // 2), (0, self.size)
        return (0, self.size), (h * (self.cols // 2), self.cols // 2)

    def half_of_full(self, h):
        if self.axis == 1:
            return (h * (self.rows // 2), self.rows // 2), (0, self.cols)
        return (0, self.rows), (h * (self.cols // 2), self.cols // 2)

    def in_half(self, j):
        if self.axis == 1:
            return (0, self.rows // 2), (self.base + j * self.size, self.size)
        return (self.base + j * self.size, self.size), (0, self.cols // 2)

    @property
    def half_shape(self):
        return (self.rows // 2, self.cols) if self.axis == 1 else (self.rows, self.cols // 2)

    @property
    def shard_half_shape(self):
        return (self.rows // 2, self.size) if self.axis == 1 else (self.size, self.cols // 2)

    @property
    def shard_shape(self):
        return (self.rows, self.size) if self.axis == 1 else (self.size, self.cols)


def _at(ref, region):
    (r0, rn), (c0, cn) = region
    if not isinstance(r0, int):
        r0 = pl.multiple_of(r0, 16)
    if not isinstance(c0, int):
        c0 = pl.multiple_of(c0, 128)
    return ref.at[pl.ds(r0, rn), pl.ds(c0, cn)]


def _whole(shape):
    return (0, shape[-2]), (0, shape[-1])


def _split_rows(region, itemsize):
    (r0, rn), cols = region
    want = max(1, (rn * cols[1] * itemsize) // DMA_CHUNK_BYTES)
    n = 1
    for cand in range(1, want + 1):
        if rn % cand == 0 and (rn // cand) % 16 == 0:
            n = cand
    step = rn // n
    return [((r0 + i * step, step), cols) for i in range(n)]


class _Chunked:
    def __init__(self, make, src, src_reg, dst, dst_reg):
        self.make, self.src, self.src_reg, self.dst, self.dst_reg = make, src, src_reg, dst, dst_reg

    @property
    def whole(self):
        return self.make(_at(self.src, self.src_reg), _at(self.dst, self.dst_reg))

    def start(self):
        itemsize = jnp.dtype(self.src.dtype).itemsize
        for a, b in zip(_split_rows(self.src_reg, itemsize), _split_rows(self.dst_reg, itemsize)):
            self.make(_at(self.src, a), _at(self.dst, b)).start()


def _mesh_place():
    x, y, c = lax.axis_index("x"), lax.axis_index("y"), lax.axis_index("c")
    chips = [(1 - x, y), (x, 1 - y), (1 - x, 1 - y)]
    return x, y, c, 2 * x + y, chips


def _remote(ssem, rsem, dev):
    return lambda src, dst: pltpu.make_async_remote_copy(
        src_ref=src, dst_ref=dst, send_sem=ssem, recv_sem=rsem, device_id=dev, device_id_type=MESH)


def _comm_call(name, body, arrays, sems_in=(), after=(), sems_out=(), new=()):
    na, ns, nf, no, nn = len(arrays), len(sems_in), len(after), len(sems_out), len(new)

    def kern(*refs):
        sin = refs[na:na + ns]
        outs = refs[na + ns + nf:]
        body(outs[no:no + na], outs[no + na:no + na + nn], sin, outs[:no])
        outs[-1][...] = jnp.zeros_like(outs[-1])

    out_shape = (tuple(pltpu.SemaphoreType.DMA((n,)) for n in sems_out)
                 + tuple(pltpu.HBM(a.shape, a.dtype) for a in arrays)
                 + tuple(pltpu.HBM(s, d) for s, d in new)
                 + (jax.ShapeDtypeStruct((8, 128), F32),))
    res = pl.pallas_call(
        kern, out_shape=out_shape, in_specs=[HBM] * na + [SEM] * ns + [ANY] * nf,
        out_specs=(SEM,) * no + (HBM,) * (na + nn) + (pl.BlockSpec(memory_space=pltpu.VMEM),),
        input_output_aliases={i: no + i for i in range(na)}, name=name,
        compiler_params=pltpu.CompilerParams(has_side_effects=EFFECT),
    )(*[pltpu.with_memory_space_constraint(a, pltpu.HBM) for a in arrays], *sems_in, *after)
    return list(res[:no]), list(res[no:no + na]), list(res[no + na:no + na + nn]), res[-1]


def _gather_start(name, arr, members, after):
    def body(arrs, news, sin, sout):
        x, y, c, j, chips = _mesh_place()
        for m, geo in enumerate(members):
            reg = geo.in_full(j, c)
            for k, chip in enumerate(chips):
                _Chunked(_remote(sout[2 * m].at[k], sout[2 * m + 1].at[k], (*chip, c)),
                         arrs[0], reg, arrs[0], reg).start()

    sems, arrs, _, tok = _comm_call(name, body, [arr], after=after, sems_out=[N_CHIPS - 1] * (2 * len(members)))
    return [sems[2 * m:2 * m + 2] for m in range(len(members))], arrs[0], tok


def _gather_forward(name, arr, members, sems, after):
    def body(arrs, news, sin, sout):
        x, y, c, j, chips = _mesh_place()
        sibling = (x, y, 1 - c)
        full = arrs[0]
        for m, geo in enumerate(members):
            mine = geo.in_full(j, c)
            for k, chip in enumerate(chips):
                got = geo.in_full(2 * chip[0] + chip[1], c)
                sent = _Chunked(_remote(sin[2 * m].at[k], sin[2 * m + 1].at[k], sibling), full, mine, full, got)
                sent.whole.wait_send()
                sent.whole.wait_recv()
                _Chunked(_remote(sout[2 * m].at[k], sout[2 * m + 1].at[k], sibling), full, got, full, got).start()

    flat = [s for pair in sems for s in pair]
    fs, arrs, _, tok = _comm_call(name, body, [arr], sems_in=flat, after=after,
                                  sems_out=[N_CHIPS - 1] * (2 * len(members)))
    return fs, arrs[0], tok


def _gather_wait(name, arr, members, fsems, after):
    def body(arrs, news, sin, sout):
        x, y, c, j, chips = _mesh_place()
        sibling = (x, y, 1 - c)
        full = arrs[0]
        for m, geo in enumerate(members):
            for k, chip in enumerate(chips):
                jk = 2 * chip[0] + chip[1]
                cp = _Chunked(_remote(sin[2 * m].at[k], sin[2 * m + 1].at[k], sibling),
                              full, geo.in_full(jk, c), full, geo.in_full(jk, 1 - c))
                cp.whole.wait_send()
                cp.whole.wait_recv()

    _, arrs, _, _ = _comm_call(name, body, [arr], sems_in=fsems, after=after)
    return arrs[0]


def _gather_conv(conv):
    L, _, cb = conv.shape
    nk = N_CHIPS - 1

    def body(s_ref, f_ref, send_sems, recv_sems, local_sem):
        x, y, c, j, chips = _mesh_place()
        at = lambda jj: f_ref.at[:, :, pl.ds(pl.multiple_of(jj * cb, 128), cb)]
        lc = pltpu.make_async_copy(s_ref, at(j), local_sem.at[0])
        lc.start()
        cps = []
        for k, chip in enumerate(chips):
            cp = _remote(send_sems.at[k], recv_sems.at[k], (*chip, c))(s_ref, at(j))
            cp.start()
            cps.append(cp)
        for k, chip in enumerate(chips):
            jk = 2 * chip[0] + chip[1]
            cps[k].wait_send()
            _remote(send_sems.at[k], recv_sems.at[k], (*chip, c))(at(jk), at(jk)).wait_recv()
        lc.wait()

    return pl.pallas_call(
        body, out_shape=jax.ShapeDtypeStruct((L, 8, N_CHIPS * cb), F32), in_specs=[ANY], out_specs=ANY,
        scratch_shapes=[pltpu.SemaphoreType.DMA((nk,)), pltpu.SemaphoreType.DMA((nk,)),
                        pltpu.SemaphoreType.DMA((1,))],
        name="gather_conv")(conv)


def _swap_start(name, dws, geos, after):
    n = len(dws)

    def body(arrs, news, sin, sout):
        x, y, c, _, _ = _mesh_place()
        for t in range(n):
            _Chunked(_remote(sout[0].at[t], sout[1].at[t], (x, y, 1 - c)),
                     arrs[t], geos[t].half_of_full(1 - c), news[t], _whole(news[t].shape)).start()

    return _comm_call(name, body, dws, after=after, sems_out=[n, n], new=[(g.half_shape, BF16) for g in geos])


def _swap_wait(name, dws, lands, sems, geos, after):
    n = len(dws)

    def body(arrs, news, sin, sout):
        x, y, c, _, _ = _mesh_place()
        for t in range(n):
            cp = _Chunked(_remote(sin[0].at[t], sin[1].at[t], (x, y, 1 - c)),
                          arrs[t], geos[t].half_of_full(1 - c), arrs[n + t], _whole(arrs[n + t].shape))
            cp.whole.wait_send()
            cp.whole.wait_recv()

    _, arrs, _, _ = _comm_call(name, body, list(dws) + list(lands), sems_in=sems, after=after)
    return arrs[:n], arrs[n:]


def _exchange_start(name, ps, entries, after):
    nk = N_CHIPS - 1

    def body(arrs, news, sin, sout):
        x, y, c, j, chips = _mesh_place()
        for e, (pi, geo) in enumerate(entries):
            for k, chip in enumerate(chips):
                dst = news[e].at[k]
                _Chunked(_remote(sout[0].at[nk * e + k], sout[1].at[nk * e + k], (*chip, c)),
                         arrs[pi], geo.in_half(2 * chip[0] + chip[1]), dst, _whole(dst.shape)).start()

    ne = len(entries)
    return _comm_call(name, body, ps, after=after, sems_out=[nk * ne, nk * ne],
                      new=[((nk,) + g.shard_half_shape, BF16) for _, g in entries])


def _exchange_wait(name, ps, lands, sems, entries, after):
    nk = N_CHIPS - 1
    n = len(ps)

    def body(arrs, news, sin, sout):
        x, y, c, j, chips = _mesh_place()
        for e, (pi, geo) in enumerate(entries):
            for k, chip in enumerate(chips):
                dst = arrs[n + e].at[k]
                cp = _Chunked(_remote(sin[0].at[nk * e + k], sin[1].at[nk * e + k], (*chip, c)),
                              arrs[pi], geo.in_half(2 * chip[0] + chip[1]), dst, _whole(dst.shape))
                cp.whole.wait_send()
                cp.whole.wait_recv()

    _, arrs, _, _ = _comm_call(name, body, list(ps) + list(lands), sems_in=sems, after=after)
    return arrs[:n], arrs[n:]


def _join_start(name, gs, geos, after):
    n = len(gs)

    def body(arrs, news, sin, sout):
        x, y, c, _, _ = _mesh_place()
        for t in range(n):
            mine = geos[t].in_shard(c)
            _Chunked(_remote(sout[0].at[t], sout[1].at[t], (x, y, 1 - c)), arrs[t], mine, arrs[t], mine).start()

    return _comm_call(name, body, gs, after=after, sems_out=[n, n])


def _join_wait(name, gs, sems, geos, after):
    n = len(gs)

    def body(arrs, news, sin, sout):
        x, y, c, _, _ = _mesh_place()
        for t in range(n):
            cp = _Chunked(_remote(sin[0].at[t], sin[1].at[t], (x, y, 1 - c)),
                          arrs[t], geos[t].in_shard(c), arrs[t], geos[t].in_shard(1 - c))
            cp.whole.wait_send()
            cp.whole.wait_recv()

    _, arrs, _, _ = _comm_call(name, body, gs, sems_in=sems, after=after)
    return arrs


def _cast_place(w, l, geo, pos, prev, name, after=()):
    if geo.axis == 1:
        tr = _rows_tile(geo.rows, geo.size, 1024 * 1024)
        grid = (geo.rows // tr,)
        blk = (tr, geo.size)
        q = geo.base // geo.size
        out_map = lambda i, p: (i, q + p[0])
    else:
        tr = _rows_tile(geo.size, geo.cols, 1024 * 1024)
        grid = (geo.size // tr,)
        blk = (tr, geo.cols)
        nb = geo.size // tr
        out_map = lambda i, p: (p[0] * nb + i, 0)

    def body(p_ref, w_ref, *rest):
        rest[-1][...] = w_ref[...].astype(BF16)

    in_specs = [pl.BlockSpec((None,) + blk, lambda i, p: (l, i, 0))]
    args = [w]
    aliases = None
    if prev is not None:
        in_specs.append(ANY)
        args.append(prev)
        aliases = {2: 0}
    return _prefetch_call(body, pos, args, after, out_shape=jax.ShapeDtypeStruct((geo.rows, geo.cols), BF16),
                          grid=grid, in_specs=in_specs, out_specs=pl.BlockSpec(blk, out_map), aliases=aliases,
                          name=name)


def _pair_sum(dw, recv, geo, pos, name):
    hs = geo.half_shape
    tr = _rows_tile(hs[0], hs[1], 2048 * 1024)
    nb = hs[0] // tr
    blk = (tr, hs[1])
    if geo.axis == 1:
        own_map = lambda i, p: (p[1] * nb + i, 0)
    else:
        own_map = lambda i, p: (i, p[1])

    def body(p_ref, a_ref, b_ref, o_ref):
        o_ref[...] = (a_ref[...].astype(F32) + b_ref[...].astype(F32)).astype(BF16)

    same = pl.BlockSpec(blk, lambda i, p: (i, 0))
    return _prefetch_call(body, pos, [dw, recv], out_shape=jax.ShapeDtypeStruct(hs, BF16), grid=(nb,),
                          in_specs=[pl.BlockSpec(blk, own_map), same], out_specs=same, name=name)


def _chip_sum(p, recv, geo, pos, name):
    ss = geo.shard_half_shape
    tr = _rows_tile(ss[0], ss[1], 1024 * 1024)
    nb = ss[0] // tr
    blk = (tr, ss[1])
    if geo.axis == 1:
        q = geo.base // geo.size
        own_map = lambda i, p_: (i, q + p_[0])
        out_map = lambda i, p_: (p_[1] * nb + i, 0)
    else:
        own_map = lambda i, p_: (p_[0] * nb + i, 0)
        out_map = lambda i, p_: (i, p_[1])

    def body(p_ref, o_ref, r_ref, out_ref):
        acc = o_ref[...].astype(F32)
        for k in range(N_CHIPS - 1):
            acc = acc + r_ref[k].astype(F32)
        out_ref[...] = acc

    return _prefetch_call(
        body, pos, [p, recv], out_shape=jax.ShapeDtypeStruct(geo.shard_shape, F32), grid=(nb,),
        in_specs=[pl.BlockSpec(blk, own_map), pl.BlockSpec((N_CHIPS - 1,) + blk, lambda i, p_: (0, i, 0))],
        out_specs=pl.BlockSpec(blk, out_map), name=name)


def _adamw_math(w, g, m, v):
    m = ADAM_B1 * m + (1.0 - ADAM_B1) * g
    v = ADAM_B2 * v + (1.0 - ADAM_B2) * (g * g)
    m_hat = m / (1.0 - ADAM_B1 ** ADAM_STEP)
    v_hat = v / (1.0 - ADAM_B2 ** ADAM_STEP)
    delta = -ADAM_LR * (m_hat / (jnp.sqrt(v_hat) + ADAM_EPS) + ADAM_WD * w)
    return delta, m, v


def _adamw_layer(l, w, g, m, v, prev, name, after=()):
    L, R, C = w.shape
    tr = _rows_tile(R, C, 512 * 1024)

    def body(w_ref, g_ref, m_ref, v_ref, *rest):
        go_ref, d_ref, mo_ref, vo_ref = rest[-4:]
        gv = g_ref[...]
        d, mn, vn = _adamw_math(w_ref[...], gv, m_ref[...], v_ref[...])
        go_ref[...] = gv
        d_ref[...] = d
        mo_ref[...] = mn
        vo_ref[...] = vn

    blk = pl.BlockSpec((None, tr, C), lambda i: (l, i, 0))
    sds = jax.ShapeDtypeStruct(w.shape, F32)
    in_specs = [blk, pl.BlockSpec((tr, C), lambda i: (i, 0)), blk, blk]
    args = [w, g, m, v]
    aliases = {}
    if prev is not None:
        in_specs += [ANY] * 4
        args += list(prev)
        aliases = {4 + i: i for i in range(4)}
    return _call(
        body, args, after, out_shape=(sds, sds, sds, sds), grid=(R // tr,), in_specs=in_specs,
        out_specs=(blk,) * 4, input_output_aliases=aliases, name=name, compiler_params=_params("parallel"))


def _adamw_sparsecore(l, w, g, m, v, name):
    L, R, C = w.shape
    bc = max(t for t in range(128, min(C, 640) + 1, 128) if C % t == 0)
    br = 8
    while br * 2 * bc <= SC_BLOCK_ELEMS and R % (br * 2) == 0:
        br *= 2
    nrb = R // br
    flat = lambda a: a.reshape(L * R, C)
    sds = jax.ShapeDtypeStruct((L * R, C), F32)

    def kern(w_hbm, g_hbm, m_hbm, v_hbm, go_hbm, d_hbm, mo_hbm, vo_hbm):
        def block(w_v, g_v, m_v, v_v, go_v, d_v, mo_v, vo_v):
            @pl.loop(0, br)
            def _(r):
                @pl.loop(0, bc, step=SC_LANES)
                def _(c):
                    at = (pl.ds(r, 1), pl.ds(c, SC_LANES))
                    gv = g_v.at[*at][...]
                    d, mn, vn = _adamw_math(w_v.at[*at][...], gv, m_v.at[*at][...], v_v.at[*at][...])
                    go_v.at[*at][...] = gv
                    d_v.at[*at][...] = d
                    mo_v.at[*at][...] = mn
                    vo_v.at[*at][...] = vn

        lay = pl.BlockSpec((br, bc), lambda i, j: (l * nrb + i, j))
        one = pl.BlockSpec((br, bc), lambda i, j: (i, j))
        pltpu.emit_pipeline(
            block, grid=(nrb, C // bc), in_specs=[lay, one, lay, lay], out_specs=[lay] * 4,
            core_axis_name=("sc_core", "sc_tile"), dimension_semantics=(pltpu.PARALLEL, pltpu.PARALLEL),
        )(w_hbm, g_hbm, m_hbm, v_hbm, go_hbm, d_hbm, mo_hbm, vo_hbm)

    outs = pl.kernel(
        kern, out_type=(sds,) * 4, name=name, scratch_types=[],
        mesh=plsc.VectorSubcoreMesh(core_axis_name="sc_core", subcore_axis_name="sc_tile"),
    )(flat(w), g, flat(m), flat(v))
    return tuple(o.reshape(L, R, C) for o in outs)


def _adamw_flat(w, g, m, v, name):
    R, C = w.shape
    tr = _tile(R, 1024) if R % 128 == 0 else R

    def body(w_ref, g_ref, m_ref, v_ref, d_ref, mo_ref, vo_ref):
        d, mn, vn = _adamw_math(w_ref[...], g_ref[...], m_ref[...], v_ref[...])
        d_ref[...] = d
        mo_ref[...] = mn
        vo_ref[...] = vn

    blk = pl.BlockSpec((tr, C), lambda i: (i, 0))
    sds = jax.ShapeDtypeStruct(w.shape, F32)
    return pl.pallas_call(
        body, out_shape=(sds, sds, sds), grid=(R // tr,), in_specs=[blk] * 4, out_specs=(blk,) * 3,
        name=name, compiler_params=_params("parallel"))(w, g, m, v)


def _allreduce_small(s, after=()):
    R, C = s.shape

    def body(s_ref, o_ref, rbuf, send_sems, recv_sems):
        x, y, c = lax.axis_index("x"), lax.axis_index("y"), lax.axis_index("c")
        peers = [(x, y, 1 - c), (1 - x, y, c), (x, 1 - y, c)]
        o_ref[...] = s_ref[...]
        for k, peer in enumerate(peers):
            cp = _remote(send_sems.at[k], recv_sems.at[k], peer)(o_ref, rbuf.at[k])
            cp.start()
            cp.wait()
            o_ref[...] = o_ref[...] + rbuf[k]

    vm = pl.BlockSpec(memory_space=pltpu.VMEM)
    return _call(
        body, [s], after, out_shape=jax.ShapeDtypeStruct((R, C), F32), in_specs=[vm], out_specs=vm,
        scratch_shapes=[pltpu.VMEM((3, R, C), F32), pltpu.SemaphoreType.DMA((3,)), pltpu.SemaphoreType.DMA((3,))],
        name="allreduce_small", compiler_params=pltpu.CompilerParams(vmem_limit_bytes=V7X_VMEM_LIMIT))


class _GradBatch:
    def __init__(self, tag, dws, geos4, entries, pos):
        self.tag, self.dws, self.geos4, self.entries, self.pos = tag, dws, geos4, entries, pos

    def start_swap(self, after):
        self.s1, self.dws, self.land1, tok = _swap_start(f"{self.tag}_swap_start", self.dws, self.geos4, after)
        return tok

    def swap_to_exchange(self, after):
        dws, lands = _swap_wait(f"{self.tag}_swap_wait", self.dws, self.land1, self.s1, self.geos4, after)
        ps = [_pair_sum(d, r, g, self.pos, f"{self.tag}_pair_sum_{i}")
              for i, (d, r, g) in enumerate(zip(dws, lands, self.geos4))]
        self.s2, self.ps, self.land2, tok = _exchange_start(f"{self.tag}_exch_start", ps, self.entries, ())
        return tok

    def exchange_to_join(self, after):
        ps, lands = _exchange_wait(f"{self.tag}_exch_wait", self.ps, self.land2, self.s2, self.entries, after)
        self.geos5 = [g for _, g in self.entries]
        gs = [_chip_sum(ps[pi], r, g, self.pos, f"{self.tag}_chip_sum_{e}")
              for e, ((pi, g), r) in enumerate(zip(self.entries, lands))]
        self.s3, self.gs, _, tok = _join_start(f"{self.tag}_join_start", gs, self.geos5, ())
        return tok

    def finish(self, after):
        return _join_wait(f"{self.tag}_join_wait", self.gs, self.s3, self.geos5, after)


def _pack(pieces):
    rows = []
    for p in pieces:
        flat = p.reshape(-1)
        pad = (-flat.shape[0]) % 1024
        rows.append(jnp.pad(flat, (0, pad)).reshape(-1, 128))
    return jnp.concatenate(rows, axis=0)


def _unpack(buf, shapes):
    out, r = [], 0
    for shp in shapes:
        n = math.prod(shp)
        nr = -(-n // 1024) * 8
        out.append(buf[r:r + nr].reshape(-1)[:n].reshape(shp))
        r += nr
    return out


def kernel(x, norm1_g, w_in, gmlp_ln_g, gmlp_ln_b, w_spatial, b_spatial, conv_w, group_norm_g, w_out, norm2_g, w_gate, w_up, w_down, final_norm_g, loss_target, m_norm1_g, m_w_in, m_gmlp_ln_g, m_gmlp_ln_b, m_w_spatial, m_b_spatial, m_conv_w, m_group_norm_g, m_w_out, m_norm2_g, m_w_gate, m_w_up, m_w_down, m_final_norm_g, v_norm1_g, v_w_in, v_gmlp_ln_g, v_gmlp_ln_b, v_w_spatial, v_b_spatial, v_conv_w, v_group_norm_g, v_w_out, v_norm2_g, v_w_gate, v_w_up, v_w_down, v_final_norm_g):
    L, D, n_in = w_in.shape
    T = x.shape[1]
    nin = N_CHIPS * n_in
    A = nin // 5
    H = A // HEAD_DIM
    n_f = w_gate.shape[2]
    F = N_CHIPS * n_f
    n_o = w_out.shape[1]
    cb = conv_w.shape[2]
    assert A == H * HEAD_DIM and T % 256 == 0 and N_CHIPS * n_o == D and N_CHIPS * cb == A

    g_in, g_out, g_down = _Geom(D, nin, 1, n_in), _Geom(D, D, 0, n_o), _Geom(F, D, 0, n_f)
    g_gate, g_up, g_gu = _Geom(D, 2 * F, 1, n_f, 0), _Geom(D, 2 * F, 1, n_f, F), _Geom(D, 2 * F, 1, n_f)
    pos = jnp.stack([2 * lax.axis_index("x") + lax.axis_index("y"), lax.axis_index("c")]).astype(jnp.int32)
    row = lambda v: v.reshape(1, -1)

    conv_full = _gather_conv(jnp.pad(conv_w, ((0, 0), (0, 8 - CONV_TAPS), (0, 0))))
    members = [[g_in], [g_out], [g_gate, g_up], [g_down]]
    sources = [[w_in], [w_out], [w_gate, w_up], [w_down]]
    placed, sems_of = {}, {}
    tok = conv_full
    for l in range(L):
        for a in range(4):
            arr = None
            for m, (w, geo) in enumerate(zip(sources[a], members[a])):
                arr = _cast_place(w, l, geo, pos, arr, f"l{l}_place_{a}_{m}", after=(tok,))
            sems_of[l, a], placed[l, a], tok = _gather_start(f"l{l}_gather_start_{a}", arr, members[a], (tok,))

    def arrive(l, a, after):
        return _gather_forward(f"l{l}_gather_fwd_{a}", placed[l, a], members[a], sems_of[l, a], after)

    def landed(l, a, fs, arr, after):
        return _gather_wait(f"l{l}_gather_wait_{a}", arr, members[a], fs, after)

    frame = jnp.arange(HEAD_DIM)
    mask = (frame[None, :] // CHUNK) <= (frame[:, None] // CHUNK)

    xs = x[0]
    acts = []
    for l in range(L):
        wm = jnp.where(mask[None], w_spatial[l], 0.0).astype(BF16)
        wmt = jnp.swapaxes(wm, 1, 2)
        bfull = jnp.broadcast_to(b_spatial[l][:, :, None], (H, HEAD_DIM, HEAD_DIM))
        sm = dict(lng=row(gmlp_ln_g[l]), lnb=row(gmlp_ln_b[l]), wm=wm, wmt=wmt, bfull=bfull,
                  cw=conv_full[l], gn=row(group_norm_g[l]))
        if l == 0:
            h = _rms_fwd(xs, row(norm1_g[l]), f"l{l}_rms1", after=(tok,))
            fs, arr, tok = arrive(l, 0, (h,))
            W_in = landed(l, 0, fs, arr, (tok,))
        else:
            fs, arr, tok = arrive(l, 0, (xs,))
            h = _rms_fwd(xs, row(norm1_g[l]), f"l{l}_rms1", after=(tok,))
            W_in = landed(l, 0, fs, arr, (h,))
        z = _mm_nn(h, W_in, out_dtype=BF16, name=f"l{l}_mm_in")
        fs, arr, tok = arrive(l, 1, (z,))
        y = _mixer_fwd(z, sm["lng"], sm["lnb"], wm, bfull, sm["cw"], sm["gn"], f"l{l}_mixer", after=(tok,))
        W_out = landed(l, 1, fs, arr, (y,))
        if l == 0:
            x1, h2 = _mm_res_rms(y, W_out, xs, row(norm2_g[l]), name=f"l{l}_mm_out")
            fs, arr, tok = arrive(l, 2, (x1,))
            W_gu = landed(l, 2, fs, arr, (tok,))
        else:
            fs, arr, tok = arrive(l, 2, (W_out,))
            x1, h2 = _mm_res_rms(y, W_out, xs, row(norm2_g[l]), name=f"l{l}_mm_out", after=(tok,))
            W_gu = landed(l, 2, fs, arr, (x1,))
        gu, act = _mm_swiglu(h2, W_gu, name=f"l{l}_mm_swiglu")
        fs, arr, tok = arrive(l, 3, (act,))
        W_down = landed(l, 3, fs, arr, (tok,))
        x2 = _mm_nn(act, W_down, res=x1, out_dtype=F32, tm=512, name=f"l{l}_mm_down")
        acts.append(dict(x=xs, h=h, z=z, y=y, x1=x1, h2=h2, gu=gu, act=act, sm=sm,
                         W_in=W_in, W_out=W_out, W_gu=W_gu, W_down=W_down))
        xs = x2

    loss_vec, dx, dxb, dgf = _loss_head(xs, row(final_norm_g), loss_target[0], "loss_head")
    loss = lax.psum(loss_vec[0, 0], ("x", "y", "c"))

    big_w = {"in": (w_in, m_w_in, v_w_in), "out": (w_out, m_w_out, v_w_out), "gate": (w_gate, m_w_gate, v_w_gate),
             "up": (w_up, m_w_up, v_w_up), "down": (w_down, m_w_down, v_w_down)}
    big = {nm: None for nm in big_w}

    def adamw(l, names, gs):
        after = ()
        for nm, g in zip(names, gs):
            w, m, v = big_w[nm]
            if l == L - 1 and L > 1:
                big[nm] = _adamw_sparsecore(l, w, g, m, v, f"l{l}_adamw_{nm}")
            else:
                big[nm] = _adamw_layer(l, w, g, m, v, big[nm], f"l{l}_adamw_{nm}", after=after)
                after = (big[nm][1],)
        return after

    small_grads = [None] * L
    pend_dg = pend_oi = None
    tok = ()
    for l in reversed(range(L)):
        a = acts[l]
        sm = a["sm"]
        dgu = _mm_nt_swiglu_bwd(dxb, a["W_down"], a["gu"], name=f"l{l}_bwd_down", after=tok)
        if pend_dg:
            tok = (pend_dg[0].exchange_to_join((dgu,)),)
        dW_down = _mm_tn(a["act"], dxb, name=f"l{l}_dw_down", after=tok)
        if pend_dg:
            tok = adamw(pend_dg[1], ["down", "gate", "up"], pend_dg[0].finish((dW_down,)))
        dh2 = _mm_nt_dgu(dgu, a["W_gu"], name=f"l{l}_bwd_gu", after=tok)
        if pend_oi:
            tok = (pend_oi[0].exchange_to_join((dh2,)),)
        dW_gu = _mm_tn(a["h2"], dgu, tkw=1024, tnw=512, name=f"l{l}_dw_gu", after=tok)
        if pend_oi:
            tok = adamw(pend_oi[1], ["out", "in"], pend_oi[0].finish((dW_gu,)))
        dg_batch = _GradBatch(f"l{l}_dg", [dW_down, dW_gu], [g_down, g_gu], [(0, g_down), (1, g_gate), (1, g_up)], pos)
        tok = (dg_batch.start_swap(tok),)
        dx1, dx1b, dg2 = _rms_bwd(dh2, a["x1"], row(norm2_g[l]), dx, f"l{l}_rms2_bwd", after=tok)
        dy = _mm_nt(dx1b, a["W_out"], out_dtype=F32, name=f"l{l}_bwd_out")
        dW_out = _mm_tn(a["y"], dx1b, name=f"l{l}_dw_out")
        tok = (dg_batch.swap_to_exchange((dW_out, dy)),)
        dz, dgn, dlng, dlnb, dws, dbs, dcw = _mixer_bwd(
            a["z"], dy, sm["lng"], sm["lnb"], sm["wm"], sm["wmt"], sm["bfull"], sm["cw"], sm["gn"],
            f"l{l}_mixer_bwd", after=tok)
        dW_in = _mm_tn(a["h"], dz, name=f"l{l}_dw_in")
        oi_batch = _GradBatch(f"l{l}_oi", [dW_out, dW_in], [g_out, g_in], [(0, g_out), (1, g_in)], pos)
        tok = (oi_batch.start_swap(()),)
        dh = _mm_nt(dz, a["W_in"], out_dtype=BF16, tm=512, name=f"l{l}_bwd_in", after=tok)
        dx, dxb, dg1 = _rms_bwd(dh, a["x"], row(norm1_g[l]), dx1, f"l{l}_rms1_bwd")
        tok = (oi_batch.swap_to_exchange((dx,)),)
        small_grads[l] = [dg1[0], dlng[0], dlnb[0], dws, dbs[:, 0, :], dcw[:CONV_TAPS], dgn[0], dg2[0]]
        pend_dg, pend_oi = (dg_batch, l), (oi_batch, l)
    grad_x = dx[None]

    tok = (pend_dg[0].exchange_to_join(tok),)
    tok = adamw(pend_dg[1], ["down", "gate", "up"], pend_dg[0].finish(tok))
    tok = (pend_oi[0].exchange_to_join(tok),)
    pieces = [p for l in range(L) for p in small_grads[l]] + [dgf[0]]
    red = _allreduce_small(_pack(pieces), after=tok)
    adamw(pend_oi[1], ["out", "in"], pend_oi[0].finish((red,)))

    red_list = _unpack(red, [p.shape for p in pieces])
    per = len(small_grads[0])
    stack = lambda i: jnp.stack([red_list[l * per + i] for l in range(L)])
    g_small = {"norm1_g": stack(0), "gmlp_ln_g": stack(1), "gmlp_ln_b": stack(2), "w_spatial": stack(3),
               "b_spatial": stack(4), "group_norm_g": stack(6), "norm2_g": stack(7),
               "final_norm_g": red_list[L * per]}
    g_small["conv_w"] = lax.dynamic_slice_in_dim(stack(5), pos[0] * cb, cb, axis=2)
    small_w = {"norm1_g": (norm1_g, m_norm1_g, v_norm1_g), "gmlp_ln_g": (gmlp_ln_g, m_gmlp_ln_g, v_gmlp_ln_g),
               "gmlp_ln_b": (gmlp_ln_b, m_gmlp_ln_b, v_gmlp_ln_b), "w_spatial": (w_spatial, m_w_spatial, v_w_spatial),
               "b_spatial": (b_spatial, m_b_spatial, v_b_spatial), "conv_w": (conv_w, m_conv_w, v_conv_w),
               "group_norm_g": (group_norm_g, m_group_norm_g, v_group_norm_g),
               "norm2_g": (norm2_g, m_norm2_g, v_norm2_g), "final_norm_g": (final_norm_g, m_final_norm_g, v_final_norm_g)}
    snames = list(small_w)
    sd, smn, svn = _adamw_flat(_pack([small_w[n][0] for n in snames]), _pack([g_small[n] for n in snames]),
                               _pack([small_w[n][1] for n in snames]), _pack([small_w[n][2] for n in snames]),
                               "adamw_small")
    sshapes = [small_w[n][0].shape for n in snames]
    sd, smn, svn = _unpack(sd, sshapes), _unpack(smn, sshapes), _unpack(svn, sshapes)
    small_out = {n: (g_small[n], sd[i], smn[i], svn[i]) for i, n in enumerate(snames)}

    order = ["norm1_g", "w_in", "gmlp_ln_g", "gmlp_ln_b", "w_spatial", "b_spatial", "conv_w", "group_norm_g",
             "w_out", "norm2_g", "w_gate", "w_up", "w_down", "final_norm_g"]
    res = {n: (big[n[2:]] if n[2:] in big else small_out[n]) for n in order}
    return (loss, grad_x, *[res[n][0] for n in order], *[res[n][1] for n in order],
            *[res[n][2] for n in order], *[res[n][3] for n in order])
```

```python
import math

import jax
import jax.numpy as jnp
from jax import lax
from jax.experimental import pallas as pl
from jax.experimental.pallas import tpu as pltpu
from jax.experimental.pallas import tpu_sc as plsc

RMS_EPS = 1e-6
LN_EPS = 1e-5
HEAD_DIM = 128
CHUNK = 64
CONV_TAPS = 3
HALO = 16
ADAM_LR = 0.001
ADAM_B1 = 0.9
ADAM_B2 = 0.999
ADAM_EPS = 1e-08
ADAM_WD = 0.01
ADAM_STEP = 10
V7X_VMEM_LIMIT = 56 * 1024 * 1024
N_CHIPS = 4
DMA_CHUNK_BYTES = 2 * 1024 * 1024
EPILOGUE_ROWS = 256
SC_LANES = 16
SC_BLOCK_ELEMS = 5120
MESH = pl.DeviceIdType.MESH
F32 = jnp.float32
BF16 = jnp.bfloat16
ANY = pl.BlockSpec(memory_space=pl.ANY)
HBM = pl.BlockSpec(memory_space=pltpu.HBM)
SEM = pl.BlockSpec(memory_space=pltpu.SEMAPHORE)
EFFECT = pltpu.SideEffectType.DATAFLOW_SIDE_EFFECTING


def _tile(n, pref):
    if n <= pref:
        return n
    best = None
    for t in range(128, pref + 1, 128):
        if n % t == 0:
            best = t
    assert best is not None, (n, pref)
    return best


def _rows_tile(rows, cols, budget_elems):
    tr = rows
    while tr * cols > budget_elems and tr % 2 == 0 and (tr // 2) % 16 == 0:
        tr //= 2
    return tr


def _params(*sem):
    return pltpu.CompilerParams(dimension_semantics=sem if sem else None,
                                vmem_limit_bytes=V7X_VMEM_LIMIT)


def _call(body, args, after, **kw):
    n, na = len(args), len(after)
    if na:
        inner = body

        def body(*refs):
            inner(*refs[:n], *refs[n + na:])

        kw["in_specs"] = list(kw["in_specs"]) + [ANY] * na
    return pl.pallas_call(body, **kw)(*args, *after)


def _prefetch_call(body, pos, args, after=(), *, out_shape, grid, in_specs, out_specs, aliases=None, name):
    n, na = 1 + len(args), len(after)
    if na:
        inner = body

        def body(*refs):
            inner(*refs[:n], *refs[n + na:])

    return pl.pallas_call(
        body, out_shape=out_shape,
        grid_spec=pltpu.PrefetchScalarGridSpec(num_scalar_prefetch=1, grid=grid,
                                               in_specs=list(in_specs) + [ANY] * na, out_specs=out_specs),
        input_output_aliases=aliases or {}, name=name,
        compiler_params=_params(*(["parallel"] * len(grid))))(pos, *args, *after)


def _gelu(x):
    c = math.sqrt(2.0 / math.pi)
    return 0.5 * x * (1.0 + jnp.tanh(c * (x + 0.044715 * x * x * x)))


def _gelu_and_grad(x):
    c = math.sqrt(2.0 / math.pi)
    x2 = x * x
    th = jnp.tanh(c * (x + 0.044715 * x * x2))
    val = 0.5 * x * (1.0 + th)
    grad = 0.5 * (1.0 + th) + 0.5 * x * (1.0 - th * th) * (c * (1.0 + 3.0 * 0.044715 * x2))
    return val, grad


def _sigmoid(x):
    return 1.0 / (1.0 + jnp.exp(-x))


def _nt(a, b):
    return lax.dot_general(a, b, (((1,), (1,)), ((), ())), preferred_element_type=F32)


def _tn(a, b):
    return lax.dot_general(a, b, (((0,), (0,)), ((), ())), preferred_element_type=F32)


def _rms_fwd(x, g, name, after=()):
    T, D = x.shape
    tr = _tile(T, 512)

    def body(x_ref, g_ref, h_ref):
        xv = x_ref[...]
        r = lax.rsqrt(jnp.mean(xv * xv, axis=-1, keepdims=True) + RMS_EPS)
        h_ref[...] = ((xv * r) * g_ref[...]).astype(h_ref.dtype)

    return _call(
        body, [x, g], after, out_shape=jax.ShapeDtypeStruct((T, D), BF16), grid=(T // tr,),
        in_specs=[pl.BlockSpec((tr, D), lambda i: (i, 0)), pl.BlockSpec((1, D), lambda i: (0, 0))],
        out_specs=pl.BlockSpec((tr, D), lambda i: (i, 0)),
        name=name, compiler_params=_params("parallel"))


def _rms_bwd(dh, x, g, dres, name, after=()):
    T, D = x.shape
    tr = _tile(T, 256)

    def body(dh_ref, x_ref, g_ref, dres_ref, dx_ref, dxb_ref, dg_ref):
        i = pl.program_id(0)
        xv = x_ref[...]
        dhv = dh_ref[...].astype(F32)
        r = lax.rsqrt(jnp.mean(xv * xv, axis=-1, keepdims=True) + RMS_EPS)
        xh = xv * r
        q = dhv * g_ref[...]
        dx = dres_ref[...] + r * (q - xh * jnp.mean(q * xh, axis=-1, keepdims=True))
        dx_ref[...] = dx
        dxb_ref[...] = dx.astype(BF16)
        part = jnp.sum(dhv * xh, axis=0, keepdims=True)

        @pl.when(i == 0)
        def _():
            dg_ref[...] = part

        @pl.when(i > 0)
        def _():
            dg_ref[...] += part

    row = pl.BlockSpec((tr, D), lambda i: (i, 0))
    vec = pl.BlockSpec((1, D), lambda i: (0, 0))
    return _call(
        body, [dh, x, g, dres], after,
        out_shape=(jax.ShapeDtypeStruct((T, D), F32), jax.ShapeDtypeStruct((T, D), BF16),
                   jax.ShapeDtypeStruct((1, D), F32)),
        grid=(T // tr,), in_specs=[row, row, vec, row], out_specs=(row, row, vec),
        name=name, compiler_params=_params("arbitrary"))


def _loss_head(x, g, tgt, name):
    T, D = x.shape
    tr = _tile(T, 256)

    def body(x_ref, g_ref, t_ref, loss_ref, dx_ref, dxb_ref, dg_ref):
        i = pl.program_id(0)
        xv = x_ref[...]
        gv = g_ref[...]
        r = lax.rsqrt(jnp.mean(xv * xv, axis=-1, keepdims=True) + RMS_EPS)
        xh = xv * r
        err = xh * gv - t_ref[...]
        lpart = jnp.full((1, 128), 0.5 * jnp.sum(jnp.mean(err * err, axis=-1, keepdims=True)), F32)
        dy = err * (1.0 / D)
        q = dy * gv
        dx = r * (q - xh * jnp.mean(q * xh, axis=-1, keepdims=True))
        dx_ref[...] = dx
        dxb_ref[...] = dx.astype(BF16)
        gpart = jnp.sum(dy * xh, axis=0, keepdims=True)

        @pl.when(i == 0)
        def _():
            loss_ref[...] = lpart
            dg_ref[...] = gpart

        @pl.when(i > 0)
        def _():
            loss_ref[...] += lpart
            dg_ref[...] += gpart

    row = pl.BlockSpec((tr, D), lambda i: (i, 0))
    vec = pl.BlockSpec((1, D), lambda i: (0, 0))
    return pl.pallas_call(
        body,
        out_shape=(jax.ShapeDtypeStruct((1, 128), F32), jax.ShapeDtypeStruct((T, D), F32),
                   jax.ShapeDtypeStruct((T, D), BF16), jax.ShapeDtypeStruct((1, D), F32)),
        grid=(T // tr,), in_specs=[row, vec, row],
        out_specs=(pl.BlockSpec((1, 128), lambda i: (0, 0)), row, row, vec),
        name=name, compiler_params=_params("arbitrary"))(x, g, tgt)


def _mm_nn(a, w, *, res=None, out_dtype, tm=1024, tn=1024, tk=None, name, after=()):
    M, K = a.shape
    N = w.shape[1]
    tm, tn = _tile(M, tm), _tile(N, tn)
    tk = K if tk is None else _tile(K, tk)
    nk = K // tk
    has_res = res is not None

    def body(*refs):
        a_ref, w_ref = refs[0], refs[1]
        r_ref = refs[2] if has_res else None
        o_ref = refs[2 + has_res]
        part = jnp.dot(a_ref[...], w_ref[...], preferred_element_type=F32)

        def finish(acc):
            if has_res:
                acc = r_ref[...] + acc
            o_ref[...] = acc.astype(o_ref.dtype)

        if nk == 1:
            finish(part)
        else:
            acc_ref = refs[3 + has_res]
            k = pl.program_id(2)

            @pl.when(k == 0)
            def _():
                acc_ref[...] = part

            @pl.when(jnp.logical_and(k > 0, k < nk - 1))
            def _():
                acc_ref[...] += part

            @pl.when(k == nk - 1)
            def _():
                finish(acc_ref[...] + part)

    in_specs = [pl.BlockSpec((tm, tk), lambda i, j, k: (i, k)),
                pl.BlockSpec((tk, tn), lambda i, j, k: (k, j))]
    args = [a, w]
    if has_res:
        in_specs.append(pl.BlockSpec((tm, tn), lambda i, j, k: (i, j)))
        args.append(res)
    return _call(
        body, args, after, out_shape=jax.ShapeDtypeStruct((M, N), out_dtype), grid=(M // tm, N // tn, nk),
        in_specs=in_specs, out_specs=pl.BlockSpec((tm, tn), lambda i, j, k: (i, j)),
        scratch_shapes=[pltpu.VMEM((tm, tn), F32)] if nk > 1 else [],
        name=name, compiler_params=_params("parallel", "parallel", "arbitrary"))


def _mm_res_rms(a, w, res, g, *, tm=512, name, after=()):
    M, K = a.shape
    N = w.shape[1]
    tm = _tile(M, tm)

    def body(a_ref, w_ref, r_ref, g_ref, x_ref, h_ref):
        xv = r_ref[...] + jnp.dot(a_ref[...], w_ref[...], preferred_element_type=F32)
        x_ref[...] = xv
        r = lax.rsqrt(jnp.mean(xv * xv, axis=-1, keepdims=True) + RMS_EPS)
        h_ref[...] = ((xv * r) * g_ref[...]).astype(BF16)

    row = lambda n: pl.BlockSpec((tm, n), lambda i: (i, 0))
    return _call(
        body, [a, w, res, g], after,
        out_shape=(jax.ShapeDtypeStruct((M, N), F32), jax.ShapeDtypeStruct((M, N), BF16)), grid=(M // tm,),
        in_specs=[row(K), pl.BlockSpec((K, N), lambda i: (0, 0)), row(N), pl.BlockSpec((1, N), lambda i: (0, 0))],
        out_specs=(row(N), row(N)), name=name, compiler_params=_params("parallel"))


def _mm_swiglu(h, wgu, *, tm=2048, tn=512, name):
    T, D = h.shape
    F = wgu.shape[1] // 2
    tm, tn = _tile(T, tm), _tile(F, tn)
    nf = F // tn

    rc = _tile(tm, EPILOGUE_ROWS)

    def body(h_ref, wg_ref, wu_ref, fac_ref, act_ref):
        for r in range(tm // rc):
            rows = slice(r * rc, (r + 1) * rc)
            hv = h_ref[rows, :]
            g = jnp.dot(hv, wg_ref[...], preferred_element_type=F32)
            u = jnp.dot(hv, wu_ref[...], preferred_element_type=F32)
            s = _sigmoid(g)
            t = g * s
            act_ref[rows, :] = (t * u).astype(BF16)
            fac_ref[0, rows, :] = (u * (s + t * (1.0 - s))).astype(BF16)
            fac_ref[1, rows, :] = t.astype(BF16)

    return pl.pallas_call(
        body,
        out_shape=(jax.ShapeDtypeStruct((2, T, F), BF16), jax.ShapeDtypeStruct((T, F), BF16)),
        grid=(T // tm, nf),
        in_specs=[pl.BlockSpec((tm, D), lambda i, j: (i, 0)),
                  pl.BlockSpec((D, tn), lambda i, j: (0, j)),
                  pl.BlockSpec((D, tn), lambda i, j: (0, j + nf))],
        out_specs=(pl.BlockSpec((2, tm, tn), lambda i, j: (0, i, j)),
                   pl.BlockSpec((tm, tn), lambda i, j: (i, j))),
        name=name, compiler_params=_params("parallel", "parallel"))(h, wgu, wgu)


def _mm_nt(a, w, *, out_dtype, tm=1024, tn=1024, tk=None, name, after=()):
    M, K = a.shape
    N = w.shape[0]
    tm, tn = _tile(M, tm), _tile(N, tn)
    tk = K if tk is None else _tile(K, tk)
    nk = K // tk

    def body(*refs):
        a_ref, w_ref, o_ref = refs[0], refs[1], refs[2]
        part = _nt(a_ref[...], w_ref[...])
        if nk == 1:
            o_ref[...] = part.astype(o_ref.dtype)
        else:
            acc_ref = refs[3]
            k = pl.program_id(2)

            @pl.when(k == 0)
            def _():
                acc_ref[...] = part

            @pl.when(jnp.logical_and(k > 0, k < nk - 1))
            def _():
                acc_ref[...] += part

            @pl.when(k == nk - 1)
            def _():
                o_ref[...] = (acc_ref[...] + part).astype(o_ref.dtype)

    return _call(
        body, [a, w], after, out_shape=jax.ShapeDtypeStruct((M, N), out_dtype), grid=(M // tm, N // tn, nk),
        in_specs=[pl.BlockSpec((tm, tk), lambda i, j, k: (i, k)),
                  pl.BlockSpec((tn, tk), lambda i, j, k: (j, k))],
        out_specs=pl.BlockSpec((tm, tn), lambda i, j, k: (i, j)),
        scratch_shapes=[pltpu.VMEM((tm, tn), F32)] if nk > 1 else [],
        name=name, compiler_params=_params("parallel", "parallel", "arbitrary"))


def _mm_nt_swiglu_bwd(dxb, wdown, gu, *, tm=2048, tn=512, name, after=()):
    T, D = dxb.shape
    F = wdown.shape[0]
    tm, tn = _tile(T, tm), _tile(F, tn)

    def body(dx_ref, w_ref, fac_ref, dgu_ref):
        da = _nt(dx_ref[...], w_ref[...])
        dgu_ref[0] = (da * fac_ref[0].astype(F32)).astype(BF16)
        dgu_ref[1] = (da * fac_ref[1].astype(F32)).astype(BF16)

    blk3 = pl.BlockSpec((2, tm, tn), lambda i, j: (0, i, j))
    return _call(
        body, [dxb, wdown, gu], after, out_shape=jax.ShapeDtypeStruct((2, T, F), BF16), grid=(T // tm, F // tn),
        in_specs=[pl.BlockSpec((tm, D), lambda i, j: (i, 0)),
                  pl.BlockSpec((tn, D), lambda i, j: (j, 0)), blk3],
        out_specs=blk3, name=name, compiler_params=_params("parallel", "parallel"))


def _mm_nt_dgu(dgu, wgu, *, tm=1024, tn=1024, tk=2816, name, after=()):
    _, T, F = dgu.shape
    D = wgu.shape[0]
    tm, tn, tk = _tile(T, tm), _tile(D, tn), _tile(F, tk)
    nkf = F // tk
    nk = 2 * nkf

    def body(a_ref, w_ref, o_ref, acc_ref):
        k = pl.program_id(2)
        part = _nt(a_ref[...], w_ref[...])

        @pl.when(k == 0)
        def _():
            acc_ref[...] = part

        @pl.when(jnp.logical_and(k > 0, k < nk - 1))
        def _():
            acc_ref[...] += part

        @pl.when(k == nk - 1)
        def _():
            o_ref[...] = (acc_ref[...] + part).astype(o_ref.dtype)

    return _call(
        body, [dgu, wgu], after, out_shape=jax.ShapeDtypeStruct((T, D), BF16), grid=(T // tm, D // tn, nk),
        in_specs=[pl.BlockSpec((None, tm, tk), lambda i, j, k: (k // nkf, i, k % nkf)),
                  pl.BlockSpec((tn, tk), lambda i, j, k: (j, k))],
        out_specs=pl.BlockSpec((tm, tn), lambda i, j, k: (i, j)),
        scratch_shapes=[pltpu.VMEM((tm, tn), F32)],
        name=name, compiler_params=_params("parallel", "parallel", "arbitrary"))


def _mm_tn(a, g, *, tkw=512, tnw=1024, name, after=()):
    T, Kw = a.shape
    pair = g.ndim == 3
    Nw = 2 * g.shape[2] if pair else g.shape[1]
    tkw = _tile(Kw, tkw)
    tnw = _tile(g.shape[2] if pair else Nw, tnw)
    nf = (Nw // 2) // tnw if pair else 0

    def body(a_ref, g_ref, o_ref):
        o_ref[...] = _tn(a_ref[...], g_ref[...]).astype(o_ref.dtype)

    if pair:
        g_spec = pl.BlockSpec((None, T, tnw), lambda i, j: (j // nf, 0, j % nf))
    else:
        g_spec = pl.BlockSpec((T, tnw), lambda i, j: (0, j))
    return _call(
        body, [a, g], after, out_shape=jax.ShapeDtypeStruct((Kw, Nw), BF16), grid=(Kw // tkw, Nw // tnw),
        in_specs=[pl.BlockSpec((T, tkw), lambda i, j: (0, i)), g_spec],
        out_specs=pl.BlockSpec((tkw, tnw), lambda i, j: (i, j)),
        name=name, compiler_params=_params("parallel", "parallel"))


def _mixer_specs(T, A, tr):
    nin = 5 * A
    nb = tr // HALO
    last = T // HALO - 1
    prev = pl.BlockSpec((HALO, nin), lambda i: (jnp.maximum(i * nb - 1, 0), 0))
    cur = pl.BlockSpec((tr, nin), lambda i: (i, 0))
    nxt = pl.BlockSpec((HALO, nin), lambda i: (jnp.minimum((i + 1) * nb, last), 0))
    return prev, cur, nxt


def _conv(p_ext, cw_ref):
    return (cw_ref[2:3, :] * p_ext + cw_ref[1:2, :] * pltpu.roll(p_ext, 1, 0)
            + cw_ref[0:1, :] * pltpu.roll(p_ext, 2, 0))


def _mixer_fwd(z, lng, lnb, wm, bfull, cw, gn, name, after=()):
    T, nin = z.shape
    A = nin // 5
    H = A // HEAD_DIM
    tr = _tile(T, 256)
    nblk = tr // HEAD_DIM

    def body(zp_ref, z_ref, lng_ref, lnb_ref, wm_ref, bf_ref, cw_ref, gn_ref, y_ref, vn_ref, mix_ref):
        i = pl.program_id(0)
        u = _gelu(z_ref[:, 0:A].astype(F32))
        vg = _gelu(z_ref[:, A:2 * A].astype(F32))
        xc = vg - jnp.mean(vg, axis=-1, keepdims=True)
        rstd = lax.rsqrt(jnp.mean(xc * xc, axis=-1, keepdims=True) + LN_EPS)
        vn_ref[...] = ((xc * rstd) * lng_ref[...] + lnb_ref[...]).astype(BF16)
        for cb in range(nblk):
            rows = slice(cb * HEAD_DIM, (cb + 1) * HEAD_DIM)
            for h in range(H):
                cols = slice(h * HEAD_DIM, (h + 1) * HEAD_DIM)
                mix_ref[rows, cols] = jnp.dot(wm_ref[h], vn_ref[rows, cols],
                                              preferred_element_type=F32) + bf_ref[h]
        ya = u * mix_ref[...]
        ra = lax.rsqrt(jnp.mean(ya * ya, axis=-1, keepdims=True) + RMS_EPS)
        y_ref[:, 0:A] = ((ya * ra) * gn_ref[:, 0:A]).astype(BF16)

        p_prev = zp_ref[:, 3 * A:4 * A].astype(F32) * zp_ref[:, 4 * A:5 * A].astype(F32)
        p_prev = jnp.where(i > 0, p_prev, 0.0)
        p_cur = z_ref[:, 3 * A:4 * A].astype(F32) * z_ref[:, 4 * A:5 * A].astype(F32)
        cv = _conv(jnp.concatenate([p_prev, p_cur], axis=0), cw_ref)[HALO:]
        yb = z_ref[:, 2 * A:3 * A].astype(F32) * cv
        rb = lax.rsqrt(jnp.mean(yb * yb, axis=-1, keepdims=True) + RMS_EPS)
        y_ref[:, A:2 * A] = ((yb * rb) * gn_ref[:, A:2 * A]).astype(BF16)

    prev, cur, _ = _mixer_specs(T, A, tr)
    full = lambda shape: pl.BlockSpec(shape, lambda i: (0,) * len(shape))
    return _call(
        body, [z, z, lng, lnb, wm, bfull, cw, gn], after,
        out_shape=jax.ShapeDtypeStruct((T, 2 * A), BF16), grid=(T // tr,),
        in_specs=[prev, cur, full((1, A)), full((1, A)), full((H, HEAD_DIM, HEAD_DIM)),
                  full((H, HEAD_DIM, HEAD_DIM)), full((8, A)), full((1, 2 * A))],
        out_specs=pl.BlockSpec((tr, 2 * A), lambda i: (i, 0)),
        scratch_shapes=[pltpu.VMEM((tr, A), BF16), pltpu.VMEM((tr, A), F32)],
        name=name, compiler_params=_params("parallel"))


def _mixer_bwd(z, dy, lng, lnb, wm, wmt, bfull, cw, gn, name, after=()):
    T, nin = z.shape
    A = nin // 5
    H = A // HEAD_DIM
    tr = _tile(T, 256)
    nblk = tr // HEAD_DIM
    ngrid = T // tr
    next_ = tr + 2 * HALO

    def body(zp_ref, z_ref, zn_ref, dy_ref, dyn_ref, lng_ref, lnb_ref, wm_ref, wmt_ref, bf_ref, cw_ref, gn_ref,
             dz_ref, dgn_ref, dlng_ref, dlnb_ref, dws_ref, dbs_ref, dcw_ref,
             vn_ref, mix_ref, dmix_ref, dvn_ref):
        i = pl.program_id(0)

        @pl.when(i == 0)
        def _():
            dgn_ref[...] = jnp.zeros_like(dgn_ref)
            dlng_ref[...] = jnp.zeros_like(dlng_ref)
            dlnb_ref[...] = jnp.zeros_like(dlnb_ref)
            dws_ref[...] = jnp.zeros_like(dws_ref)
            dbs_ref[...] = jnp.zeros_like(dbs_ref)
            dcw_ref[...] = jnp.zeros_like(dcw_ref)

        u, du_dz = _gelu_and_grad(z_ref[:, 0:A].astype(F32))
        vg, dv_dz = _gelu_and_grad(z_ref[:, A:2 * A].astype(F32))
        xc = vg - jnp.mean(vg, axis=-1, keepdims=True)
        rstd = lax.rsqrt(jnp.mean(xc * xc, axis=-1, keepdims=True) + LN_EPS)
        vhat = xc * rstd
        vn_ref[...] = (vhat * lng_ref[...] + lnb_ref[...]).astype(BF16)
        for cb in range(nblk):
            rows = slice(cb * HEAD_DIM, (cb + 1) * HEAD_DIM)
            for h in range(H):
                cols = slice(h * HEAD_DIM, (h + 1) * HEAD_DIM)
                mix_ref[rows, cols] = jnp.dot(wm_ref[h], vn_ref[rows, cols],
                                              preferred_element_type=F32) + bf_ref[h]
        mixed = mix_ref[...]
        ya = u * mixed
        ra = lax.rsqrt(jnp.mean(ya * ya, axis=-1, keepdims=True) + RMS_EPS)
        yha = ya * ra
        dyan = dy_ref[:, 0:A]
        dgn_ref[:, 0:A] += jnp.sum(dyan * yha, axis=0, keepdims=True)
        qa = dyan * gn_ref[:, 0:A]
        dya = ra * (qa - yha * jnp.mean(qa * yha, axis=-1, keepdims=True))
        dz_ref[:, 0:A] = ((dya * mixed) * du_dz).astype(BF16)
        dmix_ref[...] = (dya * u).astype(BF16)

        ii = lax.broadcasted_iota(jnp.int32, (HEAD_DIM, HEAD_DIM), 0)
        jj = lax.broadcasted_iota(jnp.int32, (HEAD_DIM, HEAD_DIM), 1)
        mask = (jj // CHUNK <= ii // CHUNK).astype(F32)
        ones = jnp.ones((8, HEAD_DIM), BF16)
        for h in range(H):
            cols = slice(h * HEAD_DIM, (h + 1) * HEAD_DIM)
            dws = jnp.zeros((HEAD_DIM, HEAD_DIM), F32)
            dbs = jnp.zeros((8, HEAD_DIM), F32)
            for cb in range(nblk):
                rows = slice(cb * HEAD_DIM, (cb + 1) * HEAD_DIM)
                dm = dmix_ref[rows, cols]
                dws = dws + _nt(dm, vn_ref[rows, cols])
                dbs = dbs + _nt(ones, dm)
                dvn_ref[rows, cols] = jnp.dot(wmt_ref[h], dm, preferred_element_type=F32)
            dws_ref[h] += dws * mask
            dbs_ref[h] += dbs
        dvn = dvn_ref[...]
        dlnb_ref[...] += jnp.sum(dvn, axis=0, keepdims=True)
        dlng_ref[...] += jnp.sum(dvn * vhat, axis=0, keepdims=True)
        dvh = dvn * lng_ref[...]
        dvg = rstd * (dvh - jnp.mean(dvh, axis=-1, keepdims=True)
                      - vhat * jnp.mean(dvh * vhat, axis=-1, keepdims=True))
        dz_ref[:, A:2 * A] = (dvg * dv_dz).astype(BF16)

        def ext(lo):
            mid = z_ref[:, lo:lo + A].astype(F32)
            return jnp.concatenate([zp_ref[:, lo:lo + A].astype(F32), mid, zn_ref[:, lo:lo + A].astype(F32)], axis=0)

        zb, zc, zh = ext(2 * A), ext(3 * A), ext(4 * A)
        row = lax.broadcasted_iota(jnp.int32, (next_, 1), 0)
        p = zc * zh
        p = jnp.where(jnp.logical_and(row < HALO, i == 0), 0.0, p)
        cv = _conv(p, cw_ref)
        yb = zb * cv
        rb = lax.rsqrt(jnp.mean(yb * yb, axis=-1, keepdims=True) + RMS_EPS)
        yhb = yb * rb
        dyn_rows = jnp.where(i < ngrid - 1, dyn_ref[:, A:2 * A], 0.0)
        dybn = jnp.concatenate([jnp.zeros((HALO, A), F32), dy_ref[:, A:2 * A], dyn_rows], axis=0)
        ctr = slice(HALO, HALO + tr)
        dgn_ref[:, A:2 * A] += jnp.sum((dybn * yhb)[ctr], axis=0, keepdims=True)
        qb = dybn * gn_ref[:, A:2 * A]
        dyb = rb * (qb - yhb * jnp.mean(qb * yhb, axis=-1, keepdims=True))
        dcv = dyb * zb
        dp = (cw_ref[2:3, :] * dcv + cw_ref[1:2, :] * pltpu.roll(dcv, next_ - 1, 0)
              + cw_ref[0:1, :] * pltpu.roll(dcv, next_ - 2, 0))
        dz_ref[:, 2 * A:3 * A] = (dyb * cv)[ctr].astype(BF16)
        dz_ref[:, 3 * A:4 * A] = (dp * zh)[ctr].astype(BF16)
        dz_ref[:, 4 * A:5 * A] = (dp * zc)[ctr].astype(BF16)
        dcw_ref[2:3, :] += jnp.sum((dcv * p)[ctr], axis=0, keepdims=True)
        dcw_ref[1:2, :] += jnp.sum((dcv * pltpu.roll(p, 1, 0))[ctr], axis=0, keepdims=True)
        dcw_ref[0:1, :] += jnp.sum((dcv * pltpu.roll(p, 2, 0))[ctr], axis=0, keepdims=True)

    prev, cur, nxt = _mixer_specs(T, A, tr)
    nb = tr // HALO
    dy_cur = pl.BlockSpec((tr, 2 * A), lambda i: (i, 0))
    dy_nxt = pl.BlockSpec((HALO, 2 * A), lambda i: (jnp.minimum((i + 1) * nb, T // HALO - 1), 0))
    full = lambda shape: pl.BlockSpec(shape, lambda i: (0,) * len(shape))
    hh = (H, HEAD_DIM, HEAD_DIM)
    return _call(
        body, [z, z, z, dy, dy, lng, lnb, wm, wmt, bfull, cw, gn], after,
        out_shape=(jax.ShapeDtypeStruct((T, nin), BF16), jax.ShapeDtypeStruct((1, 2 * A), F32),
                   jax.ShapeDtypeStruct((1, A), F32), jax.ShapeDtypeStruct((1, A), F32),
                   jax.ShapeDtypeStruct(hh, F32), jax.ShapeDtypeStruct((H, 8, HEAD_DIM), F32),
                   jax.ShapeDtypeStruct((8, A), F32)),
        grid=(ngrid,),
        in_specs=[prev, cur, nxt, dy_cur, dy_nxt, full((1, A)), full((1, A)), full(hh), full(hh), full(hh),
                  full((8, A)), full((1, 2 * A))],
        out_specs=(pl.BlockSpec((tr, nin), lambda i: (i, 0)), full((1, 2 * A)), full((1, A)), full((1, A)),
                   full(hh), full((H, 8, HEAD_DIM)), full((8, A))),
        scratch_shapes=[pltpu.VMEM((tr, A), BF16), pltpu.VMEM((tr, A), F32), pltpu.VMEM((tr, A), BF16),
                        pltpu.VMEM((tr, A), F32)],
        name=name, compiler_params=_params("arbitrary"))


class _Geom:
    def __init__(self, rows, cols, axis, size, base=0):
        self.rows, self.cols, self.axis, self.size, self.base = rows, cols, axis, size, base

    def in_full(self, j, h):
        if self.axis == 1:
            return (h * (self.rows // 2), self.rows // 2), (self.base + j * self.size, self.size)
        return (self.base + j * self.size, self.size), (h * (self.cols // 2), self.cols // 2)

    def in_shard(self, h):
        if self.axis == 1:
            return (h * (self.rows // 2), self.rows // 2), (0, self.size)
        return (0, self.size), (h * (self.cols // 2), self.cols // 2)

    def half_of_full(self, h):
        if self.axis == 1:
            return (h * (self.rows // 2), self.rows // 2), (0, self.cols)
        return (0, self.rows), (h * (self.cols // 2), self.cols // 2)

    def in_half(self, j):
        if self.axis == 1:
            return (0, self.rows // 2), (self.base + j * self.size, self.size)
        return (self.base + j * self.size, self.size), (0, self.cols // 2)

    @property
    def half_shape(self):
        return (self.rows // 2, self.cols) if self.axis == 1 else (self.rows, self.cols // 2)

    @property
    def shard_half_shape(self):
        return (self.rows // 2, self.size) if self.axis == 1 else (self.size, self.cols // 2)

    @property
    def shard_shape(self):
        return (self.rows, self.size) if self.axis == 1 else (self.size, self.cols)


def _at(ref, region):
    (r0, rn), (c0, cn) = region
    if not isinstance(r0, int):
        r0 = pl.multiple_of(r0, 16)
    if not isinstance(c0, int):
        c0 = pl.multiple_of(c0, 128)
    return ref.at[pl.ds(r0, rn), pl.ds(c0, cn)]


def _whole(shape):
    return (0, shape[-2]), (0, shape[-1])


def _split_rows(region, itemsize):
    (r0, rn), cols = region
    want = max(1, (rn * cols[1] * itemsize) // DMA_CHUNK_BYTES)
    n = 1
    for cand in range(1, want + 1):
        if rn % cand == 0 and (rn // cand) % 16 == 0:
            n = cand
    step = rn // n
    return [((r0 + i * step, step), cols) for i in range(n)]


class _Chunked:
    def __init__(self, make, src, src_reg, dst, dst_reg):
        self.make, self.src, self.src_reg, self.dst, self.dst_reg = make, src, src_reg, dst, dst_reg

    @property
    def whole(self):
        return self.make(_at(self.src, self.src_reg), _at(self.dst, self.dst_reg))

    def start(self):
        itemsize = jnp.dtype(self.src.dtype).itemsize
        for a, b in zip(_split_rows(self.src_reg, itemsize), _split_rows(self.dst_reg, itemsize)):
            self.make(_at(self.src, a), _at(self.dst, b)).start()


def _mesh_place():
    x, y, c = lax.axis_index("x"), lax.axis_index("y"), lax.axis_index("c")
    chips = [(1 - x, y), (x, 1 - y), (1 - x, 1 - y)]
    return x, y, c, 2 * x + y, chips


def _remote(ssem, rsem, dev):
    return lambda src, dst: pltpu.make_async_remote_copy(
        src_ref=src, dst_ref=dst, send_sem=ssem, recv_sem=rsem, device_id=dev, device_id_type=MESH)


def _comm_call(name, body, arrays, sems_in=(), after=(), sems_out=(), new=()):
    na, ns, nf, no, nn = len(arrays), len(sems_in), len(after), len(sems_out), len(new)

    def kern(*refs):
        sin = refs[na:na + ns]
        outs = refs[na + ns + nf:]
        body(outs[no:no + na], outs[no + na:no + na + nn], sin, outs[:no])
        outs[-1][...] = jnp.zeros_like(outs[-1])

    out_shape = (tuple(pltpu.SemaphoreType.DMA((n,)) for n in sems_out)
                 + tuple(pltpu.HBM(a.shape, a.dtype) for a in arrays)
                 + tuple(pltpu.HBM(s, d) for s, d in new)
                 + (jax.ShapeDtypeStruct((8, 128), F32),))
    res = pl.pallas_call(
        kern, out_shape=out_shape, in_specs=[HBM] * na + [SEM] * ns + [ANY] * nf,
        out_specs=(SEM,) * no + (HBM,) * (na + nn) + (pl.BlockSpec(memory_space=pltpu.VMEM),),
        input_output_aliases={i: no + i for i in range(na)}, name=name,
        compiler_params=pltpu.CompilerParams(has_side_effects=EFFECT),
    )(*[pltpu.with_memory_space_constraint(a, pltpu.HBM) for a in arrays], *sems_in, *after)
    return list(res[:no]), list(res[no:no + na]), list(res[no + na:no + na + nn]), res[-1]


def _gather_start(name, arr, members, after):
    def body(arrs, news, sin, sout):
        x, y, c, j, chips = _mesh_place()
        for m, geo in enumerate(members):
            reg = geo.in_full(j, c)
            for k, chip in enumerate(chips):
                _Chunked(_remote(sout[2 * m].at[k], sout[2 * m + 1].at[k], (*chip, c)),
                         arrs[0], reg, arrs[0], reg).start()

    sems, arrs, _, tok = _comm_call(name, body, [arr], after=after, sems_out=[N_CHIPS - 1] * (2 * len(members)))
    return [sems[2 * m:2 * m + 2] for m in range(len(members))], arrs[0], tok


def _gather_forward(name, arr, members, sems, after):
    def body(arrs, news, sin, sout):
        x, y, c, j, chips = _mesh_place()
        sibling = (x, y, 1 - c)
        full = arrs[0]
        for m, geo in enumerate(members):
            mine = geo.in_full(j, c)
            for k, chip in enumerate(chips):
                got = geo.in_full(2 * chip[0] + chip[1], c)
                sent = _Chunked(_remote(sin[2 * m].at[k], sin[2 * m + 1].at[k], sibling), full, mine, full, got)
                sent.whole.wait_send()
                sent.whole.wait_recv()
                _Chunked(_remote(sout[2 * m].at[k], sout[2 * m + 1].at[k], sibling), full, got, full, got).start()

    flat = [s for pair in sems for s in pair]
    fs, arrs, _, tok = _comm_call(name, body, [arr], sems_in=flat, after=after,
                                  sems_out=[N_CHIPS - 1] * (2 * len(members)))
    return fs, arrs[0], tok


def _gather_wait(name, arr, members, fsems, after):
    def body(arrs, news, sin, sout):
        x, y, c, j, chips = _mesh_place()
        sibling = (x, y, 1 - c)
        full = arrs[0]
        for m, geo in enumerate(members):
            for k, chip in enumerate(chips):
                jk = 2 * chip[0] + chip[1]
                cp = _Chunked(_remote(sin[2 * m].at[k], sin[2 * m + 1].at[k], sibling),
                              full, geo.in_full(jk, c), full, geo.in_full(jk, 1 - c))
                cp.whole.wait_send()
                cp.whole.wait_recv()

    _, arrs, _, _ = _comm_call(name, body, [arr], sems_in=fsems, after=after)
    return arrs[0]


def _gather_conv(conv):
    L, _, cb = conv.shape
    nk = N_CHIPS - 1

    def body(s_ref, f_ref, send_sems, recv_sems, local_sem):
        x, y, c, j, chips = _mesh_place()
        at = lambda jj: f_ref.at[:, :, pl.ds(pl.multiple_of(jj * cb, 128), cb)]
        lc = pltpu.make_async_copy(s_ref, at(j), local_sem.at[0])
        lc.start()
        cps = []
        for k, chip in enumerate(chips):
            cp = _remote(send_sems.at[k], recv_sems.at[k], (*chip, c))(s_ref, at(j))
            cp.start()
            cps.append(cp)
        for k, chip in enumerate(chips):
            jk = 2 * chip[0] + chip[1]
            cps[k].wait_send()
            _remote(send_sems.at[k], recv_sems.at[k], (*chip, c))(at(jk), at(jk)).wait_recv()
        lc.wait()

    return pl.pallas_call(
        body, out_shape=jax.ShapeDtypeStruct((L, 8, N_CHIPS * cb), F32), in_specs=[ANY], out_specs=ANY,
        scratch_shapes=[pltpu.SemaphoreType.DMA((nk,)), pltpu.SemaphoreType.DMA((nk,)),
                        pltpu.SemaphoreType.DMA((1,))],
        name="gather_conv")(conv)


def _swap_start(name, dws, geos, after):
    n = len(dws)

    def body(arrs, news, sin, sout):
        x, y, c, _, _ = _mesh_place()
        for t in range(n):
            _Chunked(_remote(sout[0].at[t], sout[1].at[t], (x, y, 1 - c)),
                     arrs[t], geos[t].half_of_full(1 - c), news[t], _whole(news[t].shape)).start()

    return _comm_call(name, body, dws, after=after, sems_out=[n, n], new=[(g.half_shape, BF16) for g in geos])


def _swap_wait(name, dws, lands, sems, geos, after):
    n = len(dws)

    def body(arrs, news, sin, sout):
        x, y, c, _, _ = _mesh_place()
        for t in range(n):
            cp = _Chunked(_remote(sin[0].at[t], sin[1].at[t], (x, y, 1 - c)),
                          arrs[t], geos[t].half_of_full(1 - c), arrs[n + t], _whole(arrs[n + t].shape))
            cp.whole.wait_send()
            cp.whole.wait_recv()

    _, arrs, _, _ = _comm_call(name, body, list(dws) + list(lands), sems_in=sems, after=after)
    return arrs[:n], arrs[n:]


def _exchange_start(name, ps, entries, after):
    nk = N_CHIPS - 1

    def body(arrs, news, sin, sout):
        x, y, c, j, chips = _mesh_place()
        for e, (pi, geo) in enumerate(entries):
            for k, chip in enumerate(chips):
                dst = news[e].at[k]
                _Chunked(_remote(sout[0].at[nk * e + k], sout[1].at[nk * e + k], (*chip, c)),
                         arrs[pi], geo.in_half(2 * chip[0] + chip[1]), dst, _whole(dst.shape)).start()

    ne = len(entries)
    return _comm_call(name, body, ps, after=after, sems_out=[nk * ne, nk * ne],
                      new=[((nk,) + g.shard_half_shape, BF16) for _, g in entries])


def _exchange_wait(name, ps, lands, sems, entries, after):
    nk = N_CHIPS - 1
    n = len(ps)

    def body(arrs, news, sin, sout):
        x, y, c, j, chips = _mesh_place()
        for e, (pi, geo) in enumerate(entries):
            for k, chip in enumerate(chips):
                dst = arrs[n + e].at[k]
                cp = _Chunked(_remote(sin[0].at[nk * e + k], sin[1].at[nk * e + k], (*chip, c)),
                              arrs[pi], geo.in_half(2 * chip[0] + chip[1]), dst, _whole(dst.shape))
                cp.whole.wait_send()
                cp.whole.wait_recv()

    _, arrs, _, _ = _comm_call(name, body, list(ps) + list(lands), sems_in=sems, after=after)
    return arrs[:n], arrs[n:]


def _join_start(name, gs, geos, after):
    n = len(gs)

    def body(arrs, news, sin, sout):
        x, y, c, _, _ = _mesh_place()
        for t in range(n):
            mine = geos[t].in_shard(c)
            _Chunked(_remote(sout[0].at[t], sout[1].at[t], (x, y, 1 - c)), arrs[t], mine, arrs[t], mine).start()

    return _comm_call(name, body, gs, after=after, sems_out=[n, n])


def _join_wait(name, gs, sems, geos, after):
    n = len(gs)

    def body(arrs, news, sin, sout):
        x, y, c, _, _ = _mesh_place()
        for t in range(n):
            cp = _Chunked(_remote(sin[0].at[t], sin[1].at[t], (x, y, 1 - c)),
                          arrs[t], geos[t].in_shard(c), arrs[t], geos[t].in_shard(1 - c))
            cp.whole.wait_send()
            cp.whole.wait_recv()

    _, arrs, _, _ = _comm_call(name, body, gs, sems_in=sems, after=after)
    return arrs


def _cast_place(w, l, geo, pos, prev, name, after=()):
    if geo.axis == 1:
        tr = _rows_tile(geo.rows, geo.size, 1024 * 1024)
        grid = (geo.rows // tr,)
        blk = (tr, geo.size)
        q = geo.base // geo.size
        out_map = lambda i, p: (i, q + p[0])
    else:
        tr = _rows_tile(geo.size, geo.cols, 1024 * 1024)
        grid = (geo.size // tr,)
        blk = (tr, geo.cols)
        nb = geo.size // tr
        out_map = lambda i, p: (p[0] * nb + i, 0)

    def body(p_ref, w_ref, *rest):
        rest[-1][...] = w_ref[...].astype(BF16)

    in_specs = [pl.BlockSpec((None,) + blk, lambda i, p: (l, i, 0))]
    args = [w]
    aliases = None
    if prev is not None:
        in_specs.append(ANY)
        args.append(prev)
        aliases = {2: 0}
    return _prefetch_call(body, pos, args, after, out_shape=jax.ShapeDtypeStruct((geo.rows, geo.cols), BF16),
                          grid=grid, in_specs=in_specs, out_specs=pl.BlockSpec(blk, out_map), aliases=aliases,
                          name=name)


def _pair_sum(dw, recv, geo, pos, name):
    hs = geo.half_shape
    tr = _rows_tile(hs[0], hs[1], 2048 * 1024)
    nb = hs[0] // tr
    blk = (tr, hs[1])
    if geo.axis == 1:
        own_map = lambda i, p: (p[1] * nb + i, 0)
    else:
        own_map = lambda i, p: (i, p[1])

    def body(p_ref, a_ref, b_ref, o_ref):
        o_ref[...] = (a_ref[...].astype(F32) + b_ref[...].astype(F32)).astype(BF16)

    same = pl.BlockSpec(blk, lambda i, p: (i, 0))
    return _prefetch_call(body, pos, [dw, recv], out_shape=jax.ShapeDtypeStruct(hs, BF16), grid=(nb,),
                          in_specs=[pl.BlockSpec(blk, own_map), same], out_specs=same, name=name)


def _chip_sum(p, recv, geo, pos, name):
    ss = geo.shard_half_shape
    tr = _rows_tile(ss[0], ss[1], 1024 * 1024)
    nb = ss[0] // tr
    blk = (tr, ss[1])
    if geo.axis == 1:
        q = geo.base // geo.size
        own_map = lambda i, p_: (i, q + p_[0])
        out_map = lambda i, p_: (p_[1] * nb + i, 0)
    else:
        own_map = lambda i, p_: (p_[0] * nb + i, 0)
        out_map = lambda i, p_: (i, p_[1])

    def body(p_ref, o_ref, r_ref, out_ref):
        acc = o_ref[...].astype(F32)
        for k in range(N_CHIPS - 1):
            acc = acc + r_ref[k].astype(F32)
        out_ref[...] = acc

    return _prefetch_call(
        body, pos, [p, recv], out_shape=jax.ShapeDtypeStruct(geo.shard_shape, F32), grid=(nb,),
        in_specs=[pl.BlockSpec(blk, own_map), pl.BlockSpec((N_CHIPS - 1,) + blk, lambda i, p_: (0, i, 0))],
        out_specs=pl.BlockSpec(blk, out_map), name=name)


def _adamw_math(w, g, m, v):
    m = ADAM_B1 * m + (1.0 - ADAM_B1) * g
    v = ADAM_B2 * v + (1.0 - ADAM_B2) * (g * g)
    m_hat = m / (1.0 - ADAM_B1 ** ADAM_STEP)
    v_hat = v / (1.0 - ADAM_B2 ** ADAM_STEP)
    delta = -ADAM_LR * (m_hat / (jnp.sqrt(v_hat) + ADAM_EPS) + ADAM_WD * w)
    return delta, m, v


def _adamw_layer(l, w, g, m, v, prev, name, after=()):
    L, R, C = w.shape
    tr = _rows_tile(R, C, 512 * 1024)

    def body(w_ref, g_ref, m_ref, v_ref, *rest):
        go_ref, d_ref, mo_ref, vo_ref = rest[-4:]
        gv = g_ref[...]
        d, mn, vn = _adamw_math(w_ref[...], gv, m_ref[...], v_ref[...])
        go_ref[...] = gv
        d_ref[...] = d
        mo_ref[...] = mn
        vo_ref[...] = vn

    blk = pl.BlockSpec((None, tr, C), lambda i: (l, i, 0))
    sds = jax.ShapeDtypeStruct(w.shape, F32)
    in_specs = [blk, pl.BlockSpec((tr, C), lambda i: (i, 0)), blk, blk]
    args = [w, g, m, v]
    aliases = {}
    if prev is not None:
        in_specs += [ANY] * 4
        args += list(prev)
        aliases = {4 + i: i for i in range(4)}
    return _call(
        body, args, after, out_shape=(sds, sds, sds, sds), grid=(R // tr,), in_specs=in_specs,
        out_specs=(blk,) * 4, input_output_aliases=aliases, name=name, compiler_params=_params("parallel"))


def _adamw_sparsecore(l, w, g, m, v, name):
    L, R, C = w.shape
    bc = max(t for t in range(128, min(C, 640) + 1, 128) if C % t == 0)
    br = 8
    while br * 2 * bc <= SC_BLOCK_ELEMS and R % (br * 2) == 0:
        br *= 2
    nrb = R // br
    flat = lambda a: a.reshape(L * R, C)
    sds = jax.ShapeDtypeStruct((L * R, C), F32)

    def kern(w_hbm, g_hbm, m_hbm, v_hbm, go_hbm, d_hbm, mo_hbm, vo_hbm):
        def block(w_v, g_v, m_v, v_v, go_v, d_v, mo_v, vo_v):
            @pl.loop(0, br)
            def _(r):
                @pl.loop(0, bc, step=SC_LANES)
                def _(c):
                    at = (pl.ds(r, 1), pl.ds(c, SC_LANES))
                    gv = g_v.at[*at][...]
                    d, mn, vn = _adamw_math(w_v.at[*at][...], gv, m_v.at[*at][...], v_v.at[*at][...])
                    go_v.at[*at][...] = gv
                    d_v.at[*at][...] = d
                    mo_v.at[*at][...] = mn
                    vo_v.at[*at][...] = vn

        lay = pl.BlockSpec((br, bc), lambda i, j: (l * nrb + i, j))
        one = pl.BlockSpec((br, bc), lambda i, j: (i, j))
        pltpu.emit_pipeline(
            block, grid=(nrb, C // bc), in_specs=[lay, one, lay, lay], out_specs=[lay] * 4,
            core_axis_name=("sc_core", "sc_tile"), dimension_semantics=(pltpu.PARALLEL, pltpu.PARALLEL),
        )(w_hbm, g_hbm, m_hbm, v_hbm, go_hbm, d_hbm, mo_hbm, vo_hbm)

    outs = pl.kernel(
        kern, out_type=(sds,) * 4, name=name, scratch_types=[],
        mesh=plsc.VectorSubcoreMesh(core_axis_name="sc_core", subcore_axis_name="sc_tile"),
    )(flat(w), g, flat(m), flat(v))
    return tuple(o.reshape(L, R, C) for o in outs)


def _adamw_flat(w, g, m, v, name):
    R, C = w.shape
    tr = _tile(R, 1024) if R % 128 == 0 else R

    def body(w_ref, g_ref, m_ref, v_ref, d_ref, mo_ref, vo_ref):
        d, mn, vn = _adamw_math(w_ref[...], g_ref[...], m_ref[...], v_ref[...])
        d_ref[...] = d
        mo_ref[...] = mn
        vo_ref[...] = vn

    blk = pl.BlockSpec((tr, C), lambda i: (i, 0))
    sds = jax.ShapeDtypeStruct(w.shape, F32)
    return pl.pallas_call(
        body, out_shape=(sds, sds, sds), grid=(R // tr,), in_specs=[blk] * 4, out_specs=(blk,) * 3,
        name=name, compiler_params=_params("parallel"))(w, g, m, v)


def _allreduce_small(s, after=()):
    R, C = s.shape

    def body(s_ref, o_ref, rbuf, send_sems, recv_sems):
        x, y, c = lax.axis_index("x"), lax.axis_index("y"), lax.axis_index("c")
        peers = [(x, y, 1 - c), (1 - x, y, c), (x, 1 - y, c)]
        o_ref[...] = s_ref[...]
        for k, peer in enumerate(peers):
            cp = _remote(send_sems.at[k], recv_sems.at[k], peer)(o_ref, rbuf.at[k])
            cp.start()
            cp.wait()
            o_ref[...] = o_ref[...] + rbuf[k]

    vm = pl.BlockSpec(memory_space=pltpu.VMEM)
    return _call(
        body, [s], after, out_shape=jax.ShapeDtypeStruct((R, C), F32), in_specs=[vm], out_specs=vm,
        scratch_shapes=[pltpu.VMEM((3, R, C), F32), pltpu.SemaphoreType.DMA((3,)), pltpu.SemaphoreType.DMA((3,))],
        name="allreduce_small", compiler_params=pltpu.CompilerParams(vmem_limit_bytes=V7X_VMEM_LIMIT))


class _GradBatch:
    def __init__(self, tag, dws, geos4, entries, pos):
        self.tag, self.dws, self.geos4, self.entries, self.pos = tag, dws, geos4, entries, pos

    def start_swap(self, after):
        self.s1, self.dws, self.land1, tok = _swap_start(f"{self.tag}_swap_start", self.dws, self.geos4, after)
        return tok

    def swap_to_exchange(self, after):
        dws, lands = _swap_wait(f"{self.tag}_swap_wait", self.dws, self.land1, self.s1, self.geos4, after)
        ps = [_pair_sum(d, r, g, self.pos, f"{self.tag}_pair_sum_{i}")
              for i, (d, r, g) in enumerate(zip(dws, lands, self.geos4))]
        self.s2, self.ps, self.land2, tok = _exchange_start(f"{self.tag}_exch_start", ps, self.entries, ())
        return tok

    def exchange_to_join(self, after):
        ps, lands = _exchange_wait(f"{self.tag}_exch_wait", self.ps, self.land2, self.s2, self.entries, after)
        self.geos5 = [g for _, g in self.entries]
        gs = [_chip_sum(ps[pi], r, g, self.pos, f"{self.tag}_chip_sum_{e}")
              for e, ((pi, g), r) in enumerate(zip(self.entries, lands))]
        self.s3, self.gs, _, tok = _join_start(f"{self.tag}_join_start", gs, self.geos5, ())
        return tok

    def finish(self, after):
        return _join_wait(f"{self.tag}_join_wait", self.gs, self.s3, self.geos5, after)


def _pack(pieces):
    rows = []
    for p in pieces:
        flat = p.reshape(-1)
        pad = (-flat.shape[0]) % 1024
        rows.append(jnp.pad(flat, (0, pad)).reshape(-1, 128))
    return jnp.concatenate(rows, axis=0)


def _unpack(buf, shapes):
    out, r = [], 0
    for shp in shapes:
        n = math.prod(shp)
        nr = -(-n // 1024) * 8
        out.append(buf[r:r + nr].reshape(-1)[:n].reshape(shp))
        r += nr
    return out


def kernel(x, norm1_g, w_in, gmlp_ln_g, gmlp_ln_b, w_spatial, b_spatial, conv_w, group_norm_g, w_out, norm2_g, w_gate, w_up, w_down, final_norm_g, loss_target, m_norm1_g, m_w_in, m_gmlp_ln_g, m_gmlp_ln_b, m_w_spatial, m_b_spatial, m_conv_w, m_group_norm_g, m_w_out, m_norm2_g, m_w_gate, m_w_up, m_w_down, m_final_norm_g, v_norm1_g, v_w_in, v_gmlp_ln_g, v_gmlp_ln_b, v_w_spatial, v_b_spatial, v_conv_w, v_group_norm_g, v_w_out, v_norm2_g, v_w_gate, v_w_up, v_w_down, v_final_norm_g):
    L, D, n_in = w_in.shape
    T = x.shape[1]
    nin = N_CHIPS * n_in
    A = nin // 5
    H = A // HEAD_DIM
    n_f = w_gate.shape[2]
    F = N_CHIPS * n_f
    n_o = w_out.shape[1]
    cb = conv_w.shape[2]
    assert A == H * HEAD_DIM and T % 256 == 0 and N_CHIPS * n_o == D and N_CHIPS * cb == A

    g_in, g_out, g_down = _Geom(D, nin, 1, n_in), _Geom(D, D, 0, n_o), _Geom(F, D, 0, n_f)
    g_gate, g_up, g_gu = _Geom(D, 2 * F, 1, n_f, 0), _Geom(D, 2 * F, 1, n_f, F), _Geom(D, 2 * F, 1, n_f)
    pos = jnp.stack([2 * lax.axis_index("x") + lax.axis_index("y"), lax.axis_index("c")]).astype(jnp.int32)
    row = lambda v: v.reshape(1, -1)

    conv_full = _gather_conv(jnp.pad(conv_w, ((0, 0), (0, 8 - CONV_TAPS), (0, 0))))
    members = [[g_in], [g_out], [g_gate, g_up], [g_down]]
    sources = [[w_in], [w_out], [w_gate, w_up], [w_down]]
    placed, sems_of = {}, {}
    tok = conv_full
    for l in range(L):
        for a in range(4):
            arr = None
            for m, (w, geo) in enumerate(zip(sources[a], members[a])):
                arr = _cast_place(w, l, geo, pos, arr, f"l{l}_place_{a}_{m}", after=(tok,))
            sems_of[l, a], placed[l, a], tok = _gather_start(f"l{l}_gather_start_{a}", arr, members[a], (tok,))

    def arrive(l, a, after):
        return _gather_forward(f"l{l}_gather_fwd_{a}", placed[l, a], members[a], sems_of[l, a], after)

    def landed(l, a, fs, arr, after):
        return _gather_wait(f"l{l}_gather_wait_{a}", arr, members[a], fs, after)

    frame = jnp.arange(HEAD_DIM)
    mask = (frame[None, :] // CHUNK) <= (frame[:, None] // CHUNK)

    xs = x[0]
    acts = []
    for l in range(L):
        wm = jnp.where(mask[None], w_spatial[l], 0.0).astype(BF16)
        wmt = jnp.swapaxes(wm, 1, 2)
        bfull = jnp.broadcast_to(b_spatial[l][:, :, None], (H, HEAD_DIM, HEAD_DIM))
        sm = dict(lng=row(gmlp_ln_g[l]), lnb=row(gmlp_ln_b[l]), wm=wm, wmt=wmt, bfull=bfull,
                  cw=conv_full[l], gn=row(group_norm_g[l]))
        if l == 0:
            h = _rms_fwd(xs, row(norm1_g[l]), f"l{l}_rms1", after=(tok,))
            fs, arr, tok = arrive(l, 0, (h,))
            W_in = landed(l, 0, fs, arr, (tok,))
        else:
            fs, arr, tok = arrive(l, 0, (xs,))
            h = _rms_fwd(xs, row(norm1_g[l]), f"l{l}_rms1", after=(tok,))
            W_in = landed(l, 0, fs, arr, (h,))
        z = _mm_nn(h, W_in, out_dtype=BF16, name=f"l{l}_mm_in")
        fs, arr, tok = arrive(l, 1, (z,))
        y = _mixer_fwd(z, sm["lng"], sm["lnb"], wm, bfull, sm["cw"], sm["gn"], f"l{l}_mixer", after=(tok,))
        W_out = landed(l, 1, fs, arr, (y,))
        if l == 0:
            x1, h2 = _mm_res_rms(y, W_out, xs, row(norm2_g[l]), name=f"l{l}_mm_out")
            fs, arr, tok = arrive(l, 2, (x1,))
            W_gu = landed(l, 2, fs, arr, (tok,))
        else:
            fs, arr, tok = arrive(l, 2, (W_out,))
            x1, h2 = _mm_res_rms(y, W_out, xs, row(norm2_g[l]), name=f"l{l}_mm_out", after=(tok,))
            W_gu = landed(l, 2, fs, arr, (x1,))
        gu, act = _mm_swiglu(h2, W_gu, name=f"l{l}_mm_swiglu")
        fs, arr, tok = arrive(l, 3, (act,))
        W_down = landed(l, 3, fs, arr, (tok,))
        x2 = _mm_nn(act, W_down, res=x1, out_dtype=F32, tn=512, name=f"l{l}_mm_down")
        acts.append(dict(x=xs, h=h, z=z, y=y, x1=x1, h2=h2, gu=gu, act=act, sm=sm,
                         W_in=W_in, W_out=W_out, W_gu=W_gu, W_down=W_down))
        xs = x2

    loss_vec, dx, dxb, dgf = _loss_head(xs, row(final_norm_g), loss_target[0], "loss_head")
    loss = lax.psum(loss_vec[0, 0], ("x", "y", "c"))

    big_w = {"in": (w_in, m_w_in, v_w_in), "out": (w_out, m_w_out, v_w_out), "gate": (w_gate, m_w_gate, v_w_gate),
             "up": (w_up, m_w_up, v_w_up), "down": (w_down, m_w_down, v_w_down)}
    big = {nm: None for nm in big_w}

    def adamw(l, names, gs):
        after = ()
        for nm, g in zip(names, gs):
            w, m, v = big_w[nm]
            if l == L - 1 and L > 1:
                big[nm] = _adamw_sparsecore(l, w, g, m, v, f"l{l}_adamw_{nm}")
            else:
                big[nm] = _adamw_layer(l, w, g, m, v, big[nm], f"l{l}_adamw_{nm}", after=after)
                after = (big[nm][1],)
        return after

    small_grads = [None] * L
    pend_dg = pend_oi = None
    tok = ()
    for l in reversed(range(L)):
        a = acts[l]
        sm = a["sm"]
        dgu = _mm_nt_swiglu_bwd(dxb, a["W_down"], a["gu"], name=f"l{l}_bwd_down", after=tok)
        if pend_dg:
            tok = (pend_dg[0].exchange_to_join((dgu,)),)
        dW_down = _mm_tn(a["act"], dxb, name=f"l{l}_dw_down", after=tok)
        if pend_dg:
            tok = adamw(pend_dg[1], ["down", "gate", "up"], pend_dg[0].finish((dW_down,)))
        dh2 = _mm_nt_dgu(dgu, a["W_gu"], name=f"l{l}_bwd_gu", after=tok)
        if pend_oi:
            tok = (pend_oi[0].exchange_to_join((dh2,)),)
        dW_gu = _mm_tn(a["h2"], dgu, tkw=1024, tnw=512, name=f"l{l}_dw_gu", after=tok)
        if pend_oi:
            tok = adamw(pend_oi[1], ["out", "in"], pend_oi[0].finish((dW_gu,)))
        dg_batch = _GradBatch(f"l{l}_dg", [dW_down, dW_gu], [g_down, g_gu], [(0, g_down), (1, g_gate), (1, g_up)], pos)
        tok = (dg_batch.start_swap(tok),)
        dx1, dx1b, dg2 = _rms_bwd(dh2, a["x1"], row(norm2_g[l]), dx, f"l{l}_rms2_bwd", after=tok)
        dy = _mm_nt(dx1b, a["W_out"], out_dtype=F32, name=f"l{l}_bwd_out")
        dW_out = _mm_tn(a["y"], dx1b, name=f"l{l}_dw_out")
        tok = (dg_batch.swap_to_exchange((dW_out, dy)),)
        dz, dgn, dlng, dlnb, dws, dbs, dcw = _mixer_bwd(
            a["z"], dy, sm["lng"], sm["lnb"], sm["wm"], sm["wmt"], sm["bfull"], sm["cw"], sm["gn"],
            f"l{l}_mixer_bwd", after=tok)
        dW_in = _mm_tn(a["h"], dz, tkw=1024, name=f"l{l}_dw_in")
        oi_batch = _GradBatch(f"l{l}_oi", [dW_out, dW_in], [g_out, g_in], [(0, g_out), (1, g_in)], pos)
        tok = (oi_batch.start_swap(()),)
        dh = _mm_nt(dz, a["W_in"], out_dtype=BF16, tn=512, name=f"l{l}_bwd_in", after=tok)
        dx, dxb, dg1 = _rms_bwd(dh, a["x"], row(norm1_g[l]), dx1, f"l{l}_rms1_bwd")
        tok = (oi_batch.swap_to_exchange((dx,)),)
        small_grads[l] = [dg1[0], dlng[0], dlnb[0], dws, dbs[:, 0, :], dcw[:CONV_TAPS], dgn[0], dg2[0]]
        pend_dg, pend_oi = (dg_batch, l), (oi_batch, l)
    grad_x = dx[None]

    tok = (pend_dg[0].exchange_to_join(tok),)
    tok = adamw(pend_dg[1], ["down", "gate", "up"], pend_dg[0].finish(tok))
    tok = (pend_oi[0].exchange_to_join(tok),)
    pieces = [p for l in range(L) for p in small_grads[l]] + [dgf[0]]
    red = _allreduce_small(_pack(pieces), after=tok)
    adamw(pend_oi[1], ["out", "in"], pend_oi[0].finish((red,)))

    red_list = _unpack(red, [p.shape for p in pieces])
    per = len(small_grads[0])
    stack = lambda i: jnp.stack([red_list[l * per + i] for l in range(L)])
    g_small = {"norm1_g": stack(0), "gmlp_ln_g": stack(1), "gmlp_ln_b": stack(2), "w_spatial": stack(3),
               "b_spatial": stack(4), "group_norm_g": stack(6), "norm2_g": stack(7),
               "final_norm_g": red_list[L * per]}
    g_small["conv_w"] = lax.dynamic_slice_in_dim(stack(5), pos[0] * cb, cb, axis=2)
    small_w = {"norm1_g": (norm1_g, m_norm1_g, v_norm1_g), "gmlp_ln_g": (gmlp_ln_g, m_gmlp_ln_g, v_gmlp_ln_g),
               "gmlp_ln_b": (gmlp_ln_b, m_gmlp_ln_b, v_gmlp_ln_b), "w_spatial": (w_spatial, m_w_spatial, v_w_spatial),
               "b_spatial": (b_spatial, m_b_spatial, v_b_spatial), "conv_w": (conv_w, m_conv_w, v_conv_w),
               "group_norm_g": (group_norm_g, m_group_norm_g, v_group_norm_g),
               "norm2_g": (norm2_g, m_norm2_g, v_norm2_g), "final_norm_g": (final_norm_g, m_final_norm_g, v_final_norm_g)}
    snames = list(small_w)
    sd, smn, svn = _adamw_flat(_pack([small_w[n][0] for n in snames]), _pack([g_small[n] for n in snames]),
                               _pack([small_w[n][1] for n in snames]), _pack([small_w[n][2] for n in snames]),
                               "adamw_small")
    sshapes = [small_w[n][0].shape for n in snames]
    sd, smn, svn = _unpack(sd, sshapes), _unpack(smn, sshapes), _unpack(svn, sshapes)
    small_out = {n: (g_small[n], sd[i], smn[i], svn[i]) for i, n in enumerate(snames)}

    order = ["norm1_g", "w_in", "gmlp_ln_g", "gmlp_ln_b", "w_spatial", "b_spatial", "conv_w", "group_norm_g",
             "w_out", "norm2_g", "w_gate", "w_up", "w_down", "final_norm_g"]
    res = {n: (big[n[2:]] if n[2:] in big else small_out[n]) for n in order}
    return (loss, grad_x, *[res[n][0] for n in order], *[res[n][1] for n in order],
            *[res[n][2] for n in order], *[res[n][3] for n in order])
```

```python
import math

import jax
import jax.numpy as jnp
from jax import lax
from jax.experimental import pallas as pl
from jax.experimental.pallas import tpu as pltpu
from jax.experimental.pallas import tpu_sc as plsc

RMS_EPS = 1e-6
LN_EPS = 1e-5
HEAD_DIM = 128
CHUNK = 64
CONV_TAPS = 3
HALO = 16
ADAM_LR = 0.001
ADAM_B1 = 0.9
ADAM_B2 = 0.999
ADAM_EPS = 1e-08
ADAM_WD = 0.01
ADAM_STEP = 10
V7X_VMEM_LIMIT = 56 * 1024 * 1024
N_CHIPS = 4
DMA_CHUNK_BYTES = 2 * 1024 * 1024
EPILOGUE_ROWS = 256
SC_LANES = 16
SC_BLOCK_ELEMS = 5120
MESH = pl.DeviceIdType.MESH
F32 = jnp.float32
BF16 = jnp.bfloat16
ANY = pl.BlockSpec(memory_space=pl.ANY)
HBM = pl.BlockSpec(memory_space=pltpu.HBM)
SEM = pl.BlockSpec(memory_space=pltpu.SEMAPHORE)
EFFECT = pltpu.SideEffectType.DATAFLOW_SIDE_EFFECTING


def _tile(n, pref):
    if n <= pref:
        return n
    best = None
    for t in range(128, pref + 1, 128):
        if n % t == 0:
            best = t
    assert best is not None, (n, pref)
    return best


def _rows_tile(rows, cols, budget_elems):
    tr = rows
    while tr * cols > budget_elems and tr % 2 == 0 and (tr // 2) % 16 == 0:
        tr //= 2
    return tr


def _params(*sem):
    return pltpu.CompilerParams(dimension_semantics=sem if sem else None,
                                vmem_limit_bytes=V7X_VMEM_LIMIT)


def _call(body, args, after, **kw):
    n, na = len(args), len(after)
    if na:
        inner = body

        def body(*refs):
            inner(*refs[:n], *refs[n + na:])

        kw["in_specs"] = list(kw["in_specs"]) + [ANY] * na
    return pl.pallas_call(body, **kw)(*args, *after)


def _prefetch_call(body, pos, args, after=(), *, out_shape, grid, in_specs, out_specs, aliases=None, name):
    n, na = 1 + len(args), len(after)
    if na:
        inner = body

        def body(*refs):
            inner(*refs[:n], *refs[n + na:])

    return pl.pallas_call(
        body, out_shape=out_shape,
        grid_spec=pltpu.PrefetchScalarGridSpec(num_scalar_prefetch=1, grid=grid,
                                               in_specs=list(in_specs) + [ANY] * na, out_specs=out_specs),
        input_output_aliases=aliases or {}, name=name,
        compiler_params=_params(*(["parallel"] * len(grid))))(pos, *args, *after)


GELU_C = math.sqrt(2.0 / math.pi)
GELU_K = 0.044715


def _gelu(x):
    th = jnp.tanh(x * (GELU_C + (GELU_C * GELU_K) * (x * x)))
    hx = 0.5 * x
    return hx + hx * th


def _gelu_and_grad(x):
    x2 = x * x
    th = jnp.tanh(x * (GELU_C + (GELU_C * GELU_K) * x2))
    hx = 0.5 * x
    grad = (0.5 + 0.5 * th) + (hx * (1.0 - th * th)) * (GELU_C + (3.0 * GELU_C * GELU_K) * x2)
    return hx + hx * th, grad


def _sigmoid(x):
    return 1.0 / (1.0 + jnp.exp(-x))


def _nt(a, b):
    return lax.dot_general(a, b, (((1,), (1,)), ((), ())), preferred_element_type=F32)


def _tn(a, b):
    return lax.dot_general(a, b, (((0,), (0,)), ((), ())), preferred_element_type=F32)


def _rms_fwd(x, g, name, after=()):
    T, D = x.shape
    tr = _tile(T, 512)

    def body(x_ref, g_ref, h_ref):
        xv = x_ref[...]
        r = lax.rsqrt(jnp.mean(xv * xv, axis=-1, keepdims=True) + RMS_EPS)
        h_ref[...] = ((xv * r) * g_ref[...]).astype(h_ref.dtype)

    return _call(
        body, [x, g], after, out_shape=jax.ShapeDtypeStruct((T, D), BF16), grid=(T // tr,),
        in_specs=[pl.BlockSpec((tr, D), lambda i: (i, 0)), pl.BlockSpec((1, D), lambda i: (0, 0))],
        out_specs=pl.BlockSpec((tr, D), lambda i: (i, 0)),
        name=name, compiler_params=_params("parallel"))


def _rms_bwd(dh, x, g, dres, name, after=()):
    T, D = x.shape
    tr = _tile(T, 256)

    def body(dh_ref, x_ref, g_ref, dres_ref, dx_ref, dxb_ref, dg_ref):
        i = pl.program_id(0)
        xv = x_ref[...]
        dhv = dh_ref[...].astype(F32)
        r = lax.rsqrt(jnp.mean(xv * xv, axis=-1, keepdims=True) + RMS_EPS)
        xh = xv * r
        q = dhv * g_ref[...]
        dx = dres_ref[...] + r * (q - xh * jnp.mean(q * xh, axis=-1, keepdims=True))
        dx_ref[...] = dx
        dxb_ref[...] = dx.astype(BF16)
        part = jnp.sum(dhv * xh, axis=0, keepdims=True)

        @pl.when(i == 0)
        def _():
            dg_ref[...] = part

        @pl.when(i > 0)
        def _():
            dg_ref[...] += part

    row = pl.BlockSpec((tr, D), lambda i: (i, 0))
    vec = pl.BlockSpec((1, D), lambda i: (0, 0))
    return _call(
        body, [dh, x, g, dres], after,
        out_shape=(jax.ShapeDtypeStruct((T, D), F32), jax.ShapeDtypeStruct((T, D), BF16),
                   jax.ShapeDtypeStruct((1, D), F32)),
        grid=(T // tr,), in_specs=[row, row, vec, row], out_specs=(row, row, vec),
        name=name, compiler_params=_params("arbitrary"))


def _loss_head(x, g, tgt, name):
    T, D = x.shape
    tr = _tile(T, 256)

    def body(x_ref, g_ref, t_ref, loss_ref, dx_ref, dxb_ref, dg_ref):
        i = pl.program_id(0)
        xv = x_ref[...]
        gv = g_ref[...]
        r = lax.rsqrt(jnp.mean(xv * xv, axis=-1, keepdims=True) + RMS_EPS)
        xh = xv * r
        err = xh * gv - t_ref[...]
        lpart = jnp.full((1, 128), 0.5 * jnp.sum(jnp.mean(err * err, axis=-1, keepdims=True)), F32)
        dy = err * (1.0 / D)
        q = dy * gv
        dx = r * (q - xh * jnp.mean(q * xh, axis=-1, keepdims=True))
        dx_ref[...] = dx
        dxb_ref[...] = dx.astype(BF16)
        gpart = jnp.sum(dy * xh, axis=0, keepdims=True)

        @pl.when(i == 0)
        def _():
            loss_ref[...] = lpart
            dg_ref[...] = gpart

        @pl.when(i > 0)
        def _():
            loss_ref[...] += lpart
            dg_ref[...] += gpart

    row = pl.BlockSpec((tr, D), lambda i: (i, 0))
    vec = pl.BlockSpec((1, D), lambda i: (0, 0))
    return pl.pallas_call(
        body,
        out_shape=(jax.ShapeDtypeStruct((1, 128), F32), jax.ShapeDtypeStruct((T, D), F32),
                   jax.ShapeDtypeStruct((T, D), BF16), jax.ShapeDtypeStruct((1, D), F32)),
        grid=(T // tr,), in_specs=[row, vec, row],
        out_specs=(pl.BlockSpec((1, 128), lambda i: (0, 0)), row, row, vec),
        name=name, compiler_params=_params("arbitrary"))(x, g, tgt)


def _mm_nn(a, w, *, res=None, out_dtype, tm=1024, tn=1024, tk=None, name, after=()):
    M, K = a.shape
    N = w.shape[1]
    tm, tn = _tile(M, tm), _tile(N, tn)
    tk = K if tk is None else _tile(K, tk)
    nk = K // tk
    has_res = res is not None

    def body(*refs):
        a_ref, w_ref = refs[0], refs[1]
        r_ref = refs[2] if has_res else None
        o_ref = refs[2 + has_res]
        part = jnp.dot(a_ref[...], w_ref[...], preferred_element_type=F32)

        def finish(acc):
            if has_res:
                acc = r_ref[...] + acc
            o_ref[...] = acc.astype(o_ref.dtype)

        if nk == 1:
            finish(part)
        else:
            acc_ref = refs[3 + has_res]
            k = pl.program_id(2)

            @pl.when(k == 0)
            def _():
                acc_ref[...] = part

            @pl.when(jnp.logical_and(k > 0, k < nk - 1))
            def _():
                acc_ref[...] += part

            @pl.when(k == nk - 1)
            def _():
                finish(acc_ref[...] + part)

    in_specs = [pl.BlockSpec((tm, tk), lambda i, j, k: (i, k)),
                pl.BlockSpec((tk, tn), lambda i, j, k: (k, j))]
    args = [a, w]
    if has_res:
        in_specs.append(pl.BlockSpec((tm, tn), lambda i, j, k: (i, j)))
        args.append(res)
    return _call(
        body, args, after, out_shape=jax.ShapeDtypeStruct((M, N), out_dtype), grid=(M // tm, N // tn, nk),
        in_specs=in_specs, out_specs=pl.BlockSpec((tm, tn), lambda i, j, k: (i, j)),
        scratch_shapes=[pltpu.VMEM((tm, tn), F32)] if nk > 1 else [],
        name=name, compiler_params=_params("parallel", "parallel", "arbitrary"))


def _mm_res_rms(a, w, res, g, *, tm=512, name, after=()):
    M, K = a.shape
    N = w.shape[1]
    tm = _tile(M, tm)

    def body(a_ref, w_ref, r_ref, g_ref, x_ref, h_ref):
        xv = r_ref[...] + jnp.dot(a_ref[...], w_ref[...], preferred_element_type=F32)
        x_ref[...] = xv
        r = lax.rsqrt(jnp.mean(xv * xv, axis=-1, keepdims=True) + RMS_EPS)
        h_ref[...] = ((xv * r) * g_ref[...]).astype(BF16)

    row = lambda n: pl.BlockSpec((tm, n), lambda i: (i, 0))
    return _call(
        body, [a, w, res, g], after,
        out_shape=(jax.ShapeDtypeStruct((M, N), F32), jax.ShapeDtypeStruct((M, N), BF16)), grid=(M // tm,),
        in_specs=[row(K), pl.BlockSpec((K, N), lambda i: (0, 0)), row(N), pl.BlockSpec((1, N), lambda i: (0, 0))],
        out_specs=(row(N), row(N)), name=name, compiler_params=_params("parallel"))


def _mm_swiglu(h, wgu, *, tm=2048, tn=512, name):
    T, D = h.shape
    F = wgu.shape[1] // 2
    tm, tn = _tile(T, tm), _tile(F, tn)
    nf = F // tn

    rc = _tile(tm, EPILOGUE_ROWS)

    def body(h_ref, wg_ref, wu_ref, fac_ref, act_ref):
        for r in range(tm // rc):
            rows = slice(r * rc, (r + 1) * rc)
            hv = h_ref[rows, :]
            g = jnp.dot(hv, wg_ref[...], preferred_element_type=F32)
            u = jnp.dot(hv, wu_ref[...], preferred_element_type=F32)
            s = _sigmoid(g)
            t = g * s
            act_ref[rows, :] = (t * u).astype(BF16)
            fac_ref[0, rows, :] = (u * (s + t * (1.0 - s))).astype(BF16)
            fac_ref[1, rows, :] = t.astype(BF16)

    return _call(
        body, [h, wgu, wgu], (),
        out_shape=(jax.ShapeDtypeStruct((2, T, F), BF16), jax.ShapeDtypeStruct((T, F), BF16)),
        grid=(T // tm, nf),
        in_specs=[pl.BlockSpec((tm, D), lambda i, j: (i, 0)),
                  pl.BlockSpec((D, tn), lambda i, j: (0, j)),
                  pl.BlockSpec((D, tn), lambda i, j: (0, j + nf))],
        out_specs=(pl.BlockSpec((2, tm, tn), lambda i, j: (0, i, j)),
                   pl.BlockSpec((tm, tn), lambda i, j: (i, j))),
        name=name, compiler_params=_params("parallel", "parallel"))


def _mm_nt(a, w, *, out_dtype, tm=1024, tn=1024, tk=None, name, after=()):
    M, K = a.shape
    N = w.shape[0]
    tm, tn = _tile(M, tm), _tile(N, tn)
    tk = K if tk is None else _tile(K, tk)
    nk = K // tk

    def body(*refs):
        a_ref, w_ref, o_ref = refs[0], refs[1], refs[2]
        part = _nt(a_ref[...], w_ref[...])
        if nk == 1:
            o_ref[...] = part.astype(o_ref.dtype)
        else:
            acc_ref = refs[3]
            k = pl.program_id(2)

            @pl.when(k == 0)
            def _():
                acc_ref[...] = part

            @pl.when(jnp.logical_and(k > 0, k < nk - 1))
            def _():
                acc_ref[...] += part

            @pl.when(k == nk - 1)
            def _():
                o_ref[...] = (acc_ref[...] + part).astype(o_ref.dtype)

    return _call(
        body, [a, w], after, out_shape=jax.ShapeDtypeStruct((M, N), out_dtype), grid=(M // tm, N // tn, nk),
        in_specs=[pl.BlockSpec((tm, tk), lambda i, j, k: (i, k)),
                  pl.BlockSpec((tn, tk), lambda i, j, k: (j, k))],
        out_specs=pl.BlockSpec((tm, tn), lambda i, j, k: (i, j)),
        scratch_shapes=[pltpu.VMEM((tm, tn), F32)] if nk > 1 else [],
        name=name, compiler_params=_params("parallel", "parallel", "arbitrary"))


def _mm_nt_swiglu_bwd(dxb, wdown, gu, *, tm=2048, tn=512, name, after=()):
    T, D = dxb.shape
    F = wdown.shape[0]
    tm, tn = _tile(T, tm), _tile(F, tn)

    def body(dx_ref, w_ref, fac_ref, dgu_ref):
        da = _nt(dx_ref[...], w_ref[...])
        dgu_ref[0] = (da * fac_ref[0].astype(F32)).astype(BF16)
        dgu_ref[1] = (da * fac_ref[1].astype(F32)).astype(BF16)

    blk3 = pl.BlockSpec((2, tm, tn), lambda i, j: (0, i, j))
    return _call(
        body, [dxb, wdown, gu], after, out_shape=jax.ShapeDtypeStruct((2, T, F), BF16), grid=(T // tm, F // tn),
        in_specs=[pl.BlockSpec((tm, D), lambda i, j: (i, 0)),
                  pl.BlockSpec((tn, D), lambda i, j: (j, 0)), blk3],
        out_specs=blk3, name=name, compiler_params=_params("parallel", "parallel"))


def _mm_nt_dgu(dgu, wgu, *, tm=1024, tn=1024, tk=2816, name, after=()):
    _, T, F = dgu.shape
    D = wgu.shape[0]
    tm, tn, tk = _tile(T, tm), _tile(D, tn), _tile(F, tk)
    nkf = F // tk
    nk = 2 * nkf

    def body(a_ref, w_ref, o_ref, acc_ref):
        k = pl.program_id(2)
        part = _nt(a_ref[...], w_ref[...])

        @pl.when(k == 0)
        def _():
            acc_ref[...] = part

        @pl.when(jnp.logical_and(k > 0, k < nk - 1))
        def _():
            acc_ref[...] += part

        @pl.when(k == nk - 1)
        def _():
            o_ref[...] = (acc_ref[...] + part).astype(o_ref.dtype)

    return _call(
        body, [dgu, wgu], after, out_shape=jax.ShapeDtypeStruct((T, D), BF16), grid=(T // tm, D // tn, nk),
        in_specs=[pl.BlockSpec((None, tm, tk), lambda i, j, k: (k // nkf, i, k % nkf)),
                  pl.BlockSpec((tn, tk), lambda i, j, k: (j, k))],
        out_specs=pl.BlockSpec((tm, tn), lambda i, j, k: (i, j)),
        scratch_shapes=[pltpu.VMEM((tm, tn), F32)],
        name=name, compiler_params=_params("parallel", "parallel", "arbitrary"))


def _mm_tn(a, g, *, tkw=512, tnw=1024, name, after=()):
    T, Kw = a.shape
    pair = g.ndim == 3
    Nw = 2 * g.shape[2] if pair else g.shape[1]
    tkw = _tile(Kw, tkw)
    tnw = _tile(g.shape[2] if pair else Nw, tnw)
    nf = (Nw // 2) // tnw if pair else 0

    def body(a_ref, g_ref, o_ref):
        o_ref[...] = _tn(a_ref[...], g_ref[...]).astype(o_ref.dtype)

    if pair:
        g_spec = pl.BlockSpec((None, T, tnw), lambda i, j: (j // nf, 0, j % nf))
    else:
        g_spec = pl.BlockSpec((T, tnw), lambda i, j: (0, j))
    return _call(
        body, [a, g], after, out_shape=jax.ShapeDtypeStruct((Kw, Nw), BF16), grid=(Kw // tkw, Nw // tnw),
        in_specs=[pl.BlockSpec((T, tkw), lambda i, j: (0, i)), g_spec],
        out_specs=pl.BlockSpec((tkw, tnw), lambda i, j: (i, j)),
        name=name, compiler_params=_params("parallel", "parallel"))


def _mixer_specs(T, A, tr):
    nin = 5 * A
    nb = tr // HALO
    last = T // HALO - 1
    prev = pl.BlockSpec((HALO, nin), lambda i: (jnp.maximum(i * nb - 1, 0), 0))
    cur = pl.BlockSpec((tr, nin), lambda i: (i, 0))
    nxt = pl.BlockSpec((HALO, nin), lambda i: (jnp.minimum((i + 1) * nb, last), 0))
    return prev, cur, nxt


def _conv(p_ext, cw_ref):
    return (cw_ref[2:3, :] * p_ext + cw_ref[1:2, :] * pltpu.roll(p_ext, 1, 0)
            + cw_ref[0:1, :] * pltpu.roll(p_ext, 2, 0))


def _mixer_fwd(z, lng, lnb, wm, bfull, cw, gn, name, after=()):
    T, nin = z.shape
    A = nin // 5
    H = A // HEAD_DIM
    tr = _tile(T, 256)
    nblk = tr // HEAD_DIM

    def body(zp_ref, z_ref, lng_ref, lnb_ref, wm_ref, bf_ref, cw_ref, gn_ref, y_ref, vn_ref, mix_ref):
        i = pl.program_id(0)
        u = _gelu(z_ref[:, 0:A].astype(F32))
        vg = _gelu(z_ref[:, A:2 * A].astype(F32))
        xc = vg - jnp.mean(vg, axis=-1, keepdims=True)
        rstd = lax.rsqrt(jnp.mean(xc * xc, axis=-1, keepdims=True) + LN_EPS)
        vn_ref[...] = ((xc * rstd) * lng_ref[...] + lnb_ref[...]).astype(BF16)
        for cb in range(nblk):
            rows = slice(cb * HEAD_DIM, (cb + 1) * HEAD_DIM)
            for h in range(H):
                cols = slice(h * HEAD_DIM, (h + 1) * HEAD_DIM)
                mix_ref[rows, cols] = jnp.dot(wm_ref[h], vn_ref[rows, cols],
                                              preferred_element_type=F32) + bf_ref[h]
        ya = u * mix_ref[...]
        ra = lax.rsqrt(jnp.mean(ya * ya, axis=-1, keepdims=True) + RMS_EPS)
        y_ref[:, 0:A] = ((ya * ra) * gn_ref[:, 0:A]).astype(BF16)

        p_prev = zp_ref[:, 3 * A:4 * A].astype(F32) * zp_ref[:, 4 * A:5 * A].astype(F32)
        p_prev = jnp.where(i > 0, p_prev, 0.0)
        p_cur = z_ref[:, 3 * A:4 * A].astype(F32) * z_ref[:, 4 * A:5 * A].astype(F32)
        cv = _conv(jnp.concatenate([p_prev, p_cur], axis=0), cw_ref)[HALO:]
        yb = z_ref[:, 2 * A:3 * A].astype(F32) * cv
        rb = lax.rsqrt(jnp.mean(yb * yb, axis=-1, keepdims=True) + RMS_EPS)
        y_ref[:, A:2 * A] = ((yb * rb) * gn_ref[:, A:2 * A]).astype(BF16)

    prev, cur, _ = _mixer_specs(T, A, tr)
    full = lambda shape: pl.BlockSpec(shape, lambda i: (0,) * len(shape))
    return _call(
        body, [z, z, lng, lnb, wm, bfull, cw, gn], after,
        out_shape=jax.ShapeDtypeStruct((T, 2 * A), BF16), grid=(T // tr,),
        in_specs=[prev, cur, full((1, A)), full((1, A)), full((H, HEAD_DIM, HEAD_DIM)),
                  full((H, HEAD_DIM, HEAD_DIM)), full((8, A)), full((1, 2 * A))],
        out_specs=pl.BlockSpec((tr, 2 * A), lambda i: (i, 0)),
        scratch_shapes=[pltpu.VMEM((tr, A), BF16), pltpu.VMEM((tr, A), F32)],
        name=name, compiler_params=_params("parallel"))


def _mixer_bwd(z, dy, lng, lnb, wm, wmt, bfull, cw, gn, name, after=()):
    T, nin = z.shape
    A = nin // 5
    H = A // HEAD_DIM
    tr = _tile(T, 256)
    nblk = tr // HEAD_DIM
    ngrid = T // tr
    next_ = tr + 2 * HALO

    def body(zp_ref, z_ref, zn_ref, dy_ref, dyn_ref, lng_ref, lnb_ref, wm_ref, wmt_ref, bf_ref, cw_ref, gn_ref,
             dz_ref, dgn_ref, dlng_ref, dlnb_ref, dws_ref, dbs_ref, dcw_ref,
             vn_ref, mix_ref, dmix_ref, dvn_ref):
        i = pl.program_id(0)

        @pl.when(i == 0)
        def _():
            dgn_ref[...] = jnp.zeros_like(dgn_ref)
            dlng_ref[...] = jnp.zeros_like(dlng_ref)
            dlnb_ref[...] = jnp.zeros_like(dlnb_ref)
            dws_ref[...] = jnp.zeros_like(dws_ref)
            dbs_ref[...] = jnp.zeros_like(dbs_ref)
            dcw_ref[...] = jnp.zeros_like(dcw_ref)

        u, du_dz = _gelu_and_grad(z_ref[:, 0:A].astype(F32))
        vg, dv_dz = _gelu_and_grad(z_ref[:, A:2 * A].astype(F32))
        xc = vg - jnp.mean(vg, axis=-1, keepdims=True)
        rstd = lax.rsqrt(jnp.mean(xc * xc, axis=-1, keepdims=True) + LN_EPS)
        vhat = xc * rstd
        vn_ref[...] = (vhat * lng_ref[...] + lnb_ref[...]).astype(BF16)
        for cb in range(nblk):
            rows = slice(cb * HEAD_DIM, (cb + 1) * HEAD_DIM)
            for h in range(H):
                cols = slice(h * HEAD_DIM, (h + 1) * HEAD_DIM)
                mix_ref[rows, cols] = jnp.dot(wm_ref[h], vn_ref[rows, cols],
                                              preferred_element_type=F32) + bf_ref[h]
        mixed = mix_ref[...]
        ya = u * mixed
        ra = lax.rsqrt(jnp.mean(ya * ya, axis=-1, keepdims=True) + RMS_EPS)
        yha = ya * ra
        dyan = dy_ref[:, 0:A]
        dgn_ref[:, 0:A] += jnp.sum(dyan * yha, axis=0, keepdims=True)
        qa = dyan * gn_ref[:, 0:A]
        dya = ra * (qa - yha * jnp.mean(qa * yha, axis=-1, keepdims=True))
        dz_ref[:, 0:A] = ((dya * mixed) * du_dz).astype(BF16)
        dmix_ref[...] = (dya * u).astype(BF16)

        ii = lax.broadcasted_iota(jnp.int32, (HEAD_DIM, HEAD_DIM), 0)
        jj = lax.broadcasted_iota(jnp.int32, (HEAD_DIM, HEAD_DIM), 1)
        mask = (jj // CHUNK <= ii // CHUNK).astype(F32)
        ones = jnp.ones((8, HEAD_DIM), BF16)
        for h in range(H):
            cols = slice(h * HEAD_DIM, (h + 1) * HEAD_DIM)
            dws = jnp.zeros((HEAD_DIM, HEAD_DIM), F32)
            dbs = jnp.zeros((8, HEAD_DIM), F32)
            for cb in range(nblk):
                rows = slice(cb * HEAD_DIM, (cb + 1) * HEAD_DIM)
                dm = dmix_ref[rows, cols]
                dws = dws + _nt(dm, vn_ref[rows, cols])
                dbs = dbs + _nt(ones, dm)
                dvn_ref[rows, cols] = jnp.dot(wmt_ref[h], dm, preferred_element_type=F32)
            dws_ref[h] += dws * mask
            dbs_ref[h] += dbs
        dvn = dvn_ref[...]
        dlnb_ref[...] += jnp.sum(dvn, axis=0, keepdims=True)
        dlng_ref[...] += jnp.sum(dvn * vhat, axis=0, keepdims=True)
        dvh = dvn * lng_ref[...]
        dvg = rstd * (dvh - jnp.mean(dvh, axis=-1, keepdims=True)
                      - vhat * jnp.mean(dvh * vhat, axis=-1, keepdims=True))
        dz_ref[:, A:2 * A] = (dvg * dv_dz).astype(BF16)

        def ext(lo):
            mid = z_ref[:, lo:lo + A].astype(F32)
            return jnp.concatenate([zp_ref[:, lo:lo + A].astype(F32), mid, zn_ref[:, lo:lo + A].astype(F32)], axis=0)

        zb, zc, zh = ext(2 * A), ext(3 * A), ext(4 * A)
        row = lax.broadcasted_iota(jnp.int32, (next_, 1), 0)
        p = zc * zh
        p = jnp.where(jnp.logical_and(row < HALO, i == 0), 0.0, p)
        cv = _conv(p, cw_ref)
        yb = zb * cv
        rb = lax.rsqrt(jnp.mean(yb * yb, axis=-1, keepdims=True) + RMS_EPS)
        yhb = yb * rb
        dyn_rows = jnp.where(i < ngrid - 1, dyn_ref[:, A:2 * A], 0.0)
        dybn = jnp.concatenate([jnp.zeros((HALO, A), F32), dy_ref[:, A:2 * A], dyn_rows], axis=0)
        ctr = slice(HALO, HALO + tr)
        dgn_ref[:, A:2 * A] += jnp.sum((dybn * yhb)[ctr], axis=0, keepdims=True)
        qb = dybn * gn_ref[:, A:2 * A]
        dyb = rb * (qb - yhb * jnp.mean(qb * yhb, axis=-1, keepdims=True))
        dcv = dyb * zb
        dp = (cw_ref[2:3, :] * dcv + cw_ref[1:2, :] * pltpu.roll(dcv, next_ - 1, 0)
              + cw_ref[0:1, :] * pltpu.roll(dcv, next_ - 2, 0))
        dz_ref[:, 2 * A:3 * A] = (dyb * cv)[ctr].astype(BF16)
        dz_ref[:, 3 * A:4 * A] = (dp * zh)[ctr].astype(BF16)
        dz_ref[:, 4 * A:5 * A] = (dp * zc)[ctr].astype(BF16)
        dcw_ref[2:3, :] += jnp.sum((dcv * p)[ctr], axis=0, keepdims=True)
        dcw_ref[1:2, :] += jnp.sum((dcv * pltpu.roll(p, 1, 0))[ctr], axis=0, keepdims=True)
        dcw_ref[0:1, :] += jnp.sum((dcv * pltpu.roll(p, 2, 0))[ctr], axis=0, keepdims=True)

    prev, cur, nxt = _mixer_specs(T, A, tr)
    nb = tr // HALO
    dy_cur = pl.BlockSpec((tr, 2 * A), lambda i: (i, 0))
    dy_nxt = pl.BlockSpec((HALO, 2 * A), lambda i: (jnp.minimum((i + 1) * nb, T // HALO - 1), 0))
    full = lambda shape: pl.BlockSpec(shape, lambda i: (0,) * len(shape))
    hh = (H, HEAD_DIM, HEAD_DIM)
    return _call(
        body, [z, z, z, dy, dy, lng, lnb, wm, wmt, bfull, cw, gn], after,
        out_shape=(jax.ShapeDtypeStruct((T, nin), BF16), jax.ShapeDtypeStruct((1, 2 * A), F32),
                   jax.ShapeDtypeStruct((1, A), F32), jax.ShapeDtypeStruct((1, A), F32),
                   jax.ShapeDtypeStruct(hh, F32), jax.ShapeDtypeStruct((H, 8, HEAD_DIM), F32),
                   jax.ShapeDtypeStruct((8, A), F32)),
        grid=(ngrid,),
        in_specs=[prev, cur, nxt, dy_cur, dy_nxt, full((1, A)), full((1, A)), full(hh), full(hh), full(hh),
                  full((8, A)), full((1, 2 * A))],
        out_specs=(pl.BlockSpec((tr, nin), lambda i: (i, 0)), full((1, 2 * A)), full((1, A)), full((1, A)),
                   full(hh), full((H, 8, HEAD_DIM)), full((8, A))),
        scratch_shapes=[pltpu.VMEM((tr, A), BF16), pltpu.VMEM((tr, A), F32), pltpu.VMEM((tr, A), BF16),
                        pltpu.VMEM((tr, A), F32)],
        name=name, compiler_params=_params("arbitrary"))


class _Geom:
    def __init__(self, rows, cols, axis, size, base=0):
        self.rows, self.cols, self.axis, self.size, self.base = rows, cols, axis, size, base

    def in_full(self, j, h):
        if self.axis == 1:
            return (h * (self.rows // 2), self.rows // 2), (self.base + j * self.size, self.size)
        return (self.base + j * self.size, self.size), (h * (self.cols // 2), self.cols // 2)

    def in_shard(self, h):
        if self.axis == 1:
            return (h * (self.rows // 2), self.rows // 2), (0, self.size)
        return (0, self.size), (h * (self.cols // 2), self.cols // 2)

    def half_of_full(self, h):
        if self.axis == 1:
            return (h * (self.rows // 2), self.rows // 2), (0, self.cols)
        return (0, self.rows), (h * (self.cols // 2), self.cols // 2)

    def in_half(self, j):
        if self.axis == 1:
            return (0, self.rows // 2), (self.base + j * self.size, self.size)
        return (self.base + j * self.size, self.size), (0, self.cols // 2)

    @property
    def half_shape(self):
        return (self.rows // 2, self.cols) if self.axis == 1 else (self.rows, self.cols // 2)

    @property
    def shard_half_shape(self):
        return (self.rows // 2, self.size) if self.axis == 1 else (self.size, self.cols // 2)

    @property
    def shard_shape(self):
        return (self.rows, self.size) if self.axis == 1 else (self.size, self.cols)


def _at(ref, region):
    (r0, rn), (c0, cn) = region
    if not isinstance(r0, int):
        r0 = pl.multiple_of(r0, 16)
    if not isinstance(c0, int):
        c0 = pl.multiple_of(c0, 128)
    return ref.at[pl.ds(r0, rn), pl.ds(c0, cn)]


def _whole(shape):
    return (0, shape[-2]), (0, shape[-1])


def _split_rows(region, itemsize):
    (r0, rn), cols = region
    want = max(1, (rn * cols[1] * itemsize) // DMA_CHUNK_BYTES)
    n = 1
    for cand in range(1, want + 1):
        if rn % cand == 0 and (rn // cand) % 16 == 0:
            n = cand
    step = rn // n
    return [((r0 + i * step, step), cols) for i in range(n)]


class _Chunked:
    def __init__(self, make, src, src_reg, dst, dst_reg):
        self.make, self.src, self.src_reg, self.dst, self.dst_reg = make, src, src_reg, dst, dst_reg

    @property
    def whole(self):
        return self.make(_at(self.src, self.src_reg), _at(self.dst, self.dst_reg))

    def start(self):
        itemsize = jnp.dtype(self.src.dtype).itemsize
        for a, b in zip(_split_rows(self.src_reg, itemsize), _split_rows(self.dst_reg, itemsize)):
            self.make(_at(self.src, a), _at(self.dst, b)).start()


def _mesh_place():
    x, y, c = lax.axis_index("x"), lax.axis_index("y"), lax.axis_index("c")
    chips = [(1 - x, y), (x, 1 - y), (1 - x, 1 - y)]
    return x, y, c, 2 * x + y, chips


def _remote(ssem, rsem, dev):
    return lambda src, dst: pltpu.make_async_remote_copy(
        src_ref=src, dst_ref=dst, send_sem=ssem, recv_sem=rsem, device_id=dev, device_id_type=MESH)


def _comm_call(name, body, arrays, sems_in=(), after=(), sems_out=(), new=()):
    na, ns, nf, no, nn = len(arrays), len(sems_in), len(after), len(sems_out), len(new)

    def kern(*refs):
        sin = refs[na:na + ns]
        outs = refs[na + ns + nf:]
        body(outs[no:no + na], outs[no + na:no + na + nn], sin, outs[:no])
        outs[-1][...] = jnp.zeros_like(outs[-1])

    out_shape = (tuple(pltpu.SemaphoreType.DMA((n,)) for n in sems_out)
                 + tuple(pltpu.HBM(a.shape, a.dtype) for a in arrays)
                 + tuple(pltpu.HBM(s, d) for s, d in new)
                 + (jax.ShapeDtypeStruct((8, 128), F32),))
    res = pl.pallas_call(
        kern, out_shape=out_shape, in_specs=[HBM] * na + [SEM] * ns + [ANY] * nf,
        out_specs=(SEM,) * no + (HBM,) * (na + nn) + (pl.BlockSpec(memory_space=pltpu.VMEM),),
        input_output_aliases={i: no + i for i in range(na)}, name=name,
        compiler_params=pltpu.CompilerParams(has_side_effects=EFFECT),
    )(*[pltpu.with_memory_space_constraint(a, pltpu.HBM) for a in arrays], *sems_in, *after)
    return list(res[:no]), list(res[no:no + na]), list(res[no + na:no + na + nn]), res[-1]


def _gather_start(name, arr, members, after):
    def body(arrs, news, sin, sout):
        x, y, c, j, chips = _mesh_place()
        for m, geo in enumerate(members):
            reg = geo.in_full(j, c)
            for k, chip in enumerate(chips):
                _Chunked(_remote(sout[2 * m].at[k], sout[2 * m + 1].at[k], (*chip, c)),
                         arrs[0], reg, arrs[0], reg).start()

    sems, arrs, _, tok = _comm_call(name, body, [arr], after=after, sems_out=[N_CHIPS - 1] * (2 * len(members)))
    return [sems[2 * m:2 * m + 2] for m in range(len(members))], arrs[0], tok


def _gather_forward(name, arr, members, sems, after):
    def body(arrs, news, sin, sout):
        x, y, c, j, chips = _mesh_place()
        sibling = (x, y, 1 - c)
        full = arrs[0]
        for m, geo in enumerate(members):
            mine = geo.in_full(j, c)
            for k, chip in enumerate(chips):
                got = geo.in_full(2 * chip[0] + chip[1], c)
                sent = _Chunked(_remote(sin[2 * m].at[k], sin[2 * m + 1].at[k], sibling), full, mine, full, got)
                sent.whole.wait_send()
                sent.whole.wait_recv()
                _Chunked(_remote(sout[2 * m].at[k], sout[2 * m + 1].at[k], sibling), full, got, full, got).start()

    flat = [s for pair in sems for s in pair]
    fs, arrs, _, tok = _comm_call(name, body, [arr], sems_in=flat, after=after,
                                  sems_out=[N_CHIPS - 1] * (2 * len(members)))
    return fs, arrs[0], tok


def _gather_wait(name, arr, members, fsems, after):
    def body(arrs, news, sin, sout):
        x, y, c, j, chips = _mesh_place()
        sibling = (x, y, 1 - c)
        full = arrs[0]
        for m, geo in enumerate(members):
            for k, chip in enumerate(chips):
                jk = 2 * chip[0] + chip[1]
                cp = _Chunked(_remote(sin[2 * m].at[k], sin[2 * m + 1].at[k], sibling),
                              full, geo.in_full(jk, c), full, geo.in_full(jk, 1 - c))
                cp.whole.wait_send()
                cp.whole.wait_recv()

    _, arrs, _, _ = _comm_call(name, body, [arr], sems_in=fsems, after=after)
    return arrs[0]


def _gather_conv(conv):
    L, _, cb = conv.shape
    nk = N_CHIPS - 1

    def body(s_ref, f_ref, send_sems, recv_sems, local_sem):
        x, y, c, j, chips = _mesh_place()
        at = lambda jj: f_ref.at[:, :, pl.ds(pl.multiple_of(jj * cb, 128), cb)]
        lc = pltpu.make_async_copy(s_ref, at(j), local_sem.at[0])
        lc.start()
        cps = []
        for k, chip in enumerate(chips):
            cp = _remote(send_sems.at[k], recv_sems.at[k], (*chip, c))(s_ref, at(j))
            cp.start()
            cps.append(cp)
        for k, chip in enumerate(chips):
            jk = 2 * chip[0] + chip[1]
            cps[k].wait_send()
            _remote(send_sems.at[k], recv_sems.at[k], (*chip, c))(at(jk), at(jk)).wait_recv()
        lc.wait()

    return pl.pallas_call(
        body, out_shape=jax.ShapeDtypeStruct((L, 8, N_CHIPS * cb), F32), in_specs=[ANY], out_specs=ANY,
        scratch_shapes=[pltpu.SemaphoreType.DMA((nk,)), pltpu.SemaphoreType.DMA((nk,)),
                        pltpu.SemaphoreType.DMA((1,))],
        name="gather_conv")(conv)


def _swap_start(name, dws, geos, after):
    n = len(dws)

    def body(arrs, news, sin, sout):
        x, y, c, _, _ = _mesh_place()
        for t in range(n):
            _Chunked(_remote(sout[0].at[t], sout[1].at[t], (x, y, 1 - c)),
                     arrs[t], geos[t].half_of_full(1 - c), news[t], _whole(news[t].shape)).start()

    return _comm_call(name, body, dws, after=after, sems_out=[n, n], new=[(g.half_shape, BF16) for g in geos])


def _swap_wait(name, dws, lands, sems, geos, after):
    n = len(dws)

    def body(arrs, news, sin, sout):
        x, y, c, _, _ = _mesh_place()
        for t in range(n):
            cp = _Chunked(_remote(sin[0].at[t], sin[1].at[t], (x, y, 1 - c)),
                          arrs[t], geos[t].half_of_full(1 - c), arrs[n + t], _whole(arrs[n + t].shape))
            cp.whole.wait_send()
            cp.whole.wait_recv()

    _, arrs, _, _ = _comm_call(name, body, list(dws) + list(lands), sems_in=sems, after=after)
    return arrs[:n], arrs[n:]


def _exchange_start(name, ps, entries, after):
    nk = N_CHIPS - 1

    def body(arrs, news, sin, sout):
        x, y, c, j, chips = _mesh_place()
        for e, (pi, geo) in enumerate(entries):
            for k, chip in enumerate(chips):
                dst = news[e].at[k]
                _Chunked(_remote(sout[0].at[nk * e + k], sout[1].at[nk * e + k], (*chip, c)),
                         arrs[pi], geo.in_half(2 * chip[0] + chip[1]), dst, _whole(dst.shape)).start()

    ne = len(entries)
    return _comm_call(name, body, ps, after=after, sems_out=[nk * ne, nk * ne],
                      new=[((nk,) + g.shard_half_shape, BF16) for _, g in entries])


def _exchange_wait(name, ps, lands, sems, entries, after):
    nk = N_CHIPS - 1
    n = len(ps)

    def body(arrs, news, sin, sout):
        x, y, c, j, chips = _mesh_place()
        for e, (pi, geo) in enumerate(entries):
            for k, chip in enumerate(chips):
                dst = arrs[n + e].at[k]
                cp = _Chunked(_remote(sin[0].at[nk * e + k], sin[1].at[nk * e + k], (*chip, c)),
                              arrs[pi], geo.in_half(2 * chip[0] + chip[1]), dst, _whole(dst.shape))
                cp.whole.wait_send()
                cp.whole.wait_recv()

    _, arrs, _, _ = _comm_call(name, body, list(ps) + list(lands), sems_in=sems, after=after)
    return arrs[:n], arrs[n:]


def _join_start(name, gs, geos, after):
    n = len(gs)

    def body(arrs, news, sin, sout):
        x, y, c, _, _ = _mesh_place()
        for t in range(n):
            mine = geos[t].in_shard(c)
            _Chunked(_remote(sout[0].at[t], sout[1].at[t], (x, y, 1 - c)), arrs[t], mine, arrs[t], mine).start()

    return _comm_call(name, body, gs, after=after, sems_out=[n, n])


def _join_wait(name, gs, sems, geos, after):
    n = len(gs)

    def body(arrs, news, sin, sout):
        x, y, c, _, _ = _mesh_place()
        for t in range(n):
            cp = _Chunked(_remote(sin[0].at[t], sin[1].at[t], (x, y, 1 - c)),
                          arrs[t], geos[t].in_shard(c), arrs[t], geos[t].in_shard(1 - c))
            cp.whole.wait_send()
            cp.whole.wait_recv()

    _, arrs, _, _ = _comm_call(name, body, gs, sems_in=sems, after=after)
    return arrs


def _cast_place(w, l, geo, pos, prev, name, after=()):
    if geo.axis == 1:
        tr = _rows_tile(geo.rows, geo.size, 1024 * 1024)
        grid = (geo.rows // tr,)
        blk = (tr, geo.size)
        q = geo.base // geo.size
        out_map = lambda i, p: (i, q + p[0])
    else:
        tr = _rows_tile(geo.size, geo.cols, 1024 * 1024)
        grid = (geo.size // tr,)
        blk = (tr, geo.cols)
        nb = geo.size // tr
        out_map = lambda i, p: (p[0] * nb + i, 0)

    def body(p_ref, w_ref, *rest):
        rest[-1][...] = w_ref[...].astype(BF16)

    in_specs = [pl.BlockSpec((None,) + blk, lambda i, p: (l, i, 0))]
    args = [w]
    aliases = None
    if prev is not None:
        in_specs.append(ANY)
        args.append(prev)
        aliases = {2: 0}
    return _prefetch_call(body, pos, args, after, out_shape=jax.ShapeDtypeStruct((geo.rows, geo.cols), BF16),
                          grid=grid, in_specs=in_specs, out_specs=pl.BlockSpec(blk, out_map), aliases=aliases,
                          name=name)


def _pair_sum(dw, recv, geo, pos, name):
    hs = geo.half_shape
    tr = _rows_tile(hs[0], hs[1], 2048 * 1024)
    nb = hs[0] // tr
    blk = (tr, hs[1])
    if geo.axis == 1:
        own_map = lambda i, p: (p[1] * nb + i, 0)
    else:
        own_map = lambda i, p: (i, p[1])

    def body(p_ref, a_ref, b_ref, o_ref):
        o_ref[...] = (a_ref[...].astype(F32) + b_ref[...].astype(F32)).astype(BF16)

    same = pl.BlockSpec(blk, lambda i, p: (i, 0))
    return _prefetch_call(body, pos, [dw, recv], out_shape=jax.ShapeDtypeStruct(hs, BF16), grid=(nb,),
                          in_specs=[pl.BlockSpec(blk, own_map), same], out_specs=same, name=name)


def _chip_sum(p, recv, geo, pos, name):
    ss = geo.shard_half_shape
    tr = _rows_tile(ss[0], ss[1], 1024 * 1024)
    nb = ss[0] // tr
    blk = (tr, ss[1])
    if geo.axis == 1:
        q = geo.base // geo.size
        own_map = lambda i, p_: (i, q + p_[0])
        out_map = lambda i, p_: (p_[1] * nb + i, 0)
    else:
        own_map = lambda i, p_: (p_[0] * nb + i, 0)
        out_map = lambda i, p_: (i, p_[1])

    def body(p_ref, o_ref, r_ref, out_ref):
        acc = o_ref[...].astype(F32)
        for k in range(N_CHIPS - 1):
            acc = acc + r_ref[k].astype(F32)
        out_ref[...] = acc

    return _prefetch_call(
        body, pos, [p, recv], out_shape=jax.ShapeDtypeStruct(geo.shard_shape, F32), grid=(nb,),
        in_specs=[pl.BlockSpec(blk, own_map), pl.BlockSpec((N_CHIPS - 1,) + blk, lambda i, p_: (0, i, 0))],
        out_specs=pl.BlockSpec(blk, out_map), name=name)


def _adamw_math(w, g, m, v):
    m = ADAM_B1 * m + (1.0 - ADAM_B1) * g
    v = ADAM_B2 * v + (1.0 - ADAM_B2) * (g * g)
    m_hat = m / (1.0 - ADAM_B1 ** ADAM_STEP)
    v_hat = v / (1.0 - ADAM_B2 ** ADAM_STEP)
    delta = -ADAM_LR * (m_hat / (jnp.sqrt(v_hat) + ADAM_EPS) + ADAM_WD * w)
    return delta, m, v


def _adamw_layer(l, w, g, m, v, prev, name, after=()):
    L, R, C = w.shape
    tr = _rows_tile(R, C, 512 * 1024)

    def body(w_ref, g_ref, m_ref, v_ref, *rest):
        go_ref, d_ref, mo_ref, vo_ref = rest[-4:]
        gv = g_ref[...]
        d, mn, vn = _adamw_math(w_ref[...], gv, m_ref[...], v_ref[...])
        go_ref[...] = gv
        d_ref[...] = d
        mo_ref[...] = mn
        vo_ref[...] = vn

    blk = pl.BlockSpec((None, tr, C), lambda i: (l, i, 0))
    sds = jax.ShapeDtypeStruct(w.shape, F32)
    in_specs = [blk, pl.BlockSpec((tr, C), lambda i: (i, 0)), blk, blk]
    args = [w, g, m, v]
    aliases = {}
    if prev is not None:
        in_specs += [ANY] * 4
        args += list(prev)
        aliases = {4 + i: i for i in range(4)}
    return _call(
        body, args, after, out_shape=(sds, sds, sds, sds), grid=(R // tr,), in_specs=in_specs,
        out_specs=(blk,) * 4, input_output_aliases=aliases, name=name, compiler_params=_params("parallel"))


def _adamw_sparsecore(l, w, g, m, v, name):
    L, R, C = w.shape
    bc = max(t for t in range(128, min(C, 640) + 1, 128) if C % t == 0)
    br = 8
    while br * 2 * bc <= SC_BLOCK_ELEMS and R % (br * 2) == 0:
        br *= 2
    nrb = R // br
    flat = lambda a: a.reshape(L * R, C)
    sds = jax.ShapeDtypeStruct((L * R, C), F32)

    def kern(w_hbm, g_hbm, m_hbm, v_hbm, go_hbm, d_hbm, mo_hbm, vo_hbm):
        def block(w_v, g_v, m_v, v_v, go_v, d_v, mo_v, vo_v):
            @pl.loop(0, br)
            def _(r):
                @pl.loop(0, bc, step=SC_LANES)
                def _(c):
                    at = (pl.ds(r, 1), pl.ds(c, SC_LANES))
                    gv = g_v.at[*at][...]
                    d, mn, vn = _adamw_math(w_v.at[*at][...], gv, m_v.at[*at][...], v_v.at[*at][...])
                    go_v.at[*at][...] = gv
                    d_v.at[*at][...] = d
                    mo_v.at[*at][...] = mn
                    vo_v.at[*at][...] = vn

        lay = pl.BlockSpec((br, bc), lambda i, j: (l * nrb + i, j))
        one = pl.BlockSpec((br, bc), lambda i, j: (i, j))
        pltpu.emit_pipeline(
            block, grid=(nrb, C // bc), in_specs=[lay, one, lay, lay], out_specs=[lay] * 4,
            core_axis_name=("sc_core", "sc_tile"), dimension_semantics=(pltpu.PARALLEL, pltpu.PARALLEL),
        )(w_hbm, g_hbm, m_hbm, v_hbm, go_hbm, d_hbm, mo_hbm, vo_hbm)

    outs = pl.kernel(
        kern, out_type=(sds,) * 4, name=name, scratch_types=[],
        mesh=plsc.VectorSubcoreMesh(core_axis_name="sc_core", subcore_axis_name="sc_tile"),
    )(flat(w), g, flat(m), flat(v))
    return tuple(o.reshape(L, R, C) for o in outs)


def _adamw_flat(w, g, m, v, name):
    R, C = w.shape
    tr = _tile(R, 1024) if R % 128 == 0 else R

    def body(w_ref, g_ref, m_ref, v_ref, d_ref, mo_ref, vo_ref):
        d, mn, vn = _adamw_math(w_ref[...], g_ref[...], m_ref[...], v_ref[...])
        d_ref[...] = d
        mo_ref[...] = mn
        vo_ref[...] = vn

    blk = pl.BlockSpec((tr, C), lambda i: (i, 0))
    sds = jax.ShapeDtypeStruct(w.shape, F32)
    return pl.pallas_call(
        body, out_shape=(sds, sds, sds), grid=(R // tr,), in_specs=[blk] * 4, out_specs=(blk,) * 3,
        name=name, compiler_params=_params("parallel"))(w, g, m, v)


def _allreduce_small(s, after=()):
    R, C = s.shape

    def body(s_ref, o_ref, rbuf, send_sems, recv_sems):
        x, y, c = lax.axis_index("x"), lax.axis_index("y"), lax.axis_index("c")
        peers = [(x, y, 1 - c), (1 - x, y, c), (x, 1 - y, c)]
        o_ref[...] = s_ref[...]
        for k, peer in enumerate(peers):
            cp = _remote(send_sems.at[k], recv_sems.at[k], peer)(o_ref, rbuf.at[k])
            cp.start()
            cp.wait()
            o_ref[...] = o_ref[...] + rbuf[k]

    vm = pl.BlockSpec(memory_space=pltpu.VMEM)
    return _call(
        body, [s], after, out_shape=jax.ShapeDtypeStruct((R, C), F32), in_specs=[vm], out_specs=vm,
        scratch_shapes=[pltpu.VMEM((3, R, C), F32), pltpu.SemaphoreType.DMA((3,)), pltpu.SemaphoreType.DMA((3,))],
        name="allreduce_small", compiler_params=pltpu.CompilerParams(vmem_limit_bytes=V7X_VMEM_LIMIT))


class _GradBatch:
    def __init__(self, tag, dws, geos4, entries, pos):
        self.tag, self.dws, self.geos4, self.entries, self.pos = tag, dws, geos4, entries, pos

    def start_swap(self, after):
        self.s1, self.dws, self.land1, tok = _swap_start(f"{self.tag}_swap_start", self.dws, self.geos4, after)
        return tok

    def swap_to_exchange(self, after):
        dws, lands = _swap_wait(f"{self.tag}_swap_wait", self.dws, self.land1, self.s1, self.geos4, after)
        ps = [_pair_sum(d, r, g, self.pos, f"{self.tag}_pair_sum_{i}")
              for i, (d, r, g) in enumerate(zip(dws, lands, self.geos4))]
        self.s2, self.ps, self.land2, tok = _exchange_start(f"{self.tag}_exch_start", ps, self.entries, ())
        return tok

    def exchange_to_join(self, after):
        ps, lands = _exchange_wait(f"{self.tag}_exch_wait", self.ps, self.land2, self.s2, self.entries, after)
        self.geos5 = [g for _, g in self.entries]
        gs = [_chip_sum(ps[pi], r, g, self.pos, f"{self.tag}_chip_sum_{e}")
              for e, ((pi, g), r) in enumerate(zip(self.entries, lands))]
        self.s3, self.gs, _, tok = _join_start(f"{self.tag}_join_start", gs, self.geos5, ())
        return tok

    def finish(self, after):
        return _join_wait(f"{self.tag}_join_wait", self.gs, self.s3, self.geos5, after)


def _pack(pieces):
    rows = []
    for p in pieces:
        flat = p.reshape(-1)
        pad = (-flat.shape[0]) % 1024
        rows.append(jnp.pad(flat, (0, pad)).reshape(-1, 128))
    return jnp.concatenate(rows, axis=0)


def _unpack(buf, shapes):
    out, r = [], 0
    for shp in shapes:
        n = math.prod(shp)
        nr = -(-n // 1024) * 8
        out.append(buf[r:r + nr].reshape(-1)[:n].reshape(shp))
        r += nr
    return out


def kernel(x, norm1_g, w_in, gmlp_ln_g, gmlp_ln_b, w_spatial, b_spatial, conv_w, group_norm_g, w_out, norm2_g, w_gate, w_up, w_down, final_norm_g, loss_target, m_norm1_g, m_w_in, m_gmlp_ln_g, m_gmlp_ln_b, m_w_spatial, m_b_spatial, m_conv_w, m_group_norm_g, m_w_out, m_norm2_g, m_w_gate, m_w_up, m_w_down, m_final_norm_g, v_norm1_g, v_w_in, v_gmlp_ln_g, v_gmlp_ln_b, v_w_spatial, v_b_spatial, v_conv_w, v_group_norm_g, v_w_out, v_norm2_g, v_w_gate, v_w_up, v_w_down, v_final_norm_g):
    L, D, n_in = w_in.shape
    T = x.shape[1]
    nin = N_CHIPS * n_in
    A = nin // 5
    H = A // HEAD_DIM
    n_f = w_gate.shape[2]
    F = N_CHIPS * n_f
    n_o = w_out.shape[1]
    cb = conv_w.shape[2]
    assert A == H * HEAD_DIM and T % 256 == 0 and N_CHIPS * n_o == D and N_CHIPS * cb == A

    g_in, g_out, g_down = _Geom(D, nin, 1, n_in), _Geom(D, D, 0, n_o), _Geom(F, D, 0, n_f)
    g_gate, g_up, g_gu = _Geom(D, 2 * F, 1, n_f, 0), _Geom(D, 2 * F, 1, n_f, F), _Geom(D, 2 * F, 1, n_f)
    pos = jnp.stack([2 * lax.axis_index("x") + lax.axis_index("y"), lax.axis_index("c")]).astype(jnp.int32)
    row = lambda v: v.reshape(1, -1)

    conv_full = _gather_conv(jnp.pad(conv_w, ((0, 0), (0, 8 - CONV_TAPS), (0, 0))))
    members = [[g_in], [g_out], [g_gate, g_up], [g_down]]
    sources = [[w_in], [w_out], [w_gate, w_up], [w_down]]
    placed, sems_of = {}, {}
    tok = conv_full
    for l in range(L):
        for a in range(4):
            arr = None
            for m, (w, geo) in enumerate(zip(sources[a], members[a])):
                arr = _cast_place(w, l, geo, pos, arr, f"l{l}_place_{a}_{m}", after=(tok,))
            sems_of[l, a], placed[l, a], tok = _gather_start(f"l{l}_gather_start_{a}", arr, members[a], (tok,))

    def arrive(l, a, after):
        return _gather_forward(f"l{l}_gather_fwd_{a}", placed[l, a], members[a], sems_of[l, a], after)

    def landed(l, a, fs, arr, after):
        return _gather_wait(f"l{l}_gather_wait_{a}", arr, members[a], fs, after)

    frame = jnp.arange(HEAD_DIM)
    mask = (frame[None, :] // CHUNK) <= (frame[:, None] // CHUNK)

    xs = x[0]
    acts = []
    for l in range(L):
        wm = jnp.where(mask[None], w_spatial[l], 0.0).astype(BF16)
        wmt = jnp.swapaxes(wm, 1, 2)
        bfull = jnp.broadcast_to(b_spatial[l][:, :, None], (H, HEAD_DIM, HEAD_DIM))
        sm = dict(lng=row(gmlp_ln_g[l]), lnb=row(gmlp_ln_b[l]), wm=wm, wmt=wmt, bfull=bfull,
                  cw=conv_full[l], gn=row(group_norm_g[l]))
        if l == 0:
            h = _rms_fwd(xs, row(norm1_g[l]), f"l{l}_rms1", after=(tok,))
            fs, arr, tok = arrive(l, 0, (h,))
            W_in = landed(l, 0, fs, arr, (tok,))
        else:
            fs, arr, tok = arrive(l, 0, (xs,))
            h = _rms_fwd(xs, row(norm1_g[l]), f"l{l}_rms1", after=(tok,))
            W_in = landed(l, 0, fs, arr, (h,))
        z = _mm_nn(h, W_in, out_dtype=BF16, name=f"l{l}_mm_in")
        fs, arr, tok = arrive(l, 1, (z,))
        y = _mixer_fwd(z, sm["lng"], sm["lnb"], wm, bfull, sm["cw"], sm["gn"], f"l{l}_mixer", after=(tok,))
        W_out = landed(l, 1, fs, arr, (y,))
        if l == 0:
            x1, h2 = _mm_res_rms(y, W_out, xs, row(norm2_g[l]), name=f"l{l}_mm_out")
            fs, arr, tok = arrive(l, 2, (x1,))
            W_gu = landed(l, 2, fs, arr, (tok,))
        else:
            fs, arr, tok = arrive(l, 2, (W_out,))
            x1, h2 = _mm_res_rms(y, W_out, xs, row(norm2_g[l]), name=f"l{l}_mm_out", after=(tok,))
            W_gu = landed(l, 2, fs, arr, (x1,))
        gu, act = _mm_swiglu(h2, W_gu, name=f"l{l}_mm_swiglu")
        fs, arr, tok = arrive(l, 3, (act,))
        W_down = landed(l, 3, fs, arr, (tok,))
        x2 = _mm_nn(act, W_down, res=x1, out_dtype=F32, tn=512, name=f"l{l}_mm_down")
        acts.append(dict(x=xs, h=h, z=z, y=y, x1=x1, h2=h2, gu=gu, act=act, sm=sm,
                         W_in=W_in, W_out=W_out, W_gu=W_gu, W_down=W_down))
        xs = x2

    loss_vec, dx, dxb, dgf = _loss_head(xs, row(final_norm_g), loss_target[0], "loss_head")
    loss = lax.psum(loss_vec[0, 0], ("x", "y", "c"))

    big_w = {"in": (w_in, m_w_in, v_w_in), "out": (w_out, m_w_out, v_w_out), "gate": (w_gate, m_w_gate, v_w_gate),
             "up": (w_up, m_w_up, v_w_up), "down": (w_down, m_w_down, v_w_down)}
    big = {nm: None for nm in big_w}

    def adamw(l, names, gs):
        after = ()
        for nm, g in zip(names, gs):
            w, m, v = big_w[nm]
            if l == L - 1 and L > 1:
                big[nm] = _adamw_sparsecore(l, w, g, m, v, f"l{l}_adamw_{nm}")
            else:
                big[nm] = _adamw_layer(l, w, g, m, v, big[nm], f"l{l}_adamw_{nm}", after=after)
                after = (big[nm][1],)
        return after

    small_grads = [None] * L
    pend_dg = pend_oi = None
    tok = ()
    for l in reversed(range(L)):
        a = acts[l]
        sm = a["sm"]
        dgu = _mm_nt_swiglu_bwd(dxb, a["W_down"], a["gu"], name=f"l{l}_bwd_down", after=tok)
        if pend_dg:
            tok = (pend_dg[0].exchange_to_join((dgu,)),)
        dW_down = _mm_tn(a["act"], dxb, name=f"l{l}_dw_down", after=tok)
        if pend_dg:
            tok = adamw(pend_dg[1], ["down", "gate", "up"], pend_dg[0].finish((dW_down,)))
        dh2 = _mm_nt_dgu(dgu, a["W_gu"], name=f"l{l}_bwd_gu", after=tok)
        if pend_oi:
            tok = (pend_oi[0].exchange_to_join((dh2,)),)
        dW_gu = _mm_tn(a["h2"], dgu, tkw=1024, tnw=512, name=f"l{l}_dw_gu", after=tok)
        if pend_oi:
            tok = adamw(pend_oi[1], ["out", "in"], pend_oi[0].finish((dW_gu,)))
        dg_batch = _GradBatch(f"l{l}_dg", [dW_down, dW_gu], [g_down, g_gu], [(0, g_down), (1, g_gate), (1, g_up)], pos)
        tok = (dg_batch.start_swap(tok),)
        dx1, dx1b, dg2 = _rms_bwd(dh2, a["x1"], row(norm2_g[l]), dx, f"l{l}_rms2_bwd", after=tok)
        dy = _mm_nt(dx1b, a["W_out"], out_dtype=F32, name=f"l{l}_bwd_out")
        dW_out = _mm_tn(a["y"], dx1b, name=f"l{l}_dw_out")
        tok = (dg_batch.swap_to_exchange((dW_out, dy)),)
        dz, dgn, dlng, dlnb, dws, dbs, dcw = _mixer_bwd(
            a["z"], dy, sm["lng"], sm["lnb"], sm["wm"], sm["wmt"], sm["bfull"], sm["cw"], sm["gn"],
            f"l{l}_mixer_bwd", after=tok)
        dW_in = _mm_tn(a["h"], dz, tkw=1024, name=f"l{l}_dw_in")
        oi_batch = _GradBatch(f"l{l}_oi", [dW_out, dW_in], [g_out, g_in], [(0, g_out), (1, g_in)], pos)
        tok = (oi_batch.start_swap(()),)
        dh = _mm_nt(dz, a["W_in"], out_dtype=BF16, tn=512, name=f"l{l}_bwd_in", after=tok)
        dx, dxb, dg1 = _rms_bwd(dh, a["x"], row(norm1_g[l]), dx1, f"l{l}_rms1_bwd")
        tok = (oi_batch.swap_to_exchange((dx,)),)
        small_grads[l] = [dg1[0], dlng[0], dlnb[0], dws, dbs[:, 0, :], dcw[:CONV_TAPS], dgn[0], dg2[0]]
        pend_dg, pend_oi = (dg_batch, l), (oi_batch, l)
    grad_x = dx[None]

    tok = (pend_dg[0].exchange_to_join(tok),)
    tok = adamw(pend_dg[1], ["down", "gate", "up"], pend_dg[0].finish(tok))
    tok = (pend_oi[0].exchange_to_join(tok),)
    pieces = [p for l in range(L) for p in small_grads[l]] + [dgf[0]]
    red = _allreduce_small(_pack(pieces), after=tok)
    adamw(pend_oi[1], ["out", "in"], pend_oi[0].finish((red,)))

    red_list = _unpack(red, [p.shape for p in pieces])
    per = len(small_grads[0])
    stack = lambda i: jnp.stack([red_list[l * per + i] for l in range(L)])
    g_small = {"norm1_g": stack(0), "gmlp_ln_g": stack(1), "gmlp_ln_b": stack(2), "w_spatial": stack(3),
               "b_spatial": stack(4), "group_norm_g": stack(6), "norm2_g": stack(7),
               "final_norm_g": red_list[L * per]}
    g_small["conv_w"] = lax.dynamic_slice_in_dim(stack(5), pos[0] * cb, cb, axis=2)
    small_w = {"norm1_g": (norm1_g, m_norm1_g, v_norm1_g), "gmlp_ln_g": (gmlp_ln_g, m_gmlp_ln_g, v_gmlp_ln_g),
               "gmlp_ln_b": (gmlp_ln_b, m_gmlp_ln_b, v_gmlp_ln_b), "w_spatial": (w_spatial, m_w_spatial, v_w_spatial),
               "b_spatial": (b_spatial, m_b_spatial, v_b_spatial), "conv_w": (conv_w, m_conv_w, v_conv_w),
               "group_norm_g": (group_norm_g, m_group_norm_g, v_group_norm_g),
               "norm2_g": (norm2_g, m_norm2_g, v_norm2_g), "final_norm_g": (final_norm_g, m_final_norm_g, v_final_norm_g)}
    snames = list(small_w)
    sd, smn, svn = _adamw_flat(_pack([small_w[n][0] for n in snames]), _pack([g_small[n] for n in snames]),
                               _pack([small_w[n][1] for n in snames]), _pack([small_w[n][2] for n in snames]),
                               "adamw_small")
    sshapes = [small_w[n][0].shape for n in snames]
    sd, smn, svn = _unpack(sd, sshapes), _unpack(smn, sshapes), _unpack(svn, sshapes)
    small_out = {n: (g_small[n], sd[i], smn[i], svn[i]) for i, n in enumerate(snames)}

    order = ["norm1_g", "w_in", "gmlp_ln_g", "gmlp_ln_b", "w_spatial", "b_spatial", "conv_w", "group_norm_g",
             "w_out", "norm2_g", "w_gate", "w_up", "w_down", "final_norm_g"]
    res = {n: (big[n[2:]] if n[2:] in big else small_out[n]) for n in order}
    return (loss, grad_x, *[res[n][0] for n in order], *[res[n][1] for n in order],
            *[res[n][2] for n in order], *[res[n][3] for n in order])
```

```python
import math

import jax
import jax.numpy as jnp
from jax import lax
from jax.experimental import pallas as pl
from jax.experimental.pallas import tpu as pltpu
from jax.experimental.pallas import tpu_sc as plsc

RMS_EPS = 1e-6
LN_EPS = 1e-5
HEAD_DIM = 128
CHUNK = 64
CONV_TAPS = 3
HALO = 16
ADAM_LR = 0.001
ADAM_B1 = 0.9
ADAM_B2 = 0.999
ADAM_EPS = 1e-08
ADAM_WD = 0.01
ADAM_STEP = 10
V7X_VMEM_LIMIT = 56 * 1024 * 1024
N_CHIPS = 4
EPILOGUE_ROWS = 256
SC_LANES = 16
SC_BLOCK_ELEMS = 5120
MESH = pl.DeviceIdType.MESH
F32 = jnp.float32
BF16 = jnp.bfloat16
ANY = pl.BlockSpec(memory_space=pl.ANY)
HBM = pl.BlockSpec(memory_space=pltpu.HBM)
SEM = pl.BlockSpec(memory_space=pltpu.SEMAPHORE)
EFFECT = pltpu.SideEffectType.DATAFLOW_SIDE_EFFECTING


def _tile(n, pref):
    if n <= pref:
        return n
    best = None
    for t in range(128, pref + 1, 128):
        if n % t == 0:
            best = t
    assert best is not None, (n, pref)
    return best


def _rows_tile(rows, cols, budget_elems):
    tr = rows
    while tr * cols > budget_elems and tr % 2 == 0 and (tr // 2) % 16 == 0:
        tr //= 2
    return tr


def _params(*sem):
    return pltpu.CompilerParams(dimension_semantics=sem if sem else None,
                                vmem_limit_bytes=V7X_VMEM_LIMIT)


def _call(body, args, after, **kw):
    n, na = len(args), len(after)
    if na:
        inner = body

        def body(*refs):
            inner(*refs[:n], *refs[n + na:])

        kw["in_specs"] = list(kw["in_specs"]) + [ANY] * na
    return pl.pallas_call(body, **kw)(*args, *after)


def _prefetch_call(body, pos, args, after=(), *, out_shape, grid, in_specs, out_specs, aliases=None, name):
    n, na = 1 + len(args), len(after)
    if na:
        inner = body

        def body(*refs):
            inner(*refs[:n], *refs[n + na:])

    return pl.pallas_call(
        body, out_shape=out_shape,
        grid_spec=pltpu.PrefetchScalarGridSpec(num_scalar_prefetch=1, grid=grid,
                                               in_specs=list(in_specs) + [ANY] * na, out_specs=out_specs),
        input_output_aliases=aliases or {}, name=name,
        compiler_params=_params(*(["parallel"] * len(grid))))(pos, *args, *after)


GELU_C = math.sqrt(2.0 / math.pi)
GELU_K = 0.044715


def _gelu(x):
    th = jnp.tanh(x * (GELU_C + (GELU_C * GELU_K) * (x * x)))
    hx = 0.5 * x
    return hx + hx * th


def _gelu_and_grad(x):
    x2 = x * x
    th = jnp.tanh(x * (GELU_C + (GELU_C * GELU_K) * x2))
    hx = 0.5 * x
    grad = (0.5 + 0.5 * th) + (hx * (1.0 - th * th)) * (GELU_C + (3.0 * GELU_C * GELU_K) * x2)
    return hx + hx * th, grad


def _sigmoid(x):
    return 1.0 / (1.0 + jnp.exp(-x))


def _nt(a, b):
    return lax.dot_general(a, b, (((1,), (1,)), ((), ())), preferred_element_type=F32)


def _tn(a, b):
    return lax.dot_general(a, b, (((0,), (0,)), ((), ())), preferred_element_type=F32)


def _rms_fwd(x, g, name, after=()):
    T, D = x.shape
    tr = _tile(T, 512)

    def body(x_ref, g_ref, h_ref):
        xv = x_ref[...]
        r = lax.rsqrt(jnp.mean(xv * xv, axis=-1, keepdims=True) + RMS_EPS)
        h_ref[...] = ((xv * r) * g_ref[...]).astype(h_ref.dtype)

    return _call(
        body, [x, g], after, out_shape=jax.ShapeDtypeStruct((T, D), BF16), grid=(T // tr,),
        in_specs=[pl.BlockSpec((tr, D), lambda i: (i, 0)), pl.BlockSpec((1, D), lambda i: (0, 0))],
        out_specs=pl.BlockSpec((tr, D), lambda i: (i, 0)),
        name=name, compiler_params=_params("parallel"))


def _rms_bwd(dh, x, g, dres, name, after=()):
    T, D = x.shape
    tr = _tile(T, 256)

    def body(dh_ref, x_ref, g_ref, dres_ref, dx_ref, dxb_ref, dg_ref):
        i = pl.program_id(0)
        xv = x_ref[...]
        dhv = dh_ref[...].astype(F32)
        r = lax.rsqrt(jnp.mean(xv * xv, axis=-1, keepdims=True) + RMS_EPS)
        xh = xv * r
        q = dhv * g_ref[...]
        dx = dres_ref[...] + r * (q - xh * jnp.mean(q * xh, axis=-1, keepdims=True))
        dx_ref[...] = dx
        dxb_ref[...] = dx.astype(BF16)
        part = jnp.sum(dhv * xh, axis=0, keepdims=True)

        @pl.when(i == 0)
        def _():
            dg_ref[...] = part

        @pl.when(i > 0)
        def _():
            dg_ref[...] += part

    row = pl.BlockSpec((tr, D), lambda i: (i, 0))
    vec = pl.BlockSpec((1, D), lambda i: (0, 0))
    return _call(
        body, [dh, x, g, dres], after,
        out_shape=(jax.ShapeDtypeStruct((T, D), F32), jax.ShapeDtypeStruct((T, D), BF16),
                   jax.ShapeDtypeStruct((1, D), F32)),
        grid=(T // tr,), in_specs=[row, row, vec, row], out_specs=(row, row, vec),
        name=name, compiler_params=_params("arbitrary"))


def _loss_head(x, g, tgt, name):
    T, D = x.shape
    tr = _tile(T, 256)

    def body(x_ref, g_ref, t_ref, loss_ref, dx_ref, dxb_ref, dg_ref):
        i = pl.program_id(0)
        xv = x_ref[...]
        gv = g_ref[...]
        r = lax.rsqrt(jnp.mean(xv * xv, axis=-1, keepdims=True) + RMS_EPS)
        xh = xv * r
        err = xh * gv - t_ref[...]
        lpart = jnp.full((1, 128), 0.5 * jnp.sum(jnp.mean(err * err, axis=-1, keepdims=True)), F32)
        dy = err * (1.0 / D)
        q = dy * gv
        dx = r * (q - xh * jnp.mean(q * xh, axis=-1, keepdims=True))
        dx_ref[...] = dx
        dxb_ref[...] = dx.astype(BF16)
        gpart = jnp.sum(dy * xh, axis=0, keepdims=True)

        @pl.when(i == 0)
        def _():
            loss_ref[...] = lpart
            dg_ref[...] = gpart

        @pl.when(i > 0)
        def _():
            loss_ref[...] += lpart
            dg_ref[...] += gpart

    row = pl.BlockSpec((tr, D), lambda i: (i, 0))
    vec = pl.BlockSpec((1, D), lambda i: (0, 0))
    return pl.pallas_call(
        body,
        out_shape=(jax.ShapeDtypeStruct((1, 128), F32), jax.ShapeDtypeStruct((T, D), F32),
                   jax.ShapeDtypeStruct((T, D), BF16), jax.ShapeDtypeStruct((1, D), F32)),
        grid=(T // tr,), in_specs=[row, vec, row],
        out_specs=(pl.BlockSpec((1, 128), lambda i: (0, 0)), row, row, vec),
        name=name, compiler_params=_params("arbitrary"))(x, g, tgt)


def _mm_nn(a, w, *, res=None, out_dtype, tm=1024, tn=1024, tk=None, name, after=()):
    M, K = a.shape
    N = w.shape[1]
    tm, tn = _tile(M, tm), _tile(N, tn)
    tk = K if tk is None else _tile(K, tk)
    nk = K // tk
    has_res = res is not None

    def body(*refs):
        a_ref, w_ref = refs[0], refs[1]
        r_ref = refs[2] if has_res else None
        o_ref = refs[2 + has_res]
        part = jnp.dot(a_ref[...], w_ref[...], preferred_element_type=F32)

        def finish(acc):
            if has_res:
                acc = r_ref[...] + acc
            o_ref[...] = acc.astype(o_ref.dtype)

        if nk == 1:
            finish(part)
        else:
            acc_ref = refs[3 + has_res]
            k = pl.program_id(2)

            @pl.when(k == 0)
            def _():
                acc_ref[...] = part

            @pl.when(jnp.logical_and(k > 0, k < nk - 1))
            def _():
                acc_ref[...] += part

            @pl.when(k == nk - 1)
            def _():
                finish(acc_ref[...] + part)

    in_specs = [pl.BlockSpec((tm, tk), lambda i, j, k: (i, k)),
                pl.BlockSpec((tk, tn), lambda i, j, k: (k, j))]
    args = [a, w]
    if has_res:
        in_specs.append(pl.BlockSpec((tm, tn), lambda i, j, k: (i, j)))
        args.append(res)
    return _call(
        body, args, after, out_shape=jax.ShapeDtypeStruct((M, N), out_dtype), grid=(M // tm, N // tn, nk),
        in_specs=in_specs, out_specs=pl.BlockSpec((tm, tn), lambda i, j, k: (i, j)),
        scratch_shapes=[pltpu.VMEM((tm, tn), F32)] if nk > 1 else [],
        name=name, compiler_params=_params("parallel", "parallel", "arbitrary"))


def _mm_res_rms(a, w, res, g, *, tm=512, name, after=()):
    M, K = a.shape
    N = w.shape[1]
    tm = _tile(M, tm)

    def body(a_ref, w_ref, r_ref, g_ref, x_ref, h_ref):
        xv = r_ref[...] + jnp.dot(a_ref[...], w_ref[...], preferred_element_type=F32)
        x_ref[...] = xv
        r = lax.rsqrt(jnp.mean(xv * xv, axis=-1, keepdims=True) + RMS_EPS)
        h_ref[...] = ((xv * r) * g_ref[...]).astype(BF16)

    row = lambda n: pl.BlockSpec((tm, n), lambda i: (i, 0))
    return _call(
        body, [a, w, res, g], after,
        out_shape=(jax.ShapeDtypeStruct((M, N), F32), jax.ShapeDtypeStruct((M, N), BF16)), grid=(M // tm,),
        in_specs=[row(K), pl.BlockSpec((K, N), lambda i: (0, 0)), row(N), pl.BlockSpec((1, N), lambda i: (0, 0))],
        out_specs=(row(N), row(N)), name=name, compiler_params=_params("parallel"))


def _mm_swiglu(h, wgu, *, tm=2048, tn=512, name):
    T, D = h.shape
    F = wgu.shape[1] // 2
    tm, tn = _tile(T, tm), _tile(F, tn)
    nf = F // tn

    rc = _tile(tm, EPILOGUE_ROWS)

    def body(h_ref, wg_ref, wu_ref, fac_ref, act_ref):
        for r in range(tm // rc):
            rows = slice(r * rc, (r + 1) * rc)
            hv = h_ref[rows, :]
            g = jnp.dot(hv, wg_ref[...], preferred_element_type=F32)
            u = jnp.dot(hv, wu_ref[...], preferred_element_type=F32)
            s = _sigmoid(g)
            t = g * s
            act_ref[rows, :] = (t * u).astype(BF16)
            fac_ref[0, rows, :] = (u * (s + t * (1.0 - s))).astype(BF16)
            fac_ref[1, rows, :] = t.astype(BF16)

    return _call(
        body, [h, wgu, wgu], (),
        out_shape=(jax.ShapeDtypeStruct((2, T, F), BF16), jax.ShapeDtypeStruct((T, F), BF16)),
        grid=(T // tm, nf),
        in_specs=[pl.BlockSpec((tm, D), lambda i, j: (i, 0)),
                  pl.BlockSpec((D, tn), lambda i, j: (0, j)),
                  pl.BlockSpec((D, tn), lambda i, j: (0, j + nf))],
        out_specs=(pl.BlockSpec((2, tm, tn), lambda i, j: (0, i, j)),
                   pl.BlockSpec((tm, tn), lambda i, j: (i, j))),
        name=name, compiler_params=_params("parallel", "parallel"))


def _mm_nt(a, w, *, out_dtype, tm=1024, tn=1024, tk=None, name, after=()):
    M, K = a.shape
    N = w.shape[0]
    tm, tn = _tile(M, tm), _tile(N, tn)
    tk = K if tk is None else _tile(K, tk)
    nk = K // tk

    def body(*refs):
        a_ref, w_ref, o_ref = refs[0], refs[1], refs[2]
        part = _nt(a_ref[...], w_ref[...])
        if nk == 1:
            o_ref[...] = part.astype(o_ref.dtype)
        else:
            acc_ref = refs[3]
            k = pl.program_id(2)

            @pl.when(k == 0)
            def _():
                acc_ref[...] = part

            @pl.when(jnp.logical_and(k > 0, k < nk - 1))
            def _():
                acc_ref[...] += part

            @pl.when(k == nk - 1)
            def _():
                o_ref[...] = (acc_ref[...] + part).astype(o_ref.dtype)

    return _call(
        body, [a, w], after, out_shape=jax.ShapeDtypeStruct((M, N), out_dtype), grid=(M // tm, N // tn, nk),
        in_specs=[pl.BlockSpec((tm, tk), lambda i, j, k: (i, k)),
                  pl.BlockSpec((tn, tk), lambda i, j, k: (j, k))],
        out_specs=pl.BlockSpec((tm, tn), lambda i, j, k: (i, j)),
        scratch_shapes=[pltpu.VMEM((tm, tn), F32)] if nk > 1 else [],
        name=name, compiler_params=_params("parallel", "parallel", "arbitrary"))


def _mm_nt_swiglu_bwd(dxb, wdown, gu, *, tm=2048, tn=512, name, after=()):
    T, D = dxb.shape
    F = wdown.shape[0]
    tm, tn = _tile(T, tm), _tile(F, tn)

    def body(dx_ref, w_ref, fac_ref, dgu_ref):
        da = _nt(dx_ref[...], w_ref[...])
        dgu_ref[0] = (da * fac_ref[0].astype(F32)).astype(BF16)
        dgu_ref[1] = (da * fac_ref[1].astype(F32)).astype(BF16)

    blk3 = pl.BlockSpec((2, tm, tn), lambda i, j: (0, i, j))
    return _call(
        body, [dxb, wdown, gu], after, out_shape=jax.ShapeDtypeStruct((2, T, F), BF16), grid=(T // tm, F // tn),
        in_specs=[pl.BlockSpec((tm, D), lambda i, j: (i, 0)),
                  pl.BlockSpec((tn, D), lambda i, j: (j, 0)), blk3],
        out_specs=blk3, name=name, compiler_params=_params("parallel", "parallel"))


def _mm_nt_dgu(dgu, wgu, *, tm=1024, tn=1024, tk=2816, name, after=()):
    _, T, F = dgu.shape
    D = wgu.shape[0]
    tm, tn, tk = _tile(T, tm), _tile(D, tn), _tile(F, tk)
    nkf = F // tk
    nk = 2 * nkf

    def body(a_ref, w_ref, o_ref, acc_ref):
        k = pl.program_id(2)
        part = _nt(a_ref[...], w_ref[...])

        @pl.when(k == 0)
        def _():
            acc_ref[...] = part

        @pl.when(jnp.logical_and(k > 0, k < nk - 1))
        def _():
            acc_ref[...] += part

        @pl.when(k == nk - 1)
        def _():
            o_ref[...] = (acc_ref[...] + part).astype(o_ref.dtype)

    return _call(
        body, [dgu, wgu], after, out_shape=jax.ShapeDtypeStruct((T, D), BF16), grid=(T // tm, D // tn, nk),
        in_specs=[pl.BlockSpec((None, tm, tk), lambda i, j, k: (k // nkf, i, k % nkf)),
                  pl.BlockSpec((tn, tk), lambda i, j, k: (j, k))],
        out_specs=pl.BlockSpec((tm, tn), lambda i, j, k: (i, j)),
        scratch_shapes=[pltpu.VMEM((tm, tn), F32)],
        name=name, compiler_params=_params("parallel", "parallel", "arbitrary"))


def _mm_tn(a, g, *, tkw=512, tnw=1024, name, after=()):
    T, Kw = a.shape
    pair = g.ndim == 3
    Nw = 2 * g.shape[2] if pair else g.shape[1]
    tkw = _tile(Kw, tkw)
    tnw = _tile(g.shape[2] if pair else Nw, tnw)
    nf = (Nw // 2) // tnw if pair else 0

    def body(a_ref, g_ref, o_ref):
        o_ref[...] = _tn(a_ref[...], g_ref[...]).astype(o_ref.dtype)

    if pair:
        g_spec = pl.BlockSpec((None, T, tnw), lambda i, j: (j // nf, 0, j % nf))
    else:
        g_spec = pl.BlockSpec((T, tnw), lambda i, j: (0, j))
    return _call(
        body, [a, g], after, out_shape=jax.ShapeDtypeStruct((Kw, Nw), BF16), grid=(Kw // tkw, Nw // tnw),
        in_specs=[pl.BlockSpec((T, tkw), lambda i, j: (0, i)), g_spec],
        out_specs=pl.BlockSpec((tkw, tnw), lambda i, j: (i, j)),
        name=name, compiler_params=_params("parallel", "parallel"))


def _mixer_specs(T, A, tr):
    nin = 5 * A
    nb = tr // HALO
    last = T // HALO - 1
    prev = pl.BlockSpec((HALO, nin), lambda i: (jnp.maximum(i * nb - 1, 0), 0))
    cur = pl.BlockSpec((tr, nin), lambda i: (i, 0))
    nxt = pl.BlockSpec((HALO, nin), lambda i: (jnp.minimum((i + 1) * nb, last), 0))
    return prev, cur, nxt


def _conv(p_ext, cw_ref):
    return (cw_ref[2:3, :] * p_ext + cw_ref[1:2, :] * pltpu.roll(p_ext, 1, 0)
            + cw_ref[0:1, :] * pltpu.roll(p_ext, 2, 0))


def _mixer_fwd(z, lng, lnb, wm, bfull, cw, gn, name, after=()):
    T, nin = z.shape
    A = nin // 5
    H = A // HEAD_DIM
    tr = _tile(T, 256)
    nblk = tr // HEAD_DIM

    def body(zp_ref, z_ref, lng_ref, lnb_ref, wm_ref, bf_ref, cw_ref, gn_ref, y_ref, vn_ref, mix_ref):
        i = pl.program_id(0)
        u = _gelu(z_ref[:, 0:A].astype(F32))
        vg = _gelu(z_ref[:, A:2 * A].astype(F32))
        xc = vg - jnp.mean(vg, axis=-1, keepdims=True)
        rstd = lax.rsqrt(jnp.mean(xc * xc, axis=-1, keepdims=True) + LN_EPS)
        vn_ref[...] = ((xc * rstd) * lng_ref[...] + lnb_ref[...]).astype(BF16)
        for cb in range(nblk):
            rows = slice(cb * HEAD_DIM, (cb + 1) * HEAD_DIM)
            for h in range(H):
                cols = slice(h * HEAD_DIM, (h + 1) * HEAD_DIM)
                mix_ref[rows, cols] = jnp.dot(wm_ref[h], vn_ref[rows, cols],
                                              preferred_element_type=F32) + bf_ref[h]
        ya = u * mix_ref[...]
        ra = lax.rsqrt(jnp.mean(ya * ya, axis=-1, keepdims=True) + RMS_EPS)
        y_ref[:, 0:A] = ((ya * ra) * gn_ref[:, 0:A]).astype(BF16)

        p_prev = zp_ref[:, 3 * A:4 * A].astype(F32) * zp_ref[:, 4 * A:5 * A].astype(F32)
        p_prev = jnp.where(i > 0, p_prev, 0.0)
        p_cur = z_ref[:, 3 * A:4 * A].astype(F32) * z_ref[:, 4 * A:5 * A].astype(F32)
        cv = _conv(jnp.concatenate([p_prev, p_cur], axis=0), cw_ref)[HALO:]
        yb = z_ref[:, 2 * A:3 * A].astype(F32) * cv
        rb = lax.rsqrt(jnp.mean(yb * yb, axis=-1, keepdims=True) + RMS_EPS)
        y_ref[:, A:2 * A] = ((yb * rb) * gn_ref[:, A:2 * A]).astype(BF16)

    prev, cur, _ = _mixer_specs(T, A, tr)
    full = lambda shape: pl.BlockSpec(shape, lambda i: (0,) * len(shape))
    return _call(
        body, [z, z, lng, lnb, wm, bfull, cw, gn], after,
        out_shape=jax.ShapeDtypeStruct((T, 2 * A), BF16), grid=(T // tr,),
        in_specs=[prev, cur, full((1, A)), full((1, A)), full((H, HEAD_DIM, HEAD_DIM)),
                  full((H, HEAD_DIM, HEAD_DIM)), full((8, A)), full((1, 2 * A))],
        out_specs=pl.BlockSpec((tr, 2 * A), lambda i: (i, 0)),
        scratch_shapes=[pltpu.VMEM((tr, A), BF16), pltpu.VMEM((tr, A), F32)],
        name=name, compiler_params=_params("parallel"))


def _mixer_bwd(z, dy, lng, lnb, wm, wmt, bfull, cw, gn, name, after=()):
    T, nin = z.shape
    A = nin // 5
    H = A // HEAD_DIM
    tr = _tile(T, 256)
    nblk = tr // HEAD_DIM
    ngrid = T // tr
    next_ = tr + 2 * HALO

    def body(zp_ref, z_ref, zn_ref, dy_ref, dyn_ref, lng_ref, lnb_ref, wm_ref, wmt_ref, bf_ref, cw_ref, gn_ref,
             dz_ref, dgn_ref, dlng_ref, dlnb_ref, dws_ref, dbs_ref, dcw_ref,
             vn_ref, mix_ref, dmix_ref, dvn_ref):
        i = pl.program_id(0)

        @pl.when(i == 0)
        def _():
            dgn_ref[...] = jnp.zeros_like(dgn_ref)
            dlng_ref[...] = jnp.zeros_like(dlng_ref)
            dlnb_ref[...] = jnp.zeros_like(dlnb_ref)
            dws_ref[...] = jnp.zeros_like(dws_ref)
            dbs_ref[...] = jnp.zeros_like(dbs_ref)
            dcw_ref[...] = jnp.zeros_like(dcw_ref)

        u, du_dz = _gelu_and_grad(z_ref[:, 0:A].astype(F32))
        vg, dv_dz = _gelu_and_grad(z_ref[:, A:2 * A].astype(F32))
        xc = vg - jnp.mean(vg, axis=-1, keepdims=True)
        rstd = lax.rsqrt(jnp.mean(xc * xc, axis=-1, keepdims=True) + LN_EPS)
        vhat = xc * rstd
        vn_ref[...] = (vhat * lng_ref[...] + lnb_ref[...]).astype(BF16)
        for cb in range(nblk):
            rows = slice(cb * HEAD_DIM, (cb + 1) * HEAD_DIM)
            for h in range(H):
                cols = slice(h * HEAD_DIM, (h + 1) * HEAD_DIM)
                mix_ref[rows, cols] = jnp.dot(wm_ref[h], vn_ref[rows, cols],
                                              preferred_element_type=F32) + bf_ref[h]
        mixed = mix_ref[...]
        ya = u * mixed
        ra = lax.rsqrt(jnp.mean(ya * ya, axis=-1, keepdims=True) + RMS_EPS)
        yha = ya * ra
        dyan = dy_ref[:, 0:A]
        dgn_ref[:, 0:A] += jnp.sum(dyan * yha, axis=0, keepdims=True)
        qa = dyan * gn_ref[:, 0:A]
        dya = ra * (qa - yha * jnp.mean(qa * yha, axis=-1, keepdims=True))
        dz_ref[:, 0:A] = ((dya * mixed) * du_dz).astype(BF16)
        dmix_ref[...] = (dya * u).astype(BF16)

        ii = lax.broadcasted_iota(jnp.int32, (HEAD_DIM, HEAD_DIM), 0)
        jj = lax.broadcasted_iota(jnp.int32, (HEAD_DIM, HEAD_DIM), 1)
        mask = (jj // CHUNK <= ii // CHUNK).astype(F32)
        ones = jnp.ones((8, HEAD_DIM), BF16)
        for h in range(H):
            cols = slice(h * HEAD_DIM, (h + 1) * HEAD_DIM)
            dws = jnp.zeros((HEAD_DIM, HEAD_DIM), F32)
            dbs = jnp.zeros((8, HEAD_DIM), F32)
            for cb in range(nblk):
                rows = slice(cb * HEAD_DIM, (cb + 1) * HEAD_DIM)
                dm = dmix_ref[rows, cols]
                dws = dws + _nt(dm, vn_ref[rows, cols])
                dbs = dbs + _nt(ones, dm)
                dvn_ref[rows, cols] = jnp.dot(wmt_ref[h], dm, preferred_element_type=F32)
            dws_ref[h] += dws * mask
            dbs_ref[h] += dbs
        dvn = dvn_ref[...]
        dlnb_ref[...] += jnp.sum(dvn, axis=0, keepdims=True)
        dlng_ref[...] += jnp.sum(dvn * vhat, axis=0, keepdims=True)
        dvh = dvn * lng_ref[...]
        dvg = rstd * (dvh - jnp.mean(dvh, axis=-1, keepdims=True)
                      - vhat * jnp.mean(dvh * vhat, axis=-1, keepdims=True))
        dz_ref[:, A:2 * A] = (dvg * dv_dz).astype(BF16)

        def ext(lo):
            mid = z_ref[:, lo:lo + A].astype(F32)
            return jnp.concatenate([zp_ref[:, lo:lo + A].astype(F32), mid, zn_ref[:, lo:lo + A].astype(F32)], axis=0)

        zb, zc, zh = ext(2 * A), ext(3 * A), ext(4 * A)
        row = lax.broadcasted_iota(jnp.int32, (next_, 1), 0)
        p = zc * zh
        p = jnp.where(jnp.logical_and(row < HALO, i == 0), 0.0, p)
        cv = _conv(p, cw_ref)
        yb = zb * cv
        rb = lax.rsqrt(jnp.mean(yb * yb, axis=-1, keepdims=True) + RMS_EPS)
        yhb = yb * rb
        dyn_rows = jnp.where(i < ngrid - 1, dyn_ref[:, A:2 * A], 0.0)
        dybn = jnp.concatenate([jnp.zeros((HALO, A), F32), dy_ref[:, A:2 * A], dyn_rows], axis=0)
        ctr = slice(HALO, HALO + tr)
        dgn_ref[:, A:2 * A] += jnp.sum((dybn * yhb)[ctr], axis=0, keepdims=True)
        qb = dybn * gn_ref[:, A:2 * A]
        dyb = rb * (qb - yhb * jnp.mean(qb * yhb, axis=-1, keepdims=True))
        dcv = dyb * zb
        dp = (cw_ref[2:3, :] * dcv + cw_ref[1:2, :] * pltpu.roll(dcv, next_ - 1, 0)
              + cw_ref[0:1, :] * pltpu.roll(dcv, next_ - 2, 0))
        dz_ref[:, 2 * A:3 * A] = (dyb * cv)[ctr].astype(BF16)
        dz_ref[:, 3 * A:4 * A] = (dp * zh)[ctr].astype(BF16)
        dz_ref[:, 4 * A:5 * A] = (dp * zc)[ctr].astype(BF16)
        dcw_ref[2:3, :] += jnp.sum((dcv * p)[ctr], axis=0, keepdims=True)
        dcw_ref[1:2, :] += jnp.sum((dcv * pltpu.roll(p, 1, 0))[ctr], axis=0, keepdims=True)
        dcw_ref[0:1, :] += jnp.sum((dcv * pltpu.roll(p, 2, 0))[ctr], axis=0, keepdims=True)

    prev, cur, nxt = _mixer_specs(T, A, tr)
    nb = tr // HALO
    dy_cur = pl.BlockSpec((tr, 2 * A), lambda i: (i, 0))
    dy_nxt = pl.BlockSpec((HALO, 2 * A), lambda i: (jnp.minimum((i + 1) * nb, T // HALO - 1), 0))
    full = lambda shape: pl.BlockSpec(shape, lambda i: (0,) * len(shape))
    hh = (H, HEAD_DIM, HEAD_DIM)
    return _call(
        body, [z, z, z, dy, dy, lng, lnb, wm, wmt, bfull, cw, gn], after,
        out_shape=(jax.ShapeDtypeStruct((T, nin), BF16), jax.ShapeDtypeStruct((1, 2 * A), F32),
                   jax.ShapeDtypeStruct((1, A), F32), jax.ShapeDtypeStruct((1, A), F32),
                   jax.ShapeDtypeStruct(hh, F32), jax.ShapeDtypeStruct((H, 8, HEAD_DIM), F32),
                   jax.ShapeDtypeStruct((8, A), F32)),
        grid=(ngrid,),
        in_specs=[prev, cur, nxt, dy_cur, dy_nxt, full((1, A)), full((1, A)), full(hh), full(hh), full(hh),
                  full((8, A)), full((1, 2 * A))],
        out_specs=(pl.BlockSpec((tr, nin), lambda i: (i, 0)), full((1, 2 * A)), full((1, A)), full((1, A)),
                   full(hh), full((H, 8, HEAD_DIM)), full((8, A))),
        scratch_shapes=[pltpu.VMEM((tr, A), BF16), pltpu.VMEM((tr, A), F32), pltpu.VMEM((tr, A), BF16),
                        pltpu.VMEM((tr, A), F32)],
        name=name, compiler_params=_params("arbitrary"))


class _Geom:
    def __init__(self, rows, cols, axis, size, base=0):
        self.rows, self.cols, self.axis, self.size, self.base = rows, cols, axis, size, base

    def in_full(self, j, h):
        if self.axis == 1:
            return (h * (self.rows // 2), self.rows // 2), (self.base + j * self.size, self.size)
        return (self.base + j * self.size, self.size), (h * (self.cols // 2), self.cols // 2)

    def in_shard(self, h):
        if self.axis == 1:
            return (h * (self.rows // 2), self.rows // 2), (0, self.size)
        return (0, self.size), (h * (self.cols // 2), self.cols // 2)

    def half_of_full(self, h):
        if self.axis == 1:
            return (h * (self.rows // 2), self.rows // 2), (0, self.cols)
        return (0, self.rows), (h * (self.cols // 2), self.cols // 2)

    def in_half(self, j):
        if self.axis == 1:
            return (0, self.rows // 2), (self.base + j * self.size, self.size)
        return (self.base + j * self.size, self.size), (0, self.cols // 2)

    @property
    def half_shape(self):
        return (self.rows // 2, self.cols) if self.axis == 1 else (self.rows, self.cols // 2)

    @property
    def shard_half_shape(self):
        return (self.rows // 2, self.size) if self.axis == 1 else (self.size, self.cols // 2)

    @property
    def shard_shape(self):
        return (self.rows, self.size) if self.axis == 1 else (self.size, self.cols)


def _at(ref, region):
    (r0, rn), (c0, cn) = region
    if not isinstance(r0, int):
        r0 = pl.multiple_of(r0, 16)
    if not isinstance(c0, int):
        c0 = pl.multiple_of(c0, 128)
    return ref.at[pl.ds(r0, rn), pl.ds(c0, cn)]


def _whole(shape):
    return (0, shape[-2]), (0, shape[-1])


class _Chunked:
    def __init__(self, make, src, src_reg, dst, dst_reg):
        self.make, self.src, self.src_reg, self.dst, self.dst_reg = make, src, src_reg, dst, dst_reg

    @property
    def whole(self):
        return self.make(_at(self.src, self.src_reg), _at(self.dst, self.dst_reg))

    def start(self):
        self.whole.start()


def _mesh_place():
    x, y, c = lax.axis_index("x"), lax.axis_index("y"), lax.axis_index("c")
    chips = [(1 - x, y), (x, 1 - y), (1 - x, 1 - y)]
    return x, y, c, 2 * x + y, chips


def _remote(ssem, rsem, dev):
    return lambda src, dst: pltpu.make_async_remote_copy(
        src_ref=src, dst_ref=dst, send_sem=ssem, recv_sem=rsem, device_id=dev, device_id_type=MESH)


def _comm_call(name, body, arrays, sems_in=(), after=(), sems_out=(), new=()):
    na, ns, nf, no, nn = len(arrays), len(sems_in), len(after), len(sems_out), len(new)

    def kern(*refs):
        sin = refs[na:na + ns]
        outs = refs[na + ns + nf:]
        body(outs[no:no + na], outs[no + na:no + na + nn], sin, outs[:no])
        outs[-1][...] = jnp.zeros_like(outs[-1])

    out_shape = (tuple(pltpu.SemaphoreType.DMA((n,)) for n in sems_out)
                 + tuple(pltpu.HBM(a.shape, a.dtype) for a in arrays)
                 + tuple(pltpu.HBM(s, d) for s, d in new)
                 + (jax.ShapeDtypeStruct((8, 128), F32),))
    res = pl.pallas_call(
        kern, out_shape=out_shape, in_specs=[HBM] * na + [SEM] * ns + [ANY] * nf,
        out_specs=(SEM,) * no + (HBM,) * (na + nn) + (pl.BlockSpec(memory_space=pltpu.VMEM),),
        input_output_aliases={i: no + i for i in range(na)}, name=name,
        compiler_params=pltpu.CompilerParams(has_side_effects=EFFECT),
    )(*[pltpu.with_memory_space_constraint(a, pltpu.HBM) for a in arrays], *sems_in, *after)
    return list(res[:no]), list(res[no:no + na]), list(res[no + na:no + na + nn]), res[-1]


def _gather_start(name, arr, members, after):
    def body(arrs, news, sin, sout):
        x, y, c, j, chips = _mesh_place()
        for m, geo in enumerate(members):
            reg = geo.in_full(j, c)
            for k, chip in enumerate(chips):
                _Chunked(_remote(sout[2 * m].at[k], sout[2 * m + 1].at[k], (*chip, c)),
                         arrs[0], reg, arrs[0], reg).start()

    sems, arrs, _, tok = _comm_call(name, body, [arr], after=after, sems_out=[N_CHIPS - 1] * (2 * len(members)))
    return [sems[2 * m:2 * m + 2] for m in range(len(members))], arrs[0], tok


def _gather_forward(name, arr, members, sems, after):
    def body(arrs, news, sin, sout):
        x, y, c, j, chips = _mesh_place()
        sibling = (x, y, 1 - c)
        full = arrs[0]
        for m, geo in enumerate(members):
            mine = geo.in_full(j, c)
            for k, chip in enumerate(chips):
                got = geo.in_full(2 * chip[0] + chip[1], c)
                sent = _Chunked(_remote(sin[2 * m].at[k], sin[2 * m + 1].at[k], sibling), full, mine, full, got)
                sent.whole.wait_send()
                sent.whole.wait_recv()
                _Chunked(_remote(sout[2 * m].at[k], sout[2 * m + 1].at[k], sibling), full, got, full, got).start()

    flat = [s for pair in sems for s in pair]
    fs, arrs, _, tok = _comm_call(name, body, [arr], sems_in=flat, after=after,
                                  sems_out=[N_CHIPS - 1] * (2 * len(members)))
    return fs, arrs[0], tok


def _gather_wait(name, arr, members, fsems, after):
    def body(arrs, news, sin, sout):
        x, y, c, j, chips = _mesh_place()
        sibling = (x, y, 1 - c)
        full = arrs[0]
        for m, geo in enumerate(members):
            for k, chip in enumerate(chips):
                jk = 2 * chip[0] + chip[1]
                cp = _Chunked(_remote(sin[2 * m].at[k], sin[2 * m + 1].at[k], sibling),
                              full, geo.in_full(jk, c), full, geo.in_full(jk, 1 - c))
                cp.whole.wait_send()
                cp.whole.wait_recv()

    _, arrs, _, _ = _comm_call(name, body, [arr], sems_in=fsems, after=after)
    return arrs[0]


def _gather_conv(conv):
    L, _, cb = conv.shape
    nk = N_CHIPS - 1

    def body(s_ref, f_ref, send_sems, recv_sems, local_sem):
        x, y, c, j, chips = _mesh_place()
        at = lambda jj: f_ref.at[:, :, pl.ds(pl.multiple_of(jj * cb, 128), cb)]
        lc = pltpu.make_async_copy(s_ref, at(j), local_sem.at[0])
        lc.start()
        cps = []
        for k, chip in enumerate(chips):
            cp = _remote(send_sems.at[k], recv_sems.at[k], (*chip, c))(s_ref, at(j))
            cp.start()
            cps.append(cp)
        for k, chip in enumerate(chips):
            jk = 2 * chip[0] + chip[1]
            cps[k].wait_send()
            _remote(send_sems.at[k], recv_sems.at[k], (*chip, c))(at(jk), at(jk)).wait_recv()
        lc.wait()

    return pl.pallas_call(
        body, out_shape=jax.ShapeDtypeStruct((L, 8, N_CHIPS * cb), F32), in_specs=[ANY], out_specs=ANY,
        scratch_shapes=[pltpu.SemaphoreType.DMA((nk,)), pltpu.SemaphoreType.DMA((nk,)),
                        pltpu.SemaphoreType.DMA((1,))],
        name="gather_conv")(conv)


def _swap_start(name, dws, geos, after):
    n = len(dws)

    def body(arrs, news, sin, sout):
        x, y, c, _, _ = _mesh_place()
        for t in range(n):
            _Chunked(_remote(sout[0].at[t], sout[1].at[t], (x, y, 1 - c)),
                     arrs[t], geos[t].half_of_full(1 - c), news[t], _whole(news[t].shape)).start()

    return _comm_call(name, body, dws, after=after, sems_out=[n, n], new=[(g.half_shape, BF16) for g in geos])


def _swap_wait(name, dws, lands, sems, geos, after):
    n = len(dws)

    def body(arrs, news, sin, sout):
        x, y, c, _, _ = _mesh_place()
        for t in range(n):
            cp = _Chunked(_remote(sin[0].at[t], sin[1].at[t], (x, y, 1 - c)),
                          arrs[t], geos[t].half_of_full(1 - c), arrs[n + t], _whole(arrs[n + t].shape))
            cp.whole.wait_send()
            cp.whole.wait_recv()

    _, arrs, _, _ = _comm_call(name, body, list(dws) + list(lands), sems_in=sems, after=after)
    return arrs[:n], arrs[n:]


def _exchange_start(name, ps, entries, after):
    nk = N_CHIPS - 1

    def body(arrs, news, sin, sout):
        x, y, c, j, chips = _mesh_place()
        for e, (pi, geo) in enumerate(entries):
            for k, chip in enumerate(chips):
                dst = news[e].at[k]
                _Chunked(_remote(sout[0].at[nk * e + k], sout[1].at[nk * e + k], (*chip, c)),
                         arrs[pi], geo.in_half(2 * chip[0] + chip[1]), dst, _whole(dst.shape)).start()

    ne = len(entries)
    return _comm_call(name, body, ps, after=after, sems_out=[nk * ne, nk * ne],
                      new=[((nk,) + g.shard_half_shape, BF16) for _, g in entries])


def _exchange_wait(name, ps, lands, sems, entries, after):
    nk = N_CHIPS - 1
    n = len(ps)

    def body(arrs, news, sin, sout):
        x, y, c, j, chips = _mesh_place()
        for e, (pi, geo) in enumerate(entries):
            for k, chip in enumerate(chips):
                dst = arrs[n + e].at[k]
                cp = _Chunked(_remote(sin[0].at[nk * e + k], sin[1].at[nk * e + k], (*chip, c)),
                              arrs[pi], geo.in_half(2 * chip[0] + chip[1]), dst, _whole(dst.shape))
                cp.whole.wait_send()
                cp.whole.wait_recv()

    _, arrs, _, _ = _comm_call(name, body, list(ps) + list(lands), sems_in=sems, after=after)
    return arrs[:n], arrs[n:]


def _join_start(name, gs, geos, after):
    n = len(gs)

    def body(arrs, news, sin, sout):
        x, y, c, _, _ = _mesh_place()
        for t in range(n):
            mine = geos[t].in_shard(c)
            _Chunked(_remote(sout[0].at[t], sout[1].at[t], (x, y, 1 - c)), arrs[t], mine, arrs[t], mine).start()

    return _comm_call(name, body, gs, after=after, sems_out=[n, n])


def _join_wait(name, gs, sems, geos, after):
    n = len(gs)

    def body(arrs, news, sin, sout):
        x, y, c, _, _ = _mesh_place()
        for t in range(n):
            cp = _Chunked(_remote(sin[0].at[t], sin[1].at[t], (x, y, 1 - c)),
                          arrs[t], geos[t].in_shard(c), arrs[t], geos[t].in_shard(1 - c))
            cp.whole.wait_send()
            cp.whole.wait_recv()

    _, arrs, _, _ = _comm_call(name, body, gs, sems_in=sems, after=after)
    return arrs


def _cast_place(w, l, geo, pos, prev, name, after=()):
    if geo.axis == 1:
        tr = _rows_tile(geo.rows, geo.size, 1024 * 1024)
        grid = (geo.rows // tr,)
        blk = (tr, geo.size)
        q = geo.base // geo.size
        out_map = lambda i, p: (i, q + p[0])
    else:
        tr = _rows_tile(geo.size, geo.cols, 1024 * 1024)
        grid = (geo.size // tr,)
        blk = (tr, geo.cols)
        nb = geo.size // tr
        out_map = lambda i, p: (p[0] * nb + i, 0)

    def body(p_ref, w_ref, *rest):
        rest[-1][...] = w_ref[...].astype(BF16)

    in_specs = [pl.BlockSpec((None,) + blk, lambda i, p: (l, i, 0))]
    args = [w]
    aliases = None
    if prev is not None:
        in_specs.append(ANY)
        args.append(prev)
        aliases = {2: 0}
    return _prefetch_call(body, pos, args, after, out_shape=jax.ShapeDtypeStruct((geo.rows, geo.cols), BF16),
                          grid=grid, in_specs=in_specs, out_specs=pl.BlockSpec(blk, out_map), aliases=aliases,
                          name=name)


def _pair_sum(dw, recv, geo, pos, name):
    hs = geo.half_shape
    tr = _rows_tile(hs[0], hs[1], 2048 * 1024)
    nb = hs[0] // tr
    blk = (tr, hs[1])
    if geo.axis == 1:
        own_map = lambda i, p: (p[1] * nb + i, 0)
    else:
        own_map = lambda i, p: (i, p[1])

    def body(p_ref, a_ref, b_ref, o_ref):
        o_ref[...] = (a_ref[...].astype(F32) + b_ref[...].astype(F32)).astype(BF16)

    same = pl.BlockSpec(blk, lambda i, p: (i, 0))
    return _prefetch_call(body, pos, [dw, recv], out_shape=jax.ShapeDtypeStruct(hs, BF16), grid=(nb,),
                          in_specs=[pl.BlockSpec(blk, own_map), same], out_specs=same, name=name)


def _chip_sum(p, recv, geo, pos, name):
    ss = geo.shard_half_shape
    tr = _rows_tile(ss[0], ss[1], 1024 * 1024)
    nb = ss[0] // tr
    blk = (tr, ss[1])
    if geo.axis == 1:
        q = geo.base // geo.size
        own_map = lambda i, p_: (i, q + p_[0])
        out_map = lambda i, p_: (p_[1] * nb + i, 0)
    else:
        own_map = lambda i, p_: (p_[0] * nb + i, 0)
        out_map = lambda i, p_: (i, p_[1])

    def body(p_ref, o_ref, r_ref, out_ref):
        acc = o_ref[...].astype(F32)
        for k in range(N_CHIPS - 1):
            acc = acc + r_ref[k].astype(F32)
        out_ref[...] = acc

    return _prefetch_call(
        body, pos, [p, recv], out_shape=jax.ShapeDtypeStruct(geo.shard_shape, F32), grid=(nb,),
        in_specs=[pl.BlockSpec(blk, own_map), pl.BlockSpec((N_CHIPS - 1,) + blk, lambda i, p_: (0, i, 0))],
        out_specs=pl.BlockSpec(blk, out_map), name=name)


def _adamw_math(w, g, m, v):
    m = ADAM_B1 * m + (1.0 - ADAM_B1) * g
    v = ADAM_B2 * v + (1.0 - ADAM_B2) * (g * g)
    m_hat = m / (1.0 - ADAM_B1 ** ADAM_STEP)
    v_hat = v / (1.0 - ADAM_B2 ** ADAM_STEP)
    delta = -ADAM_LR * (m_hat / (jnp.sqrt(v_hat) + ADAM_EPS) + ADAM_WD * w)
    return delta, m, v


def _adamw_layer(l, w, g, m, v, prev, name, after=()):
    L, R, C = w.shape
    tr = _rows_tile(R, C, 512 * 1024)

    def body(w_ref, g_ref, m_ref, v_ref, *rest):
        go_ref, d_ref, mo_ref, vo_ref = rest[-4:]
        gv = g_ref[...]
        d, mn, vn = _adamw_math(w_ref[...], gv, m_ref[...], v_ref[...])
        go_ref[...] = gv
        d_ref[...] = d
        mo_ref[...] = mn
        vo_ref[...] = vn

    blk = pl.BlockSpec((None, tr, C), lambda i: (l, i, 0))
    sds = jax.ShapeDtypeStruct(w.shape, F32)
    in_specs = [blk, pl.BlockSpec((tr, C), lambda i: (i, 0)), blk, blk]
    args = [w, g, m, v]
    aliases = {}
    if prev is not None:
        in_specs += [ANY] * 4
        args += list(prev)
        aliases = {4 + i: i for i in range(4)}
    return _call(
        body, args, after, out_shape=(sds, sds, sds, sds), grid=(R // tr,), in_specs=in_specs,
        out_specs=(blk,) * 4, input_output_aliases=aliases, name=name, compiler_params=_params("parallel"))


def _adamw_sparsecore(l, w, g, m, v, name):
    L, R, C = w.shape
    bc = max(t for t in range(128, min(C, 640) + 1, 128) if C % t == 0)
    br = 8
    while br * 2 * bc <= SC_BLOCK_ELEMS and R % (br * 2) == 0:
        br *= 2
    nrb = R // br
    flat = lambda a: a.reshape(L * R, C)
    sds = jax.ShapeDtypeStruct((L * R, C), F32)

    def kern(w_hbm, g_hbm, m_hbm, v_hbm, go_hbm, d_hbm, mo_hbm, vo_hbm):
        def block(w_v, g_v, m_v, v_v, go_v, d_v, mo_v, vo_v):
            @pl.loop(0, br)
            def _(r):
                @pl.loop(0, bc, step=SC_LANES)
                def _(c):
                    at = (pl.ds(r, 1), pl.ds(c, SC_LANES))
                    gv = g_v.at[*at][...]
                    d, mn, vn = _adamw_math(w_v.at[*at][...], gv, m_v.at[*at][...], v_v.at[*at][...])
                    go_v.at[*at][...] = gv
                    d_v.at[*at][...] = d
                    mo_v.at[*at][...] = mn
                    vo_v.at[*at][...] = vn

        lay = pl.BlockSpec((br, bc), lambda i, j: (l * nrb + i, j))
        one = pl.BlockSpec((br, bc), lambda i, j: (i, j))
        pltpu.emit_pipeline(
            block, grid=(nrb, C // bc), in_specs=[lay, one, lay, lay], out_specs=[lay] * 4,
            core_axis_name=("sc_core", "sc_tile"), dimension_semantics=(pltpu.PARALLEL, pltpu.PARALLEL),
        )(w_hbm, g_hbm, m_hbm, v_hbm, go_hbm, d_hbm, mo_hbm, vo_hbm)

    outs = pl.kernel(
        kern, out_type=(sds,) * 4, name=name, scratch_types=[],
        mesh=plsc.VectorSubcoreMesh(core_axis_name="sc_core", subcore_axis_name="sc_tile"),
    )(flat(w), g, flat(m), flat(v))
    return tuple(o.reshape(L, R, C) for o in outs)


def _adamw_flat(w, g, m, v, name):
    R, C = w.shape
    tr = _tile(R, 1024) if R % 128 == 0 else R

    def body(w_ref, g_ref, m_ref, v_ref, d_ref, mo_ref, vo_ref):
        d, mn, vn = _adamw_math(w_ref[...], g_ref[...], m_ref[...], v_ref[...])
        d_ref[...] = d
        mo_ref[...] = mn
        vo_ref[...] = vn

    blk = pl.BlockSpec((tr, C), lambda i: (i, 0))
    sds = jax.ShapeDtypeStruct(w.shape, F32)
    return pl.pallas_call(
        body, out_shape=(sds, sds, sds), grid=(R // tr,), in_specs=[blk] * 4, out_specs=(blk,) * 3,
        name=name, compiler_params=_params("parallel"))(w, g, m, v)


def _allreduce_small(s, after=()):
    R, C = s.shape

    def body(s_ref, o_ref, rbuf, send_sems, recv_sems):
        x, y, c = lax.axis_index("x"), lax.axis_index("y"), lax.axis_index("c")
        peers = [(x, y, 1 - c), (1 - x, y, c), (x, 1 - y, c)]
        o_ref[...] = s_ref[...]
        for k, peer in enumerate(peers):
            cp = _remote(send_sems.at[k], recv_sems.at[k], peer)(o_ref, rbuf.at[k])
            cp.start()
            cp.wait()
            o_ref[...] = o_ref[...] + rbuf[k]

    vm = pl.BlockSpec(memory_space=pltpu.VMEM)
    return _call(
        body, [s], after, out_shape=jax.ShapeDtypeStruct((R, C), F32), in_specs=[vm], out_specs=vm,
        scratch_shapes=[pltpu.VMEM((3, R, C), F32), pltpu.SemaphoreType.DMA((3,)), pltpu.SemaphoreType.DMA((3,))],
        name="allreduce_small", compiler_params=pltpu.CompilerParams(vmem_limit_bytes=V7X_VMEM_LIMIT))


class _GradBatch:
    def __init__(self, tag, dws, geos4, entries, pos):
        self.tag, self.dws, self.geos4, self.entries, self.pos = tag, dws, geos4, entries, pos

    def start_swap(self, after):
        self.s1, self.dws, self.land1, tok = _swap_start(f"{self.tag}_swap_start", self.dws, self.geos4, after)
        return tok

    def swap_to_exchange(self, after):
        dws, lands = _swap_wait(f"{self.tag}_swap_wait", self.dws, self.land1, self.s1, self.geos4, after)
        ps = [_pair_sum(d, r, g, self.pos, f"{self.tag}_pair_sum_{i}")
              for i, (d, r, g) in enumerate(zip(dws, lands, self.geos4))]
        self.s2, self.ps, self.land2, tok = _exchange_start(f"{self.tag}_exch_start", ps, self.entries, ())
        return tok

    def exchange_to_join(self, after):
        ps, lands = _exchange_wait(f"{self.tag}_exch_wait", self.ps, self.land2, self.s2, self.entries, after)
        self.geos5 = [g for _, g in self.entries]
        gs = [_chip_sum(ps[pi], r, g, self.pos, f"{self.tag}_chip_sum_{e}")
              for e, ((pi, g), r) in enumerate(zip(self.entries, lands))]
        self.s3, self.gs, _, tok = _join_start(f"{self.tag}_join_start", gs, self.geos5, ())
        return tok

    def finish(self, after):
        return _join_wait(f"{self.tag}_join_wait", self.gs, self.s3, self.geos5, after)


def _pack(pieces):
    rows = []
    for p in pieces:
        flat = p.reshape(-1)
        pad = (-flat.shape[0]) % 1024
        rows.append(jnp.pad(flat, (0, pad)).reshape(-1, 128))
    return jnp.concatenate(rows, axis=0)


def _unpack(buf, shapes):
    out, r = [], 0
    for shp in shapes:
        n = math.prod(shp)
        nr = -(-n // 1024) * 8
        out.append(buf[r:r + nr].reshape(-1)[:n].reshape(shp))
        r += nr
    return out


def kernel(x, norm1_g, w_in, gmlp_ln_g, gmlp_ln_b, w_spatial, b_spatial, conv_w, group_norm_g, w_out, norm2_g, w_gate, w_up, w_down, final_norm_g, loss_target, m_norm1_g, m_w_in, m_gmlp_ln_g, m_gmlp_ln_b, m_w_spatial, m_b_spatial, m_conv_w, m_group_norm_g, m_w_out, m_norm2_g, m_w_gate, m_w_up, m_w_down, m_final_norm_g, v_norm1_g, v_w_in, v_gmlp_ln_g, v_gmlp_ln_b, v_w_spatial, v_b_spatial, v_conv_w, v_group_norm_g, v_w_out, v_norm2_g, v_w_gate, v_w_up, v_w_down, v_final_norm_g):
    L, D, n_in = w_in.shape
    T = x.shape[1]
    nin = N_CHIPS * n_in
    A = nin // 5
    H = A // HEAD_DIM
    n_f = w_gate.shape[2]
    F = N_CHIPS * n_f
    n_o = w_out.shape[1]
    cb = conv_w.shape[2]
    assert A == H * HEAD_DIM and T % 256 == 0 and N_CHIPS * n_o == D and N_CHIPS * cb == A

    g_in, g_out, g_down = _Geom(D, nin, 1, n_in), _Geom(D, D, 0, n_o), _Geom(F, D, 0, n_f)
    g_gate, g_up, g_gu = _Geom(D, 2 * F, 1, n_f, 0), _Geom(D, 2 * F, 1, n_f, F), _Geom(D, 2 * F, 1, n_f)
    pos = jnp.stack([2 * lax.axis_index("x") + lax.axis_index("y"), lax.axis_index("c")]).astype(jnp.int32)
    row = lambda v: v.reshape(1, -1)

    conv_full = _gather_conv(jnp.pad(conv_w, ((0, 0), (0, 8 - CONV_TAPS), (0, 0))))
    members = [[g_in], [g_out], [g_gate, g_up], [g_down]]
    sources = [[w_in], [w_out], [w_gate, w_up], [w_down]]
    placed, sems_of = {}, {}
    tok = conv_full
    for l in range(L):
        for a in range(4):
            arr = None
            for m, (w, geo) in enumerate(zip(sources[a], members[a])):
                arr = _cast_place(w, l, geo, pos, arr, f"l{l}_place_{a}_{m}", after=(tok,))
            sems_of[l, a], placed[l, a], tok = _gather_start(f"l{l}_gather_start_{a}", arr, members[a], (tok,))

    def arrive(l, a, after):
        return _gather_forward(f"l{l}_gather_fwd_{a}", placed[l, a], members[a], sems_of[l, a], after)

    def landed(l, a, fs, arr, after):
        return _gather_wait(f"l{l}_gather_wait_{a}", arr, members[a], fs, after)

    frame = jnp.arange(HEAD_DIM)
    mask = (frame[None, :] // CHUNK) <= (frame[:, None] // CHUNK)

    xs = x[0]
    acts = []
    for l in range(L):
        wm = jnp.where(mask[None], w_spatial[l], 0.0).astype(BF16)
        wmt = jnp.swapaxes(wm, 1, 2)
        bfull = jnp.broadcast_to(b_spatial[l][:, :, None], (H, HEAD_DIM, HEAD_DIM))
        sm = dict(lng=row(gmlp_ln_g[l]), lnb=row(gmlp_ln_b[l]), wm=wm, wmt=wmt, bfull=bfull,
                  cw=conv_full[l], gn=row(group_norm_g[l]))
        if l == 0:
            h = _rms_fwd(xs, row(norm1_g[l]), f"l{l}_rms1", after=(tok,))
            fs, arr, tok = arrive(l, 0, (h,))
            W_in = landed(l, 0, fs, arr, (tok,))
        else:
            fs, arr, tok = arrive(l, 0, (xs,))
            h = _rms_fwd(xs, row(norm1_g[l]), f"l{l}_rms1", after=(tok,))
            W_in = landed(l, 0, fs, arr, (h,))
        z = _mm_nn(h, W_in, out_dtype=BF16, name=f"l{l}_mm_in")
        fs, arr, tok = arrive(l, 1, (z,))
        y = _mixer_fwd(z, sm["lng"], sm["lnb"], wm, bfull, sm["cw"], sm["gn"], f"l{l}_mixer", after=(tok,))
        W_out = landed(l, 1, fs, arr, (y,))
        if l == 0:
            x1, h2 = _mm_res_rms(y, W_out, xs, row(norm2_g[l]), name=f"l{l}_mm_out")
            fs, arr, tok = arrive(l, 2, (x1,))
            W_gu = landed(l, 2, fs, arr, (tok,))
        else:
            fs, arr, tok = arrive(l, 2, (W_out,))
            x1, h2 = _mm_res_rms(y, W_out, xs, row(norm2_g[l]), name=f"l{l}_mm_out", after=(tok,))
            W_gu = landed(l, 2, fs, arr, (x1,))
        gu, act = _mm_swiglu(h2, W_gu, name=f"l{l}_mm_swiglu")
        fs, arr, tok = arrive(l, 3, (act,))
        W_down = landed(l, 3, fs, arr, (tok,))
        x2 = _mm_nn(act, W_down, res=x1, out_dtype=F32, tn=512, name=f"l{l}_mm_down")
        acts.append(dict(x=xs, h=h, z=z, y=y, x1=x1, h2=h2, gu=gu, act=act, sm=sm,
                         W_in=W_in, W_out=W_out, W_gu=W_gu, W_down=W_down))
        xs = x2

    loss_vec, dx, dxb, dgf = _loss_head(xs, row(final_norm_g), loss_target[0], "loss_head")
    loss = lax.psum(loss_vec[0, 0], ("x", "y", "c"))

    big_w = {"in": (w_in, m_w_in, v_w_in), "out": (w_out, m_w_out, v_w_out), "gate": (w_gate, m_w_gate, v_w_gate),
             "up": (w_up, m_w_up, v_w_up), "down": (w_down, m_w_down, v_w_down)}
    big = {nm: None for nm in big_w}

    def adamw(l, names, gs):
        after = ()
        for nm, g in zip(names, gs):
            w, m, v = big_w[nm]
            if l == L - 1 and L > 1:
                big[nm] = _adamw_sparsecore(l, w, g, m, v, f"l{l}_adamw_{nm}")
            else:
                big[nm] = _adamw_layer(l, w, g, m, v, big[nm], f"l{l}_adamw_{nm}", after=after)
                after = (big[nm][1],)
        return after

    small_grads = [None] * L
    pend_dg = pend_oi = None
    tok = ()
    for l in reversed(range(L)):
        a = acts[l]
        sm = a["sm"]
        dgu = _mm_nt_swiglu_bwd(dxb, a["W_down"], a["gu"], name=f"l{l}_bwd_down", after=tok)
        if pend_dg:
            tok = (pend_dg[0].exchange_to_join((dgu,)),)
        dW_down = _mm_tn(a["act"], dxb, name=f"l{l}_dw_down", after=tok)
        if pend_dg:
            tok = adamw(pend_dg[1], ["down", "gate", "up"], pend_dg[0].finish((dW_down,)))
        dh2 = _mm_nt_dgu(dgu, a["W_gu"], name=f"l{l}_bwd_gu", after=tok)
        if pend_oi:
            tok = (pend_oi[0].exchange_to_join((dh2,)),)
        dW_gu = _mm_tn(a["h2"], dgu, tkw=1024, tnw=512, name=f"l{l}_dw_gu", after=tok)
        if pend_oi:
            tok = adamw(pend_oi[1], ["out", "in"], pend_oi[0].finish((dW_gu,)))
        dg_batch = _GradBatch(f"l{l}_dg", [dW_down, dW_gu], [g_down, g_gu], [(0, g_down), (1, g_gate), (1, g_up)], pos)
        tok = (dg_batch.start_swap(tok),)
        dx1, dx1b, dg2 = _rms_bwd(dh2, a["x1"], row(norm2_g[l]), dx, f"l{l}_rms2_bwd", after=tok)
        dy = _mm_nt(dx1b, a["W_out"], out_dtype=F32, name=f"l{l}_bwd_out")
        dW_out = _mm_tn(a["y"], dx1b, name=f"l{l}_dw_out")
        tok = (dg_batch.swap_to_exchange((dW_out, dy)),)
        dz, dgn, dlng, dlnb, dws, dbs, dcw = _mixer_bwd(
            a["z"], dy, sm["lng"], sm["lnb"], sm["wm"], sm["wmt"], sm["bfull"], sm["cw"], sm["gn"],
            f"l{l}_mixer_bwd", after=tok)
        dW_in = _mm_tn(a["h"], dz, tkw=1024, name=f"l{l}_dw_in")
        oi_batch = _GradBatch(f"l{l}_oi", [dW_out, dW_in], [g_out, g_in], [(0, g_out), (1, g_in)], pos)
        tok = (oi_batch.start_swap(()),)
        dh = _mm_nt(dz, a["W_in"], out_dtype=BF16, tn=512, name=f"l{l}_bwd_in", after=tok)
        dx, dxb, dg1 = _rms_bwd(dh, a["x"], row(norm1_g[l]), dx1, f"l{l}_rms1_bwd")
        tok = (oi_batch.swap_to_exchange((dx,)),)
        small_grads[l] = [dg1[0], dlng[0], dlnb[0], dws, dbs[:, 0, :], dcw[:CONV_TAPS], dgn[0], dg2[0]]
        pend_dg, pend_oi = (dg_batch, l), (oi_batch, l)
    grad_x = dx[None]

    tok = (pend_dg[0].exchange_to_join(tok),)
    tok = adamw(pend_dg[1], ["down", "gate", "up"], pend_dg[0].finish(tok))
    tok = (pend_oi[0].exchange_to_join(tok),)
    pieces = [p for l in range(L) for p in small_grads[l]] + [dgf[0]]
    red = _allreduce_small(_pack(pieces), after=tok)
    adamw(pend_oi[1], ["out", "in"], pend_oi[0].finish((red,)))

    red_list = _unpack(red, [p.shape for p in pieces])
    per = len(small_grads[0])
    stack = lambda i: jnp.stack([red_list[l * per + i] for l in range(L)])
    g_small = {"norm1_g": stack(0), "gmlp_ln_g": stack(1), "gmlp_ln_b": stack(2), "w_spatial": stack(3),
               "b_spatial": stack(4), "group_norm_g": stack(6), "norm2_g": stack(7),
               "final_norm_g": red_list[L * per]}
    g_small["conv_w"] = lax.dynamic_slice_in_dim(stack(5), pos[0] * cb, cb, axis=2)
    small_w = {"norm1_g": (norm1_g, m_norm1_g, v_norm1_g), "gmlp_ln_g": (gmlp_ln_g, m_gmlp_ln_g, v_gmlp_ln_g),
               "gmlp_ln_b": (gmlp_ln_b, m_gmlp_ln_b, v_gmlp_ln_b), "w_spatial": (w_spatial, m_w_spatial, v_w_spatial),
               "b_spatial": (b_spatial, m_b_spatial, v_b_spatial), "conv_w": (conv_w, m_conv_w, v_conv_w),
               "group_norm_g": (group_norm_g, m_group_norm_g, v_group_norm_g),
               "norm2_g": (norm2_g, m_norm2_g, v_norm2_g), "final_norm_g": (final_norm_g, m_final_norm_g, v_final_norm_g)}
    snames = list(small_w)
    sd, smn, svn = _adamw_flat(_pack([small_w[n][0] for n in snames]), _pack([g_small[n] for n in snames]),
                               _pack([small_w[n][1] for n in snames]), _pack([small_w[n][2] for n in snames]),
                               "adamw_small")
    sshapes = [small_w[n][0].shape for n in snames]
    sd, smn, svn = _unpack(sd, sshapes), _unpack(smn, sshapes), _unpack(svn, sshapes)
    small_out = {n: (g_small[n], sd[i], smn[i], svn[i]) for i, n in enumerate(snames)}

    order = ["norm1_g", "w_in", "gmlp_ln_g", "gmlp_ln_b", "w_spatial", "b_spatial", "conv_w", "group_norm_g",
             "w_out", "norm2_g", "w_gate", "w_up", "w_down", "final_norm_g"]
    res = {n: (big[n[2:]] if n[2:] in big else small_out[n]) for n in order}
    return (loss, grad_x, *[res[n][0] for n in order], *[res[n][1] for n in order],
            *[res[n][2] for n in order], *[res[n][3] for n in order])
```

```python
import math

import jax
import jax.numpy as jnp
from jax import lax
from jax.experimental import pallas as pl
from jax.experimental.pallas import tpu as pltpu
from jax.experimental.pallas import tpu_sc as plsc

RMS_EPS = 1e-6
LN_EPS = 1e-5
HEAD_DIM = 128
CHUNK = 64
CONV_TAPS = 3
HALO = 16
ADAM_LR = 0.001
ADAM_B1 = 0.9
ADAM_B2 = 0.999
ADAM_EPS = 1e-08
ADAM_WD = 0.01
ADAM_STEP = 10
V7X_VMEM_LIMIT = 56 * 1024 * 1024
N_CHIPS = 4
EPILOGUE_ROWS = 256
SC_LANES = 16
SC_BLOCK_ELEMS = 5120
MESH = pl.DeviceIdType.MESH
F32 = jnp.float32
BF16 = jnp.bfloat16
ANY = pl.BlockSpec(memory_space=pl.ANY)
HBM = pl.BlockSpec(memory_space=pltpu.HBM)
SEM = pl.BlockSpec(memory_space=pltpu.SEMAPHORE)
EFFECT = pltpu.SideEffectType.DATAFLOW_SIDE_EFFECTING


def _tile(n, pref):
    if n <= pref:
        return n
    best = None
    for t in range(128, pref + 1, 128):
        if n % t == 0:
            best = t
    assert best is not None, (n, pref)
    return best


def _rows_tile(rows, cols, budget_elems):
    tr = rows
    while tr * cols > budget_elems and tr % 2 == 0 and (tr // 2) % 16 == 0:
        tr //= 2
    return tr


def _params(*sem):
    return pltpu.CompilerParams(dimension_semantics=sem if sem else None,
                                vmem_limit_bytes=V7X_VMEM_LIMIT)


def _call(body, args, after, **kw):
    n, na = len(args), len(after)
    if na:
        inner = body

        def body(*refs):
            inner(*refs[:n], *refs[n + na:])

        kw["in_specs"] = list(kw["in_specs"]) + [ANY] * na
    return pl.pallas_call(body, **kw)(*args, *after)


def _prefetch_call(body, pos, args, after=(), *, out_shape, grid, in_specs, out_specs, aliases=None, name):
    n, na = 1 + len(args), len(after)
    if na:
        inner = body

        def body(*refs):
            inner(*refs[:n], *refs[n + na:])

    return pl.pallas_call(
        body, out_shape=out_shape,
        grid_spec=pltpu.PrefetchScalarGridSpec(num_scalar_prefetch=1, grid=grid,
                                               in_specs=list(in_specs) + [ANY] * na, out_specs=out_specs),
        input_output_aliases=aliases or {}, name=name,
        compiler_params=_params(*(["parallel"] * len(grid))))(pos, *args, *after)


GELU_C = math.sqrt(2.0 / math.pi)
GELU_K = 0.044715


def _gelu(x):
    th = jnp.tanh(x * (GELU_C + (GELU_C * GELU_K) * (x * x)))
    hx = 0.5 * x
    return hx + hx * th


def _gelu_and_grad(x):
    x2 = x * x
    th = jnp.tanh(x * (GELU_C + (GELU_C * GELU_K) * x2))
    hx = 0.5 * x
    grad = (0.5 + 0.5 * th) + (hx * (1.0 - th * th)) * (GELU_C + (3.0 * GELU_C * GELU_K) * x2)
    return hx + hx * th, grad


def _sigmoid(x):
    return 1.0 / (1.0 + jnp.exp(-x))


def _nt(a, b):
    return lax.dot_general(a, b, (((1,), (1,)), ((), ())), preferred_element_type=F32)


def _tn(a, b):
    return lax.dot_general(a, b, (((0,), (0,)), ((), ())), preferred_element_type=F32)


def _rms_fwd(x, g, name, after=()):
    T, D = x.shape
    tr = _tile(T, 512)

    def body(x_ref, g_ref, h_ref):
        xv = x_ref[...]
        r = lax.rsqrt(jnp.mean(xv * xv, axis=-1, keepdims=True) + RMS_EPS)
        h_ref[...] = ((xv * r) * g_ref[...]).astype(h_ref.dtype)

    return _call(
        body, [x, g], after, out_shape=jax.ShapeDtypeStruct((T, D), BF16), grid=(T // tr,),
        in_specs=[pl.BlockSpec((tr, D), lambda i: (i, 0)), pl.BlockSpec((1, D), lambda i: (0, 0))],
        out_specs=pl.BlockSpec((tr, D), lambda i: (i, 0)),
        name=name, compiler_params=_params("parallel"))


def _rms_bwd(dh, x, g, dres, name, after=()):
    T, D = x.shape
    tr = _tile(T, 256)

    def body(dh_ref, x_ref, g_ref, dres_ref, dx_ref, dxb_ref, dg_ref):
        i = pl.program_id(0)
        xv = x_ref[...]
        dhv = dh_ref[...].astype(F32)
        r = lax.rsqrt(jnp.mean(xv * xv, axis=-1, keepdims=True) + RMS_EPS)
        xh = xv * r
        q = dhv * g_ref[...]
        dx = dres_ref[...] + r * (q - xh * jnp.mean(q * xh, axis=-1, keepdims=True))
        dx_ref[...] = dx
        dxb_ref[...] = dx.astype(BF16)
        part = jnp.sum(dhv * xh, axis=0, keepdims=True)

        @pl.when(i == 0)
        def _():
            dg_ref[...] = part

        @pl.when(i > 0)
        def _():
            dg_ref[...] += part

    row = pl.BlockSpec((tr, D), lambda i: (i, 0))
    vec = pl.BlockSpec((1, D), lambda i: (0, 0))
    return _call(
        body, [dh, x, g, dres], after,
        out_shape=(jax.ShapeDtypeStruct((T, D), F32), jax.ShapeDtypeStruct((T, D), BF16),
                   jax.ShapeDtypeStruct((1, D), F32)),
        grid=(T // tr,), in_specs=[row, row, vec, row], out_specs=(row, row, vec),
        name=name, compiler_params=_params("arbitrary"))


def _loss_head(x, g, tgt, name):
    T, D = x.shape
    tr = _tile(T, 256)

    def body(x_ref, g_ref, t_ref, loss_ref, dx_ref, dxb_ref, dg_ref):
        i = pl.program_id(0)
        xv = x_ref[...]
        gv = g_ref[...]
        r = lax.rsqrt(jnp.mean(xv * xv, axis=-1, keepdims=True) + RMS_EPS)
        xh = xv * r
        err = xh * gv - t_ref[...]
        lpart = jnp.full((1, 128), 0.5 * jnp.sum(jnp.mean(err * err, axis=-1, keepdims=True)), F32)
        dy = err * (1.0 / D)
        q = dy * gv
        dx = r * (q - xh * jnp.mean(q * xh, axis=-1, keepdims=True))
        dx_ref[...] = dx
        dxb_ref[...] = dx.astype(BF16)
        gpart = jnp.sum(dy * xh, axis=0, keepdims=True)

        @pl.when(i == 0)
        def _():
            loss_ref[...] = lpart
            dg_ref[...] = gpart

        @pl.when(i > 0)
        def _():
            loss_ref[...] += lpart
            dg_ref[...] += gpart

    row = pl.BlockSpec((tr, D), lambda i: (i, 0))
    vec = pl.BlockSpec((1, D), lambda i: (0, 0))
    return pl.pallas_call(
        body,
        out_shape=(jax.ShapeDtypeStruct((1, 128), F32), jax.ShapeDtypeStruct((T, D), F32),
                   jax.ShapeDtypeStruct((T, D), BF16), jax.ShapeDtypeStruct((1, D), F32)),
        grid=(T // tr,), in_specs=[row, vec, row],
        out_specs=(pl.BlockSpec((1, 128), lambda i: (0, 0)), row, row, vec),
        name=name, compiler_params=_params("arbitrary"))(x, g, tgt)


def _mm_nn(a, w, *, res=None, out_dtype, tm=1024, tn=1024, tk=None, name, after=()):
    M, K = a.shape
    N = w.shape[1]
    tm, tn = _tile(M, tm), _tile(N, tn)
    tk = K if tk is None else _tile(K, tk)
    nk = K // tk
    has_res = res is not None

    def body(*refs):
        a_ref, w_ref = refs[0], refs[1]
        r_ref = refs[2] if has_res else None
        o_ref = refs[2 + has_res]
        part = jnp.dot(a_ref[...], w_ref[...], preferred_element_type=F32)

        def finish(acc):
            if has_res:
                acc = r_ref[...] + acc
            o_ref[...] = acc.astype(o_ref.dtype)

        if nk == 1:
            finish(part)
        else:
            acc_ref = refs[3 + has_res]
            k = pl.program_id(2)

            @pl.when(k == 0)
            def _():
                acc_ref[...] = part

            @pl.when(jnp.logical_and(k > 0, k < nk - 1))
            def _():
                acc_ref[...] += part

            @pl.when(k == nk - 1)
            def _():
                finish(acc_ref[...] + part)

    in_specs = [pl.BlockSpec((tm, tk), lambda i, j, k: (i, k)),
                pl.BlockSpec((tk, tn), lambda i, j, k: (k, j))]
    args = [a, w]
    if has_res:
        in_specs.append(pl.BlockSpec((tm, tn), lambda i, j, k: (i, j)))
        args.append(res)
    return _call(
        body, args, after, out_shape=jax.ShapeDtypeStruct((M, N), out_dtype), grid=(M // tm, N // tn, nk),
        in_specs=in_specs, out_specs=pl.BlockSpec((tm, tn), lambda i, j, k: (i, j)),
        scratch_shapes=[pltpu.VMEM((tm, tn), F32)] if nk > 1 else [],
        name=name, compiler_params=_params("parallel", "parallel", "arbitrary"))


def _mm_res_rms(a, w, res, g, *, tm=512, name, after=()):
    M, K = a.shape
    N = w.shape[1]
    tm = _tile(M, tm)

    def body(a_ref, w_ref, r_ref, g_ref, x_ref, h_ref):
        xv = r_ref[...] + jnp.dot(a_ref[...], w_ref[...], preferred_element_type=F32)
        x_ref[...] = xv
        r = lax.rsqrt(jnp.mean(xv * xv, axis=-1, keepdims=True) + RMS_EPS)
        h_ref[...] = ((xv * r) * g_ref[...]).astype(BF16)

    row = lambda n: pl.BlockSpec((tm, n), lambda i: (i, 0))
    return _call(
        body, [a, w, res, g], after,
        out_shape=(jax.ShapeDtypeStruct((M, N), F32), jax.ShapeDtypeStruct((M, N), BF16)), grid=(M // tm,),
        in_specs=[row(K), pl.BlockSpec((K, N), lambda i: (0, 0)), row(N), pl.BlockSpec((1, N), lambda i: (0, 0))],
        out_specs=(row(N), row(N)), name=name, compiler_params=_params("parallel"))


def _mm_swiglu(h, wgu, *, tm=2048, tn=512, name):
    T, D = h.shape
    F = wgu.shape[1] // 2
    tm, tn = _tile(T, tm), _tile(F, tn)
    nf = F // tn

    rc = _tile(tm, EPILOGUE_ROWS)

    def body(h_ref, wg_ref, wu_ref, fac_ref, act_ref):
        for r in range(tm // rc):
            rows = slice(r * rc, (r + 1) * rc)
            hv = h_ref[rows, :]
            g = jnp.dot(hv, wg_ref[...], preferred_element_type=F32)
            u = jnp.dot(hv, wu_ref[...], preferred_element_type=F32)
            s = _sigmoid(g)
            t = g * s
            act_ref[rows, :] = (t * u).astype(BF16)
            fac_ref[0, rows, :] = (u * (s + t * (1.0 - s))).astype(BF16)
            fac_ref[1, rows, :] = t.astype(BF16)

    return _call(
        body, [h, wgu, wgu], (),
        out_shape=(jax.ShapeDtypeStruct((2, T, F), BF16), jax.ShapeDtypeStruct((T, F), BF16)),
        grid=(T // tm, nf),
        in_specs=[pl.BlockSpec((tm, D), lambda i, j: (i, 0)),
                  pl.BlockSpec((D, tn), lambda i, j: (0, j)),
                  pl.BlockSpec((D, tn), lambda i, j: (0, j + nf))],
        out_specs=(pl.BlockSpec((2, tm, tn), lambda i, j: (0, i, j)),
                   pl.BlockSpec((tm, tn), lambda i, j: (i, j))),
        name=name, compiler_params=_params("parallel", "parallel"))


def _mm_nt(a, w, *, out_dtype, tm=1024, tn=1024, tk=None, name, after=()):
    M, K = a.shape
    N = w.shape[0]
    tm, tn = _tile(M, tm), _tile(N, tn)
    tk = K if tk is None else _tile(K, tk)
    nk = K // tk

    def body(*refs):
        a_ref, w_ref, o_ref = refs[0], refs[1], refs[2]
        part = _nt(a_ref[...], w_ref[...])
        if nk == 1:
            o_ref[...] = part.astype(o_ref.dtype)
        else:
            acc_ref = refs[3]
            k = pl.program_id(2)

            @pl.when(k == 0)
            def _():
                acc_ref[...] = part

            @pl.when(jnp.logical_and(k > 0, k < nk - 1))
            def _():
                acc_ref[...] += part

            @pl.when(k == nk - 1)
            def _():
                o_ref[...] = (acc_ref[...] + part).astype(o_ref.dtype)

    return _call(
        body, [a, w], after, out_shape=jax.ShapeDtypeStruct((M, N), out_dtype), grid=(M // tm, N // tn, nk),
        in_specs=[pl.BlockSpec((tm, tk), lambda i, j, k: (i, k)),
                  pl.BlockSpec((tn, tk), lambda i, j, k: (j, k))],
        out_specs=pl.BlockSpec((tm, tn), lambda i, j, k: (i, j)),
        scratch_shapes=[pltpu.VMEM((tm, tn), F32)] if nk > 1 else [],
        name=name, compiler_params=_params("parallel", "parallel", "arbitrary"))


def _mm_nt_swiglu_bwd(dxb, wdown, gu, *, tm=2048, tn=512, name, after=()):
    T, D = dxb.shape
    F = wdown.shape[0]
    tm, tn = _tile(T, tm), _tile(F, tn)

    def body(dx_ref, w_ref, fac_ref, dgu_ref):
        da = _nt(dx_ref[...], w_ref[...])
        dgu_ref[0] = (da * fac_ref[0].astype(F32)).astype(BF16)
        dgu_ref[1] = (da * fac_ref[1].astype(F32)).astype(BF16)

    blk3 = pl.BlockSpec((2, tm, tn), lambda i, j: (0, i, j))
    return _call(
        body, [dxb, wdown, gu], after, out_shape=jax.ShapeDtypeStruct((2, T, F), BF16), grid=(T // tm, F // tn),
        in_specs=[pl.BlockSpec((tm, D), lambda i, j: (i, 0)),
                  pl.BlockSpec((tn, D), lambda i, j: (j, 0)), blk3],
        out_specs=blk3, name=name, compiler_params=_params("parallel", "parallel"))


def _mm_nt_dgu(dgu, wgu, *, tm=1024, tn=1024, tk=2816, name, after=()):
    _, T, F = dgu.shape
    D = wgu.shape[0]
    tm, tn, tk = _tile(T, tm), _tile(D, tn), _tile(F, tk)
    nkf = F // tk
    nk = 2 * nkf

    def body(a_ref, w_ref, o_ref, acc_ref):
        k = pl.program_id(2)
        part = _nt(a_ref[...], w_ref[...])

        @pl.when(k == 0)
        def _():
            acc_ref[...] = part

        @pl.when(jnp.logical_and(k > 0, k < nk - 1))
        def _():
            acc_ref[...] += part

        @pl.when(k == nk - 1)
        def _():
            o_ref[...] = (acc_ref[...] + part).astype(o_ref.dtype)

    return _call(
        body, [dgu, wgu], after, out_shape=jax.ShapeDtypeStruct((T, D), BF16), grid=(T // tm, D // tn, nk),
        in_specs=[pl.BlockSpec((None, tm, tk), lambda i, j, k: (k // nkf, i, k % nkf)),
                  pl.BlockSpec((tn, tk), lambda i, j, k: (j, k))],
        out_specs=pl.BlockSpec((tm, tn), lambda i, j, k: (i, j)),
        scratch_shapes=[pltpu.VMEM((tm, tn), F32)],
        name=name, compiler_params=_params("parallel", "parallel", "arbitrary"))


def _mm_tn(a, g, *, tkw=512, tnw=1024, name, after=()):
    T, Kw = a.shape
    pair = g.ndim == 3
    Nw = 2 * g.shape[2] if pair else g.shape[1]
    tkw = _tile(Kw, tkw)
    tnw = _tile(g.shape[2] if pair else Nw, tnw)
    nf = (Nw // 2) // tnw if pair else 0

    def body(a_ref, g_ref, o_ref):
        o_ref[...] = _tn(a_ref[...], g_ref[...]).astype(o_ref.dtype)

    if pair:
        g_spec = pl.BlockSpec((None, T, tnw), lambda i, j: (j // nf, 0, j % nf))
    else:
        g_spec = pl.BlockSpec((T, tnw), lambda i, j: (0, j))
    return _call(
        body, [a, g], after, out_shape=jax.ShapeDtypeStruct((Kw, Nw), BF16), grid=(Kw // tkw, Nw // tnw),
        in_specs=[pl.BlockSpec((T, tkw), lambda i, j: (0, i)), g_spec],
        out_specs=pl.BlockSpec((tkw, tnw), lambda i, j: (i, j)),
        name=name, compiler_params=_params("parallel", "parallel"))


def _mixer_specs(T, A, tr):
    nin = 5 * A
    nb = tr // HALO
    last = T // HALO - 1
    prev = pl.BlockSpec((HALO, nin), lambda i: (jnp.maximum(i * nb - 1, 0), 0))
    cur = pl.BlockSpec((tr, nin), lambda i: (i, 0))
    nxt = pl.BlockSpec((HALO, nin), lambda i: (jnp.minimum((i + 1) * nb, last), 0))
    return prev, cur, nxt


def _conv(p_ext, cw_ref):
    return (cw_ref[2:3, :] * p_ext + cw_ref[1:2, :] * pltpu.roll(p_ext, 1, 0)
            + cw_ref[0:1, :] * pltpu.roll(p_ext, 2, 0))


def _mixer_fwd(z, lng, lnb, wm, bfull, cw, gn, name, after=()):
    T, nin = z.shape
    A = nin // 5
    H = A // HEAD_DIM
    tr = _tile(T, 256)
    nblk = tr // HEAD_DIM

    def body(zp_ref, z_ref, lng_ref, lnb_ref, wm_ref, bf_ref, cw_ref, gn_ref, y_ref, vn_ref, mix_ref):
        i = pl.program_id(0)
        u = _gelu(z_ref[:, 0:A].astype(F32))
        vg = _gelu(z_ref[:, A:2 * A].astype(F32))
        xc = vg - jnp.mean(vg, axis=-1, keepdims=True)
        rstd = lax.rsqrt(jnp.mean(xc * xc, axis=-1, keepdims=True) + LN_EPS)
        vn_ref[...] = ((xc * rstd) * lng_ref[...] + lnb_ref[...]).astype(BF16)
        for cb in range(nblk):
            rows = slice(cb * HEAD_DIM, (cb + 1) * HEAD_DIM)
            for h in range(H):
                cols = slice(h * HEAD_DIM, (h + 1) * HEAD_DIM)
                mix_ref[rows, cols] = jnp.dot(wm_ref[h], vn_ref[rows, cols],
                                              preferred_element_type=F32) + bf_ref[h]
        ya = u * mix_ref[...]
        ra = lax.rsqrt(jnp.mean(ya * ya, axis=-1, keepdims=True) + RMS_EPS)
        y_ref[:, 0:A] = ((ya * ra) * gn_ref[:, 0:A]).astype(BF16)

        p_prev = zp_ref[:, 3 * A:4 * A].astype(F32) * zp_ref[:, 4 * A:5 * A].astype(F32)
        p_prev = jnp.where(i > 0, p_prev, 0.0)
        p_cur = z_ref[:, 3 * A:4 * A].astype(F32) * z_ref[:, 4 * A:5 * A].astype(F32)
        cv = _conv(jnp.concatenate([p_prev, p_cur], axis=0), cw_ref)[HALO:]
        yb = z_ref[:, 2 * A:3 * A].astype(F32) * cv
        rb = lax.rsqrt(jnp.mean(yb * yb, axis=-1, keepdims=True) + RMS_EPS)
        y_ref[:, A:2 * A] = ((yb * rb) * gn_ref[:, A:2 * A]).astype(BF16)

    prev, cur, _ = _mixer_specs(T, A, tr)
    full = lambda shape: pl.BlockSpec(shape, lambda i: (0,) * len(shape))
    return _call(
        body, [z, z, lng, lnb, wm, bfull, cw, gn], after,
        out_shape=jax.ShapeDtypeStruct((T, 2 * A), BF16), grid=(T // tr,),
        in_specs=[prev, cur, full((1, A)), full((1, A)), full((H, HEAD_DIM, HEAD_DIM)),
                  full((H, HEAD_DIM, HEAD_DIM)), full((8, A)), full((1, 2 * A))],
        out_specs=pl.BlockSpec((tr, 2 * A), lambda i: (i, 0)),
        scratch_shapes=[pltpu.VMEM((tr, A), BF16), pltpu.VMEM((tr, A), F32)],
        name=name, compiler_params=_params("parallel"))


def _mixer_bwd(z, dy, lng, lnb, wm, wmt, bfull, cw, gn, name, after=()):
    T, nin = z.shape
    A = nin // 5
    H = A // HEAD_DIM
    tr = _tile(T, 256)
    nblk = tr // HEAD_DIM
    ngrid = T // tr
    next_ = tr + 2 * HALO

    def body(zp_ref, z_ref, zn_ref, dy_ref, dyn_ref, lng_ref, lnb_ref, wm_ref, wmt_ref, bf_ref, cw_ref, gn_ref,
             dz_ref, dgn_ref, dlng_ref, dlnb_ref, dws_ref, dbs_ref, dcw_ref,
             vn_ref, mix_ref, dmix_ref, dvn_ref):
        i = pl.program_id(0)

        @pl.when(i == 0)
        def _():
            dgn_ref[...] = jnp.zeros_like(dgn_ref)
            dlng_ref[...] = jnp.zeros_like(dlng_ref)
            dlnb_ref[...] = jnp.zeros_like(dlnb_ref)
            dws_ref[...] = jnp.zeros_like(dws_ref)
            dbs_ref[...] = jnp.zeros_like(dbs_ref)
            dcw_ref[...] = jnp.zeros_like(dcw_ref)

        u, du_dz = _gelu_and_grad(z_ref[:, 0:A].astype(F32))
        vg, dv_dz = _gelu_and_grad(z_ref[:, A:2 * A].astype(F32))
        xc = vg - jnp.mean(vg, axis=-1, keepdims=True)
        rstd = lax.rsqrt(jnp.mean(xc * xc, axis=-1, keepdims=True) + LN_EPS)
        vhat = xc * rstd
        vn_ref[...] = (vhat * lng_ref[...] + lnb_ref[...]).astype(BF16)
        for cb in range(nblk):
            rows = slice(cb * HEAD_DIM, (cb + 1) * HEAD_DIM)
            for h in range(H):
                cols = slice(h * HEAD_DIM, (h + 1) * HEAD_DIM)
                mix_ref[rows, cols] = jnp.dot(wm_ref[h], vn_ref[rows, cols],
                                              preferred_element_type=F32) + bf_ref[h]
        mixed = mix_ref[...]
        ya = u * mixed
        ra = lax.rsqrt(jnp.mean(ya * ya, axis=-1, keepdims=True) + RMS_EPS)
        yha = ya * ra
        dyan = dy_ref[:, 0:A].astype(F32)
        dgn_ref[:, 0:A] += jnp.sum(dyan * yha, axis=0, keepdims=True)
        qa = dyan * gn_ref[:, 0:A]
        dya = ra * (qa - yha * jnp.mean(qa * yha, axis=-1, keepdims=True))
        dz_ref[:, 0:A] = ((dya * mixed) * du_dz).astype(BF16)
        dmix_ref[...] = (dya * u).astype(BF16)

        ii = lax.broadcasted_iota(jnp.int32, (HEAD_DIM, HEAD_DIM), 0)
        jj = lax.broadcasted_iota(jnp.int32, (HEAD_DIM, HEAD_DIM), 1)
        mask = (jj // CHUNK <= ii // CHUNK).astype(F32)
        ones = jnp.ones((8, HEAD_DIM), BF16)
        for h in range(H):
            cols = slice(h * HEAD_DIM, (h + 1) * HEAD_DIM)
            dws = jnp.zeros((HEAD_DIM, HEAD_DIM), F32)
            dbs = jnp.zeros((8, HEAD_DIM), F32)
            for cb in range(nblk):
                rows = slice(cb * HEAD_DIM, (cb + 1) * HEAD_DIM)
                dm = dmix_ref[rows, cols]
                dws = dws + _nt(dm, vn_ref[rows, cols])
                dbs = dbs + _nt(ones, dm)
                dvn_ref[rows, cols] = jnp.dot(wmt_ref[h], dm, preferred_element_type=F32)
            dws_ref[h] += dws * mask
            dbs_ref[h] += dbs
        dvn = dvn_ref[...]
        dlnb_ref[...] += jnp.sum(dvn, axis=0, keepdims=True)
        dlng_ref[...] += jnp.sum(dvn * vhat, axis=0, keepdims=True)
        dvh = dvn * lng_ref[...]
        dvg = rstd * (dvh - jnp.mean(dvh, axis=-1, keepdims=True)
                      - vhat * jnp.mean(dvh * vhat, axis=-1, keepdims=True))
        dz_ref[:, A:2 * A] = (dvg * dv_dz).astype(BF16)

        def ext(lo):
            mid = z_ref[:, lo:lo + A].astype(F32)
            return jnp.concatenate([zp_ref[:, lo:lo + A].astype(F32), mid, zn_ref[:, lo:lo + A].astype(F32)], axis=0)

        zb, zc, zh = ext(2 * A), ext(3 * A), ext(4 * A)
        row = lax.broadcasted_iota(jnp.int32, (next_, 1), 0)
        p = zc * zh
        p = jnp.where(jnp.logical_and(row < HALO, i == 0), 0.0, p)
        cv = _conv(p, cw_ref)
        yb = zb * cv
        rb = lax.rsqrt(jnp.mean(yb * yb, axis=-1, keepdims=True) + RMS_EPS)
        yhb = yb * rb
        dyn_rows = jnp.where(i < ngrid - 1, dyn_ref[:, A:2 * A].astype(F32), 0.0)
        dybn = jnp.concatenate([jnp.zeros((HALO, A), F32), dy_ref[:, A:2 * A].astype(F32), dyn_rows], axis=0)
        ctr = slice(HALO, HALO + tr)
        dgn_ref[:, A:2 * A] += jnp.sum((dybn * yhb)[ctr], axis=0, keepdims=True)
        qb = dybn * gn_ref[:, A:2 * A]
        dyb = rb * (qb - yhb * jnp.mean(qb * yhb, axis=-1, keepdims=True))
        dcv = dyb * zb
        dp = (cw_ref[2:3, :] * dcv + cw_ref[1:2, :] * pltpu.roll(dcv, next_ - 1, 0)
              + cw_ref[0:1, :] * pltpu.roll(dcv, next_ - 2, 0))
        dz_ref[:, 2 * A:3 * A] = (dyb * cv)[ctr].astype(BF16)
        dz_ref[:, 3 * A:4 * A] = (dp * zh)[ctr].astype(BF16)
        dz_ref[:, 4 * A:5 * A] = (dp * zc)[ctr].astype(BF16)
        dcw_ref[2:3, :] += jnp.sum((dcv * p)[ctr], axis=0, keepdims=True)
        dcw_ref[1:2, :] += jnp.sum((dcv * pltpu.roll(p, 1, 0))[ctr], axis=0, keepdims=True)
        dcw_ref[0:1, :] += jnp.sum((dcv * pltpu.roll(p, 2, 0))[ctr], axis=0, keepdims=True)

    prev, cur, nxt = _mixer_specs(T, A, tr)
    nb = tr // HALO
    dy_cur = pl.BlockSpec((tr, 2 * A), lambda i: (i, 0))
    dy_nxt = pl.BlockSpec((HALO, 2 * A), lambda i: (jnp.minimum((i + 1) * nb, T // HALO - 1), 0))
    full = lambda shape: pl.BlockSpec(shape, lambda i: (0,) * len(shape))
    hh = (H, HEAD_DIM, HEAD_DIM)
    return _call(
        body, [z, z, z, dy, dy, lng, lnb, wm, wmt, bfull, cw, gn], after,
        out_shape=(jax.ShapeDtypeStruct((T, nin), BF16), jax.ShapeDtypeStruct((1, 2 * A), F32),
                   jax.ShapeDtypeStruct((1, A), F32), jax.ShapeDtypeStruct((1, A), F32),
                   jax.ShapeDtypeStruct(hh, F32), jax.ShapeDtypeStruct((H, 8, HEAD_DIM), F32),
                   jax.ShapeDtypeStruct((8, A), F32)),
        grid=(ngrid,),
        in_specs=[prev, cur, nxt, dy_cur, dy_nxt, full((1, A)), full((1, A)), full(hh), full(hh), full(hh),
                  full((8, A)), full((1, 2 * A))],
        out_specs=(pl.BlockSpec((tr, nin), lambda i: (i, 0)), full((1, 2 * A)), full((1, A)), full((1, A)),
                   full(hh), full((H, 8, HEAD_DIM)), full((8, A))),
        scratch_shapes=[pltpu.VMEM((tr, A), BF16), pltpu.VMEM((tr, A), F32), pltpu.VMEM((tr, A), BF16),
                        pltpu.VMEM((tr, A), F32)],
        name=name, compiler_params=_params("arbitrary"))


class _Geom:
    def __init__(self, rows, cols, axis, size, base=0):
        self.rows, self.cols, self.axis, self.size, self.base = rows, cols, axis, size, base

    def in_full(self, j, h):
        if self.axis == 1:
            return (h * (self.rows // 2), self.rows // 2), (self.base + j * self.size, self.size)
        return (self.base + j * self.size, self.size), (h * (self.cols // 2), self.cols // 2)

    def in_shard(self, h):
        if self.axis == 1:
            return (h * (self.rows // 2), self.rows // 2), (0, self.size)
        return (0, self.size), (h * (self.cols // 2), self.cols // 2)

    def half_of_full(self, h):
        if self.axis == 1:
            return (h * (self.rows // 2), self.rows // 2), (0, self.cols)
        return (0, self.rows), (h * (self.cols // 2), self.cols // 2)

    def in_half(self, j):
        if self.axis == 1:
            return (0, self.rows // 2), (self.base + j * self.size, self.size)
        return (self.base + j * self.size, self.size), (0, self.cols // 2)

    @property
    def half_shape(self):
        return (self.rows // 2, self.cols) if self.axis == 1 else (self.rows, self.cols // 2)

    @property
    def shard_half_shape(self):
        return (self.rows // 2, self.size) if self.axis == 1 else (self.size, self.cols // 2)

    @property
    def shard_shape(self):
        return (self.rows, self.size) if self.axis == 1 else (self.size, self.cols)


def _at(ref, region):
    (r0, rn), (c0, cn) = region
    if not isinstance(r0, int):
        r0 = pl.multiple_of(r0, 16)
    if not isinstance(c0, int):
        c0 = pl.multiple_of(c0, 128)
    return ref.at[pl.ds(r0, rn), pl.ds(c0, cn)]


def _whole(shape):
    return (0, shape[-2]), (0, shape[-1])


class _Chunked:
    def __init__(self, make, src, src_reg, dst, dst_reg):
        self.make, self.src, self.src_reg, self.dst, self.dst_reg = make, src, src_reg, dst, dst_reg

    @property
    def whole(self):
        return self.make(_at(self.src, self.src_reg), _at(self.dst, self.dst_reg))

    def start(self):
        self.whole.start()


def _mesh_place():
    x, y, c = lax.axis_index("x"), lax.axis_index("y"), lax.axis_index("c")
    chips = [(1 - x, y), (x, 1 - y), (1 - x, 1 - y)]
    return x, y, c, 2 * x + y, chips


def _remote(ssem, rsem, dev):
    return lambda src, dst: pltpu.make_async_remote_copy(
        src_ref=src, dst_ref=dst, send_sem=ssem, recv_sem=rsem, device_id=dev, device_id_type=MESH)


def _comm_call(name, body, arrays, sems_in=(), after=(), sems_out=(), new=()):
    na, ns, nf, no, nn = len(arrays), len(sems_in), len(after), len(sems_out), len(new)

    def kern(*refs):
        sin = refs[na:na + ns]
        outs = refs[na + ns + nf:]
        body(outs[no:no + na], outs[no + na:no + na + nn], sin, outs[:no])
        outs[-1][...] = jnp.zeros_like(outs[-1])

    out_shape = (tuple(pltpu.SemaphoreType.DMA((n,)) for n in sems_out)
                 + tuple(pltpu.HBM(a.shape, a.dtype) for a in arrays)
                 + tuple(pltpu.HBM(s, d) for s, d in new)
                 + (jax.ShapeDtypeStruct((8, 128), F32),))
    res = pl.pallas_call(
        kern, out_shape=out_shape, in_specs=[HBM] * na + [SEM] * ns + [ANY] * nf,
        out_specs=(SEM,) * no + (HBM,) * (na + nn) + (pl.BlockSpec(memory_space=pltpu.VMEM),),
        input_output_aliases={i: no + i for i in range(na)}, name=name,
        compiler_params=pltpu.CompilerParams(has_side_effects=EFFECT),
    )(*[pltpu.with_memory_space_constraint(a, pltpu.HBM) for a in arrays], *sems_in, *after)
    return list(res[:no]), list(res[no:no + na]), list(res[no + na:no + na + nn]), res[-1]


def _gather_start(name, arr, members, after):
    def body(arrs, news, sin, sout):
        x, y, c, j, chips = _mesh_place()
        for m, geo in enumerate(members):
            reg = geo.in_full(j, c)
            for k, chip in enumerate(chips):
                _Chunked(_remote(sout[2 * m].at[k], sout[2 * m + 1].at[k], (*chip, c)),
                         arrs[0], reg, arrs[0], reg).start()

    sems, arrs, _, tok = _comm_call(name, body, [arr], after=after, sems_out=[N_CHIPS - 1] * (2 * len(members)))
    return [sems[2 * m:2 * m + 2] for m in range(len(members))], arrs[0], tok


def _gather_forward(name, arr, members, sems, after):
    def body(arrs, news, sin, sout):
        x, y, c, j, chips = _mesh_place()
        sibling = (x, y, 1 - c)
        full = arrs[0]
        for m, geo in enumerate(members):
            mine = geo.in_full(j, c)
            for k, chip in enumerate(chips):
                got = geo.in_full(2 * chip[0] + chip[1], c)
                sent = _Chunked(_remote(sin[2 * m].at[k], sin[2 * m + 1].at[k], sibling), full, mine, full, got)
                sent.whole.wait_send()
                sent.whole.wait_recv()
                _Chunked(_remote(sout[2 * m].at[k], sout[2 * m + 1].at[k], sibling), full, got, full, got).start()

    flat = [s for pair in sems for s in pair]
    fs, arrs, _, tok = _comm_call(name, body, [arr], sems_in=flat, after=after,
                                  sems_out=[N_CHIPS - 1] * (2 * len(members)))
    return fs, arrs[0], tok


def _gather_wait(name, arr, members, fsems, after):
    def body(arrs, news, sin, sout):
        x, y, c, j, chips = _mesh_place()
        sibling = (x, y, 1 - c)
        full = arrs[0]
        for m, geo in enumerate(members):
            for k, chip in enumerate(chips):
                jk = 2 * chip[0] + chip[1]
                cp = _Chunked(_remote(sin[2 * m].at[k], sin[2 * m + 1].at[k], sibling),
                              full, geo.in_full(jk, c), full, geo.in_full(jk, 1 - c))
                cp.whole.wait_send()
                cp.whole.wait_recv()

    _, arrs, _, _ = _comm_call(name, body, [arr], sems_in=fsems, after=after)
    return arrs[0]


def _gather_conv(conv):
    L, _, cb = conv.shape
    nk = N_CHIPS - 1

    def body(s_ref, f_ref, send_sems, recv_sems, local_sem):
        x, y, c, j, chips = _mesh_place()
        at = lambda jj: f_ref.at[:, :, pl.ds(pl.multiple_of(jj * cb, 128), cb)]
        lc = pltpu.make_async_copy(s_ref, at(j), local_sem.at[0])
        lc.start()
        cps = []
        for k, chip in enumerate(chips):
            cp = _remote(send_sems.at[k], recv_sems.at[k], (*chip, c))(s_ref, at(j))
            cp.start()
            cps.append(cp)
        for k, chip in enumerate(chips):
            jk = 2 * chip[0] + chip[1]
            cps[k].wait_send()
            _remote(send_sems.at[k], recv_sems.at[k], (*chip, c))(at(jk), at(jk)).wait_recv()
        lc.wait()

    return pl.pallas_call(
        body, out_shape=jax.ShapeDtypeStruct((L, 8, N_CHIPS * cb), F32), in_specs=[ANY], out_specs=ANY,
        scratch_shapes=[pltpu.SemaphoreType.DMA((nk,)), pltpu.SemaphoreType.DMA((nk,)),
                        pltpu.SemaphoreType.DMA((1,))],
        name="gather_conv")(conv)


def _swap_start(name, dws, geos, after):
    n = len(dws)

    def body(arrs, news, sin, sout):
        x, y, c, _, _ = _mesh_place()
        for t in range(n):
            _Chunked(_remote(sout[0].at[t], sout[1].at[t], (x, y, 1 - c)),
                     arrs[t], geos[t].half_of_full(1 - c), news[t], _whole(news[t].shape)).start()

    return _comm_call(name, body, dws, after=after, sems_out=[n, n], new=[(g.half_shape, BF16) for g in geos])


def _swap_wait(name, dws, lands, sems, geos, after):
    n = len(dws)

    def body(arrs, news, sin, sout):
        x, y, c, _, _ = _mesh_place()
        for t in range(n):
            cp = _Chunked(_remote(sin[0].at[t], sin[1].at[t], (x, y, 1 - c)),
                          arrs[t], geos[t].half_of_full(1 - c), arrs[n + t], _whole(arrs[n + t].shape))
            cp.whole.wait_send()
            cp.whole.wait_recv()

    _, arrs, _, _ = _comm_call(name, body, list(dws) + list(lands), sems_in=sems, after=after)
    return arrs[:n], arrs[n:]


def _exchange_start(name, ps, entries, after):
    nk = N_CHIPS - 1

    def body(arrs, news, sin, sout):
        x, y, c, j, chips = _mesh_place()
        for e, (pi, geo) in enumerate(entries):
            for k, chip in enumerate(chips):
                dst = news[e].at[k]
                _Chunked(_remote(sout[0].at[nk * e + k], sout[1].at[nk * e + k], (*chip, c)),
                         arrs[pi], geo.in_half(2 * chip[0] + chip[1]), dst, _whole(dst.shape)).start()

    ne = len(entries)
    return _comm_call(name, body, ps, after=after, sems_out=[nk * ne, nk * ne],
                      new=[((nk,) + g.shard_half_shape, BF16) for _, g in entries])


def _exchange_wait(name, ps, lands, sems, entries, after):
    nk = N_CHIPS - 1
    n = len(ps)

    def body(arrs, news, sin, sout):
        x, y, c, j, chips = _mesh_place()
        for e, (pi, geo) in enumerate(entries):
            for k, chip in enumerate(chips):
                dst = arrs[n + e].at[k]
                cp = _Chunked(_remote(sin[0].at[nk * e + k], sin[1].at[nk * e + k], (*chip, c)),
                              arrs[pi], geo.in_half(2 * chip[0] + chip[1]), dst, _whole(dst.shape))
                cp.whole.wait_send()
                cp.whole.wait_recv()

    _, arrs, _, _ = _comm_call(name, body, list(ps) + list(lands), sems_in=sems, after=after)
    return arrs[:n], arrs[n:]


def _join_start(name, gs, geos, after):
    n = len(gs)

    def body(arrs, news, sin, sout):
        x, y, c, _, _ = _mesh_place()
        for t in range(n):
            mine = geos[t].in_shard(c)
            _Chunked(_remote(sout[0].at[t], sout[1].at[t], (x, y, 1 - c)), arrs[t], mine, arrs[t], mine).start()

    return _comm_call(name, body, gs, after=after, sems_out=[n, n])


def _join_wait(name, gs, sems, geos, after):
    n = len(gs)

    def body(arrs, news, sin, sout):
        x, y, c, _, _ = _mesh_place()
        for t in range(n):
            cp = _Chunked(_remote(sin[0].at[t], sin[1].at[t], (x, y, 1 - c)),
                          arrs[t], geos[t].in_shard(c), arrs[t], geos[t].in_shard(1 - c))
            cp.whole.wait_send()
            cp.whole.wait_recv()

    _, arrs, _, _ = _comm_call(name, body, gs, sems_in=sems, after=after)
    return arrs


def _cast_place(w, l, geo, pos, prev, name, after=()):
    if geo.axis == 1:
        tr = _rows_tile(geo.rows, geo.size, 1024 * 1024)
        grid = (geo.rows // tr,)
        blk = (tr, geo.size)
        q = geo.base // geo.size
        out_map = lambda i, p: (i, q + p[0])
    else:
        tr = _rows_tile(geo.size, geo.cols, 1024 * 1024)
        grid = (geo.size // tr,)
        blk = (tr, geo.cols)
        nb = geo.size // tr
        out_map = lambda i, p: (p[0] * nb + i, 0)

    def body(p_ref, w_ref, *rest):
        rest[-1][...] = w_ref[...].astype(BF16)

    in_specs = [pl.BlockSpec((None,) + blk, lambda i, p: (l, i, 0))]
    args = [w]
    aliases = None
    if prev is not None:
        in_specs.append(ANY)
        args.append(prev)
        aliases = {2: 0}
    return _prefetch_call(body, pos, args, after, out_shape=jax.ShapeDtypeStruct((geo.rows, geo.cols), BF16),
                          grid=grid, in_specs=in_specs, out_specs=pl.BlockSpec(blk, out_map), aliases=aliases,
                          name=name)


def _pair_sum(dw, recv, geo, pos, name):
    hs = geo.half_shape
    tr = _rows_tile(hs[0], hs[1], 2048 * 1024)
    nb = hs[0] // tr
    blk = (tr, hs[1])
    if geo.axis == 1:
        own_map = lambda i, p: (p[1] * nb + i, 0)
    else:
        own_map = lambda i, p: (i, p[1])

    def body(p_ref, a_ref, b_ref, o_ref):
        o_ref[...] = (a_ref[...].astype(F32) + b_ref[...].astype(F32)).astype(BF16)

    same = pl.BlockSpec(blk, lambda i, p: (i, 0))
    return _prefetch_call(body, pos, [dw, recv], out_shape=jax.ShapeDtypeStruct(hs, BF16), grid=(nb,),
                          in_specs=[pl.BlockSpec(blk, own_map), same], out_specs=same, name=name)


def _chip_sum(p, recv, geo, pos, name):
    ss = geo.shard_half_shape
    tr = _rows_tile(ss[0], ss[1], 1024 * 1024)
    nb = ss[0] // tr
    blk = (tr, ss[1])
    if geo.axis == 1:
        q = geo.base // geo.size
        own_map = lambda i, p_: (i, q + p_[0])
        out_map = lambda i, p_: (p_[1] * nb + i, 0)
    else:
        own_map = lambda i, p_: (p_[0] * nb + i, 0)
        out_map = lambda i, p_: (i, p_[1])

    def body(p_ref, o_ref, r_ref, out_ref):
        acc = o_ref[...].astype(F32)
        for k in range(N_CHIPS - 1):
            acc = acc + r_ref[k].astype(F32)
        out_ref[...] = acc

    return _prefetch_call(
        body, pos, [p, recv], out_shape=jax.ShapeDtypeStruct(geo.shard_shape, F32), grid=(nb,),
        in_specs=[pl.BlockSpec(blk, own_map), pl.BlockSpec((N_CHIPS - 1,) + blk, lambda i, p_: (0, i, 0))],
        out_specs=pl.BlockSpec(blk, out_map), name=name)


def _adamw_math(w, g, m, v):
    m = ADAM_B1 * m + (1.0 - ADAM_B1) * g
    v = ADAM_B2 * v + (1.0 - ADAM_B2) * (g * g)
    m_hat = m / (1.0 - ADAM_B1 ** ADAM_STEP)
    v_hat = v / (1.0 - ADAM_B2 ** ADAM_STEP)
    delta = -ADAM_LR * (m_hat / (jnp.sqrt(v_hat) + ADAM_EPS) + ADAM_WD * w)
    return delta, m, v


def _adamw_layer(l, w, g, m, v, prev, name, after=()):
    L, R, C = w.shape
    tr = _rows_tile(R, C, 512 * 1024)

    def body(w_ref, g_ref, m_ref, v_ref, *rest):
        go_ref, d_ref, mo_ref, vo_ref = rest[-4:]
        gv = g_ref[...]
        d, mn, vn = _adamw_math(w_ref[...], gv, m_ref[...], v_ref[...])
        go_ref[...] = gv
        d_ref[...] = d
        mo_ref[...] = mn
        vo_ref[...] = vn

    blk = pl.BlockSpec((None, tr, C), lambda i: (l, i, 0))
    sds = jax.ShapeDtypeStruct(w.shape, F32)
    in_specs = [blk, pl.BlockSpec((tr, C), lambda i: (i, 0)), blk, blk]
    args = [w, g, m, v]
    aliases = {}
    if prev is not None:
        in_specs += [ANY] * 4
        args += list(prev)
        aliases = {4 + i: i for i in range(4)}
    return _call(
        body, args, after, out_shape=(sds, sds, sds, sds), grid=(R // tr,), in_specs=in_specs,
        out_specs=(blk,) * 4, input_output_aliases=aliases, name=name, compiler_params=_params("parallel"))


def _adamw_sparsecore(l, w, g, m, v, name):
    L, R, C = w.shape
    bc = max(t for t in range(128, min(C, 640) + 1, 128) if C % t == 0)
    br = 8
    while br * 2 * bc <= SC_BLOCK_ELEMS and R % (br * 2) == 0:
        br *= 2
    nrb = R // br
    flat = lambda a: a.reshape(L * R, C)
    sds = jax.ShapeDtypeStruct((L * R, C), F32)

    def kern(w_hbm, g_hbm, m_hbm, v_hbm, go_hbm, d_hbm, mo_hbm, vo_hbm):
        def block(w_v, g_v, m_v, v_v, go_v, d_v, mo_v, vo_v):
            @pl.loop(0, br)
            def _(r):
                @pl.loop(0, bc, step=SC_LANES)
                def _(c):
                    at = (pl.ds(r, 1), pl.ds(c, SC_LANES))
                    gv = g_v.at[*at][...]
                    d, mn, vn = _adamw_math(w_v.at[*at][...], gv, m_v.at[*at][...], v_v.at[*at][...])
                    go_v.at[*at][...] = gv
                    d_v.at[*at][...] = d
                    mo_v.at[*at][...] = mn
                    vo_v.at[*at][...] = vn

        lay = pl.BlockSpec((br, bc), lambda i, j: (l * nrb + i, j))
        one = pl.BlockSpec((br, bc), lambda i, j: (i, j))
        pltpu.emit_pipeline(
            block, grid=(nrb, C // bc), in_specs=[lay, one, lay, lay], out_specs=[lay] * 4,
            core_axis_name=("sc_core", "sc_tile"), dimension_semantics=(pltpu.PARALLEL, pltpu.PARALLEL),
        )(w_hbm, g_hbm, m_hbm, v_hbm, go_hbm, d_hbm, mo_hbm, vo_hbm)

    outs = pl.kernel(
        kern, out_type=(sds,) * 4, name=name, scratch_types=[],
        mesh=plsc.VectorSubcoreMesh(core_axis_name="sc_core", subcore_axis_name="sc_tile"),
    )(flat(w), g, flat(m), flat(v))
    return tuple(o.reshape(L, R, C) for o in outs)


def _adamw_flat(w, g, m, v, name):
    R, C = w.shape
    tr = _tile(R, 1024) if R % 128 == 0 else R

    def body(w_ref, g_ref, m_ref, v_ref, d_ref, mo_ref, vo_ref):
        d, mn, vn = _adamw_math(w_ref[...], g_ref[...], m_ref[...], v_ref[...])
        d_ref[...] = d
        mo_ref[...] = mn
        vo_ref[...] = vn

    blk = pl.BlockSpec((tr, C), lambda i: (i, 0))
    sds = jax.ShapeDtypeStruct(w.shape, F32)
    return pl.pallas_call(
        body, out_shape=(sds, sds, sds), grid=(R // tr,), in_specs=[blk] * 4, out_specs=(blk,) * 3,
        name=name, compiler_params=_params("parallel"))(w, g, m, v)


def _allreduce_small(s, after=()):
    R, C = s.shape

    def body(s_ref, o_ref, rbuf, send_sems, recv_sems):
        x, y, c = lax.axis_index("x"), lax.axis_index("y"), lax.axis_index("c")
        peers = [(x, y, 1 - c), (1 - x, y, c), (x, 1 - y, c)]
        o_ref[...] = s_ref[...]
        for k, peer in enumerate(peers):
            cp = _remote(send_sems.at[k], recv_sems.at[k], peer)(o_ref, rbuf.at[k])
            cp.start()
            cp.wait()
            o_ref[...] = o_ref[...] + rbuf[k]

    vm = pl.BlockSpec(memory_space=pltpu.VMEM)
    return _call(
        body, [s], after, out_shape=jax.ShapeDtypeStruct((R, C), F32), in_specs=[vm], out_specs=vm,
        scratch_shapes=[pltpu.VMEM((3, R, C), F32), pltpu.SemaphoreType.DMA((3,)), pltpu.SemaphoreType.DMA((3,))],
        name="allreduce_small", compiler_params=pltpu.CompilerParams(vmem_limit_bytes=V7X_VMEM_LIMIT))


class _GradBatch:
    def __init__(self, tag, dws, geos4, entries, pos):
        self.tag, self.dws, self.geos4, self.entries, self.pos = tag, dws, geos4, entries, pos

    def start_swap(self, after):
        self.s1, self.dws, self.land1, tok = _swap_start(f"{self.tag}_swap_start", self.dws, self.geos4, after)
        return tok

    def swap_to_exchange(self, after):
        dws, lands = _swap_wait(f"{self.tag}_swap_wait", self.dws, self.land1, self.s1, self.geos4, after)
        ps = [_pair_sum(d, r, g, self.pos, f"{self.tag}_pair_sum_{i}")
              for i, (d, r, g) in enumerate(zip(dws, lands, self.geos4))]
        self.s2, self.ps, self.land2, tok = _exchange_start(f"{self.tag}_exch_start", ps, self.entries, ())
        return tok

    def exchange_to_join(self, after):
        ps, lands = _exchange_wait(f"{self.tag}_exch_wait", self.ps, self.land2, self.s2, self.entries, after)
        self.geos5 = [g for _, g in self.entries]
        gs = [_chip_sum(ps[pi], r, g, self.pos, f"{self.tag}_chip_sum_{e}")
              for e, ((pi, g), r) in enumerate(zip(self.entries, lands))]
        self.s3, self.gs, _, tok = _join_start(f"{self.tag}_join_start", gs, self.geos5, ())
        return tok

    def finish(self, after):
        return _join_wait(f"{self.tag}_join_wait", self.gs, self.s3, self.geos5, after)


def _pack(pieces):
    rows = []
    for p in pieces:
        flat = p.reshape(-1)
        pad = (-flat.shape[0]) % 1024
        rows.append(jnp.pad(flat, (0, pad)).reshape(-1, 128))
    return jnp.concatenate(rows, axis=0)


def _unpack(buf, shapes):
    out, r = [], 0
    for shp in shapes:
        n = math.prod(shp)
        nr = -(-n // 1024) * 8
        out.append(buf[r:r + nr].reshape(-1)[:n].reshape(shp))
        r += nr
    return out


def kernel(x, norm1_g, w_in, gmlp_ln_g, gmlp_ln_b, w_spatial, b_spatial, conv_w, group_norm_g, w_out, norm2_g, w_gate, w_up, w_down, final_norm_g, loss_target, m_norm1_g, m_w_in, m_gmlp_ln_g, m_gmlp_ln_b, m_w_spatial, m_b_spatial, m_conv_w, m_group_norm_g, m_w_out, m_norm2_g, m_w_gate, m_w_up, m_w_down, m_final_norm_g, v_norm1_g, v_w_in, v_gmlp_ln_g, v_gmlp_ln_b, v_w_spatial, v_b_spatial, v_conv_w, v_group_norm_g, v_w_out, v_norm2_g, v_w_gate, v_w_up, v_w_down, v_final_norm_g):
    L, D, n_in = w_in.shape
    T = x.shape[1]
    nin = N_CHIPS * n_in
    A = nin // 5
    H = A // HEAD_DIM
    n_f = w_gate.shape[2]
    F = N_CHIPS * n_f
    n_o = w_out.shape[1]
    cb = conv_w.shape[2]
    assert A == H * HEAD_DIM and T % 256 == 0 and N_CHIPS * n_o == D and N_CHIPS * cb == A

    g_in, g_out, g_down = _Geom(D, nin, 1, n_in), _Geom(D, D, 0, n_o), _Geom(F, D, 0, n_f)
    g_gate, g_up, g_gu = _Geom(D, 2 * F, 1, n_f, 0), _Geom(D, 2 * F, 1, n_f, F), _Geom(D, 2 * F, 1, n_f)
    pos = jnp.stack([2 * lax.axis_index("x") + lax.axis_index("y"), lax.axis_index("c")]).astype(jnp.int32)
    row = lambda v: v.reshape(1, -1)

    conv_full = _gather_conv(jnp.pad(conv_w, ((0, 0), (0, 8 - CONV_TAPS), (0, 0))))
    members = [[g_in], [g_out], [g_gate, g_up], [g_down]]
    sources = [[w_in], [w_out], [w_gate, w_up], [w_down]]
    placed, sems_of = {}, {}
    tok = conv_full
    for l in range(L):
        for a in range(4):
            arr = None
            for m, (w, geo) in enumerate(zip(sources[a], members[a])):
                arr = _cast_place(w, l, geo, pos, arr, f"l{l}_place_{a}_{m}", after=(tok,))
            sems_of[l, a], placed[l, a], tok = _gather_start(f"l{l}_gather_start_{a}", arr, members[a], (tok,))

    def arrive(l, a, after):
        return _gather_forward(f"l{l}_gather_fwd_{a}", placed[l, a], members[a], sems_of[l, a], after)

    def landed(l, a, fs, arr, after):
        return _gather_wait(f"l{l}_gather_wait_{a}", arr, members[a], fs, after)

    frame = jnp.arange(HEAD_DIM)
    mask = (frame[None, :] // CHUNK) <= (frame[:, None] // CHUNK)

    xs = x[0]
    acts = []
    for l in range(L):
        wm = jnp.where(mask[None], w_spatial[l], 0.0).astype(BF16)
        wmt = jnp.swapaxes(wm, 1, 2)
        bfull = jnp.broadcast_to(b_spatial[l][:, :, None], (H, HEAD_DIM, HEAD_DIM))
        sm = dict(lng=row(gmlp_ln_g[l]), lnb=row(gmlp_ln_b[l]), wm=wm, wmt=wmt, bfull=bfull,
                  cw=conv_full[l], gn=row(group_norm_g[l]))
        if l == 0:
            h = _rms_fwd(xs, row(norm1_g[l]), f"l{l}_rms1", after=(tok,))
            fs, arr, tok = arrive(l, 0, (h,))
            W_in = landed(l, 0, fs, arr, (tok,))
        else:
            fs, arr, tok = arrive(l, 0, (xs,))
            h = _rms_fwd(xs, row(norm1_g[l]), f"l{l}_rms1", after=(tok,))
            W_in = landed(l, 0, fs, arr, (h,))
        z = _mm_nn(h, W_in, out_dtype=BF16, name=f"l{l}_mm_in")
        fs, arr, tok = arrive(l, 1, (z,))
        y = _mixer_fwd(z, sm["lng"], sm["lnb"], wm, bfull, sm["cw"], sm["gn"], f"l{l}_mixer", after=(tok,))
        W_out = landed(l, 1, fs, arr, (y,))
        if l == 0:
            x1, h2 = _mm_res_rms(y, W_out, xs, row(norm2_g[l]), name=f"l{l}_mm_out")
            fs, arr, tok = arrive(l, 2, (x1,))
            W_gu = landed(l, 2, fs, arr, (tok,))
        else:
            fs, arr, tok = arrive(l, 2, (W_out,))
            x1, h2 = _mm_res_rms(y, W_out, xs, row(norm2_g[l]), name=f"l{l}_mm_out", after=(tok,))
            W_gu = landed(l, 2, fs, arr, (x1,))
        gu, act = _mm_swiglu(h2, W_gu, name=f"l{l}_mm_swiglu")
        fs, arr, tok = arrive(l, 3, (act,))
        W_down = landed(l, 3, fs, arr, (tok,))
        x2 = _mm_nn(act, W_down, res=x1, out_dtype=F32, tn=512, name=f"l{l}_mm_down")
        acts.append(dict(x=xs, h=h, z=z, y=y, x1=x1, h2=h2, gu=gu, act=act, sm=sm,
                         W_in=W_in, W_out=W_out, W_gu=W_gu, W_down=W_down))
        xs = x2

    loss_vec, dx, dxb, dgf = _loss_head(xs, row(final_norm_g), loss_target[0], "loss_head")
    loss = lax.psum(loss_vec[0, 0], ("x", "y", "c"))

    big_w = {"in": (w_in, m_w_in, v_w_in), "out": (w_out, m_w_out, v_w_out), "gate": (w_gate, m_w_gate, v_w_gate),
             "up": (w_up, m_w_up, v_w_up), "down": (w_down, m_w_down, v_w_down)}
    big = {nm: None for nm in big_w}

    def adamw(l, names, gs):
        after = ()
        for nm, g in zip(names, gs):
            w, m, v = big_w[nm]
            if l == L - 1 and L > 1:
                big[nm] = _adamw_sparsecore(l, w, g, m, v, f"l{l}_adamw_{nm}")
            else:
                big[nm] = _adamw_layer(l, w, g, m, v, big[nm], f"l{l}_adamw_{nm}", after=after)
                after = (big[nm][1],)
        return after

    small_grads = [None] * L
    pend_dg = pend_oi = None
    tok = ()
    for l in reversed(range(L)):
        a = acts[l]
        sm = a["sm"]
        dgu = _mm_nt_swiglu_bwd(dxb, a["W_down"], a["gu"], name=f"l{l}_bwd_down", after=tok)
        if pend_dg:
            tok = (pend_dg[0].exchange_to_join((dgu,)),)
        dW_down = _mm_tn(a["act"], dxb, name=f"l{l}_dw_down", after=tok)
        if pend_dg:
            tok = adamw(pend_dg[1], ["down", "gate", "up"], pend_dg[0].finish((dW_down,)))
        dh2 = _mm_nt_dgu(dgu, a["W_gu"], name=f"l{l}_bwd_gu", after=tok)
        if pend_oi:
            tok = (pend_oi[0].exchange_to_join((dh2,)),)
        dW_gu = _mm_tn(a["h2"], dgu, tkw=1024, tnw=512, name=f"l{l}_dw_gu", after=tok)
        if pend_oi:
            tok = adamw(pend_oi[1], ["out", "in"], pend_oi[0].finish((dW_gu,)))
        dg_batch = _GradBatch(f"l{l}_dg", [dW_down, dW_gu], [g_down, g_gu], [(0, g_down), (1, g_gate), (1, g_up)], pos)
        tok = (dg_batch.start_swap(tok),)
        dx1, dx1b, dg2 = _rms_bwd(dh2, a["x1"], row(norm2_g[l]), dx, f"l{l}_rms2_bwd", after=tok)
        dy = _mm_nt(dx1b, a["W_out"], out_dtype=BF16, name=f"l{l}_bwd_out")
        dW_out = _mm_tn(a["y"], dx1b, name=f"l{l}_dw_out")
        tok = (dg_batch.swap_to_exchange((dW_out, dy)),)
        dz, dgn, dlng, dlnb, dws, dbs, dcw = _mixer_bwd(
            a["z"], dy, sm["lng"], sm["lnb"], sm["wm"], sm["wmt"], sm["bfull"], sm["cw"], sm["gn"],
            f"l{l}_mixer_bwd", after=tok)
        dW_in = _mm_tn(a["h"], dz, tkw=1024, name=f"l{l}_dw_in")
        oi_batch = _GradBatch(f"l{l}_oi", [dW_out, dW_in], [g_out, g_in], [(0, g_out), (1, g_in)], pos)
        tok = (oi_batch.start_swap(()),)
        dh = _mm_nt(dz, a["W_in"], out_dtype=BF16, tn=512, name=f"l{l}_bwd_in", after=tok)
        dx, dxb, dg1 = _rms_bwd(dh, a["x"], row(norm1_g[l]), dx1, f"l{l}_rms1_bwd")
        tok = (oi_batch.swap_to_exchange((dx,)),)
        small_grads[l] = [dg1[0], dlng[0], dlnb[0], dws, dbs[:, 0, :], dcw[:CONV_TAPS], dgn[0], dg2[0]]
        pend_dg, pend_oi = (dg_batch, l), (oi_batch, l)
    grad_x = dx[None]

    tok = (pend_dg[0].exchange_to_join(tok),)
    tok = adamw(pend_dg[1], ["down", "gate", "up"], pend_dg[0].finish(tok))
    tok = (pend_oi[0].exchange_to_join(tok),)
    pieces = [p for l in range(L) for p in small_grads[l]] + [dgf[0]]
    red = _allreduce_small(_pack(pieces), after=tok)
    adamw(pend_oi[1], ["out", "in"], pend_oi[0].finish((red,)))

    red_list = _unpack(red, [p.shape for p in pieces])
    per = len(small_grads[0])
    stack = lambda i: jnp.stack([red_list[l * per + i] for l in range(L)])
    g_small = {"norm1_g": stack(0), "gmlp_ln_g": stack(1), "gmlp_ln_b": stack(2), "w_spatial": stack(3),
               "b_spatial": stack(4), "group_norm_g": stack(6), "norm2_g": stack(7),
               "final_norm_g": red_list[L * per]}
    g_small["conv_w"] = lax.dynamic_slice_in_dim(stack(5), pos[0] * cb, cb, axis=2)
    small_w = {"norm1_g": (norm1_g, m_norm1_g, v_norm1_g), "gmlp_ln_g": (gmlp_ln_g, m_gmlp_ln_g, v_gmlp_ln_g),
               "gmlp_ln_b": (gmlp_ln_b, m_gmlp_ln_b, v_gmlp_ln_b), "w_spatial": (w_spatial, m_w_spatial, v_w_spatial),
               "b_spatial": (b_spatial, m_b_spatial, v_b_spatial), "conv_w": (conv_w, m_conv_w, v_conv_w),
               "group_norm_g": (group_norm_g, m_group_norm_g, v_group_norm_g),
               "norm2_g": (norm2_g, m_norm2_g, v_norm2_g), "final_norm_g": (final_norm_g, m_final_norm_g, v_final_norm_g)}
    snames = list(small_w)
    sd, smn, svn = _adamw_flat(_pack([small_w[n][0] for n in snames]), _pack([g_small[n] for n in snames]),
                               _pack([small_w[n][1] for n in snames]), _pack([small_w[n][2] for n in snames]),
                               "adamw_small")
    sshapes = [small_w[n][0].shape for n in snames]
    sd, smn, svn = _unpack(sd, sshapes), _unpack(smn, sshapes), _unpack(svn, sshapes)
    small_out = {n: (g_small[n], sd[i], smn[i], svn[i]) for i, n in enumerate(snames)}

    order = ["norm1_g", "w_in", "gmlp_ln_g", "gmlp_ln_b", "w_spatial", "b_spatial", "conv_w", "group_norm_g",
             "w_out", "norm2_g", "w_gate", "w_up", "w_down", "final_norm_g"]
    res = {n: (big[n[2:]] if n[2:] in big else small_out[n]) for n in order}
    return (loss, grad_x, *[res[n][0] for n in order], *[res[n][1] for n in order],
            *[res[n][2] for n in order], *[res[n][3] for n in order])
```
